```python
import jax, jax.numpy as jnp
from jax import lax
import numpy as np

D_MODEL = 1024
BATCH = 8
SEQ = 4096
DEPTH = 2

D_MIX = D_MODEL
GLA_HEADS = 4
GLA_WIDTH = D_MIX // 2
GLA_DV = GLA_WIDTH // GLA_HEADS
GLA_DK = GLA_DV // 2
GLA_LOWRANK = 16
GLA_TAU = 16.0
GLA_CHUNK = 64
DIL_HEADS = 4
DIL_WIDTH = D_MIX - GLA_WIDTH
DIL_HD = DIL_WIDTH // DIL_HEADS
DIL_PATTERNS = ((128, 1), (512, 4), (2048, 16))
ROPE_THETA = 10000.0
EPS = 1e-6
IN_SIZES = (GLA_HEADS * GLA_DK, GLA_HEADS * GLA_DK, GLA_WIDTH, GLA_WIDTH, GLA_LOWRANK,
            DIL_WIDTH, DIL_WIDTH, DIL_WIDTH, DIL_WIDTH)
IN_COLS = sum(IN_SIZES)

kernel_name = "hybrid_gla_dilated_parallel_heads"


def rmsnorm(x, g):
    x32 = x.astype(jnp.float32)
    r = x32 * lax.rsqrt(jnp.mean(x32 * x32, axis=-1, keepdims=True) + EPS)
    return (r * g.astype(jnp.float32)).astype(x.dtype)


def head_rmsnorm(o, g):
    r = o * lax.rsqrt(jnp.mean(o * o, axis=-1, keepdims=True) + EPS)
    B, S, H, dh = o.shape
    return r.reshape(B, S, H * dh) * g.astype(jnp.float32)


def rope(x):
    S, hd = x.shape[1], x.shape[3]
    inv_freq = ROPE_THETA ** (-jnp.arange(0, hd, 2, dtype=jnp.float32) / hd)
    ang = jnp.arange(S, dtype=jnp.float32)[:, None] * inv_freq[None, :]
    cos = jnp.cos(ang)[None, :, None, :]
    sin = jnp.sin(ang)[None, :, None, :]
    x32 = x.astype(jnp.float32)
    x1, x2 = x32[..., : hd // 2], x32[..., hd // 2:]
    return jnp.concatenate([x1 * cos - x2 * sin, x2 * cos + x1 * sin], axis=-1)


def gla_mixer(q, k, v, lr, w_gate_up, b_gate_up):
    B, S, H, DK = q.shape
    DV = v.shape[-1]
    C = GLA_CHUNK
    n = S // C
    z = jnp.einsum('bsr,rk->bsk', lr.astype(jnp.float32), w_gate_up.astype(jnp.float32)) + b_gate_up.astype(jnp.float32)
    log_a = jax.nn.log_sigmoid(z).reshape(B, S, H, DK) / GLA_TAU

    def chunks(t):
        return t.astype(jnp.float32).reshape(B, n, C, H, t.shape[-1]).transpose(0, 3, 1, 2, 4)

    qc, kc, vc, la = chunks(q), chunks(k), chunks(v), chunks(log_a)
    b = jnp.cumsum(la, axis=3)
    q_e = qc * jnp.exp(b)
    k_e = kc * jnp.exp(-b)
    causal = jnp.tril(jnp.ones((C, C), dtype=bool))
    A = jnp.where(causal, jnp.einsum('bhnik,bhnjk->bhnij', q_e, k_e), 0.0)
    o_intra = jnp.einsum('bhnij,bhnjv->bhniv', A, vc)
    b_last = b[:, :, :, -1, :]
    k_end = kc * jnp.exp(b_last[:, :, :, None, :] - b)
    chunk_state = jnp.einsum('bhnjk,bhnjv->bhnkv', k_end, vc)
    decay = jnp.exp(b_last)

    def step(state, inp):
        dec, cs = inp
        return dec[..., None] * state + cs, state

    init = jnp.zeros((B, H, DK, DV), jnp.float32)
    _, before = lax.scan(step, init, (jnp.moveaxis(decay, 2, 0), jnp.moveaxis(chunk_state, 2, 0)))
    before = jnp.moveaxis(before, 0, 2)
    o = o_intra + jnp.einsum('bhnik,bhnkv->bhniv', q_e, before)
    return o.transpose(0, 2, 3, 1, 4).reshape(B, S, H, DV)


def dilated_pattern(q, k, v, window, dilation):
    B, S, H, hd = q.shape
    lb = window // dilation
    span = lb * dilation
    s_pad = -(-S // span) * span
    pad = s_pad - S
    L = s_pad // dilation
    nb = L // lb

    def to_blocks(t):
        t = jnp.pad(t, ((0, 0), (0, pad), (0, 0), (0, 0)))
        t = t.reshape(B, L, dilation, H, hd).transpose(0, 3, 2, 1, 4)
        return t.reshape(B, H, dilation, nb, lb, hd)

    def with_prev(t):
        prev = jnp.concatenate([jnp.zeros_like(t[:, :, :, :1]), t[:, :, :, :-1]], axis=3)
        return jnp.concatenate([prev, t], axis=4)

    qb = to_blocks(q)
    kk = with_prev(to_blocks(k))
    vv = with_prev(to_blocks(v))
    s = jnp.einsum('bhrnqd,bhrnkd->bhrnqk', qb, kk) * (hd ** -0.5)
    qi = jnp.arange(lb)[:, None]
    ki = jnp.arange(2 * lb)[None, :]
    dist = qi + lb - ki
    blk = jnp.arange(nb)[:, None, None]
    valid = (dist >= 0) & (dist <= lb) & ((blk > 0) | (ki >= lb)[None])
    s = jnp.where(valid, s, -jnp.inf)
    m = jnp.max(s, axis=-1, keepdims=True)
    p = jnp.exp(s - m)
    den = jnp.sum(p, axis=-1)
    o = jnp.einsum('bhrnqk,bhrnkd->bhrnqd', p, vv) / den[..., None]
    lse = m[..., 0] + jnp.log(den)
    o = o.reshape(B, H, dilation, L, hd).transpose(0, 3, 2, 1, 4).reshape(B, s_pad, H, hd)[:, :S]
    lse = lse.reshape(B, H, dilation, L).transpose(0, 3, 2, 1).reshape(B, s_pad, H)[:, :S]
    return o, lse


def dilated_mixer(q, k, v):
    v = v.astype(jnp.float32)
    outs, lses = [], []
    for window, dilation in DIL_PATTERNS:
        o, lse = dilated_pattern(q, k, v, window, dilation)
        outs.append(o)
        lses.append(lse)
    w = jax.nn.softmax(jnp.stack(lses, axis=0), axis=0)
    return jnp.sum(w[..., None] * jnp.stack(outs, axis=0), axis=0)


def hybrid_layer(x, c, w_ada, b_ada, g_pre, w_in, w_gate_up, b_gate_up, g_gla, g_dil, w_out, g_post):
    B, S, D = x.shape
    mod = jnp.einsum('bd,de->be', jax.nn.silu(c), w_ada) + b_ada
    shift, scale, gate = jnp.split(mod, 3, axis=-1)
    h = rmsnorm(x, g_pre) * (1 + scale[:, None, :]) + shift[:, None, :]
    proj = jnp.einsum('bsd,de->bse', h, w_in)
    offs = [int(o) for o in np.cumsum(IN_SIZES)[:-1]]
    q_a, k_a, v_a, z_a, lr, q_b, k_b, v_b, z_b = jnp.split(proj, offs, axis=-1)
    q_a = q_a.reshape(B, S, GLA_HEADS, GLA_DK) * (GLA_DK ** -0.5)
    k_a = k_a.reshape(B, S, GLA_HEADS, GLA_DK)
    v_a = v_a.reshape(B, S, GLA_HEADS, GLA_DV)
    o_a = gla_mixer(q_a, k_a, v_a, lr, w_gate_up, b_gate_up)
    y_a = head_rmsnorm(o_a, g_gla) * jax.nn.silu(z_a.astype(jnp.float32))
    q_b = rope(q_b.reshape(B, S, DIL_HEADS, DIL_HD))
    k_b = rope(k_b.reshape(B, S, DIL_HEADS, DIL_HD))
    v_b = v_b.reshape(B, S, DIL_HEADS, DIL_HD)
    o_b = dilated_mixer(q_b, k_b, v_b)
    y_b = head_rmsnorm(o_b, g_dil) * jax.nn.silu(z_b.astype(jnp.float32))
    y = jnp.concatenate([y_a, y_b], axis=-1).astype(x.dtype)
    y = jnp.einsum('bse,ed->bsd', y, w_out)
    return x + gate[:, None, :] * rmsnorm(y, g_post)


def setup_inputs(seed: int = 0) -> dict:
    key = jax.random.key(seed)
    ks = jax.random.split(key, 14)
    f = jnp.float32
    nrm = lambda k, shape, s: jax.random.normal(k, shape, f) * s
    return {
        "x": nrm(ks[0], (BATCH, SEQ, D_MODEL), 1.0),
        "c": nrm(ks[1], (BATCH, D_MODEL), 1.0),
        "w_ada": nrm(ks[2], (DEPTH, D_MODEL, 3 * D_MODEL), D_MODEL ** -0.5),
        "b_ada": nrm(ks[3], (DEPTH, 3 * D_MODEL), 0.02),
        "g_pre": 1.0 + nrm(ks[4], (DEPTH, D_MODEL), 0.02),
        "w_in": nrm(ks[5], (DEPTH, D_MODEL, IN_COLS), D_MODEL ** -0.5),
        "w_gate_up": nrm(ks[6], (DEPTH, GLA_LOWRANK, GLA_HEADS * GLA_DK), GLA_LOWRANK ** -0.5),
        "b_gate_up": nrm(ks[7], (DEPTH, GLA_HEADS * GLA_DK), 0.1),
        "g_gla": 1.0 + nrm(ks[8], (DEPTH, GLA_WIDTH), 0.02),
        "g_dil": 1.0 + nrm(ks[9], (DEPTH, DIL_WIDTH), 0.02),
        "w_out": nrm(ks[10], (DEPTH, D_MIX, D_MODEL), D_MIX ** -0.5),
        "g_post": 1.0 + nrm(ks[11], (DEPTH, D_MODEL), 0.02),
    }


def reference(x, c, w_ada, b_ada, g_pre, w_in, w_gate_up, b_gate_up, g_gla, g_dil, w_out, g_post):
    for l in range(DEPTH):
        x = hybrid_layer(x, c, w_ada[l], b_ada[l], g_pre[l], w_in[l], w_gate_up[l], b_gate_up[l],
                         g_gla[l], g_dil[l], w_out[l], g_post[l])
    return x
```

```python
import functools

import jax
import jax.numpy as jnp
import numpy as np
from jax import lax
from jax.experimental import pallas as pl
from jax.experimental.pallas import tpu as pltpu

F32 = jnp.float32
BF16 = jnp.bfloat16

D_MODEL = 1024
GLA_HEADS = 4
GLA_DK = 64
GLA_DV = 128
GLA_QK = GLA_HEADS * GLA_DK
GLA_WIDTH = GLA_HEADS * GLA_DV
GLA_LOWRANK = 16
GLA_TAU = 16.0
GLA_CHUNK = 64
DIL_HEADS = 4
DIL_HD = 128
DIL_WIDTH = DIL_HEADS * DIL_HD
DIL_PATTERNS = ((128, 1), (512, 4), (2048, 16))
DIL_LB = 128
ROPE_THETA = 10000.0
EPS = 1e-6
LANES = 128
LR_PAD = LANES

VMEM_LIMIT = 56 * 1024 * 1024
ROW_TILE = 512
NEG_INF = float("-inf")


def _silu(v):
    return v * jax.nn.sigmoid(v)


def _dot(a, b):
    return jnp.dot(a, b, preferred_element_type=F32)


def _dot_nt(a, b):
    return lax.dot_general(a, b, (((1,), (1,)), ((), ())), preferred_element_type=F32)


def _dot_tn(a, b):
    return lax.dot_general(a, b, (((0,), (0,)), ((), ())), preferred_element_type=F32)


def _mod_kernel(c_ref, w_ref, b_ref, o_ref):
    sc = _silu(c_ref[...]).astype(BF16)
    o_ref[...] = _dot(sc, w_ref[...].astype(BF16)) + b_ref[...]


def _modulation(c, w_ada, b_ada):
    depth, d, e = w_ada.shape
    bsz = c.shape[0]
    nt = e // d
    return pl.pallas_call(
        _mod_kernel,
        grid=(depth, nt),
        in_specs=[
            pl.BlockSpec((bsz, d), lambda l, j: (0, 0)),
            pl.BlockSpec((None, d, d), lambda l, j: (l, 0, j)),
            pl.BlockSpec((None, 1, d), lambda l, j: (l, 0, j)),
        ],
        out_specs=pl.BlockSpec((None, bsz, d), lambda l, j: (l, 0, j)),
        out_shape=jax.ShapeDtypeStruct((depth, bsz, e), F32),
        compiler_params=pltpu.CompilerParams(
            dimension_semantics=("arbitrary", "arbitrary"), vmem_limit_bytes=VMEM_LIMIT),
        name="adaln_mod",
    )(c, w_ada, b_ada.reshape(depth, 1, e))


def _rope(v, cos, sin_signed):
    return v * cos + pltpu.roll(v, DIL_HD // 2, axis=1) * sin_signed


def _in_proj_kernel(x_ref, mod_ref, gpre_ref, wqa_ref, wka_ref, wva_ref, wza_ref, wlr_ref,
                    wqb_ref, wkb_ref, wvb_ref, wzb_ref, wg_ref, bg_ref, cos_ref, sin_ref,
                    qa_ref, ka_ref, va_ref, za_ref, la_ref, qb_ref, kb_ref, vb_ref, zb_ref):
    x = x_ref[...]
    shift = mod_ref[:, 0:D_MODEL]
    scale = mod_ref[:, D_MODEL:2 * D_MODEL]
    r = x * lax.rsqrt(jnp.mean(x * x, axis=-1, keepdims=True) + EPS) * gpre_ref[...]
    h = (r * (1.0 + scale) + shift).astype(BF16)

    qa_ref[...] = (_dot(h, wqa_ref[...]) * (GLA_DK ** -0.5)).astype(qa_ref.dtype)
    ka_ref[...] = _dot(h, wka_ref[...]).astype(ka_ref.dtype)
    va_ref[...] = _dot(h, wva_ref[...]).astype(va_ref.dtype)
    za_ref[...] = _dot(h, wza_ref[...]).astype(za_ref.dtype)

    lr = _dot(h, wlr_ref[...]).astype(BF16)
    z = _dot(lr, wg_ref[...]) + bg_ref[...]
    log_sig = jnp.minimum(z, 0.0) - jnp.log1p(jnp.exp(-jnp.abs(z)))
    la_ref[...] = log_sig / GLA_TAU

    cos = cos_ref[...]
    sin = sin_ref[...]
    for hd in range(DIL_HEADS):
        cols = slice(hd * DIL_HD, (hd + 1) * DIL_HD)
        q = _dot(h, wqb_ref[:, cols])
        qb_ref[:, cols] = _rope(q, cos, sin) * (DIL_HD ** -0.5)
        k = _dot(h, wkb_ref[:, cols])
        kb_ref[:, cols] = _rope(k, cos, sin)
    vb_ref[...] = _dot(h, wvb_ref[...])
    zb_ref[...] = _dot(h, wzb_ref[...]).astype(zb_ref.dtype)


def _in_proj(x, mod_l, g_pre, w, cos, sin):
    bsz, seq, d = x.shape
    tm = ROW_TILE
    row = lambda b, i: (b, i, 0)
    const = lambda b, i: (0, 0)

    def full(a):
        return pl.BlockSpec(a.shape, const)

    out_cols = (("qa", GLA_QK, BF16), ("ka", GLA_QK, BF16), ("va", GLA_WIDTH, BF16),
                ("za", GLA_WIDTH, BF16), ("la", GLA_QK, F32), ("qb", DIL_WIDTH, F32),
                ("kb", DIL_WIDTH, F32), ("vb", DIL_WIDTH, F32), ("zb", DIL_WIDTH, BF16))
    weights = [w[k] for k in ("qa", "ka", "va", "za", "lr", "qb", "kb", "vb", "zb", "wg", "bg")]
    return pl.pallas_call(
        _in_proj_kernel,
        grid=(bsz, seq // tm),
        in_specs=[
            pl.BlockSpec((None, tm, d), row),
            pl.BlockSpec((None, 1, 3 * d), lambda b, i: (b, 0, 0)),
            full(g_pre),
            *[full(a) for a in weights],
            pl.BlockSpec((tm, DIL_HD), lambda b, i: (i, 0)),
            pl.BlockSpec((tm, DIL_HD), lambda b, i: (i, 0)),
        ],
        out_specs=[pl.BlockSpec((None, tm, n), row) for _, n, _ in out_cols],
        out_shape=[jax.ShapeDtypeStruct((bsz, seq, n), dt) for _, n, dt in out_cols],
        compiler_params=pltpu.CompilerParams(
            dimension_semantics=("arbitrary", "arbitrary"), vmem_limit_bytes=VMEM_LIMIT),
        name="in_proj",
    )(x, mod_l, g_pre, *weights, cos, sin)


def _head_norm_gate(o, g, z):
    r = o * lax.rsqrt(jnp.mean(o * o, axis=-1, keepdims=True) + EPS)
    return r * g * _silu(z.astype(F32))


def _split3(v):
    h1 = v.astype(BF16)
    r1 = v - h1.astype(F32)
    h2 = r1.astype(BF16)
    h3 = (r1 - h2.astype(F32)).astype(BF16)
    return h1, h2, h3


def _gla_kernel(qa_ref, ka_ref, va_ref, za_ref, la_ref, g_ref, ya_ref, st_ref, *, chunks):
    @pl.when(pl.program_id(1) == 0)
    def _():
        st_ref[...] = jnp.zeros_like(st_ref)

    c_len = GLA_CHUNK
    ri = lax.broadcasted_iota(jnp.int32, (c_len, c_len), 0)
    ci = lax.broadcasted_iota(jnp.int32, (c_len, c_len), 1)
    causal = ri >= ci
    tril = causal.astype(BF16)

    def chunk(c, carry):
        rows = pl.ds(pl.multiple_of(c * c_len, c_len), c_len)
        h1, h2, h3 = _split3(la_ref[rows, :])
        b = _dot(tril, h1) + _dot(tril, h2) + _dot(tril, h3)
        b_last = b[c_len - 1:c_len, :]
        q = qa_ref[rows, :].astype(F32)
        k = ka_ref[rows, :].astype(F32)
        q_e = (q * jnp.exp(b)).astype(BF16)
        k_e = (k * jnp.exp(-b)).astype(BF16)
        k_end = (k * jnp.exp(b_last - b)).astype(BF16)
        decay = jnp.exp(b_last)
        for hd in range(GLA_HEADS):
            kc = slice(hd * GLA_DK, (hd + 1) * GLA_DK)
            vc = slice(hd * GLA_DV, (hd + 1) * GLA_DV)
            v = va_ref[rows, vc]
            st = st_ref[hd]
            a = jnp.where(causal, _dot_nt(q_e[:, kc], k_e[:, kc]), 0.0).astype(BF16)
            o = _dot(a, v) + _dot_nt(q_e[:, kc], st.astype(BF16))
            st_ref[hd] = st * decay[:, kc] + _dot_tn(v, k_end[:, kc])
            ya_ref[rows, vc] = _head_norm_gate(o, g_ref[:, vc], za_ref[rows, vc]).astype(ya_ref.dtype)
        return carry

    lax.fori_loop(0, chunks, chunk, 0)


def _gla(qa, ka, va, za, la, g_gla):
    bsz, seq, _ = qa.shape
    ts = ROW_TILE
    row = lambda b, i: (b, i, 0)
    return pl.pallas_call(
        functools.partial(_gla_kernel, chunks=ts // GLA_CHUNK),
        grid=(bsz, seq // ts),
        in_specs=[
            pl.BlockSpec((None, ts, GLA_QK), row),
            pl.BlockSpec((None, ts, GLA_QK), row),
            pl.BlockSpec((None, ts, GLA_WIDTH), row),
            pl.BlockSpec((None, ts, GLA_WIDTH), row),
            pl.BlockSpec((None, ts, GLA_QK), row),
            pl.BlockSpec((1, GLA_WIDTH), lambda b, i: (0, 0)),
        ],
        out_specs=pl.BlockSpec((None, ts, GLA_WIDTH), row),
        out_shape=jax.ShapeDtypeStruct((bsz, seq, GLA_WIDTH), BF16),
        scratch_shapes=[pltpu.VMEM((GLA_HEADS, GLA_DV, GLA_DK), F32)],
        compiler_params=pltpu.CompilerParams(
            dimension_semantics=("arbitrary", "arbitrary"), vmem_limit_bytes=VMEM_LIMIT),
        name="gla",
    )(qa, ka, va, za, la, g_gla)


def _dil_kernel(q_ref, k_ref, v_ref, z_ref, g_ref, y_ref, o1, o2, o3, l1, l2, l3, *, seq):
    lb = DIL_LB
    qi = lax.broadcasted_iota(jnp.int32, (lb, 2 * lb), 0)
    ki = lax.broadcasted_iota(jnp.int32, (lb, 2 * lb), 1)
    dist = qi + lb - ki
    band = (dist >= 0) & (dist <= lb)
    cur = ki >= lb

    for (window, dil), o_scr, l_scr in zip(DIL_PATTERNS, (o1, o2, o3), (l1, l2, l3)):
        nb = seq // window

        def block(idx, carry, dil=dil, window=window, nb=nb, o_scr=o_scr, l_scr=l_scr):
            res = idx // nb
            n = idx % nb
            q0 = n * window + res
            p0 = jnp.maximum(n - 1, 0) * window + res
            if dil == 1:
                rows_c = pl.ds(pl.multiple_of(q0, lb), lb)
                rows_p = pl.ds(pl.multiple_of(p0, lb), lb)
            else:
                rows_c = pl.ds(q0, lb, stride=dil)
                rows_p = pl.ds(p0, lb, stride=dil)
            q = q_ref[rows_c, :].astype(BF16)
            kk = jnp.concatenate([k_ref[rows_p, :], k_ref[rows_c, :]], axis=0).astype(BF16)
            vv = jnp.concatenate([v_ref[rows_p, :], v_ref[rows_c, :]], axis=0).astype(BF16)
            valid = band & (cur | (n > 0))
            s = jnp.where(valid, _dot_nt(q, kk), NEG_INF)
            m = jnp.max(s, axis=-1, keepdims=True)
            p = jnp.exp(s - m)
            den = jnp.sum(p, axis=-1, keepdims=True)
            o = _dot(p.astype(BF16), vv) / den
            o_scr[rows_c, :] = o
            l_scr[rows_c, :] = jnp.broadcast_to(m + jnp.log(den), (lb, DIL_HD))
            return carry

        lax.fori_loop(0, seq // lb, block, 0)

    tr = ROW_TILE

    def combine(t, carry):
        rows = pl.ds(pl.multiple_of(t * tr, tr), tr)
        a1, a2, a3 = l1[rows, :], l2[rows, :], l3[rows, :]
        m = jnp.maximum(jnp.maximum(a1, a2), a3)
        e1, e2, e3 = jnp.exp(a1 - m), jnp.exp(a2 - m), jnp.exp(a3 - m)
        o = (e1 * o1[rows, :] + e2 * o2[rows, :] + e3 * o3[rows, :]) / (e1 + e2 + e3)
        y_ref[rows, :] = _head_norm_gate(o, g_ref[...], z_ref[rows, :]).astype(y_ref.dtype)
        return carry

    lax.fori_loop(0, seq // tr, combine, 0)


def _dilated(qb, kb, vb, zb, g_dil):
    bsz, seq, _ = qb.shape
    head = lambda b, h: (b, 0, h)
    blk = pl.BlockSpec((None, seq, DIL_HD), head)
    return pl.pallas_call(
        functools.partial(_dil_kernel, seq=seq),
        grid=(bsz, DIL_HEADS),
        in_specs=[blk, blk, blk, blk, pl.BlockSpec((1, DIL_HD), lambda b, h: (0, h))],
        out_specs=blk,
        out_shape=jax.ShapeDtypeStruct((bsz, seq, DIL_WIDTH), BF16),
        scratch_shapes=[pltpu.VMEM((seq, DIL_HD), F32) for _ in range(6)],
        compiler_params=pltpu.CompilerParams(
            dimension_semantics=("arbitrary", "arbitrary"), vmem_limit_bytes=VMEM_LIMIT),
        name="dilated",
    )(qb, kb, vb, zb, g_dil)


def _out_proj_kernel(ya_ref, yb_ref, x_ref, mod_ref, wa_ref, wb_ref, gpost_ref, o_ref):
    y = _dot(ya_ref[...], wa_ref[...]) + _dot(yb_ref[...], wb_ref[...])
    r = y * lax.rsqrt(jnp.mean(y * y, axis=-1, keepdims=True) + EPS) * gpost_ref[...]
    gate = mod_ref[:, 2 * D_MODEL:3 * D_MODEL]
    o_ref[...] = x_ref[...] + gate * r


def _out_proj(ya, yb, x, mod_l, w_out_a, w_out_b, g_post):
    bsz, seq, d = x.shape
    tm = ROW_TILE
    row = lambda b, i: (b, i, 0)
    const = lambda b, i: (0, 0)
    return pl.pallas_call(
        _out_proj_kernel,
        grid=(bsz, seq // tm),
        in_specs=[
            pl.BlockSpec((None, tm, GLA_WIDTH), row),
            pl.BlockSpec((None, tm, DIL_WIDTH), row),
            pl.BlockSpec((None, tm, d), row),
            pl.BlockSpec((None, 1, 3 * d), lambda b, i: (b, 0, 0)),
            pl.BlockSpec(w_out_a.shape, const),
            pl.BlockSpec(w_out_b.shape, const),
            pl.BlockSpec((1, d), const),
        ],
        out_specs=pl.BlockSpec((None, tm, d), row),
        out_shape=jax.ShapeDtypeStruct((bsz, seq, d), F32),
        compiler_params=pltpu.CompilerParams(
            dimension_semantics=("arbitrary", "arbitrary"), vmem_limit_bytes=VMEM_LIMIT),
        name="out_proj",
    )(ya, yb, x, mod_l, w_out_a, w_out_b, g_post)


def _rope_tables(seq):
    half = DIL_HD // 2
    inv_freq = ROPE_THETA ** (-jnp.arange(0, DIL_HD, 2, dtype=F32) / DIL_HD)
    ang = jnp.arange(seq, dtype=F32)[:, None] * inv_freq[None, :]
    cos, sin = jnp.cos(ang), jnp.sin(ang)
    del half
    return jnp.concatenate([cos, cos], axis=-1), jnp.concatenate([-sin, sin], axis=-1)


def _split_w_in(w_in_l, w_gate_up_l, b_gate_up_l):
    sizes = (GLA_QK, GLA_QK, GLA_WIDTH, GLA_WIDTH, GLA_LOWRANK,
             DIL_WIDTH, DIL_WIDTH, DIL_WIDTH, DIL_WIDTH)
    offs = np.cumsum((0,) + sizes)
    names = ("qa", "ka", "va", "za", "lr", "qb", "kb", "vb", "zb")
    w = {n: w_in_l[:, offs[i]:offs[i + 1]].astype(BF16) for i, n in enumerate(names)}
    w["lr"] = jnp.pad(w["lr"], ((0, 0), (0, LR_PAD - GLA_LOWRANK)))
    w["wg"] = jnp.pad(w_gate_up_l.astype(BF16), ((0, LR_PAD - GLA_LOWRANK), (0, 0)))
    w["bg"] = b_gate_up_l.reshape(1, GLA_QK)
    return w


def kernel(x, c, w_ada, b_ada, g_pre, w_in, w_gate_up, b_gate_up, g_gla, g_dil, w_out, g_post):
    bsz, seq, d = x.shape
    depth = w_ada.shape[0]
    mod = _modulation(c, w_ada, b_ada).reshape(depth, bsz, 1, 3 * d)
    cos, sin = _rope_tables(seq)
    for l in range(depth):
        w = _split_w_in(w_in[l], w_gate_up[l], b_gate_up[l])
        qa, ka, va, za, la, qb, kb, vb, zb = _in_proj(x, mod[l], g_pre[l].reshape(1, d), w, cos, sin)
        ya = _gla(qa, ka, va, za, la, g_gla[l].reshape(1, GLA_WIDTH))
        yb = _dilated(qb, kb, vb, zb, g_dil[l].reshape(1, DIL_WIDTH))
        w_o = w_out[l].astype(BF16)
        x = _out_proj(ya, yb, x, mod[l], w_o[:GLA_WIDTH], w_o[GLA_WIDTH:], g_post[l].reshape(1, d))
    return x
```

```python
import functools

import jax
import jax.numpy as jnp
import numpy as np
from jax import lax
from jax.experimental import pallas as pl
from jax.experimental.pallas import tpu as pltpu

F32 = jnp.float32
BF16 = jnp.bfloat16

D_MODEL = 1024
GLA_HEADS = 4
GLA_DK = 64
GLA_DV = 128
GLA_QK = GLA_HEADS * GLA_DK
GLA_WIDTH = GLA_HEADS * GLA_DV
GLA_LOWRANK = 16
GLA_TAU = 16.0
GLA_CHUNK = 64
DIL_HEADS = 4
DIL_HD = 128
DIL_WIDTH = DIL_HEADS * DIL_HD
DIL_PATTERNS = ((128, 1), (512, 4), (2048, 16))
DIL_LB = 128
ROPE_THETA = 10000.0
EPS = 1e-6
LANES = 128
LR_PAD = LANES

VMEM_LIMIT = 56 * 1024 * 1024
DIL_UNROLL = 8
ROW_TILE = 512
NEG_INF = float("-inf")


def _silu(v):
    return v * jax.nn.sigmoid(v)


def _dot(a, b):
    return jnp.dot(a, b, preferred_element_type=F32)


def _dot_nt(a, b):
    return lax.dot_general(a, b, (((1,), (1,)), ((), ())), preferred_element_type=F32)


def _dot_tn(a, b):
    return lax.dot_general(a, b, (((0,), (0,)), ((), ())), preferred_element_type=F32)


def _mod_kernel(c_ref, w_ref, b_ref, o_ref):
    sc = _silu(c_ref[...]).astype(BF16)
    o_ref[...] = _dot(sc, w_ref[...].astype(BF16)) + b_ref[...]


def _modulation(c, w_ada, b_ada):
    depth, d, e = w_ada.shape
    bsz = c.shape[0]
    nt = e // d
    return pl.pallas_call(
        _mod_kernel,
        grid=(depth, nt),
        in_specs=[
            pl.BlockSpec((bsz, d), lambda l, j: (0, 0)),
            pl.BlockSpec((None, d, d), lambda l, j: (l, 0, j)),
            pl.BlockSpec((None, 1, d), lambda l, j: (l, 0, j)),
        ],
        out_specs=pl.BlockSpec((None, bsz, d), lambda l, j: (l, 0, j)),
        out_shape=jax.ShapeDtypeStruct((depth, bsz, e), F32),
        compiler_params=pltpu.CompilerParams(
            dimension_semantics=("arbitrary", "arbitrary"), vmem_limit_bytes=VMEM_LIMIT),
        name="adaln_mod",
    )(c, w_ada, b_ada.reshape(depth, 1, e))


def _rope(v, cos, sin_signed):
    return v * cos + pltpu.roll(v, DIL_HD // 2, axis=1) * sin_signed


def _in_proj_kernel(x_ref, mod_ref, gpre_ref, wqa_ref, wka_ref, wva_ref, wza_ref, wlr_ref,
                    wqb_ref, wkb_ref, wvb_ref, wzb_ref, wg_ref, bg_ref, cos_ref, sin_ref,
                    qa_ref, ka_ref, va_ref, za_ref, la_ref, qb_ref, kb_ref, vb_ref, zb_ref):
    x = x_ref[...]
    shift = mod_ref[:, 0:D_MODEL]
    scale = mod_ref[:, D_MODEL:2 * D_MODEL]
    r = x * lax.rsqrt(jnp.mean(x * x, axis=-1, keepdims=True) + EPS) * gpre_ref[...]
    h = (r * (1.0 + scale) + shift).astype(BF16)

    qa_ref[...] = (_dot(h, wqa_ref[...]) * (GLA_DK ** -0.5)).astype(qa_ref.dtype)
    ka_ref[...] = _dot(h, wka_ref[...]).astype(ka_ref.dtype)
    va_ref[...] = _dot(h, wva_ref[...]).astype(va_ref.dtype)
    za_ref[...] = _dot(h, wza_ref[...]).astype(za_ref.dtype)

    lr = _dot(h, wlr_ref[...]).astype(BF16)
    z = _dot(lr, wg_ref[...]) + bg_ref[...]
    log_sig = jnp.minimum(z, 0.0) - jnp.log1p(jnp.exp(-jnp.abs(z)))
    la_ref[...] = log_sig / GLA_TAU

    cos = cos_ref[...]
    sin = sin_ref[...]
    for hd in range(DIL_HEADS):
        cols = slice(hd * DIL_HD, (hd + 1) * DIL_HD)
        q = _dot(h, wqb_ref[:, cols])
        qb_ref[:, cols] = _rope(q, cos, sin) * (DIL_HD ** -0.5)
        k = _dot(h, wkb_ref[:, cols])
        kb_ref[:, cols] = _rope(k, cos, sin)
    vb_ref[...] = _dot(h, wvb_ref[...])
    zb_ref[...] = _dot(h, wzb_ref[...]).astype(zb_ref.dtype)


def _in_proj(x, mod_l, g_pre, w, cos, sin):
    bsz, seq, d = x.shape
    tm = ROW_TILE
    row = lambda b, i: (b, i, 0)
    const = lambda b, i: (0, 0)

    def full(a):
        return pl.BlockSpec(a.shape, const)

    out_cols = (("qa", GLA_QK, BF16), ("ka", GLA_QK, BF16), ("va", GLA_WIDTH, BF16),
                ("za", GLA_WIDTH, BF16), ("la", GLA_QK, F32), ("qb", DIL_WIDTH, F32),
                ("kb", DIL_WIDTH, F32), ("vb", DIL_WIDTH, F32), ("zb", DIL_WIDTH, BF16))
    weights = [w[k] for k in ("qa", "ka", "va", "za", "lr", "qb", "kb", "vb", "zb", "wg", "bg")]
    return pl.pallas_call(
        _in_proj_kernel,
        grid=(bsz, seq // tm),
        in_specs=[
            pl.BlockSpec((None, tm, d), row),
            pl.BlockSpec((None, 1, 3 * d), lambda b, i: (b, 0, 0)),
            full(g_pre),
            *[full(a) for a in weights],
            pl.BlockSpec((tm, DIL_HD), lambda b, i: (i, 0)),
            pl.BlockSpec((tm, DIL_HD), lambda b, i: (i, 0)),
        ],
        out_specs=[pl.BlockSpec((None, tm, n), row) for _, n, _ in out_cols],
        out_shape=[jax.ShapeDtypeStruct((bsz, seq, n), dt) for _, n, dt in out_cols],
        compiler_params=pltpu.CompilerParams(
            dimension_semantics=("arbitrary", "arbitrary"), vmem_limit_bytes=VMEM_LIMIT),
        name="in_proj",
    )(x, mod_l, g_pre, *weights, cos, sin)


def _head_norm_gate(o, g, z):
    r = o * lax.rsqrt(jnp.mean(o * o, axis=-1, keepdims=True) + EPS)
    return r * g * _silu(z.astype(F32))


def _split3(v):
    h1 = v.astype(BF16)
    r1 = v - h1.astype(F32)
    h2 = r1.astype(BF16)
    h3 = (r1 - h2.astype(F32)).astype(BF16)
    return h1, h2, h3


def _gla_kernel(qa_ref, ka_ref, va_ref, za_ref, la_ref, g_ref, ya_ref, st_ref, *, chunks):
    @pl.when(pl.program_id(1) == 0)
    def _():
        st_ref[...] = jnp.zeros_like(st_ref)

    c_len = GLA_CHUNK
    ri = lax.broadcasted_iota(jnp.int32, (c_len, c_len), 0)
    ci = lax.broadcasted_iota(jnp.int32, (c_len, c_len), 1)
    causal = ri >= ci
    tril = causal.astype(BF16)

    def chunk(c, carry):
        rows = pl.ds(pl.multiple_of(c * c_len, c_len), c_len)
        h1, h2, h3 = _split3(la_ref[rows, :])
        b = _dot(tril, h1) + _dot(tril, h2) + _dot(tril, h3)
        b_last = b[c_len - 1:c_len, :]
        q = qa_ref[rows, :].astype(F32)
        k = ka_ref[rows, :].astype(F32)
        q_e = (q * jnp.exp(b)).astype(BF16)
        k_e = (k * jnp.exp(-b)).astype(BF16)
        k_end = (k * jnp.exp(b_last - b)).astype(BF16)
        decay = jnp.exp(b_last)
        for hd in range(GLA_HEADS):
            kc = slice(hd * GLA_DK, (hd + 1) * GLA_DK)
            vc = slice(hd * GLA_DV, (hd + 1) * GLA_DV)
            v = va_ref[rows, vc]
            st = st_ref[hd]
            a = jnp.where(causal, _dot_nt(q_e[:, kc], k_e[:, kc]), 0.0).astype(BF16)
            o = _dot(a, v) + _dot_nt(q_e[:, kc], st.astype(BF16))
            st_ref[hd] = st * decay[:, kc] + _dot_tn(v, k_end[:, kc])
            ya_ref[rows, vc] = _head_norm_gate(o, g_ref[:, vc], za_ref[rows, vc]).astype(ya_ref.dtype)
        return carry

    lax.fori_loop(0, chunks, chunk, 0)


def _gla(qa, ka, va, za, la, g_gla):
    bsz, seq, _ = qa.shape
    ts = ROW_TILE
    row = lambda b, i: (b, i, 0)
    return pl.pallas_call(
        functools.partial(_gla_kernel, chunks=ts // GLA_CHUNK),
        grid=(bsz, seq // ts),
        in_specs=[
            pl.BlockSpec((None, ts, GLA_QK), row),
            pl.BlockSpec((None, ts, GLA_QK), row),
            pl.BlockSpec((None, ts, GLA_WIDTH), row),
            pl.BlockSpec((None, ts, GLA_WIDTH), row),
            pl.BlockSpec((None, ts, GLA_QK), row),
            pl.BlockSpec((1, GLA_WIDTH), lambda b, i: (0, 0)),
        ],
        out_specs=pl.BlockSpec((None, ts, GLA_WIDTH), row),
        out_shape=jax.ShapeDtypeStruct((bsz, seq, GLA_WIDTH), BF16),
        scratch_shapes=[pltpu.VMEM((GLA_HEADS, GLA_DV, GLA_DK), F32)],
        compiler_params=pltpu.CompilerParams(
            dimension_semantics=("arbitrary", "arbitrary"), vmem_limit_bytes=VMEM_LIMIT),
        name="gla",
    )(qa, ka, va, za, la, g_gla)


def _dil_kernel(q_ref, k_ref, v_ref, z_ref, g_ref, y_ref, qd, kd, vd, o1, o2, o3, l1, l2, l3, *, seq):
    lb = DIL_LB
    n_blocks = seq // lb
    qi = lax.broadcasted_iota(jnp.int32, (lb, 2 * lb), 0)
    ki = lax.broadcasted_iota(jnp.int32, (lb, 2 * lb), 1)
    dist = qi + lb - ki
    band = (dist >= 0) & (dist <= lb)
    bias_any = jnp.where(band, 0.0, NEG_INF)
    bias_first = jnp.where(band & (ki >= lb), 0.0, NEG_INF)
    bias_cur = bias_first[:, lb:]

    kd[0:lb, :] = jnp.zeros((lb, DIL_HD), BF16)
    vd[0:lb, :] = jnp.zeros((lb, DIL_HD), BF16)

    for (window, dil), o_scr, l_scr in zip(DIL_PATTERNS, (o1, o2, o3), (l1, l2, l3)):
        nb = seq // window

        def nat_rows(idx, dil=dil, window=window, nb=nb):
            start = (idx % nb) * window + idx // nb
            if dil == 1:
                return pl.ds(pl.multiple_of(start, lb), lb)
            return pl.ds(start, lb, stride=dil)

        def gather(it, carry, nat_rows=nat_rows):
            for u in range(DIL_UNROLL):
                idx = it * DIL_UNROLL + u
                dst = pl.ds(pl.multiple_of(lb + idx * lb, lb), lb)
                rows = nat_rows(idx)
                qd[dst, :] = q_ref[rows, :].astype(BF16)
                kd[dst, :] = k_ref[rows, :].astype(BF16)
                vd[dst, :] = v_ref[rows, :].astype(BF16)
            return carry

        lax.fori_loop(0, n_blocks // DIL_UNROLL, gather, 0)

        def blocks(it, carry, nb=nb, nat_rows=nat_rows, o_scr=o_scr, l_scr=l_scr):
            keys, scores, probs = [], [], []
            for u in range(DIL_UNROLL):
                idx = it * DIL_UNROLL + u
                cur = pl.ds(pl.multiple_of(lb + idx * lb, lb), lb)
                if u % nb == 0 and DIL_UNROLL % nb == 0:
                    keys.append(cur)
                    scores.append(_dot_nt(qd[cur, :], kd[cur, :]) + bias_cur)
                    continue
                if u == 0 and nb > DIL_UNROLL:
                    bias = jnp.where(idx % nb == 0, bias_first, bias_any)
                else:
                    bias = bias_any
                keys.append(pl.ds(pl.multiple_of(idx * lb, lb), 2 * lb))
                scores.append(_dot_nt(qd[cur, :], kd[keys[-1], :]) + bias)
            for s in scores:
                m = jnp.max(s, axis=-1, keepdims=True)
                p = jnp.exp(s - m)
                probs.append((p.astype(BF16), m, jnp.sum(p, axis=-1, keepdims=True)))
            for u, (p, m, den) in enumerate(probs):
                rows = nat_rows(it * DIL_UNROLL + u)
                o_scr[rows, :] = _dot(p, vd[keys[u], :]) / den
                l_scr[rows, :] = jnp.broadcast_to(m + jnp.log(den), (lb, DIL_HD))
            return carry

        lax.fori_loop(0, n_blocks // DIL_UNROLL, blocks, 0)

    tr = ROW_TILE

    def combine(t, carry):
        rows = pl.ds(pl.multiple_of(t * tr, tr), tr)
        a1, a2, a3 = l1[rows, :], l2[rows, :], l3[rows, :]
        m = jnp.maximum(jnp.maximum(a1, a2), a3)
        e1, e2, e3 = jnp.exp(a1 - m), jnp.exp(a2 - m), jnp.exp(a3 - m)
        o = (e1 * o1[rows, :] + e2 * o2[rows, :] + e3 * o3[rows, :]) / (e1 + e2 + e3)
        y_ref[rows, :] = _head_norm_gate(o, g_ref[...], z_ref[rows, :]).astype(y_ref.dtype)
        return carry

    lax.fori_loop(0, seq // tr, combine, 0)


def _dilated(qb, kb, vb, zb, g_dil):
    bsz, seq, _ = qb.shape
    head = lambda b, h: (b, 0, h)
    blk = pl.BlockSpec((None, seq, DIL_HD), head)
    return pl.pallas_call(
        functools.partial(_dil_kernel, seq=seq),
        grid=(bsz, DIL_HEADS),
        in_specs=[blk, blk, blk, blk, pl.BlockSpec((1, DIL_HD), lambda b, h: (0, h))],
        out_specs=blk,
        out_shape=jax.ShapeDtypeStruct((bsz, seq, DIL_WIDTH), BF16),
        scratch_shapes=[pltpu.VMEM((seq + DIL_LB, DIL_HD), BF16) for _ in range(3)]
        + [pltpu.VMEM((seq, DIL_HD), F32) for _ in range(6)],
        compiler_params=pltpu.CompilerParams(
            dimension_semantics=("arbitrary", "arbitrary"), vmem_limit_bytes=VMEM_LIMIT),
        name="dilated",
    )(qb, kb, vb, zb, g_dil)


def _out_proj_kernel(ya_ref, yb_ref, x_ref, mod_ref, wa_ref, wb_ref, gpost_ref, o_ref):
    y = _dot(ya_ref[...], wa_ref[...]) + _dot(yb_ref[...], wb_ref[...])
    r = y * lax.rsqrt(jnp.mean(y * y, axis=-1, keepdims=True) + EPS) * gpost_ref[...]
    gate = mod_ref[:, 2 * D_MODEL:3 * D_MODEL]
    o_ref[...] = x_ref[...] + gate * r


def _out_proj(ya, yb, x, mod_l, w_out_a, w_out_b, g_post):
    bsz, seq, d = x.shape
    tm = ROW_TILE
    row = lambda b, i: (b, i, 0)
    const = lambda b, i: (0, 0)
    return pl.pallas_call(
        _out_proj_kernel,
        grid=(bsz, seq // tm),
        in_specs=[
            pl.BlockSpec((None, tm, GLA_WIDTH), row),
            pl.BlockSpec((None, tm, DIL_WIDTH), row),
            pl.BlockSpec((None, tm, d), row),
            pl.BlockSpec((None, 1, 3 * d), lambda b, i: (b, 0, 0)),
            pl.BlockSpec(w_out_a.shape, const),
            pl.BlockSpec(w_out_b.shape, const),
            pl.BlockSpec((1, d), const),
        ],
        out_specs=pl.BlockSpec((None, tm, d), row),
        out_shape=jax.ShapeDtypeStruct((bsz, seq, d), F32),
        compiler_params=pltpu.CompilerParams(
            dimension_semantics=("arbitrary", "arbitrary"), vmem_limit_bytes=VMEM_LIMIT),
        name="out_proj",
    )(ya, yb, x, mod_l, w_out_a, w_out_b, g_post)


def _rope_tables(seq):
    half = DIL_HD // 2
    inv_freq = ROPE_THETA ** (-jnp.arange(0, DIL_HD, 2, dtype=F32) / DIL_HD)
    ang = jnp.arange(seq, dtype=F32)[:, None] * inv_freq[None, :]
    cos, sin = jnp.cos(ang), jnp.sin(ang)
    del half
    return jnp.concatenate([cos, cos], axis=-1), jnp.concatenate([-sin, sin], axis=-1)


def _split_w_in(w_in_l, w_gate_up_l, b_gate_up_l):
    sizes = (GLA_QK, GLA_QK, GLA_WIDTH, GLA_WIDTH, GLA_LOWRANK,
             DIL_WIDTH, DIL_WIDTH, DIL_WIDTH, DIL_WIDTH)
    offs = np.cumsum((0,) + sizes)
    names = ("qa", "ka", "va", "za", "lr", "qb", "kb", "vb", "zb")
    w = {n: w_in_l[:, offs[i]:offs[i + 1]].astype(BF16) for i, n in enumerate(names)}
    w["lr"] = jnp.pad(w["lr"], ((0, 0), (0, LR_PAD - GLA_LOWRANK)))
    w["wg"] = jnp.pad(w_gate_up_l.astype(BF16), ((0, LR_PAD - GLA_LOWRANK), (0, 0)))
    w["bg"] = b_gate_up_l.reshape(1, GLA_QK)
    return w


def kernel(x, c, w_ada, b_ada, g_pre, w_in, w_gate_up, b_gate_up, g_gla, g_dil, w_out, g_post):
    bsz, seq, d = x.shape
    depth = w_ada.shape[0]
    mod = _modulation(c, w_ada, b_ada).reshape(depth, bsz, 1, 3 * d)
    cos, sin = _rope_tables(seq)
    for l in range(depth):
        w = _split_w_in(w_in[l], w_gate_up[l], b_gate_up[l])
        qa, ka, va, za, la, qb, kb, vb, zb = _in_proj(x, mod[l], g_pre[l].reshape(1, d), w, cos, sin)
        ya = _gla(qa, ka, va, za, la, g_gla[l].reshape(1, GLA_WIDTH))
        yb = _dilated(qb, kb, vb, zb, g_dil[l].reshape(1, DIL_WIDTH))
        w_o = w_out[l].astype(BF16)
        x = _out_proj(ya, yb, x, mod[l], w_o[:GLA_WIDTH], w_o[GLA_WIDTH:], g_post[l].reshape(1, d))
    return x
```

```python
import functools

import jax
import jax.numpy as jnp
import numpy as np
from jax import lax
from jax.experimental import pallas as pl
from jax.experimental.pallas import tpu as pltpu

F32 = jnp.float32
BF16 = jnp.bfloat16

D_MODEL = 1024
GLA_HEADS = 4
GLA_DK = 64
GLA_DV = 128
GLA_QK = GLA_HEADS * GLA_DK
GLA_WIDTH = GLA_HEADS * GLA_DV
GLA_LOWRANK = 16
GLA_TAU = 16.0
GLA_CHUNK = 64
GLA_GROUP = 4
DIL_HEADS = 4
DIL_HD = 128
DIL_WIDTH = DIL_HEADS * DIL_HD
DIL_PATTERNS = ((128, 1), (512, 4), (2048, 16))
DIL_LB = 128
PERM_TILE, PERM_D = DIL_PATTERNS[1]
ROPE_THETA = 10000.0
EPS = 1e-6
LANES = 128
LR_PAD = LANES

VMEM_LIMIT = 56 * 1024 * 1024
DIL_UNROLL = 8
ROW_TILE = 512
NEG_INF = float("-inf")


def _silu(v):
    return v * jax.nn.sigmoid(v)


def _dot(a, b):
    return jnp.dot(a, b, preferred_element_type=F32)


def _dot_nt(a, b):
    return lax.dot_general(a, b, (((1,), (1,)), ((), ())), preferred_element_type=F32)


def _dot_tn(a, b):
    return lax.dot_general(a, b, (((0,), (0,)), ((), ())), preferred_element_type=F32)


def _mod_kernel(c_ref, w_ref, b_ref, o_ref):
    sc = _silu(c_ref[...]).astype(BF16)
    o_ref[...] = _dot(sc, w_ref[...].astype(BF16)) + b_ref[...]


def _modulation(c, w_ada, b_ada):
    depth, d, e = w_ada.shape
    bsz = c.shape[0]
    nt = e // d
    return pl.pallas_call(
        _mod_kernel,
        grid=(depth, nt),
        in_specs=[
            pl.BlockSpec((bsz, d), lambda l, j: (0, 0)),
            pl.BlockSpec((None, d, d), lambda l, j: (l, 0, j)),
            pl.BlockSpec((None, 1, d), lambda l, j: (l, 0, j)),
        ],
        out_specs=pl.BlockSpec((None, bsz, d), lambda l, j: (l, 0, j)),
        out_shape=jax.ShapeDtypeStruct((depth, bsz, e), F32),
        compiler_params=pltpu.CompilerParams(
            dimension_semantics=("arbitrary", "arbitrary"), vmem_limit_bytes=VMEM_LIMIT),
        name="adaln_mod",
    )(c, w_ada, b_ada.reshape(depth, 1, e))


def _rope(v, cos, sin_signed):
    return v * cos + pltpu.roll(v, DIL_HD // 2, axis=1) * sin_signed


def _in_proj_kernel(x_ref, mod_ref, gpre_ref, wqa_ref, wka_ref, wva_ref, wza_ref, wlr_ref,
                    wqb_ref, wkb_ref, wvb_ref, wzb_ref, wg_ref, bg_ref, cos_ref, sin_ref,
                    qa_ref, ka_ref, va_ref, za_ref, la_ref, qb_ref, kb_ref, vb_ref, zb_ref, perm_ref):
    x = x_ref[...]
    shift = mod_ref[:, 0:D_MODEL]
    scale = mod_ref[:, D_MODEL:2 * D_MODEL]
    r = x * lax.rsqrt(jnp.mean(x * x, axis=-1, keepdims=True) + EPS) * gpre_ref[...]
    h = (r * (1.0 + scale) + shift).astype(BF16)

    qa_ref[...] = (_dot(h, wqa_ref[...]) * (GLA_DK ** -0.5)).astype(qa_ref.dtype)
    ka_ref[...] = _dot(h, wka_ref[...]).astype(ka_ref.dtype)
    va_ref[...] = _dot(h, wva_ref[...]).astype(va_ref.dtype)
    za_ref[...] = _dot(h, wza_ref[...]).astype(za_ref.dtype)

    lr = _dot(h, wlr_ref[...]).astype(BF16)
    z = _dot(lr, wg_ref[...]) + bg_ref[...]
    log_sig = jnp.minimum(z, 0.0) - jnp.log1p(jnp.exp(-jnp.abs(z)))
    la_ref[...] = log_sig / GLA_TAU

    def store_residue_order(out_ref, cols, slab, val):
        perm_ref[slab] = val
        n = PERM_TILE // PERM_D
        for r in range(PERM_D):
            out_ref[r * n:(r + 1) * n, cols] = perm_ref[slab, pl.ds(r, n, stride=PERM_D), :].astype(out_ref.dtype)

    cos = cos_ref[...]
    sin = sin_ref[...]
    q = _dot(h, wqb_ref[...])
    k = _dot(h, wkb_ref[...])
    v = _dot(h, wvb_ref[...])
    zg = _dot(h, wzb_ref[...])
    for hd in range(DIL_HEADS):
        cols = slice(hd * DIL_HD, (hd + 1) * DIL_HD)
        store_residue_order(qb_ref, cols, 4 * hd, _rope(q[:, cols], cos, sin) * (DIL_HD ** -0.5))
        store_residue_order(kb_ref, cols, 4 * hd + 1, _rope(k[:, cols], cos, sin))
        store_residue_order(vb_ref, cols, 4 * hd + 2, v[:, cols])
        store_residue_order(zb_ref, cols, 4 * hd + 3, zg[:, cols])


def _in_proj(x, mod_l, g_pre, w, cos, sin):
    bsz, seq, d = x.shape
    tm = PERM_TILE
    row = lambda b, i: (b, i, 0)
    const = lambda b, i: (0, 0)

    def full(a):
        return pl.BlockSpec(a.shape, const)

    out_cols = (("qa", GLA_QK, BF16), ("ka", GLA_QK, BF16), ("va", GLA_WIDTH, BF16),
                ("za", GLA_WIDTH, BF16), ("la", GLA_QK, F32), ("qb", DIL_WIDTH, F32),
                ("kb", DIL_WIDTH, F32), ("vb", DIL_WIDTH, F32), ("zb", DIL_WIDTH, BF16))
    weights = [w[k] for k in ("qa", "ka", "va", "za", "lr", "qb", "kb", "vb", "zb", "wg", "bg")]
    return pl.pallas_call(
        _in_proj_kernel,
        grid=(bsz, seq // tm),
        in_specs=[
            pl.BlockSpec((None, tm, d), row),
            pl.BlockSpec((None, 1, 3 * d), lambda b, i: (b, 0, 0)),
            full(g_pre),
            *[full(a) for a in weights],
            pl.BlockSpec((tm, DIL_HD), lambda b, i: (i, 0)),
            pl.BlockSpec((tm, DIL_HD), lambda b, i: (i, 0)),
        ],
        out_specs=[pl.BlockSpec((None, tm, n), row) for _, n, _ in out_cols],
        out_shape=[jax.ShapeDtypeStruct((bsz, seq, n), dt) for _, n, dt in out_cols],
        scratch_shapes=[pltpu.VMEM((4 * DIL_HEADS, tm, DIL_HD), F32)],
        compiler_params=pltpu.CompilerParams(
            dimension_semantics=("arbitrary", "arbitrary"), vmem_limit_bytes=VMEM_LIMIT),
        name="in_proj",
    )(x, mod_l, g_pre, *weights, cos, sin)


def _head_norm_gate(o, g, z):
    r = o * lax.rsqrt(jnp.mean(o * o, axis=-1, keepdims=True) + EPS)
    return r * g * _silu(z.astype(F32))


def _split2(v):
    h1 = v.astype(BF16)
    h2 = (v - h1.astype(F32)).astype(BF16)
    return h1, h2


def _gla_kernel(qa_ref, ka_ref, va_ref, za_ref, la_ref, g_ref, ya_ref, st_ref, *, chunks):
    @pl.when(pl.program_id(1) == 0)
    def _():
        st_ref[...] = jnp.zeros_like(st_ref)

    c_len = GLA_CHUNK
    grp = GLA_GROUP * c_len
    ri = lax.broadcasted_iota(jnp.int32, (grp, grp), 0)
    ci = lax.broadcasted_iota(jnp.int32, (grp, grp), 1)
    same_chunk = (ri & -c_len) == (ci & -c_len)
    tril_bd = ((ri >= ci) & same_chunk).astype(BF16)
    causal = (lax.broadcasted_iota(jnp.int32, (c_len, c_len), 0)
              >= lax.broadcasted_iota(jnp.int32, (c_len, c_len), 1))
    heads = range(GLA_HEADS)
    kcol = [slice(hd * GLA_DK, (hd + 1) * GLA_DK) for hd in heads]
    vcol = [slice(hd * GLA_DV, (hd + 1) * GLA_DV) for hd in heads]

    def prep(r0):
        rows = pl.ds(r0, grp)
        h1, h2 = _split2(la_ref[rows, :])
        b = _dot(tril_bd, h1) + _dot(tril_bd, h2)
        decay = [jnp.exp(b[(c + 1) * c_len - 1:(c + 1) * c_len, :]) for c in range(GLA_GROUP)]
        decay_rows = jnp.concatenate([jnp.broadcast_to(d, (c_len, GLA_QK)) for d in decay], axis=0)
        k_e32 = ka_ref[rows, :].astype(F32) * jnp.exp(-b)
        q_e = (qa_ref[rows, :].astype(F32) * jnp.exp(b)).astype(BF16)
        k_end = (k_e32 * decay_rows).astype(BF16)
        v = [[va_ref[pl.ds(r0 + c * c_len, c_len), vcol[hd]] for hd in heads] for c in range(GLA_GROUP)]
        return dict(r0=r0, q_e=q_e, k_e=k_e32.astype(BF16), k_end=k_end, decay=decay, v=v)

    crow = [slice(c * c_len, (c + 1) * c_len) for c in range(GLA_GROUP)]

    def intra(p):
        q_e, k_e, k_end, v = p["q_e"], p["k_e"], p["k_end"], p["v"]
        a = [[jnp.where(causal, _dot_nt(q_e[crow[c], kcol[hd]], k_e[crow[c], kcol[hd]]), 0.0).astype(BF16)
              for hd in heads] for c in range(GLA_GROUP)]
        p["inc"] = [[_dot_tn(v[c][hd], k_end[crow[c], kcol[hd]]) for hd in heads] for c in range(GLA_GROUP)]
        p["o"] = [[_dot(a[c][hd], v[c][hd]) for hd in heads] for c in range(GLA_GROUP)]

    def inter(p, st):
        for hd in heads:
            for c in range(GLA_GROUP):
                p["o"][c][hd] = p["o"][c][hd] + _dot_nt(p["q_e"][crow[c], kcol[hd]], st[hd].astype(BF16))
                st[hd] = st[hd] * p["decay"][c][:, kcol[hd]] + p["inc"][c][hd]

    def epilogue(p):
        for c in range(GLA_GROUP):
            for hd in heads:
                out_rows = pl.ds(p["r0"] + c * c_len, c_len)
                ya_ref[out_rows, vcol[hd]] = _head_norm_gate(
                    p["o"][c][hd], g_ref[:, vcol[hd]], za_ref[out_rows, vcol[hd]]).astype(ya_ref.dtype)

    st = [st_ref[hd] for hd in heads]
    groups = [prep(g * grp) for g in range(chunks // GLA_GROUP)]
    intra(groups[0])
    for g, p in enumerate(groups):
        inter(p, st)
        if g + 1 < len(groups):
            intra(groups[g + 1])
        epilogue(p)
    for hd in heads:
        st_ref[hd] = st[hd]


def _gla(qa, ka, va, za, la, g_gla):
    bsz, seq, _ = qa.shape
    ts = ROW_TILE
    row = lambda b, i: (b, i, 0)
    return pl.pallas_call(
        functools.partial(_gla_kernel, chunks=ts // GLA_CHUNK),
        grid=(bsz, seq // ts),
        in_specs=[
            pl.BlockSpec((None, ts, GLA_QK), row),
            pl.BlockSpec((None, ts, GLA_QK), row),
            pl.BlockSpec((None, ts, GLA_WIDTH), row),
            pl.BlockSpec((None, ts, GLA_WIDTH), row),
            pl.BlockSpec((None, ts, GLA_QK), row),
            pl.BlockSpec((1, GLA_WIDTH), lambda b, i: (0, 0)),
        ],
        out_specs=pl.BlockSpec((None, ts, GLA_WIDTH), row),
        out_shape=jax.ShapeDtypeStruct((bsz, seq, GLA_WIDTH), BF16),
        scratch_shapes=[pltpu.VMEM((GLA_HEADS, GLA_DV, GLA_DK), F32)],
        compiler_params=pltpu.CompilerParams(
            dimension_semantics=("arbitrary", "arbitrary"), vmem_limit_bytes=VMEM_LIMIT),
        name="gla",
    )(qa, ka, va, za, la, g_gla)


def _dil_block_chunks(pattern, idx, seq):
    window, dil = DIL_PATTERNS[pattern]
    nb = seq // window
    lb = DIL_LB
    n_r = PERM_TILE // PERM_D
    if dil == PERM_D:
        return [(pl.multiple_of((idx % nb) * PERM_TILE + (idx // nb) * n_r, lb), 1)]
    if dil == 1:
        tile, part = idx // (PERM_TILE // lb), idx % (PERM_TILE // lb)
        rows = lb // PERM_D
        return [(pl.multiple_of(tile * PERM_TILE + r * n_r + part * rows, rows), 1) for r in range(PERM_D)]
    sub = dil // PERM_D
    res, n = idx // nb, idx % nb
    r4, c = res % PERM_D, res // PERM_D
    tiles = window // PERM_TILE
    return [(n * window + t * PERM_TILE + r4 * n_r + c, sub) for t in range(tiles)]


def _dil_kernel(q_ref, k_ref, v_ref, z_ref, g_ref, y_ref, qd, kd, vd, ynat, o1, o2, o3, l1, l2, l3, *, seq):
    lb = DIL_LB
    n_blocks = seq // lb
    qi = lax.broadcasted_iota(jnp.int32, (lb, 2 * lb), 0)
    ki = lax.broadcasted_iota(jnp.int32, (lb, 2 * lb), 1)

    def biases(pos_in_block):
        dist = pos_in_block(qi) + lb - (pos_in_block(ki & (lb - 1)) + (ki & lb))
        band = (dist >= 0) & (dist <= lb)
        b_any = jnp.where(band, 0.0, NEG_INF)
        b_first = jnp.where(band & (ki >= lb), 0.0, NEG_INF)
        return b_any, b_first, b_first[:, lb:]

    step_order = biases(lambda a: a)
    rows_p1 = lb // PERM_D
    p1_order = biases(lambda a: PERM_D * (a % rows_p1) + a // rows_p1)

    kd[0:lb, :] = jnp.zeros((lb, DIL_HD), BF16)
    vd[0:lb, :] = jnp.zeros((lb, DIL_HD), BF16)

    for pat, ((window, dil), o_scr, l_scr) in enumerate(zip(DIL_PATTERNS, (o1, o2, o3), (l1, l2, l3))):
        nb = seq // window
        bias_any, bias_first, bias_cur = p1_order if dil == 1 else step_order

        def load_block(ref, idx, pat=pat):
            chunks = _dil_block_chunks(pat, idx, seq)
            rows = lb // len(chunks)
            parts = [ref[pl.ds(s0, rows) if st == 1 else pl.ds(s0, rows, stride=st), :] for s0, st in chunks]
            return parts[0] if len(parts) == 1 else jnp.concatenate(parts, axis=0)

        def store_block(ref, idx, val, pat=pat):
            chunks = _dil_block_chunks(pat, idx, seq)
            rows = lb // len(chunks)
            for i, (s0, st) in enumerate(chunks):
                dst = pl.ds(s0, rows) if st == 1 else pl.ds(s0, rows, stride=st)
                ref[dst, :] = val[i * rows:(i + 1) * rows, :]

        def gather(it, carry, load_block=load_block):
            for u in range(DIL_UNROLL):
                idx = it * DIL_UNROLL + u
                dst = pl.ds(pl.multiple_of(lb + idx * lb, lb), lb)
                qd[dst, :] = load_block(q_ref, idx).astype(BF16)
                kd[dst, :] = load_block(k_ref, idx).astype(BF16)
                vd[dst, :] = load_block(v_ref, idx).astype(BF16)
            return carry

        lax.fori_loop(0, n_blocks // DIL_UNROLL, gather, 0)

        def blocks(it, carry, nb=nb, store_block=store_block, o_scr=o_scr, l_scr=l_scr,
                   bias_any=bias_any, bias_first=bias_first, bias_cur=bias_cur):
            keys, scores, probs = [], [], []
            for u in range(DIL_UNROLL):
                idx = it * DIL_UNROLL + u
                cur = pl.ds(pl.multiple_of(lb + idx * lb, lb), lb)
                if u % nb == 0 and DIL_UNROLL % nb == 0:
                    keys.append(cur)
                    scores.append(_dot_nt(qd[cur, :], kd[cur, :]) + bias_cur)
                    continue
                if u == 0 and nb > DIL_UNROLL:
                    bias = jnp.where(idx % nb == 0, bias_first, bias_any)
                else:
                    bias = bias_any
                keys.append(pl.ds(pl.multiple_of(idx * lb, lb), 2 * lb))
                scores.append(_dot_nt(qd[cur, :], kd[keys[-1], :]) + bias)
            for s in scores:
                m = jnp.max(s, axis=-1, keepdims=True)
                p = jnp.exp(s - m)
                probs.append((p.astype(BF16), m, jnp.sum(p, axis=-1, keepdims=True)))
            for u, (p, m, den) in enumerate(probs):
                idx = it * DIL_UNROLL + u
                store_block(o_scr, idx, _dot(p, vd[keys[u], :]) / den)
                store_block(l_scr, idx, jnp.broadcast_to(m + jnp.log(den), (lb, DIL_HD)))
            return carry

        lax.fori_loop(0, n_blocks // DIL_UNROLL, blocks, 0)

    n_r = PERM_TILE // PERM_D

    def combine(t, carry):
        t0 = pl.multiple_of(t * PERM_TILE, PERM_TILE)
        rows = pl.ds(t0, PERM_TILE)
        a1, a2, a3 = l1[rows, :], l2[rows, :], l3[rows, :]
        m = jnp.maximum(jnp.maximum(a1, a2), a3)
        e1, e2, e3 = jnp.exp(a1 - m), jnp.exp(a2 - m), jnp.exp(a3 - m)
        o = (e1 * o1[rows, :] + e2 * o2[rows, :] + e3 * o3[rows, :]) / (e1 + e2 + e3)
        y = _head_norm_gate(o, g_ref[...], z_ref[rows, :])
        for r in range(PERM_D):
            ynat[pl.ds(t0 + r, n_r, stride=PERM_D), :] = y[r * n_r:(r + 1) * n_r, :]
        y_ref[rows, :] = ynat[rows, :].astype(y_ref.dtype)
        return carry

    lax.fori_loop(0, seq // PERM_TILE, combine, 0)


def _dilated(qb, kb, vb, zb, g_dil):
    bsz, seq, _ = qb.shape
    head = lambda b, h: (b, 0, h)
    blk = pl.BlockSpec((None, seq, DIL_HD), head)
    return pl.pallas_call(
        functools.partial(_dil_kernel, seq=seq),
        grid=(bsz, DIL_HEADS),
        in_specs=[blk, blk, blk, blk, pl.BlockSpec((1, DIL_HD), lambda b, h: (0, h))],
        out_specs=blk,
        out_shape=jax.ShapeDtypeStruct((bsz, seq, DIL_WIDTH), BF16),
        scratch_shapes=[pltpu.VMEM((seq + DIL_LB, DIL_HD), BF16) for _ in range(3)]
        + [pltpu.VMEM((seq, DIL_HD), F32) for _ in range(7)],
        compiler_params=pltpu.CompilerParams(
            dimension_semantics=("arbitrary", "arbitrary"), vmem_limit_bytes=VMEM_LIMIT),
        name="dilated",
    )(qb, kb, vb, zb, g_dil)


def _out_proj_kernel(ya_ref, yb_ref, x_ref, mod_ref, wa_ref, wb_ref, gpost_ref, o_ref):
    y = _dot(ya_ref[...], wa_ref[...]) + _dot(yb_ref[...], wb_ref[...])
    r = y * lax.rsqrt(jnp.mean(y * y, axis=-1, keepdims=True) + EPS) * gpost_ref[...]
    gate = mod_ref[:, 2 * D_MODEL:3 * D_MODEL]
    o_ref[...] = x_ref[...] + gate * r


def _out_proj(ya, yb, x, mod_l, w_out_a, w_out_b, g_post):
    bsz, seq, d = x.shape
    tm = ROW_TILE
    row = lambda b, i: (b, i, 0)
    const = lambda b, i: (0, 0)
    return pl.pallas_call(
        _out_proj_kernel,
        grid=(bsz, seq // tm),
        in_specs=[
            pl.BlockSpec((None, tm, GLA_WIDTH), row),
            pl.BlockSpec((None, tm, DIL_WIDTH), row),
            pl.BlockSpec((None, tm, d), row),
            pl.BlockSpec((None, 1, 3 * d), lambda b, i: (b, 0, 0)),
            pl.BlockSpec(w_out_a.shape, const),
            pl.BlockSpec(w_out_b.shape, const),
            pl.BlockSpec((1, d), const),
        ],
        out_specs=pl.BlockSpec((None, tm, d), row),
        out_shape=jax.ShapeDtypeStruct((bsz, seq, d), F32),
        compiler_params=pltpu.CompilerParams(
            dimension_semantics=("arbitrary", "arbitrary"), vmem_limit_bytes=VMEM_LIMIT),
        name="out_proj",
    )(ya, yb, x, mod_l, w_out_a, w_out_b, g_post)


def _rope_tables(seq):
    half = DIL_HD // 2
    inv_freq = ROPE_THETA ** (-jnp.arange(0, DIL_HD, 2, dtype=F32) / DIL_HD)
    ang = jnp.arange(seq, dtype=F32)[:, None] * inv_freq[None, :]
    cos, sin = jnp.cos(ang), jnp.sin(ang)
    del half
    return jnp.concatenate([cos, cos], axis=-1), jnp.concatenate([-sin, sin], axis=-1)


def _split_w_in(w_in_l, w_gate_up_l, b_gate_up_l):
    sizes = (GLA_QK, GLA_QK, GLA_WIDTH, GLA_WIDTH, GLA_LOWRANK,
             DIL_WIDTH, DIL_WIDTH, DIL_WIDTH, DIL_WIDTH)
    offs = np.cumsum((0,) + sizes)
    names = ("qa", "ka", "va", "za", "lr", "qb", "kb", "vb", "zb")
    w = {n: w_in_l[:, offs[i]:offs[i + 1]].astype(BF16) for i, n in enumerate(names)}
    w["lr"] = jnp.pad(w["lr"], ((0, 0), (0, LR_PAD - GLA_LOWRANK)))
    w["wg"] = jnp.pad(w_gate_up_l.astype(BF16), ((0, LR_PAD - GLA_LOWRANK), (0, 0)))
    w["bg"] = b_gate_up_l.reshape(1, GLA_QK)
    return w


def kernel(x, c, w_ada, b_ada, g_pre, w_in, w_gate_up, b_gate_up, g_gla, g_dil, w_out, g_post):
    bsz, seq, d = x.shape
    depth = w_ada.shape[0]
    mod = _modulation(c, w_ada, b_ada).reshape(depth, bsz, 1, 3 * d)
    cos, sin = _rope_tables(seq)
    for l in range(depth):
        w = _split_w_in(w_in[l], w_gate_up[l], b_gate_up[l])
        qa, ka, va, za, la, qb, kb, vb, zb = _in_proj(x, mod[l], g_pre[l].reshape(1, d), w, cos, sin)
        ya = _gla(qa, ka, va, za, la, g_gla[l].reshape(1, GLA_WIDTH))
        yb = _dilated(qb, kb, vb, zb, g_dil[l].reshape(1, DIL_WIDTH))
        w_o = w_out[l].astype(BF16)
        x = _out_proj(ya, yb, x, mod[l], w_o[:GLA_WIDTH], w_o[GLA_WIDTH:], g_post[l].reshape(1, d))
    return x
```

```python
import functools

import jax
import jax.numpy as jnp
import numpy as np
from jax import lax
from jax.experimental import pallas as pl
from jax.experimental.pallas import tpu as pltpu

F32 = jnp.float32
BF16 = jnp.bfloat16

D_MODEL = 1024
GLA_HEADS = 4
GLA_DK = 64
GLA_DV = 128
GLA_QK = GLA_HEADS * GLA_DK
GLA_WIDTH = GLA_HEADS * GLA_DV
GLA_LOWRANK = 16
GLA_TAU = 16.0
GLA_CHUNK = 64
GLA_GROUP = 4
DIL_HEADS = 4
DIL_HD = 128
DIL_WIDTH = DIL_HEADS * DIL_HD
DIL_PATTERNS = ((128, 1), (512, 4), (2048, 16))
DIL_LB = 128
PERM_TILE, PERM_D = DIL_PATTERNS[1]
ROPE_THETA = 10000.0
EPS = 1e-6
LANES = 128
LR_PAD = LANES

VMEM_LIMIT = 56 * 1024 * 1024
DIL_UNROLL = 8
ROW_TILE = 512
NEG_INF = float("-inf")


def _silu(v):
    return v * jax.nn.sigmoid(v)


def _dot(a, b):
    return jnp.dot(a, b, preferred_element_type=F32)


def _dot_nt(a, b):
    return lax.dot_general(a, b, (((1,), (1,)), ((), ())), preferred_element_type=F32)


def _dot_tn(a, b):
    return lax.dot_general(a, b, (((0,), (0,)), ((), ())), preferred_element_type=F32)


def _mod_kernel(c_ref, w_ref, b_ref, o_ref):
    sc = _silu(c_ref[...]).astype(BF16)
    o_ref[...] = _dot(sc, w_ref[...].astype(BF16)) + b_ref[...]


def _modulation(c, w_ada, b_ada):
    depth, d, e = w_ada.shape
    bsz = c.shape[0]
    nt = e // d
    return pl.pallas_call(
        _mod_kernel,
        grid=(depth, nt),
        in_specs=[
            pl.BlockSpec((bsz, d), lambda l, j: (0, 0)),
            pl.BlockSpec((None, d, d), lambda l, j: (l, 0, j)),
            pl.BlockSpec((None, 1, d), lambda l, j: (l, 0, j)),
        ],
        out_specs=pl.BlockSpec((None, bsz, d), lambda l, j: (l, 0, j)),
        out_shape=jax.ShapeDtypeStruct((depth, bsz, e), F32),
        compiler_params=pltpu.CompilerParams(
            dimension_semantics=("arbitrary", "arbitrary"), vmem_limit_bytes=VMEM_LIMIT),
        name="adaln_mod",
    )(c, w_ada, b_ada.reshape(depth, 1, e))


def _rope(v, cos, sin_signed):
    return v * cos + pltpu.roll(v, DIL_HD // 2, axis=1) * sin_signed


def _in_proj_kernel(x_ref, mod_ref, gpre_ref, wqa_ref, wka_ref, wva_ref, wza_ref, wlr_ref,
                    wqb_ref, wkb_ref, wvb_ref, wzb_ref, wg_ref, bg_ref, cos_ref, sin_ref,
                    qa_ref, ka_ref, va_ref, za_ref, la_ref, qb_ref, kb_ref, vb_ref, zb_ref, perm_ref):
    x = x_ref[...]
    shift = mod_ref[:, 0:D_MODEL]
    scale = mod_ref[:, D_MODEL:2 * D_MODEL]
    r = x * lax.rsqrt(jnp.mean(x * x, axis=-1, keepdims=True) + EPS) * gpre_ref[...]
    h = (r * (1.0 + scale) + shift).astype(BF16)

    qa_ref[...] = (_dot(h, wqa_ref[...]) * (GLA_DK ** -0.5)).astype(qa_ref.dtype)
    ka_ref[...] = _dot(h, wka_ref[...]).astype(ka_ref.dtype)
    va_ref[...] = _dot(h, wva_ref[...]).astype(va_ref.dtype)
    za_ref[...] = _dot(h, wza_ref[...]).astype(za_ref.dtype)

    lr = _dot(h, wlr_ref[...]).astype(BF16)
    z = _dot(lr, wg_ref[...]) + bg_ref[...]
    log_sig = jnp.minimum(z, 0.0) - jnp.log1p(jnp.exp(-jnp.abs(z)))
    la_ref[...] = log_sig / GLA_TAU

    def store_residue_order(out_ref, cols, slab, val):
        perm_ref[slab] = val
        n = PERM_TILE // PERM_D
        for r in range(PERM_D):
            out_ref[r * n:(r + 1) * n, cols] = perm_ref[slab, pl.ds(r, n, stride=PERM_D), :].astype(out_ref.dtype)

    cos = cos_ref[...]
    sin = sin_ref[...]
    q = _dot(h, wqb_ref[...])
    k = _dot(h, wkb_ref[...])
    v = _dot(h, wvb_ref[...])
    zg = _dot(h, wzb_ref[...])
    for hd in range(DIL_HEADS):
        cols = slice(hd * DIL_HD, (hd + 1) * DIL_HD)
        store_residue_order(qb_ref, cols, 4 * hd, _rope(q[:, cols], cos, sin) * (DIL_HD ** -0.5))
        store_residue_order(kb_ref, cols, 4 * hd + 1, _rope(k[:, cols], cos, sin))
        store_residue_order(vb_ref, cols, 4 * hd + 2, v[:, cols])
        store_residue_order(zb_ref, cols, 4 * hd + 3, zg[:, cols])


def _in_proj(x, mod_l, g_pre, w, cos, sin):
    bsz, seq, d = x.shape
    tm = PERM_TILE
    row = lambda b, i: (b, i, 0)
    const = lambda b, i: (0, 0)

    def full(a):
        return pl.BlockSpec(a.shape, const)

    out_cols = (("qa", GLA_QK, BF16), ("ka", GLA_QK, BF16), ("va", GLA_WIDTH, BF16),
                ("za", GLA_WIDTH, BF16), ("la", GLA_QK, F32), ("qb", DIL_WIDTH, F32),
                ("kb", DIL_WIDTH, F32), ("vb", DIL_WIDTH, F32), ("zb", DIL_WIDTH, BF16))
    weights = [w[k] for k in ("qa", "ka", "va", "za", "lr", "qb", "kb", "vb", "zb", "wg", "bg")]
    return pl.pallas_call(
        _in_proj_kernel,
        grid=(bsz, seq // tm),
        in_specs=[
            pl.BlockSpec((None, tm, d), row),
            pl.BlockSpec((None, 1, 3 * d), lambda b, i: (b, 0, 0)),
            full(g_pre),
            *[full(a) for a in weights],
            pl.BlockSpec((tm, DIL_HD), lambda b, i: (i, 0)),
            pl.BlockSpec((tm, DIL_HD), lambda b, i: (i, 0)),
        ],
        out_specs=[pl.BlockSpec((None, tm, n), row) for _, n, _ in out_cols],
        out_shape=[jax.ShapeDtypeStruct((bsz, seq, n), dt) for _, n, dt in out_cols],
        scratch_shapes=[pltpu.VMEM((4 * DIL_HEADS, tm, DIL_HD), F32)],
        compiler_params=pltpu.CompilerParams(
            dimension_semantics=("arbitrary", "arbitrary"), vmem_limit_bytes=VMEM_LIMIT),
        name="in_proj",
    )(x, mod_l, g_pre, *weights, cos, sin)


def _head_norm_gate(o, g, z):
    r = o * lax.rsqrt(jnp.mean(o * o, axis=-1, keepdims=True) + EPS)
    return r * g * _silu(z.astype(F32))


def _split2(v):
    h1 = v.astype(BF16)
    h2 = (v - h1.astype(F32)).astype(BF16)
    return h1, h2


def _gla_kernel(qa_ref, ka_ref, va_ref, za_ref, la_ref, g_ref, ya_ref, st_ref, *, chunks):
    @pl.when(pl.program_id(1) == 0)
    def _():
        st_ref[...] = jnp.zeros_like(st_ref)

    c_len = GLA_CHUNK
    grp = GLA_GROUP * c_len
    ri = lax.broadcasted_iota(jnp.int32, (grp, grp), 0)
    ci = lax.broadcasted_iota(jnp.int32, (grp, grp), 1)
    same_chunk = (ri & -c_len) == (ci & -c_len)
    tril_bd = ((ri >= ci) & same_chunk).astype(BF16)
    causal = (lax.broadcasted_iota(jnp.int32, (c_len, c_len), 0)
              >= lax.broadcasted_iota(jnp.int32, (c_len, c_len), 1))
    heads = range(GLA_HEADS)
    kcol = [slice(hd * GLA_DK, (hd + 1) * GLA_DK) for hd in heads]
    vcol = [slice(hd * GLA_DV, (hd + 1) * GLA_DV) for hd in heads]

    def prep(r0):
        rows = pl.ds(r0, grp)
        h1, h2 = _split2(la_ref[rows, :])
        b = _dot(tril_bd, h1) + _dot(tril_bd, h2)
        decay = [jnp.exp(b[(c + 1) * c_len - 1:(c + 1) * c_len, :]) for c in range(GLA_GROUP)]
        decay_rows = jnp.concatenate([jnp.broadcast_to(d, (c_len, GLA_QK)) for d in decay], axis=0)
        k_e32 = ka_ref[rows, :].astype(F32) * jnp.exp(-b)
        q_e = (qa_ref[rows, :].astype(F32) * jnp.exp(b)).astype(BF16)
        k_end = (k_e32 * decay_rows).astype(BF16)
        v = [[va_ref[pl.ds(r0 + c * c_len, c_len), vcol[hd]] for hd in heads] for c in range(GLA_GROUP)]
        return dict(r0=r0, q_e=q_e, k_e=k_e32.astype(BF16), k_end=k_end, decay=decay, v=v)

    crow = [slice(c * c_len, (c + 1) * c_len) for c in range(GLA_GROUP)]

    def intra(p):
        q_e, k_e, k_end, v = p["q_e"], p["k_e"], p["k_end"], p["v"]
        a = [[jnp.where(causal, _dot_nt(q_e[crow[c], kcol[hd]], k_e[crow[c], kcol[hd]]), 0.0).astype(BF16)
              for hd in heads] for c in range(GLA_GROUP)]
        p["inc"] = [[_dot_tn(v[c][hd], k_end[crow[c], kcol[hd]]) for hd in heads] for c in range(GLA_GROUP)]
        p["o"] = [[_dot(a[c][hd], v[c][hd]) for hd in heads] for c in range(GLA_GROUP)]

    def inter(p, st):
        for hd in heads:
            for c in range(GLA_GROUP):
                p["o"][c][hd] = p["o"][c][hd] + _dot_nt(p["q_e"][crow[c], kcol[hd]], st[hd].astype(BF16))
                st[hd] = st[hd] * p["decay"][c][:, kcol[hd]] + p["inc"][c][hd]

    def epilogue(p):
        for c in range(GLA_GROUP):
            for hd in heads:
                out_rows = pl.ds(p["r0"] + c * c_len, c_len)
                ya_ref[out_rows, vcol[hd]] = _head_norm_gate(
                    p["o"][c][hd], g_ref[:, vcol[hd]], za_ref[out_rows, vcol[hd]]).astype(ya_ref.dtype)

    st = [st_ref[hd] for hd in heads]
    groups = [prep(g * grp) for g in range(chunks // GLA_GROUP)]
    intra(groups[0])
    for g, p in enumerate(groups):
        inter(p, st)
        if g + 1 < len(groups):
            intra(groups[g + 1])
        epilogue(p)
    for hd in heads:
        st_ref[hd] = st[hd]


def _gla(qa, ka, va, za, la, g_gla):
    bsz, seq, _ = qa.shape
    ts = ROW_TILE
    row = lambda b, i: (b, i, 0)
    return pl.pallas_call(
        functools.partial(_gla_kernel, chunks=ts // GLA_CHUNK),
        grid=(bsz, seq // ts),
        in_specs=[
            pl.BlockSpec((None, ts, GLA_QK), row),
            pl.BlockSpec((None, ts, GLA_QK), row),
            pl.BlockSpec((None, ts, GLA_WIDTH), row),
            pl.BlockSpec((None, ts, GLA_WIDTH), row),
            pl.BlockSpec((None, ts, GLA_QK), row),
            pl.BlockSpec((1, GLA_WIDTH), lambda b, i: (0, 0)),
        ],
        out_specs=pl.BlockSpec((None, ts, GLA_WIDTH), row),
        out_shape=jax.ShapeDtypeStruct((bsz, seq, GLA_WIDTH), BF16),
        scratch_shapes=[pltpu.VMEM((GLA_HEADS, GLA_DV, GLA_DK), F32)],
        compiler_params=pltpu.CompilerParams(
            dimension_semantics=("arbitrary", "arbitrary"), vmem_limit_bytes=VMEM_LIMIT),
        name="gla",
    )(qa, ka, va, za, la, g_gla)


def _dil_block_chunks(pattern, idx, seq):
    window, dil = DIL_PATTERNS[pattern]
    nb = seq // window
    lb = DIL_LB
    n_r = PERM_TILE // PERM_D
    if dil == PERM_D:
        return [(pl.multiple_of((idx % nb) * PERM_TILE + (idx // nb) * n_r, lb), 1)]
    if dil == 1:
        tile, part = idx // (PERM_TILE // lb), idx % (PERM_TILE // lb)
        rows = lb // PERM_D
        return [(pl.multiple_of(tile * PERM_TILE + r * n_r + part * rows, rows), 1) for r in range(PERM_D)]
    sub = dil // PERM_D
    res, n = idx // nb, idx % nb
    r4, c = res % PERM_D, res // PERM_D
    tiles = window // PERM_TILE
    return [(n * window + t * PERM_TILE + r4 * n_r + c, sub) for t in range(tiles)]


def _dil_kernel(q_ref, k_ref, v_ref, z_ref, g_ref, y_ref, qd, kd, vd, ynat,
                o1, o2, o3, m1, m2, m3, d1, d2, d3, *, seq):
    lb = DIL_LB
    n_blocks = seq // lb
    qi = lax.broadcasted_iota(jnp.int32, (lb, 2 * lb), 0)
    ki = lax.broadcasted_iota(jnp.int32, (lb, 2 * lb), 1)

    def biases(pos_in_block):
        dist = pos_in_block(qi) + lb - (pos_in_block(ki & (lb - 1)) + (ki & lb))
        band = (dist >= 0) & (dist <= lb)
        b_any = jnp.where(band, 0.0, NEG_INF)
        b_first = jnp.where(band & (ki >= lb), 0.0, NEG_INF)
        return b_any, b_first, b_first[:, lb:]

    step_order = biases(lambda a: a)
    rows_p1 = lb // PERM_D
    p1_order = biases(lambda a: PERM_D * (a % rows_p1) + a // rows_p1)

    kd[0:lb, :] = jnp.zeros((lb, DIL_HD), BF16)
    vd[0:lb, :] = jnp.zeros((lb, DIL_HD), BF16)

    for pat, ((window, dil), o_scr, m_scr, d_scr) in enumerate(
            zip(DIL_PATTERNS, (o1, o2, o3), (m1, m2, m3), (d1, d2, d3))):
        nb = seq // window
        bias_any, bias_first, bias_cur = p1_order if dil == 1 else step_order

        def load_block(ref, idx, pat=pat):
            chunks = _dil_block_chunks(pat, idx, seq)
            rows = lb // len(chunks)
            parts = [ref[pl.ds(s0, rows) if st == 1 else pl.ds(s0, rows, stride=st), :] for s0, st in chunks]
            return parts[0] if len(parts) == 1 else jnp.concatenate(parts, axis=0)

        def store_block(ref, idx, val, pat=pat):
            chunks = _dil_block_chunks(pat, idx, seq)
            rows = lb // len(chunks)
            for i, (s0, st) in enumerate(chunks):
                dst = pl.ds(s0, rows) if st == 1 else pl.ds(s0, rows, stride=st)
                ref[dst, :] = val[i * rows:(i + 1) * rows, :]

        def gather(it, carry, load_block=load_block):
            for u in range(DIL_UNROLL):
                idx = it * DIL_UNROLL + u
                dst = pl.ds(pl.multiple_of(lb + idx * lb, lb), lb)
                qd[dst, :] = load_block(q_ref, idx).astype(BF16)
                kd[dst, :] = load_block(k_ref, idx).astype(BF16)
                vd[dst, :] = load_block(v_ref, idx).astype(BF16)
            return carry

        lax.fori_loop(0, n_blocks // DIL_UNROLL, gather, 0)

        def blocks(it, carry, nb=nb, store_block=store_block, o_scr=o_scr, m_scr=m_scr, d_scr=d_scr,
                   bias_any=bias_any, bias_first=bias_first, bias_cur=bias_cur):
            keys, scores, probs = [], [], []
            for u in range(DIL_UNROLL):
                idx = it * DIL_UNROLL + u
                cur = pl.ds(pl.multiple_of(lb + idx * lb, lb), lb)
                if u % nb == 0 and DIL_UNROLL % nb == 0:
                    keys.append(cur)
                    scores.append(_dot_nt(qd[cur, :], kd[cur, :]) + bias_cur)
                    continue
                if u == 0 and nb > DIL_UNROLL:
                    bias = jnp.where(idx % nb == 0, bias_first, bias_any)
                else:
                    bias = bias_any
                keys.append(pl.ds(pl.multiple_of(idx * lb, lb), 2 * lb))
                scores.append(_dot_nt(qd[cur, :], kd[keys[-1], :]) + bias)
            for s in scores:
                m = jnp.max(s, axis=-1, keepdims=True)
                probs.append((jnp.exp((s - m).astype(BF16)), m))
            for u, (p, m) in enumerate(probs):
                idx = it * DIL_UNROLL + u
                v_ones = jnp.concatenate([vd[keys[u], :], jnp.ones((p.shape[1], DIL_HD), BF16)], axis=1)
                acc = _dot(p, v_ones)
                store_block(o_scr, idx, acc[:, :DIL_HD])
                store_block(d_scr, idx, acc[:, DIL_HD:])
                store_block(m_scr, idx, jnp.broadcast_to(m, (lb, DIL_HD)))
            return carry

        lax.fori_loop(0, n_blocks // DIL_UNROLL, blocks, 0)

    n_r = PERM_TILE // PERM_D

    def combine(t, carry):
        t0 = pl.multiple_of(t * PERM_TILE, PERM_TILE)
        rows = pl.ds(t0, PERM_TILE)
        a1, a2, a3 = m1[rows, :], m2[rows, :], m3[rows, :]
        m = jnp.maximum(jnp.maximum(a1, a2), a3)
        e1, e2, e3 = jnp.exp(a1 - m), jnp.exp(a2 - m), jnp.exp(a3 - m)
        den = e1 * d1[rows, :] + e2 * d2[rows, :] + e3 * d3[rows, :]
        o = (e1 * o1[rows, :] + e2 * o2[rows, :] + e3 * o3[rows, :]) / den
        y = _head_norm_gate(o, g_ref[...], z_ref[rows, :])
        for r in range(PERM_D):
            ynat[pl.ds(t0 + r, n_r, stride=PERM_D), :] = y[r * n_r:(r + 1) * n_r, :]
        y_ref[rows, :] = ynat[rows, :].astype(y_ref.dtype)
        return carry

    lax.fori_loop(0, seq // PERM_TILE, combine, 0)


def _dilated(qb, kb, vb, zb, g_dil):
    bsz, seq, _ = qb.shape
    head = lambda b, h: (b, 0, h)
    blk = pl.BlockSpec((None, seq, DIL_HD), head)
    return pl.pallas_call(
        functools.partial(_dil_kernel, seq=seq),
        grid=(bsz, DIL_HEADS),
        in_specs=[blk, blk, blk, blk, pl.BlockSpec((1, DIL_HD), lambda b, h: (0, h))],
        out_specs=blk,
        out_shape=jax.ShapeDtypeStruct((bsz, seq, DIL_WIDTH), BF16),
        scratch_shapes=[pltpu.VMEM((seq + DIL_LB, DIL_HD), BF16) for _ in range(3)]
        + [pltpu.VMEM((seq, DIL_HD), F32) for _ in range(10)],
        compiler_params=pltpu.CompilerParams(
            dimension_semantics=("arbitrary", "arbitrary"), vmem_limit_bytes=VMEM_LIMIT),
        name="dilated",
    )(qb, kb, vb, zb, g_dil)


def _out_proj_kernel(ya_ref, yb_ref, x_ref, mod_ref, wa_ref, wb_ref, gpost_ref, o_ref):
    y = _dot(ya_ref[...], wa_ref[...]) + _dot(yb_ref[...], wb_ref[...])
    r = y * lax.rsqrt(jnp.mean(y * y, axis=-1, keepdims=True) + EPS) * gpost_ref[...]
    gate = mod_ref[:, 2 * D_MODEL:3 * D_MODEL]
    o_ref[...] = x_ref[...] + gate * r


def _out_proj(ya, yb, x, mod_l, w_out_a, w_out_b, g_post):
    bsz, seq, d = x.shape
    tm = ROW_TILE
    row = lambda b, i: (b, i, 0)
    const = lambda b, i: (0, 0)
    return pl.pallas_call(
        _out_proj_kernel,
        grid=(bsz, seq // tm),
        in_specs=[
            pl.BlockSpec((None, tm, GLA_WIDTH), row),
            pl.BlockSpec((None, tm, DIL_WIDTH), row),
            pl.BlockSpec((None, tm, d), row),
            pl.BlockSpec((None, 1, 3 * d), lambda b, i: (b, 0, 0)),
            pl.BlockSpec(w_out_a.shape, const),
            pl.BlockSpec(w_out_b.shape, const),
            pl.BlockSpec((1, d), const),
        ],
        out_specs=pl.BlockSpec((None, tm, d), row),
        out_shape=jax.ShapeDtypeStruct((bsz, seq, d), F32),
        compiler_params=pltpu.CompilerParams(
            dimension_semantics=("arbitrary", "arbitrary"), vmem_limit_bytes=VMEM_LIMIT),
        name="out_proj",
    )(ya, yb, x, mod_l, w_out_a, w_out_b, g_post)


def _rope_tables(seq):
    half = DIL_HD // 2
    inv_freq = ROPE_THETA ** (-jnp.arange(0, DIL_HD, 2, dtype=F32) / DIL_HD)
    ang = jnp.arange(seq, dtype=F32)[:, None] * inv_freq[None, :]
    cos, sin = jnp.cos(ang), jnp.sin(ang)
    del half
    return jnp.concatenate([cos, cos], axis=-1), jnp.concatenate([-sin, sin], axis=-1)


def _split_w_in(w_in_l, w_gate_up_l, b_gate_up_l):
    sizes = (GLA_QK, GLA_QK, GLA_WIDTH, GLA_WIDTH, GLA_LOWRANK,
             DIL_WIDTH, DIL_WIDTH, DIL_WIDTH, DIL_WIDTH)
    offs = np.cumsum((0,) + sizes)
    names = ("qa", "ka", "va", "za", "lr", "qb", "kb", "vb", "zb")
    w = {n: w_in_l[:, offs[i]:offs[i + 1]].astype(BF16) for i, n in enumerate(names)}
    w["lr"] = jnp.pad(w["lr"], ((0, 0), (0, LR_PAD - GLA_LOWRANK)))
    w["wg"] = jnp.pad(w_gate_up_l.astype(BF16), ((0, LR_PAD - GLA_LOWRANK), (0, 0)))
    w["bg"] = b_gate_up_l.reshape(1, GLA_QK)
    return w


def kernel(x, c, w_ada, b_ada, g_pre, w_in, w_gate_up, b_gate_up, g_gla, g_dil, w_out, g_post):
    bsz, seq, d = x.shape
    depth = w_ada.shape[0]
    mod = _modulation(c, w_ada, b_ada).reshape(depth, bsz, 1, 3 * d)
    cos, sin = _rope_tables(seq)
    for l in range(depth):
        w = _split_w_in(w_in[l], w_gate_up[l], b_gate_up[l])
        qa, ka, va, za, la, qb, kb, vb, zb = _in_proj(x, mod[l], g_pre[l].reshape(1, d), w, cos, sin)
        ya = _gla(qa, ka, va, za, la, g_gla[l].reshape(1, GLA_WIDTH))
        yb = _dilated(qb, kb, vb, zb, g_dil[l].reshape(1, DIL_WIDTH))
        w_o = w_out[l].astype(BF16)
        x = _out_proj(ya, yb, x, mod[l], w_o[:GLA_WIDTH], w_o[GLA_WIDTH:], g_post[l].reshape(1, d))
    return x
```

```python
import functools

import jax
import jax.numpy as jnp
import numpy as np
from jax import lax
from jax.experimental import pallas as pl
from jax.experimental.pallas import tpu as pltpu

F32 = jnp.float32
BF16 = jnp.bfloat16

D_MODEL = 1024
GLA_HEADS = 4
GLA_DK = 64
GLA_DV = 128
GLA_QK = GLA_HEADS * GLA_DK
GLA_WIDTH = GLA_HEADS * GLA_DV
GLA_LOWRANK = 16
GLA_TAU = 16.0
GLA_CHUNK = 64
GLA_GROUP = 4
DIL_HEADS = 4
DIL_HD = 128
DIL_WIDTH = DIL_HEADS * DIL_HD
DIL_PATTERNS = ((128, 1), (512, 4), (2048, 16))
DIL_LB = 128
PERM_TILE, PERM_D = DIL_PATTERNS[1]
ROPE_THETA = 10000.0
EPS = 1e-6
LANES = 128
LR_PAD = LANES

VMEM_LIMIT = 56 * 1024 * 1024
DIL_UNROLL = 8
ROW_TILE = 512
NEG_INF = float("-inf")


def _silu(v):
    return v * jax.nn.sigmoid(v)


def _dot(a, b):
    return jnp.dot(a, b, preferred_element_type=F32)


def _dot_nt(a, b):
    return lax.dot_general(a, b, (((1,), (1,)), ((), ())), preferred_element_type=F32)


def _dot_tn(a, b):
    return lax.dot_general(a, b, (((0,), (0,)), ((), ())), preferred_element_type=F32)


def _mod_kernel(c_ref, w_ref, b_ref, o_ref):
    sc = _silu(c_ref[...]).astype(BF16)
    o_ref[...] = _dot(sc, w_ref[...].astype(BF16)) + b_ref[...]


def _modulation(c, w_ada, b_ada):
    depth, d, e = w_ada.shape
    bsz = c.shape[0]
    nt = e // d
    return pl.pallas_call(
        _mod_kernel,
        grid=(depth, nt),
        in_specs=[
            pl.BlockSpec((bsz, d), lambda l, j: (0, 0)),
            pl.BlockSpec((None, d, d), lambda l, j: (l, 0, j)),
            pl.BlockSpec((None, 1, d), lambda l, j: (l, 0, j)),
        ],
        out_specs=pl.BlockSpec((None, bsz, d), lambda l, j: (l, 0, j)),
        out_shape=jax.ShapeDtypeStruct((depth, bsz, e), F32),
        compiler_params=pltpu.CompilerParams(
            dimension_semantics=("arbitrary", "arbitrary"), vmem_limit_bytes=VMEM_LIMIT),
        name="adaln_mod",
    )(c, w_ada, b_ada.reshape(depth, 1, e))


def _rope(v, cos, sin_signed):
    return v * cos + pltpu.roll(v, DIL_HD // 2, axis=1) * sin_signed


def _in_proj_kernel(x_ref, mod_ref, gpre_ref, wqa_ref, wka_ref, wva_ref, wza_ref, wlr_ref,
                    wqb_ref, wkb_ref, wvb_ref, wzb_ref, wg_ref, bg_ref, cos_ref, sin_ref,
                    qa_ref, ka_ref, va_ref, za_ref, la_ref, qb_ref, kb_ref, vb_ref, zb_ref, perm_ref):
    x = x_ref[...]
    shift = mod_ref[:, 0:D_MODEL]
    scale = mod_ref[:, D_MODEL:2 * D_MODEL]
    r = x * lax.rsqrt(jnp.mean(x * x, axis=-1, keepdims=True) + EPS) * gpre_ref[...]
    h = (r * (1.0 + scale) + shift).astype(BF16)

    qa_ref[...] = (_dot(h, wqa_ref[...]) * (GLA_DK ** -0.5)).astype(qa_ref.dtype)
    ka_ref[...] = _dot(h, wka_ref[...]).astype(ka_ref.dtype)
    va_ref[...] = _dot(h, wva_ref[...]).astype(va_ref.dtype)
    za_ref[...] = _dot(h, wza_ref[...]).astype(za_ref.dtype)

    lr = _dot(h, wlr_ref[...]).astype(BF16)
    z = _dot(lr, wg_ref[...]) + bg_ref[...]
    log_sig = jnp.minimum(z, 0.0) - jnp.log1p(jnp.exp(-jnp.abs(z)))
    la_ref[...] = log_sig / GLA_TAU

    def store_residue_order(out_ref, cols, slab, val):
        perm_ref[slab] = val
        n = PERM_TILE // PERM_D
        for r in range(PERM_D):
            out_ref[r * n:(r + 1) * n, cols] = perm_ref[slab, pl.ds(r, n, stride=PERM_D), :].astype(out_ref.dtype)

    cos = cos_ref[...]
    sin = sin_ref[...]
    q = _dot(h, wqb_ref[...])
    k = _dot(h, wkb_ref[...])
    v = _dot(h, wvb_ref[...])
    zg = _dot(h, wzb_ref[...])
    for hd in range(DIL_HEADS):
        cols = slice(hd * DIL_HD, (hd + 1) * DIL_HD)
        store_residue_order(qb_ref, cols, 4 * hd, _rope(q[:, cols], cos, sin) * (DIL_HD ** -0.5))
        store_residue_order(kb_ref, cols, 4 * hd + 1, _rope(k[:, cols], cos, sin))
        store_residue_order(vb_ref, cols, 4 * hd + 2, v[:, cols])
        store_residue_order(zb_ref, cols, 4 * hd + 3, zg[:, cols])


def _in_proj(x, mod_l, g_pre, w, cos, sin):
    bsz, seq, d = x.shape
    tm = PERM_TILE
    row = lambda b, i: (b, i, 0)
    const = lambda b, i: (0, 0)

    def full(a):
        return pl.BlockSpec(a.shape, const)

    out_cols = (("qa", GLA_QK, BF16), ("ka", GLA_QK, BF16), ("va", GLA_WIDTH, BF16),
                ("za", GLA_WIDTH, BF16), ("la", GLA_QK, F32), ("qb", DIL_WIDTH, F32),
                ("kb", DIL_WIDTH, F32), ("vb", DIL_WIDTH, F32), ("zb", DIL_WIDTH, BF16))
    weights = [w[k] for k in ("qa", "ka", "va", "za", "lr", "qb", "kb", "vb", "zb", "wg", "bg")]
    return pl.pallas_call(
        _in_proj_kernel,
        grid=(bsz, seq // tm),
        in_specs=[
            pl.BlockSpec((None, tm, d), row),
            pl.BlockSpec((None, 1, 3 * d), lambda b, i: (b, 0, 0)),
            full(g_pre),
            *[full(a) for a in weights],
            pl.BlockSpec((tm, DIL_HD), lambda b, i: (i, 0)),
            pl.BlockSpec((tm, DIL_HD), lambda b, i: (i, 0)),
        ],
        out_specs=[pl.BlockSpec((None, tm, n), row) for _, n, _ in out_cols],
        out_shape=[jax.ShapeDtypeStruct((bsz, seq, n), dt) for _, n, dt in out_cols],
        scratch_shapes=[pltpu.VMEM((4 * DIL_HEADS, tm, DIL_HD), F32)],
        compiler_params=pltpu.CompilerParams(
            dimension_semantics=("arbitrary", "arbitrary"), vmem_limit_bytes=VMEM_LIMIT),
        name="in_proj",
    )(x, mod_l, g_pre, *weights, cos, sin)


def _head_norm_gate(o, g, z):
    r = o * lax.rsqrt(jnp.mean(o * o, axis=-1, keepdims=True) + EPS)
    return r * g * _silu(z.astype(F32))


def _split2(v):
    h1 = v.astype(BF16)
    h2 = (v - h1.astype(F32)).astype(BF16)
    return h1, h2


def _gla_kernel(qa_ref, ka_ref, va_ref, za_ref, la_ref, g_ref, ya_ref, st_ref, *, chunks):
    @pl.when(pl.program_id(1) == 0)
    def _():
        st_ref[...] = jnp.zeros_like(st_ref)

    c_len = GLA_CHUNK
    grp = GLA_GROUP * c_len
    ri = lax.broadcasted_iota(jnp.int32, (grp, grp), 0)
    ci = lax.broadcasted_iota(jnp.int32, (grp, grp), 1)
    same_chunk = (ri & -c_len) == (ci & -c_len)
    tril_bd = ((ri >= ci) & same_chunk).astype(BF16)
    causal = (lax.broadcasted_iota(jnp.int32, (c_len, c_len), 0)
              >= lax.broadcasted_iota(jnp.int32, (c_len, c_len), 1))
    heads = range(GLA_HEADS)
    kcol = [slice(hd * GLA_DK, (hd + 1) * GLA_DK) for hd in heads]
    vcol = [slice(hd * GLA_DV, (hd + 1) * GLA_DV) for hd in heads]

    def prep(r0):
        rows = pl.ds(r0, grp)
        h1, h2 = _split2(la_ref[rows, :])
        b = _dot(tril_bd, h1) + _dot(tril_bd, h2)
        decay = [jnp.exp(b[(c + 1) * c_len - 1:(c + 1) * c_len, :]) for c in range(GLA_GROUP)]
        decay_rows = jnp.concatenate([jnp.broadcast_to(d, (c_len, GLA_QK)) for d in decay], axis=0)
        k_e32 = ka_ref[rows, :].astype(F32) * jnp.exp(-b)
        q_e = (qa_ref[rows, :].astype(F32) * jnp.exp(b)).astype(BF16)
        k_end = (k_e32 * decay_rows).astype(BF16)
        v = [[va_ref[pl.ds(r0 + c * c_len, c_len), vcol[hd]] for hd in heads] for c in range(GLA_GROUP)]
        return dict(r0=r0, q_e=q_e, k_e=k_e32.astype(BF16), k_end=k_end, decay=decay, v=v)

    crow = [slice(c * c_len, (c + 1) * c_len) for c in range(GLA_GROUP)]

    def intra(p):
        q_e, k_e, k_end, v = p["q_e"], p["k_e"], p["k_end"], p["v"]
        a = [[jnp.where(causal, _dot_nt(q_e[crow[c], kcol[hd]], k_e[crow[c], kcol[hd]]), 0.0).astype(BF16)
              for hd in heads] for c in range(GLA_GROUP)]
        p["inc"] = [[_dot_tn(v[c][hd], k_end[crow[c], kcol[hd]]) for hd in heads] for c in range(GLA_GROUP)]
        p["o"] = [[_dot(a[c][hd], v[c][hd]) for hd in heads] for c in range(GLA_GROUP)]

    def inter(p, st):
        for hd in heads:
            for c in range(GLA_GROUP):
                p["o"][c][hd] = p["o"][c][hd] + _dot_nt(p["q_e"][crow[c], kcol[hd]], st[hd].astype(BF16))
                st[hd] = st[hd] * p["decay"][c][:, kcol[hd]] + p["inc"][c][hd]

    def epilogue(p):
        for c in range(GLA_GROUP):
            for hd in heads:
                out_rows = pl.ds(p["r0"] + c * c_len, c_len)
                ya_ref[out_rows, vcol[hd]] = _head_norm_gate(
                    p["o"][c][hd], g_ref[:, vcol[hd]], za_ref[out_rows, vcol[hd]]).astype(ya_ref.dtype)

    st = [st_ref[hd] for hd in heads]
    groups = [prep(g * grp) for g in range(chunks // GLA_GROUP)]
    intra(groups[0])
    for g, p in enumerate(groups):
        inter(p, st)
        if g + 1 < len(groups):
            intra(groups[g + 1])
        epilogue(p)
    for hd in heads:
        st_ref[hd] = st[hd]


def _gla(qa, ka, va, za, la, g_gla):
    bsz, seq, _ = qa.shape
    ts = ROW_TILE
    row = lambda b, i: (b, i, 0)
    return pl.pallas_call(
        functools.partial(_gla_kernel, chunks=ts // GLA_CHUNK),
        grid=(bsz, seq // ts),
        in_specs=[
            pl.BlockSpec((None, ts, GLA_QK), row),
            pl.BlockSpec((None, ts, GLA_QK), row),
            pl.BlockSpec((None, ts, GLA_WIDTH), row),
            pl.BlockSpec((None, ts, GLA_WIDTH), row),
            pl.BlockSpec((None, ts, GLA_QK), row),
            pl.BlockSpec((1, GLA_WIDTH), lambda b, i: (0, 0)),
        ],
        out_specs=pl.BlockSpec((None, ts, GLA_WIDTH), row),
        out_shape=jax.ShapeDtypeStruct((bsz, seq, GLA_WIDTH), BF16),
        scratch_shapes=[pltpu.VMEM((GLA_HEADS, GLA_DV, GLA_DK), F32)],
        compiler_params=pltpu.CompilerParams(
            dimension_semantics=("arbitrary", "arbitrary"), vmem_limit_bytes=VMEM_LIMIT),
        name="gla",
    )(qa, ka, va, za, la, g_gla)


def _dil_block_chunks(pattern, idx, seq):
    window, dil = DIL_PATTERNS[pattern]
    nb = seq // window
    lb = DIL_LB
    n_r = PERM_TILE // PERM_D
    if dil == PERM_D:
        return [(pl.multiple_of((idx % nb) * PERM_TILE + (idx // nb) * n_r, lb), 1)]
    if dil == 1:
        tile, part = idx // (PERM_TILE // lb), idx % (PERM_TILE // lb)
        rows = lb // PERM_D
        return [(pl.multiple_of(tile * PERM_TILE + r * n_r + part * rows, rows), 1) for r in range(PERM_D)]
    sub = dil // PERM_D
    res, n = idx // nb, idx % nb
    r4, c = res % PERM_D, res // PERM_D
    tiles = window // PERM_TILE
    return [(n * window + t * PERM_TILE + r4 * n_r + c, sub) for t in range(tiles)]


def _dil_kernel(q_ref, k_ref, v_ref, z_ref, g_ref, y_ref, qd, kd, vd, ynat,
                o1, o2, o3, m1, m2, m3, d1, d2, d3, *, seq):
    lb = DIL_LB
    n_blocks = seq // lb
    qi = lax.broadcasted_iota(jnp.int32, (lb, 2 * lb), 0)
    ki = lax.broadcasted_iota(jnp.int32, (lb, 2 * lb), 1)

    def biases(pos_in_block):
        dist = pos_in_block(qi) + lb - (pos_in_block(ki & (lb - 1)) + (ki & lb))
        band = (dist >= 0) & (dist <= lb)
        b_any = jnp.where(band, 0.0, NEG_INF)
        b_first = jnp.where(band & (ki >= lb), 0.0, NEG_INF)
        return b_any, b_first, b_first[:, lb:]

    step_order = biases(lambda a: a)
    rows_p1 = lb // PERM_D
    p1_order = biases(lambda a: PERM_D * (a % rows_p1) + a // rows_p1)

    kd[0] = jnp.zeros((DIL_HD, lb), BF16)
    vd[0:lb, :] = jnp.zeros((lb, DIL_HD), BF16)

    for pat, ((window, dil), o_scr, m_scr, d_scr) in enumerate(
            zip(DIL_PATTERNS, (o1, o2, o3), (m1, m2, m3), (d1, d2, d3))):
        nb = seq // window
        bias_any, bias_first, bias_cur = p1_order if dil == 1 else step_order

        def load_block(ref, idx, pat=pat):
            chunks = _dil_block_chunks(pat, idx, seq)
            rows = lb // len(chunks)
            parts = [ref[pl.ds(s0, rows) if st == 1 else pl.ds(s0, rows, stride=st), :] for s0, st in chunks]
            return parts[0] if len(parts) == 1 else jnp.concatenate(parts, axis=0)

        def store_block(ref, idx, val, pat=pat):
            chunks = _dil_block_chunks(pat, idx, seq)
            rows = lb // len(chunks)
            for i, (s0, st) in enumerate(chunks):
                dst = pl.ds(s0, rows) if st == 1 else pl.ds(s0, rows, stride=st)
                ref[dst, :] = val[i * rows:(i + 1) * rows, :]

        def gather(it, carry, load_block=load_block):
            for u in range(DIL_UNROLL):
                idx = it * DIL_UNROLL + u
                dst = pl.ds(pl.multiple_of(lb + idx * lb, lb), lb)
                qd[dst, :] = load_block(q_ref, idx).astype(BF16)
                kd[idx + 1] = load_block(k_ref, idx).T.astype(BF16)
                vd[dst, :] = load_block(v_ref, idx).astype(BF16)
            return carry

        lax.fori_loop(0, n_blocks // DIL_UNROLL, gather, 0)

        def blocks(it, carry, nb=nb, store_block=store_block, o_scr=o_scr, m_scr=m_scr, d_scr=d_scr,
                   bias_any=bias_any, bias_first=bias_first, bias_cur=bias_cur):
            keys, scores, probs = [], [], []
            for u in range(DIL_UNROLL):
                idx = it * DIL_UNROLL + u
                cur = pl.ds(pl.multiple_of(lb + idx * lb, lb), lb)
                if u % nb == 0 and DIL_UNROLL % nb == 0:
                    keys.append(cur)
                    scores.append(_dot(qd[cur, :], kd[idx + 1]) + bias_cur)
                    continue
                if u == 0 and nb > DIL_UNROLL:
                    bias = jnp.where(idx % nb == 0, bias_first, bias_any)
                else:
                    bias = bias_any
                keys.append(pl.ds(pl.multiple_of(idx * lb, lb), 2 * lb))
                scores.append(_dot(qd[cur, :], jnp.concatenate([kd[idx], kd[idx + 1]], axis=1)) + bias)
            for s in scores:
                m = jnp.max(s, axis=-1, keepdims=True)
                probs.append((jnp.exp((s - m).astype(BF16)), m))
            for u, (p, m) in enumerate(probs):
                idx = it * DIL_UNROLL + u
                v_ones = jnp.concatenate([vd[keys[u], :], jnp.ones((p.shape[1], DIL_HD), BF16)], axis=1)
                acc = _dot(p, v_ones)
                store_block(o_scr, idx, acc[:, :DIL_HD])
                store_block(d_scr, idx, acc[:, DIL_HD:])
                store_block(m_scr, idx, jnp.broadcast_to(m, (lb, DIL_HD)))
            return carry

        lax.fori_loop(0, n_blocks // DIL_UNROLL, blocks, 0)

    n_r = PERM_TILE // PERM_D

    def combine(t, carry):
        t0 = pl.multiple_of(t * PERM_TILE, PERM_TILE)
        rows = pl.ds(t0, PERM_TILE)
        a1, a2, a3 = m1[rows, :], m2[rows, :], m3[rows, :]
        m = jnp.maximum(jnp.maximum(a1, a2), a3)
        e1, e2, e3 = jnp.exp(a1 - m), jnp.exp(a2 - m), jnp.exp(a3 - m)
        den = e1 * d1[rows, :] + e2 * d2[rows, :] + e3 * d3[rows, :]
        o = (e1 * o1[rows, :] + e2 * o2[rows, :] + e3 * o3[rows, :]) / den
        y = _head_norm_gate(o, g_ref[...], z_ref[rows, :])
        for r in range(PERM_D):
            ynat[pl.ds(t0 + r, n_r, stride=PERM_D), :] = y[r * n_r:(r + 1) * n_r, :]
        y_ref[rows, :] = ynat[rows, :].astype(y_ref.dtype)
        return carry

    lax.fori_loop(0, seq // PERM_TILE, combine, 0)


def _dilated(qb, kb, vb, zb, g_dil):
    bsz, seq, _ = qb.shape
    head = lambda b, h: (b, 0, h)
    blk = pl.BlockSpec((None, seq, DIL_HD), head)
    return pl.pallas_call(
        functools.partial(_dil_kernel, seq=seq),
        grid=(bsz, DIL_HEADS),
        in_specs=[blk, blk, blk, blk, pl.BlockSpec((1, DIL_HD), lambda b, h: (0, h))],
        out_specs=blk,
        out_shape=jax.ShapeDtypeStruct((bsz, seq, DIL_WIDTH), BF16),
        scratch_shapes=[pltpu.VMEM((seq + DIL_LB, DIL_HD), BF16),
                        pltpu.VMEM((seq // DIL_LB + 1, DIL_HD, DIL_LB), BF16),
                        pltpu.VMEM((seq + DIL_LB, DIL_HD), BF16)]
        + [pltpu.VMEM((seq, DIL_HD), F32) for _ in range(10)],
        compiler_params=pltpu.CompilerParams(
            dimension_semantics=("arbitrary", "arbitrary"), vmem_limit_bytes=VMEM_LIMIT),
        name="dilated",
    )(qb, kb, vb, zb, g_dil)


def _out_proj_kernel(ya_ref, yb_ref, x_ref, mod_ref, wa_ref, wb_ref, gpost_ref, o_ref):
    y = _dot(ya_ref[...], wa_ref[...]) + _dot(yb_ref[...], wb_ref[...])
    r = y * lax.rsqrt(jnp.mean(y * y, axis=-1, keepdims=True) + EPS) * gpost_ref[...]
    gate = mod_ref[:, 2 * D_MODEL:3 * D_MODEL]
    o_ref[...] = x_ref[...] + gate * r


def _out_proj(ya, yb, x, mod_l, w_out_a, w_out_b, g_post):
    bsz, seq, d = x.shape
    tm = ROW_TILE
    row = lambda b, i: (b, i, 0)
    const = lambda b, i: (0, 0)
    return pl.pallas_call(
        _out_proj_kernel,
        grid=(bsz, seq // tm),
        in_specs=[
            pl.BlockSpec((None, tm, GLA_WIDTH), row),
            pl.BlockSpec((None, tm, DIL_WIDTH), row),
            pl.BlockSpec((None, tm, d), row),
            pl.BlockSpec((None, 1, 3 * d), lambda b, i: (b, 0, 0)),
            pl.BlockSpec(w_out_a.shape, const),
            pl.BlockSpec(w_out_b.shape, const),
            pl.BlockSpec((1, d), const),
        ],
        out_specs=pl.BlockSpec((None, tm, d), row),
        out_shape=jax.ShapeDtypeStruct((bsz, seq, d), F32),
        compiler_params=pltpu.CompilerParams(
            dimension_semantics=("arbitrary", "arbitrary"), vmem_limit_bytes=VMEM_LIMIT),
        name="out_proj",
    )(ya, yb, x, mod_l, w_out_a, w_out_b, g_post)


def _rope_tables(seq):
    half = DIL_HD // 2
    inv_freq = ROPE_THETA ** (-jnp.arange(0, DIL_HD, 2, dtype=F32) / DIL_HD)
    ang = jnp.arange(seq, dtype=F32)[:, None] * inv_freq[None, :]
    cos, sin = jnp.cos(ang), jnp.sin(ang)
    del half
    return jnp.concatenate([cos, cos], axis=-1), jnp.concatenate([-sin, sin], axis=-1)


def _split_w_in(w_in_l, w_gate_up_l, b_gate_up_l):
    sizes = (GLA_QK, GLA_QK, GLA_WIDTH, GLA_WIDTH, GLA_LOWRANK,
             DIL_WIDTH, DIL_WIDTH, DIL_WIDTH, DIL_WIDTH)
    offs = np.cumsum((0,) + sizes)
    names = ("qa", "ka", "va", "za", "lr", "qb", "kb", "vb", "zb")
    w = {n: w_in_l[:, offs[i]:offs[i + 1]].astype(BF16) for i, n in enumerate(names)}
    w["lr"] = jnp.pad(w["lr"], ((0, 0), (0, LR_PAD - GLA_LOWRANK)))
    w["wg"] = jnp.pad(w_gate_up_l.astype(BF16), ((0, LR_PAD - GLA_LOWRANK), (0, 0)))
    w["bg"] = b_gate_up_l.reshape(1, GLA_QK)
    return w


def kernel(x, c, w_ada, b_ada, g_pre, w_in, w_gate_up, b_gate_up, g_gla, g_dil, w_out, g_post):
    bsz, seq, d = x.shape
    depth = w_ada.shape[0]
    mod = _modulation(c, w_ada, b_ada).reshape(depth, bsz, 1, 3 * d)
    cos, sin = _rope_tables(seq)
    for l in range(depth):
        w = _split_w_in(w_in[l], w_gate_up[l], b_gate_up[l])
        qa, ka, va, za, la, qb, kb, vb, zb = _in_proj(x, mod[l], g_pre[l].reshape(1, d), w, cos, sin)
        ya = _gla(qa, ka, va, za, la, g_gla[l].reshape(1, GLA_WIDTH))
        yb = _dilated(qb, kb, vb, zb, g_dil[l].reshape(1, DIL_WIDTH))
        w_o = w_out[l].astype(BF16)
        x = _out_proj(ya, yb, x, mod[l], w_o[:GLA_WIDTH], w_o[GLA_WIDTH:], g_post[l].reshape(1, d))
    return x
```

```python
import functools

import jax
import jax.numpy as jnp
import numpy as np
from jax import lax
from jax.experimental import pallas as pl
from jax.experimental.pallas import tpu as pltpu

F32 = jnp.float32
BF16 = jnp.bfloat16

D_MODEL = 1024
GLA_HEADS = 4
GLA_DK = 64
GLA_DV = 128
GLA_QK = GLA_HEADS * GLA_DK
GLA_WIDTH = GLA_HEADS * GLA_DV
GLA_LOWRANK = 16
GLA_TAU = 16.0
GLA_CHUNK = 64
GLA_GROUP = 4
DIL_HEADS = 4
DIL_HD = 128
DIL_WIDTH = DIL_HEADS * DIL_HD
DIL_PATTERNS = ((128, 1), (512, 4), (2048, 16))
DIL_LB = 128
PERM_TILE, PERM_D = DIL_PATTERNS[1]
ROPE_THETA = 10000.0
EPS = 1e-6
LANES = 128
LR_PAD = LANES

VMEM_LIMIT = 56 * 1024 * 1024
DIL_UNROLL = 8
ROW_TILE = 512
NEG_INF = float("-inf")


def _silu(v):
    return v * jax.nn.sigmoid(v)


def _dot(a, b):
    return jnp.dot(a, b, preferred_element_type=F32)


def _dot_nt(a, b):
    return lax.dot_general(a, b, (((1,), (1,)), ((), ())), preferred_element_type=F32)


def _dot_tn(a, b):
    return lax.dot_general(a, b, (((0,), (0,)), ((), ())), preferred_element_type=F32)


def _mod_kernel(c_ref, w_ref, b_ref, o_ref):
    sc = _silu(c_ref[...]).astype(BF16)
    o_ref[...] = _dot(sc, w_ref[...].astype(BF16)) + b_ref[...]


def _modulation(c, w_ada, b_ada):
    depth, d, e = w_ada.shape
    bsz = c.shape[0]
    nt = e // d
    return pl.pallas_call(
        _mod_kernel,
        grid=(depth, nt),
        in_specs=[
            pl.BlockSpec((bsz, d), lambda l, j: (0, 0)),
            pl.BlockSpec((None, d, d), lambda l, j: (l, 0, j)),
            pl.BlockSpec((None, 1, d), lambda l, j: (l, 0, j)),
        ],
        out_specs=pl.BlockSpec((None, bsz, d), lambda l, j: (l, 0, j)),
        out_shape=jax.ShapeDtypeStruct((depth, bsz, e), F32),
        compiler_params=pltpu.CompilerParams(
            dimension_semantics=("arbitrary", "arbitrary"), vmem_limit_bytes=VMEM_LIMIT),
        name="adaln_mod",
    )(c, w_ada, b_ada.reshape(depth, 1, e))


def _rope(v, cos, sin_signed):
    return v * cos + pltpu.roll(v, DIL_HD // 2, axis=1) * sin_signed


def _in_proj_kernel(x_ref, mod_ref, gpre_ref, wqa_ref, wka_ref, wva_ref, wza_ref, wlr_ref,
                    wqb_ref, wkb_ref, wvb_ref, wzb_ref, wg_ref, bg_ref, cos_ref, sin_ref,
                    qa_ref, ka_ref, va_ref, za_ref, la_ref, qb_ref, kb_ref, vb_ref, zb_ref, perm_ref):
    x = x_ref[...]
    shift = mod_ref[:, 0:D_MODEL]
    scale = mod_ref[:, D_MODEL:2 * D_MODEL]
    r = x * lax.rsqrt(jnp.mean(x * x, axis=-1, keepdims=True) + EPS) * gpre_ref[...]
    h = (r * (1.0 + scale) + shift).astype(BF16)

    qa_ref[...] = (_dot(h, wqa_ref[...]) * (GLA_DK ** -0.5)).astype(qa_ref.dtype)
    ka_ref[...] = _dot(h, wka_ref[...]).astype(ka_ref.dtype)
    va_ref[...] = _dot(h, wva_ref[...]).astype(va_ref.dtype)
    za_ref[...] = _dot(h, wza_ref[...]).astype(za_ref.dtype)

    lr = _dot(h, wlr_ref[...]).astype(BF16)
    z = _dot(lr, wg_ref[...]) + bg_ref[...]
    log_sig = jnp.minimum(z, 0.0) - jnp.log1p(jnp.exp(-jnp.abs(z)))
    la_ref[...] = log_sig / GLA_TAU

    def store_residue_order(out_ref, cols, slab, val):
        perm_ref[slab] = val
        n = PERM_TILE // PERM_D
        for r in range(PERM_D):
            out_ref[r * n:(r + 1) * n, cols] = perm_ref[slab, pl.ds(r, n, stride=PERM_D), :].astype(out_ref.dtype)

    cos = cos_ref[...]
    sin = sin_ref[...]
    q = _dot(h, wqb_ref[...])
    k = _dot(h, wkb_ref[...])
    v = _dot(h, wvb_ref[...])
    zg = _dot(h, wzb_ref[...])
    for hd in range(DIL_HEADS):
        cols = slice(hd * DIL_HD, (hd + 1) * DIL_HD)
        store_residue_order(qb_ref, cols, 4 * hd, _rope(q[:, cols], cos, sin) * (DIL_HD ** -0.5))
        store_residue_order(kb_ref, cols, 4 * hd + 1, _rope(k[:, cols], cos, sin))
        store_residue_order(vb_ref, cols, 4 * hd + 2, v[:, cols])
        store_residue_order(zb_ref, cols, 4 * hd + 3, zg[:, cols])


def _in_proj(x, mod_l, g_pre, w, cos, sin):
    bsz, seq, d = x.shape
    tm = PERM_TILE
    row = lambda b, i: (b, i, 0)
    const = lambda b, i: (0, 0)

    def full(a):
        return pl.BlockSpec(a.shape, const)

    out_cols = (("qa", GLA_QK, BF16), ("ka", GLA_QK, BF16), ("va", GLA_WIDTH, BF16),
                ("za", GLA_WIDTH, BF16), ("la", GLA_QK, F32), ("qb", DIL_WIDTH, F32),
                ("kb", DIL_WIDTH, F32), ("vb", DIL_WIDTH, F32), ("zb", DIL_WIDTH, BF16))
    weights = [w[k] for k in ("qa", "ka", "va", "za", "lr", "qb", "kb", "vb", "zb", "wg", "bg")]
    return pl.pallas_call(
        _in_proj_kernel,
        grid=(bsz, seq // tm),
        in_specs=[
            pl.BlockSpec((None, tm, d), row),
            pl.BlockSpec((None, 1, 3 * d), lambda b, i: (b, 0, 0)),
            full(g_pre),
            *[full(a) for a in weights],
            pl.BlockSpec((tm, DIL_HD), lambda b, i: (i, 0)),
            pl.BlockSpec((tm, DIL_HD), lambda b, i: (i, 0)),
        ],
        out_specs=[pl.BlockSpec((None, tm, n), row) for _, n, _ in out_cols],
        out_shape=[jax.ShapeDtypeStruct((bsz, seq, n), dt) for _, n, dt in out_cols],
        scratch_shapes=[pltpu.VMEM((4 * DIL_HEADS, tm, DIL_HD), F32)],
        compiler_params=pltpu.CompilerParams(
            dimension_semantics=("arbitrary", "arbitrary"), vmem_limit_bytes=VMEM_LIMIT),
        name="in_proj",
    )(x, mod_l, g_pre, *weights, cos, sin)


def _head_norm_gate(o, g, z):
    r = o * lax.rsqrt(jnp.mean(o * o, axis=-1, keepdims=True) + EPS)
    return r * g * _silu(z.astype(F32))


def _split2(v):
    h1 = v.astype(BF16)
    h2 = (v - h1.astype(F32)).astype(BF16)
    return h1, h2


def _gla_kernel(qa_ref, ka_ref, va_ref, za_ref, la_ref, g_ref, ya_ref, st_ref, *, chunks):
    @pl.when(pl.program_id(1) == 0)
    def _():
        st_ref[...] = jnp.zeros_like(st_ref)

    c_len = GLA_CHUNK
    grp = GLA_GROUP * c_len
    ri = lax.broadcasted_iota(jnp.int32, (grp, grp), 0)
    ci = lax.broadcasted_iota(jnp.int32, (grp, grp), 1)
    same_chunk = (ri & -c_len) == (ci & -c_len)
    tril_bd = ((ri >= ci) & same_chunk).astype(BF16)
    causal = (lax.broadcasted_iota(jnp.int32, (c_len, c_len), 0)
              >= lax.broadcasted_iota(jnp.int32, (c_len, c_len), 1))
    heads = range(GLA_HEADS)
    kcol = [slice(hd * GLA_DK, (hd + 1) * GLA_DK) for hd in heads]
    vcol = [slice(hd * GLA_DV, (hd + 1) * GLA_DV) for hd in heads]

    def prep(r0):
        rows = pl.ds(r0, grp)
        h1, h2 = _split2(la_ref[rows, :])
        b = _dot(tril_bd, h1) + _dot(tril_bd, h2)
        decay = [jnp.exp(b[(c + 1) * c_len - 1:(c + 1) * c_len, :]) for c in range(GLA_GROUP)]
        decay_rows = jnp.concatenate([jnp.broadcast_to(d, (c_len, GLA_QK)) for d in decay], axis=0)
        k_e32 = ka_ref[rows, :].astype(F32) * jnp.exp(-b)
        q_e = (qa_ref[rows, :].astype(F32) * jnp.exp(b)).astype(BF16)
        k_end = (k_e32 * decay_rows).astype(BF16)
        v = [[va_ref[pl.ds(r0 + c * c_len, c_len), vcol[hd]] for hd in heads] for c in range(GLA_GROUP)]
        return dict(r0=r0, q_e=q_e, k_e=k_e32.astype(BF16), k_end=k_end, decay=decay, v=v)

    crow = [slice(c * c_len, (c + 1) * c_len) for c in range(GLA_GROUP)]

    def intra(p):
        q_e, k_e, k_end, v = p["q_e"], p["k_e"], p["k_end"], p["v"]
        a = [[jnp.where(causal, _dot_nt(q_e[crow[c], kcol[hd]], k_e[crow[c], kcol[hd]]), 0.0).astype(BF16)
              for hd in heads] for c in range(GLA_GROUP)]
        p["inc"] = [[_dot_tn(v[c][hd], k_end[crow[c], kcol[hd]]) for hd in heads] for c in range(GLA_GROUP)]
        p["o"] = [[_dot(a[c][hd], v[c][hd]) for hd in heads] for c in range(GLA_GROUP)]

    def inter(p, st):
        for hd in heads:
            for c in range(GLA_GROUP):
                p["o"][c][hd] = p["o"][c][hd] + _dot_nt(p["q_e"][crow[c], kcol[hd]], st[hd].astype(BF16))
                st[hd] = st[hd] * p["decay"][c][:, kcol[hd]] + p["inc"][c][hd]

    def epilogue(p):
        for c in range(GLA_GROUP):
            for hd in heads:
                out_rows = pl.ds(p["r0"] + c * c_len, c_len)
                ya_ref[out_rows, vcol[hd]] = _head_norm_gate(
                    p["o"][c][hd], g_ref[:, vcol[hd]], za_ref[out_rows, vcol[hd]]).astype(ya_ref.dtype)

    st = [st_ref[hd] for hd in heads]
    groups = [prep(g * grp) for g in range(chunks // GLA_GROUP)]
    intra(groups[0])
    for g, p in enumerate(groups):
        inter(p, st)
        if g + 1 < len(groups):
            intra(groups[g + 1])
        epilogue(p)
    for hd in heads:
        st_ref[hd] = st[hd]


def _gla(qa, ka, va, za, la, g_gla):
    bsz, seq, _ = qa.shape
    ts = ROW_TILE
    row = lambda b, i: (b, i, 0)
    return pl.pallas_call(
        functools.partial(_gla_kernel, chunks=ts // GLA_CHUNK),
        grid=(bsz, seq // ts),
        in_specs=[
            pl.BlockSpec((None, ts, GLA_QK), row),
            pl.BlockSpec((None, ts, GLA_QK), row),
            pl.BlockSpec((None, ts, GLA_WIDTH), row),
            pl.BlockSpec((None, ts, GLA_WIDTH), row),
            pl.BlockSpec((None, ts, GLA_QK), row),
            pl.BlockSpec((1, GLA_WIDTH), lambda b, i: (0, 0)),
        ],
        out_specs=pl.BlockSpec((None, ts, GLA_WIDTH), row),
        out_shape=jax.ShapeDtypeStruct((bsz, seq, GLA_WIDTH), BF16),
        scratch_shapes=[pltpu.VMEM((GLA_HEADS, GLA_DV, GLA_DK), F32)],
        compiler_params=pltpu.CompilerParams(
            dimension_semantics=("arbitrary", "arbitrary"), vmem_limit_bytes=VMEM_LIMIT),
        name="gla",
    )(qa, ka, va, za, la, g_gla)


def _dil_block_chunks(pattern, idx, seq):
    window, dil = DIL_PATTERNS[pattern]
    nb = seq // window
    lb = DIL_LB
    n_r = PERM_TILE // PERM_D
    if dil == PERM_D:
        return [((idx % nb) * PERM_TILE + (idx // nb) * n_r, 1)]
    if dil == 1:
        tile, part = idx // (PERM_TILE // lb), idx % (PERM_TILE // lb)
        rows = lb // PERM_D
        return [(tile * PERM_TILE + r * n_r + part * rows, 1) for r in range(PERM_D)]
    sub = dil // PERM_D
    res, n = idx // nb, idx % nb
    r4, c = res % PERM_D, res // PERM_D
    tiles = window // PERM_TILE
    return [(n * window + t * PERM_TILE + r4 * n_r + c, sub) for t in range(tiles)]


def _dil_kernel(q_ref, k_ref, v_ref, z_ref, g_ref, y_ref, qd, kd, vd, ynat,
                o1, o2, o3, m1, m2, m3, d1, d2, d3, *, seq):
    lb = DIL_LB
    n_blocks = seq // lb
    qi = lax.broadcasted_iota(jnp.int32, (lb, 2 * lb), 0)
    ki = lax.broadcasted_iota(jnp.int32, (lb, 2 * lb), 1)

    def biases(pos_in_block):
        dist = pos_in_block(qi) + lb - (pos_in_block(ki & (lb - 1)) + (ki & lb))
        band = (dist >= 0) & (dist <= lb)
        b_any = jnp.where(band, 0.0, NEG_INF)
        b_first = jnp.where(band & (ki >= lb), 0.0, NEG_INF)
        return b_any, b_first, b_first[:, lb:]

    step_order = biases(lambda a: a)
    rows_p1 = lb // PERM_D
    p1_order = biases(lambda a: PERM_D * (a % rows_p1) + a // rows_p1)

    kd[0] = jnp.zeros((DIL_HD, lb), BF16)
    vd[0:lb, :] = jnp.zeros((lb, DIL_HD), BF16)

    for pat, ((window, dil), o_scr, m_scr, d_scr) in enumerate(
            zip(DIL_PATTERNS, (o1, o2, o3), (m1, m2, m3), (d1, d2, d3))):
        nb = seq // window
        bias_any, bias_first, bias_cur = p1_order if dil == 1 else step_order

        def load_block(ref, idx, pat=pat):
            chunks = _dil_block_chunks(pat, idx, seq)
            rows = lb // len(chunks)
            parts = [ref[pl.ds(s0, rows) if st == 1 else pl.ds(s0, rows, stride=st), :] for s0, st in chunks]
            return parts[0] if len(parts) == 1 else jnp.concatenate(parts, axis=0)

        def store_block(ref, idx, val, pat=pat):
            chunks = _dil_block_chunks(pat, idx, seq)
            rows = lb // len(chunks)
            for i, (s0, st) in enumerate(chunks):
                dst = pl.ds(s0, rows) if st == 1 else pl.ds(s0, rows, stride=st)
                ref[dst, :] = val[i * rows:(i + 1) * rows, :]

        for idx in range(n_blocks):
            dst = pl.ds(lb + idx * lb, lb)
            qd[dst, :] = load_block(q_ref, idx).astype(BF16)
            kd[idx + 1] = load_block(k_ref, idx).T.astype(BF16)
            vd[dst, :] = load_block(v_ref, idx).astype(BF16)

        def scores(idx, nb=nb, bias_any=bias_any, bias_cur=bias_cur):
            q = qd[pl.ds(lb + idx * lb, lb), :]
            if idx % nb == 0:
                return _dot(q, kd[idx + 1]) + bias_cur, pl.ds(lb + idx * lb, lb)
            s = _dot(q, jnp.concatenate([kd[idx], kd[idx + 1]], axis=1)) + bias_any
            return s, pl.ds(idx * lb, 2 * lb)

        def softmax(s):
            m = jnp.max(s, axis=-1, keepdims=True)
            return jnp.exp((s - m).astype(BF16)), m

        def values(idx, p, m, keys, store_block=store_block, o_scr=o_scr, m_scr=m_scr, d_scr=d_scr):
            v_ones = jnp.concatenate([vd[keys, :], jnp.ones((p.shape[1], DIL_HD), BF16)], axis=1)
            acc = _dot(p, v_ones)
            store_block(o_scr, idx, acc[:, :DIL_HD])
            store_block(d_scr, idx, acc[:, DIL_HD:])
            store_block(m_scr, idx, jnp.broadcast_to(m, (lb, DIL_HD)))

        groups = [range(g * DIL_UNROLL, (g + 1) * DIL_UNROLL) for g in range(n_blocks // DIL_UNROLL)]
        pending = [scores(idx) for idx in groups[0]]
        for g, group in enumerate(groups):
            upcoming = [scores(idx) for idx in groups[g + 1]] if g + 1 < len(groups) else []
            probs = [softmax(s) for s, _ in pending]
            for idx, (p, m), (_, keys) in zip(group, probs, pending):
                values(idx, p, m, keys)
            pending = upcoming

    n_r = PERM_TILE // PERM_D

    def combine(t, carry):
        t0 = pl.multiple_of(t * PERM_TILE, PERM_TILE)
        rows = pl.ds(t0, PERM_TILE)
        a1, a2, a3 = m1[rows, :], m2[rows, :], m3[rows, :]
        m = jnp.maximum(jnp.maximum(a1, a2), a3)
        e1, e2, e3 = jnp.exp(a1 - m), jnp.exp(a2 - m), jnp.exp(a3 - m)
        den = e1 * d1[rows, :] + e2 * d2[rows, :] + e3 * d3[rows, :]
        o = (e1 * o1[rows, :] + e2 * o2[rows, :] + e3 * o3[rows, :]) / den
        y = _head_norm_gate(o, g_ref[...], z_ref[rows, :])
        for r in range(PERM_D):
            ynat[pl.ds(t0 + r, n_r, stride=PERM_D), :] = y[r * n_r:(r + 1) * n_r, :]
        y_ref[rows, :] = ynat[rows, :].astype(y_ref.dtype)
        return carry

    lax.fori_loop(0, seq // PERM_TILE, combine, 0)


def _dilated(qb, kb, vb, zb, g_dil):
    bsz, seq, _ = qb.shape
    head = lambda b, h: (b, 0, h)
    blk = pl.BlockSpec((None, seq, DIL_HD), head)
    return pl.pallas_call(
        functools.partial(_dil_kernel, seq=seq),
        grid=(bsz, DIL_HEADS),
        in_specs=[blk, blk, blk, blk, pl.BlockSpec((1, DIL_HD), lambda b, h: (0, h))],
        out_specs=blk,
        out_shape=jax.ShapeDtypeStruct((bsz, seq, DIL_WIDTH), BF16),
        scratch_shapes=[pltpu.VMEM((seq + DIL_LB, DIL_HD), BF16),
                        pltpu.VMEM((seq // DIL_LB + 1, DIL_HD, DIL_LB), BF16),
                        pltpu.VMEM((seq + DIL_LB, DIL_HD), BF16)]
        + [pltpu.VMEM((seq, DIL_HD), F32) for _ in range(10)],
        compiler_params=pltpu.CompilerParams(
            dimension_semantics=("arbitrary", "arbitrary"), vmem_limit_bytes=VMEM_LIMIT),
        name="dilated",
    )(qb, kb, vb, zb, g_dil)


def _out_proj_kernel(ya_ref, yb_ref, x_ref, mod_ref, wa_ref, wb_ref, gpost_ref, o_ref):
    y = _dot(ya_ref[...], wa_ref[...]) + _dot(yb_ref[...], wb_ref[...])
    r = y * lax.rsqrt(jnp.mean(y * y, axis=-1, keepdims=True) + EPS) * gpost_ref[...]
    gate = mod_ref[:, 2 * D_MODEL:3 * D_MODEL]
    o_ref[...] = x_ref[...] + gate * r


def _out_proj(ya, yb, x, mod_l, w_out_a, w_out_b, g_post):
    bsz, seq, d = x.shape
    tm = ROW_TILE
    row = lambda b, i: (b, i, 0)
    const = lambda b, i: (0, 0)
    return pl.pallas_call(
        _out_proj_kernel,
        grid=(bsz, seq // tm),
        in_specs=[
            pl.BlockSpec((None, tm, GLA_WIDTH), row),
            pl.BlockSpec((None, tm, DIL_WIDTH), row),
            pl.BlockSpec((None, tm, d), row),
            pl.BlockSpec((None, 1, 3 * d), lambda b, i: (b, 0, 0)),
            pl.BlockSpec(w_out_a.shape, const),
            pl.BlockSpec(w_out_b.shape, const),
            pl.BlockSpec((1, d), const),
        ],
        out_specs=pl.BlockSpec((None, tm, d), row),
        out_shape=jax.ShapeDtypeStruct((bsz, seq, d), F32),
        compiler_params=pltpu.CompilerParams(
            dimension_semantics=("arbitrary", "arbitrary"), vmem_limit_bytes=VMEM_LIMIT),
        name="out_proj",
    )(ya, yb, x, mod_l, w_out_a, w_out_b, g_post)


def _rope_tables(seq):
    half = DIL_HD // 2
    inv_freq = ROPE_THETA ** (-jnp.arange(0, DIL_HD, 2, dtype=F32) / DIL_HD)
    ang = jnp.arange(seq, dtype=F32)[:, None] * inv_freq[None, :]
    cos, sin = jnp.cos(ang), jnp.sin(ang)
    del half
    return jnp.concatenate([cos, cos], axis=-1), jnp.concatenate([-sin, sin], axis=-1)


def _split_w_in(w_in_l, w_gate_up_l, b_gate_up_l):
    sizes = (GLA_QK, GLA_QK, GLA_WIDTH, GLA_WIDTH, GLA_LOWRANK,
             DIL_WIDTH, DIL_WIDTH, DIL_WIDTH, DIL_WIDTH)
    offs = np.cumsum((0,) + sizes)
    names = ("qa", "ka", "va", "za", "lr", "qb", "kb", "vb", "zb")
    w = {n: w_in_l[:, offs[i]:offs[i + 1]].astype(BF16) for i, n in enumerate(names)}
    w["lr"] = jnp.pad(w["lr"], ((0, 0), (0, LR_PAD - GLA_LOWRANK)))
    w["wg"] = jnp.pad(w_gate_up_l.astype(BF16), ((0, LR_PAD - GLA_LOWRANK), (0, 0)))
    w["bg"] = b_gate_up_l.reshape(1, GLA_QK)
    return w


def kernel(x, c, w_ada, b_ada, g_pre, w_in, w_gate_up, b_gate_up, g_gla, g_dil, w_out, g_post):
    bsz, seq, d = x.shape
    depth = w_ada.shape[0]
    mod = _modulation(c, w_ada, b_ada).reshape(depth, bsz, 1, 3 * d)
    cos, sin = _rope_tables(seq)
    for l in range(depth):
        w = _split_w_in(w_in[l], w_gate_up[l], b_gate_up[l])
        qa, ka, va, za, la, qb, kb, vb, zb = _in_proj(x, mod[l], g_pre[l].reshape(1, d), w, cos, sin)
        ya = _gla(qa, ka, va, za, la, g_gla[l].reshape(1, GLA_WIDTH))
        yb = _dilated(qb, kb, vb, zb, g_dil[l].reshape(1, DIL_WIDTH))
        w_o = w_out[l].astype(BF16)
        x = _out_proj(ya, yb, x, mod[l], w_o[:GLA_WIDTH], w_o[GLA_WIDTH:], g_post[l].reshape(1, d))
    return x
```

```python
import functools

import jax
import jax.numpy as jnp
import numpy as np
from jax import lax
from jax.experimental import pallas as pl
from jax.experimental.pallas import tpu as pltpu

F32 = jnp.float32
BF16 = jnp.bfloat16

D_MODEL = 1024
GLA_HEADS = 4
GLA_DK = 64
GLA_DV = 128
GLA_QK = GLA_HEADS * GLA_DK
GLA_WIDTH = GLA_HEADS * GLA_DV
GLA_LOWRANK = 16
GLA_TAU = 16.0
GLA_CHUNK = 64
GLA_GROUP = 4
DIL_HEADS = 4
DIL_HD = 128
DIL_WIDTH = DIL_HEADS * DIL_HD
DIL_PATTERNS = ((128, 1), (512, 4), (2048, 16))
DIL_LB = 128
PERM_TILE, PERM_D = DIL_PATTERNS[1]
ROPE_THETA = 10000.0
EPS = 1e-6
LANES = 128
LR_PAD = LANES

VMEM_LIMIT = 56 * 1024 * 1024
DIL_UNROLL = 8
ROW_TILE = 512
NEG_INF = float("-inf")


def _silu(v):
    return v * jax.nn.sigmoid(v)


def _dot(a, b):
    return jnp.dot(a, b, preferred_element_type=F32)


def _dot_nt(a, b):
    return lax.dot_general(a, b, (((1,), (1,)), ((), ())), preferred_element_type=F32)


def _dot_tn(a, b):
    return lax.dot_general(a, b, (((0,), (0,)), ((), ())), preferred_element_type=F32)


def _mod_kernel(c_ref, w_ref, b_ref, o_ref):
    sc = _silu(c_ref[...]).astype(BF16)
    o_ref[...] = _dot(sc, w_ref[...].astype(BF16)) + b_ref[...]


def _modulation(c, w_ada, b_ada):
    depth, d, e = w_ada.shape
    bsz = c.shape[0]
    nt = e // d
    return pl.pallas_call(
        _mod_kernel,
        grid=(depth, nt),
        in_specs=[
            pl.BlockSpec((bsz, d), lambda l, j: (0, 0)),
            pl.BlockSpec((None, d, d), lambda l, j: (l, 0, j)),
            pl.BlockSpec((None, 1, d), lambda l, j: (l, 0, j)),
        ],
        out_specs=pl.BlockSpec((None, bsz, d), lambda l, j: (l, 0, j)),
        out_shape=jax.ShapeDtypeStruct((depth, bsz, e), F32),
        compiler_params=pltpu.CompilerParams(
            dimension_semantics=("arbitrary", "arbitrary"), vmem_limit_bytes=VMEM_LIMIT),
        name="adaln_mod",
    )(c, w_ada, b_ada.reshape(depth, 1, e))


def _rope(v, cos, sin_signed):
    return v * cos + pltpu.roll(v, DIL_HD // 2, axis=1) * sin_signed


def _in_proj_kernel(x_ref, mod_ref, gpre_ref, wqa_ref, wka_ref, wva_ref, wza_ref, wlr_ref,
                    wqb_ref, wkb_ref, wvb_ref, wzb_ref, wg_ref, bg_ref, cos_ref, sin_ref,
                    qa_ref, ka_ref, va_ref, za_ref, la_ref, qb_ref, kb_ref, vb_ref, zb_ref, perm_ref):
    x = x_ref[...]
    shift = mod_ref[:, 0:D_MODEL]
    scale = mod_ref[:, D_MODEL:2 * D_MODEL]
    r = x * lax.rsqrt(jnp.mean(x * x, axis=-1, keepdims=True) + EPS) * gpre_ref[...]
    h = (r * (1.0 + scale) + shift).astype(BF16)

    qa_ref[...] = (_dot(h, wqa_ref[...]) * (GLA_DK ** -0.5)).astype(qa_ref.dtype)
    ka_ref[...] = _dot(h, wka_ref[...]).astype(ka_ref.dtype)
    va_ref[...] = _dot(h, wva_ref[...]).astype(va_ref.dtype)
    za_ref[...] = _dot(h, wza_ref[...]).astype(za_ref.dtype)

    lr = _dot(h, wlr_ref[...]).astype(BF16)
    z = _dot(lr, wg_ref[...]) + bg_ref[...]
    log_sig = jnp.minimum(z, 0.0) - jnp.log1p(jnp.exp(-jnp.abs(z)))
    la_ref[...] = log_sig / GLA_TAU

    def store_residue_order(out_ref, cols, slab, val):
        perm_ref[slab] = val
        n = PERM_TILE // PERM_D
        for r in range(PERM_D):
            out_ref[r * n:(r + 1) * n, cols] = perm_ref[slab, pl.ds(r, n, stride=PERM_D), :].astype(out_ref.dtype)

    cos = cos_ref[...]
    sin = sin_ref[...]
    q = _dot(h, wqb_ref[...])
    k = _dot(h, wkb_ref[...])
    v = _dot(h, wvb_ref[...])
    zg = _dot(h, wzb_ref[...])
    for hd in range(DIL_HEADS):
        cols = slice(hd * DIL_HD, (hd + 1) * DIL_HD)
        store_residue_order(qb_ref, cols, 4 * hd, _rope(q[:, cols], cos, sin) * (DIL_HD ** -0.5))
        store_residue_order(kb_ref, cols, 4 * hd + 1, _rope(k[:, cols], cos, sin))
        store_residue_order(vb_ref, cols, 4 * hd + 2, v[:, cols])
        store_residue_order(zb_ref, cols, 4 * hd + 3, zg[:, cols])


def _in_proj(x, mod_l, g_pre, w, cos, sin):
    bsz, seq, d = x.shape
    tm = PERM_TILE
    row = lambda b, i: (b, i, 0)
    const = lambda b, i: (0, 0)

    def full(a):
        return pl.BlockSpec(a.shape, const)

    out_cols = (("qa", GLA_QK, BF16), ("ka", GLA_QK, BF16), ("va", GLA_WIDTH, BF16),
                ("za", GLA_WIDTH, BF16), ("la", GLA_QK, F32), ("qb", DIL_WIDTH, F32),
                ("kb", DIL_WIDTH, F32), ("vb", DIL_WIDTH, F32), ("zb", DIL_WIDTH, BF16))
    weights = [w[k] for k in ("qa", "ka", "va", "za", "lr", "qb", "kb", "vb", "zb", "wg", "bg")]
    return pl.pallas_call(
        _in_proj_kernel,
        grid=(bsz, seq // tm),
        in_specs=[
            pl.BlockSpec((None, tm, d), row),
            pl.BlockSpec((None, 1, 3 * d), lambda b, i: (b, 0, 0)),
            full(g_pre),
            *[full(a) for a in weights],
            pl.BlockSpec((tm, DIL_HD), lambda b, i: (i, 0)),
            pl.BlockSpec((tm, DIL_HD), lambda b, i: (i, 0)),
        ],
        out_specs=[pl.BlockSpec((None, tm, n), row) for _, n, _ in out_cols],
        out_shape=[jax.ShapeDtypeStruct((bsz, seq, n), dt) for _, n, dt in out_cols],
        scratch_shapes=[pltpu.VMEM((4 * DIL_HEADS, tm, DIL_HD), F32)],
        compiler_params=pltpu.CompilerParams(
            dimension_semantics=("arbitrary", "arbitrary"), vmem_limit_bytes=VMEM_LIMIT),
        name="in_proj",
    )(x, mod_l, g_pre, *weights, cos, sin)


def _head_norm_gate(o, g, z):
    r = o * lax.rsqrt(jnp.mean(o * o, axis=-1, keepdims=True) + EPS)
    return r * g * _silu(z.astype(F32))


def _split2(v):
    h1 = v.astype(BF16)
    h2 = (v - h1.astype(F32)).astype(BF16)
    return h1, h2


def _gla_out_kernel(qa_ref, ka_ref, va_ref, za_ref, la_ref, g_ref, yb_ref, x_ref, mod_ref, wa_ref, wb_ref,
                    gpost_ref, o_ref, st_ref, ya_ref, *, chunks):
    @pl.when(pl.program_id(1) == 0)
    def _():
        st_ref[...] = jnp.zeros_like(st_ref)

    c_len = GLA_CHUNK
    grp = GLA_GROUP * c_len
    ri = lax.broadcasted_iota(jnp.int32, (grp, grp), 0)
    ci = lax.broadcasted_iota(jnp.int32, (grp, grp), 1)
    same_chunk = (ri & -c_len) == (ci & -c_len)
    tril_bd = ((ri >= ci) & same_chunk).astype(BF16)
    causal = (lax.broadcasted_iota(jnp.int32, (c_len, c_len), 0)
              >= lax.broadcasted_iota(jnp.int32, (c_len, c_len), 1))
    heads = range(GLA_HEADS)
    kcol = [slice(hd * GLA_DK, (hd + 1) * GLA_DK) for hd in heads]
    vcol = [slice(hd * GLA_DV, (hd + 1) * GLA_DV) for hd in heads]

    def prep(r0):
        rows = pl.ds(r0, grp)
        h1, h2 = _split2(la_ref[rows, :])
        b = _dot(tril_bd, h1) + _dot(tril_bd, h2)
        decay = [jnp.exp(b[(c + 1) * c_len - 1:(c + 1) * c_len, :]) for c in range(GLA_GROUP)]
        decay_rows = jnp.concatenate([jnp.broadcast_to(d, (c_len, GLA_QK)) for d in decay], axis=0)
        k_e32 = ka_ref[rows, :].astype(F32) * jnp.exp(-b)
        q_e = (qa_ref[rows, :].astype(F32) * jnp.exp(b)).astype(BF16)
        k_end = (k_e32 * decay_rows).astype(BF16)
        v = [[va_ref[pl.ds(r0 + c * c_len, c_len), vcol[hd]] for hd in heads] for c in range(GLA_GROUP)]
        return dict(r0=r0, q_e=q_e, k_e=k_e32.astype(BF16), k_end=k_end, decay=decay, v=v)

    crow = [slice(c * c_len, (c + 1) * c_len) for c in range(GLA_GROUP)]

    def intra(p):
        q_e, k_e, k_end, v = p["q_e"], p["k_e"], p["k_end"], p["v"]
        a = [[jnp.where(causal, _dot_nt(q_e[crow[c], kcol[hd]], k_e[crow[c], kcol[hd]]), 0.0).astype(BF16)
              for hd in heads] for c in range(GLA_GROUP)]
        p["inc"] = [[_dot_tn(v[c][hd], k_end[crow[c], kcol[hd]]) for hd in heads] for c in range(GLA_GROUP)]
        p["o"] = [[_dot(a[c][hd], v[c][hd]) for hd in heads] for c in range(GLA_GROUP)]

    def inter(p, st):
        for hd in heads:
            for c in range(GLA_GROUP):
                p["o"][c][hd] = p["o"][c][hd] + _dot_nt(p["q_e"][crow[c], kcol[hd]], st[hd].astype(BF16))
                st[hd] = st[hd] * p["decay"][c][:, kcol[hd]] + p["inc"][c][hd]

    def epilogue(p):
        for c in range(GLA_GROUP):
            for hd in heads:
                out_rows = pl.ds(p["r0"] + c * c_len, c_len)
                ya_ref[out_rows, vcol[hd]] = _head_norm_gate(
                    p["o"][c][hd], g_ref[:, vcol[hd]], za_ref[out_rows, vcol[hd]]).astype(ya_ref.dtype)

    def project(p):
        rows = pl.ds(p["r0"], grp)
        y = _dot(ya_ref[rows, :], wa_ref[...]) + _dot(yb_ref[rows, :], wb_ref[...])
        r = y * lax.rsqrt(jnp.mean(y * y, axis=-1, keepdims=True) + EPS) * gpost_ref[...]
        o_ref[rows, :] = x_ref[rows, :] + mod_ref[:, 2 * D_MODEL:3 * D_MODEL] * r

    st = [st_ref[hd] for hd in heads]
    groups = [prep(g * grp) for g in range(chunks // GLA_GROUP)]
    intra(groups[0])
    for g, p in enumerate(groups):
        inter(p, st)
        if g + 1 < len(groups):
            intra(groups[g + 1])
        epilogue(p)
        project(p)
    for hd in heads:
        st_ref[hd] = st[hd]


def _gla_out_proj(qa, ka, va, za, la, g_gla, yb, x, mod_l, w_out_a, w_out_b, g_post):
    bsz, seq, d = x.shape
    ts = ROW_TILE
    row = lambda b, i: (b, i, 0)
    const = lambda b, i: (0, 0)
    return pl.pallas_call(
        functools.partial(_gla_out_kernel, chunks=ts // GLA_CHUNK),
        grid=(bsz, seq // ts),
        in_specs=[
            pl.BlockSpec((None, ts, GLA_QK), row),
            pl.BlockSpec((None, ts, GLA_QK), row),
            pl.BlockSpec((None, ts, GLA_WIDTH), row),
            pl.BlockSpec((None, ts, GLA_WIDTH), row),
            pl.BlockSpec((None, ts, GLA_QK), row),
            pl.BlockSpec((1, GLA_WIDTH), const),
            pl.BlockSpec((None, ts, DIL_WIDTH), row),
            pl.BlockSpec((None, ts, d), row),
            pl.BlockSpec((None, 1, 3 * d), lambda b, i: (b, 0, 0)),
            pl.BlockSpec(w_out_a.shape, const),
            pl.BlockSpec(w_out_b.shape, const),
            pl.BlockSpec((1, d), const),
        ],
        out_specs=pl.BlockSpec((None, ts, d), row),
        out_shape=jax.ShapeDtypeStruct((bsz, seq, d), F32),
        scratch_shapes=[pltpu.VMEM((GLA_HEADS, GLA_DV, GLA_DK), F32), pltpu.VMEM((ts, GLA_WIDTH), BF16)],
        compiler_params=pltpu.CompilerParams(
            dimension_semantics=("arbitrary", "arbitrary"), vmem_limit_bytes=VMEM_LIMIT),
        name="gla_out_proj",
    )(qa, ka, va, za, la, g_gla, yb, x, mod_l, w_out_a, w_out_b, g_post)


def _dil_block_chunks(pattern, idx, seq):
    window, dil = DIL_PATTERNS[pattern]
    nb = seq // window
    lb = DIL_LB
    n_r = PERM_TILE // PERM_D
    if dil == PERM_D:
        return [((idx % nb) * PERM_TILE + (idx // nb) * n_r, 1)]
    if dil == 1:
        tile, part = idx // (PERM_TILE // lb), idx % (PERM_TILE // lb)
        rows = lb // PERM_D
        return [(tile * PERM_TILE + r * n_r + part * rows, 1) for r in range(PERM_D)]
    sub = dil // PERM_D
    res, n = idx // nb, idx % nb
    r4, c = res % PERM_D, res // PERM_D
    tiles = window // PERM_TILE
    return [(n * window + t * PERM_TILE + r4 * n_r + c, sub) for t in range(tiles)]


def _dil_kernel(q_ref, k_ref, v_ref, z_ref, g_ref, y_ref, qd, kd, vd, ynat,
                o1, o2, o3, m1, m2, m3, d1, d2, d3, *, seq):
    lb = DIL_LB
    n_blocks = seq // lb
    qi = lax.broadcasted_iota(jnp.int32, (lb, 2 * lb), 0)
    ki = lax.broadcasted_iota(jnp.int32, (lb, 2 * lb), 1)

    def biases(pos_in_block):
        dist = pos_in_block(qi) + lb - (pos_in_block(ki & (lb - 1)) + (ki & lb))
        band = (dist >= 0) & (dist <= lb)
        b_any = jnp.where(band, 0.0, NEG_INF)
        b_first = jnp.where(band & (ki >= lb), 0.0, NEG_INF)
        return b_any, b_first, b_first[:, lb:]

    step_order = biases(lambda a: a)
    rows_p1 = lb // PERM_D
    p1_order = biases(lambda a: PERM_D * (a % rows_p1) + a // rows_p1)

    kd[0] = jnp.zeros((DIL_HD, lb), BF16)
    vd[0:lb, :] = jnp.zeros((lb, DIL_HD), BF16)

    for pat, ((window, dil), o_scr, m_scr, d_scr) in enumerate(
            zip(DIL_PATTERNS, (o1, o2, o3), (m1, m2, m3), (d1, d2, d3))):
        nb = seq // window
        bias_any, bias_first, bias_cur = p1_order if dil == 1 else step_order

        def load_block(ref, idx, pat=pat):
            chunks = _dil_block_chunks(pat, idx, seq)
            rows = lb // len(chunks)
            parts = [ref[pl.ds(s0, rows) if st == 1 else pl.ds(s0, rows, stride=st), :] for s0, st in chunks]
            return parts[0] if len(parts) == 1 else jnp.concatenate(parts, axis=0)

        def store_block(ref, idx, val, pat=pat):
            chunks = _dil_block_chunks(pat, idx, seq)
            rows = lb // len(chunks)
            for i, (s0, st) in enumerate(chunks):
                dst = pl.ds(s0, rows) if st == 1 else pl.ds(s0, rows, stride=st)
                ref[dst, :] = val[i * rows:(i + 1) * rows, :]

        for idx in range(n_blocks):
            dst = pl.ds(lb + idx * lb, lb)
            qd[dst, :] = load_block(q_ref, idx).astype(BF16)
            kd[idx + 1] = load_block(k_ref, idx).T.astype(BF16)
            vd[dst, :] = load_block(v_ref, idx).astype(BF16)

        def scores(idx, nb=nb, bias_any=bias_any, bias_cur=bias_cur):
            q = qd[pl.ds(lb + idx * lb, lb), :]
            if idx % nb == 0:
                return _dot(q, kd[idx + 1]) + bias_cur, pl.ds(lb + idx * lb, lb)
            s = _dot(q, jnp.concatenate([kd[idx], kd[idx + 1]], axis=1)) + bias_any
            return s, pl.ds(idx * lb, 2 * lb)

        def softmax(s):
            m = jnp.max(s, axis=-1, keepdims=True)
            return jnp.exp((s - m).astype(BF16)), m

        def values(idx, p, m, keys, store_block=store_block, o_scr=o_scr, m_scr=m_scr, d_scr=d_scr):
            v_ones = jnp.concatenate([vd[keys, :], jnp.ones((p.shape[1], DIL_HD), BF16)], axis=1)
            acc = _dot(p, v_ones)
            store_block(o_scr, idx, acc[:, :DIL_HD])
            store_block(d_scr, idx, acc[:, DIL_HD:])
            store_block(m_scr, idx, jnp.broadcast_to(m, (lb, DIL_HD)))

        groups = [range(g * DIL_UNROLL, (g + 1) * DIL_UNROLL) for g in range(n_blocks // DIL_UNROLL)]
        pending = [scores(idx) for idx in groups[0]]
        for g, group in enumerate(groups):
            upcoming = [scores(idx) for idx in groups[g + 1]] if g + 1 < len(groups) else []
            probs = [softmax(s) for s, _ in pending]
            for idx, (p, m), (_, keys) in zip(group, probs, pending):
                values(idx, p, m, keys)
            pending = upcoming

    n_r = PERM_TILE // PERM_D

    def combine(t, carry):
        t0 = pl.multiple_of(t * PERM_TILE, PERM_TILE)
        rows = pl.ds(t0, PERM_TILE)
        a1, a2, a3 = m1[rows, :], m2[rows, :], m3[rows, :]
        m = jnp.maximum(jnp.maximum(a1, a2), a3)
        e1, e2, e3 = jnp.exp(a1 - m), jnp.exp(a2 - m), jnp.exp(a3 - m)
        den = e1 * d1[rows, :] + e2 * d2[rows, :] + e3 * d3[rows, :]
        o = (e1 * o1[rows, :] + e2 * o2[rows, :] + e3 * o3[rows, :]) / den
        y = _head_norm_gate(o, g_ref[...], z_ref[rows, :])
        for r in range(PERM_D):
            ynat[pl.ds(t0 + r, n_r, stride=PERM_D), :] = y[r * n_r:(r + 1) * n_r, :]
        y_ref[rows, :] = ynat[rows, :].astype(y_ref.dtype)
        return carry

    lax.fori_loop(0, seq // PERM_TILE, combine, 0)


def _dilated(qb, kb, vb, zb, g_dil):
    bsz, seq, _ = qb.shape
    head = lambda b, h: (b, 0, h)
    blk = pl.BlockSpec((None, seq, DIL_HD), head)
    return pl.pallas_call(
        functools.partial(_dil_kernel, seq=seq),
        grid=(bsz, DIL_HEADS),
        in_specs=[blk, blk, blk, blk, pl.BlockSpec((1, DIL_HD), lambda b, h: (0, h))],
        out_specs=blk,
        out_shape=jax.ShapeDtypeStruct((bsz, seq, DIL_WIDTH), BF16),
        scratch_shapes=[pltpu.VMEM((seq + DIL_LB, DIL_HD), BF16),
                        pltpu.VMEM((seq // DIL_LB + 1, DIL_HD, DIL_LB), BF16),
                        pltpu.VMEM((seq + DIL_LB, DIL_HD), BF16)]
        + [pltpu.VMEM((seq, DIL_HD), F32) for _ in range(10)],
        compiler_params=pltpu.CompilerParams(
            dimension_semantics=("arbitrary", "arbitrary"), vmem_limit_bytes=VMEM_LIMIT),
        name="dilated",
    )(qb, kb, vb, zb, g_dil)


def _rope_tables(seq):
    half = DIL_HD // 2
    inv_freq = ROPE_THETA ** (-jnp.arange(0, DIL_HD, 2, dtype=F32) / DIL_HD)
    ang = jnp.arange(seq, dtype=F32)[:, None] * inv_freq[None, :]
    cos, sin = jnp.cos(ang), jnp.sin(ang)
    del half
    return jnp.concatenate([cos, cos], axis=-1), jnp.concatenate([-sin, sin], axis=-1)


def _split_w_in(w_in_l, w_gate_up_l, b_gate_up_l):
    sizes = (GLA_QK, GLA_QK, GLA_WIDTH, GLA_WIDTH, GLA_LOWRANK,
             DIL_WIDTH, DIL_WIDTH, DIL_WIDTH, DIL_WIDTH)
    offs = np.cumsum((0,) + sizes)
    names = ("qa", "ka", "va", "za", "lr", "qb", "kb", "vb", "zb")
    w = {n: w_in_l[:, offs[i]:offs[i + 1]].astype(BF16) for i, n in enumerate(names)}
    w["lr"] = jnp.pad(w["lr"], ((0, 0), (0, LR_PAD - GLA_LOWRANK)))
    w["wg"] = jnp.pad(w_gate_up_l.astype(BF16), ((0, LR_PAD - GLA_LOWRANK), (0, 0)))
    w["bg"] = b_gate_up_l.reshape(1, GLA_QK)
    return w


def kernel(x, c, w_ada, b_ada, g_pre, w_in, w_gate_up, b_gate_up, g_gla, g_dil, w_out, g_post):
    bsz, seq, d = x.shape
    depth = w_ada.shape[0]
    mod = _modulation(c, w_ada, b_ada).reshape(depth, bsz, 1, 3 * d)
    cos, sin = _rope_tables(seq)
    for l in range(depth):
        w = _split_w_in(w_in[l], w_gate_up[l], b_gate_up[l])
        qa, ka, va, za, la, qb, kb, vb, zb = _in_proj(x, mod[l], g_pre[l].reshape(1, d), w, cos, sin)
        yb = _dilated(qb, kb, vb, zb, g_dil[l].reshape(1, DIL_WIDTH))
        w_o = w_out[l].astype(BF16)
        x = _gla_out_proj(qa, ka, va, za, la, g_gla[l].reshape(1, GLA_WIDTH), yb, x, mod[l],
                          w_o[:GLA_WIDTH], w_o[GLA_WIDTH:], g_post[l].reshape(1, d))
    return x
```

```python
import functools

import jax
import jax.numpy as jnp
import numpy as np
from jax import lax
from jax.experimental import pallas as pl
from jax.experimental.pallas import tpu as pltpu

F32 = jnp.float32
BF16 = jnp.bfloat16

D_MODEL = 1024
GLA_HEADS = 4
GLA_DK = 64
GLA_DV = 128
GLA_QK = GLA_HEADS * GLA_DK
GLA_WIDTH = GLA_HEADS * GLA_DV
GLA_LOWRANK = 16
GLA_TAU = 16.0
GLA_CHUNK = 64
GLA_GROUP = 4
DIL_HEADS = 4
DIL_HD = 128
DIL_WIDTH = DIL_HEADS * DIL_HD
DIL_PATTERNS = ((128, 1), (512, 4), (2048, 16))
DIL_LB = 128
PERM_TILE, PERM_D = DIL_PATTERNS[1]
ROPE_THETA = 10000.0
EPS = 1e-6
LANES = 128
LR_PAD = LANES

VMEM_LIMIT = 56 * 1024 * 1024
DIL_UNROLL = 8
ROW_TILE = 512
NEG_INF = float("-inf")


def _silu(v):
    return v * jax.nn.sigmoid(v)


def _dot(a, b):
    return jnp.dot(a, b, preferred_element_type=F32)


def _dot_nt(a, b):
    return lax.dot_general(a, b, (((1,), (1,)), ((), ())), preferred_element_type=F32)


def _dot_tn(a, b):
    return lax.dot_general(a, b, (((0,), (0,)), ((), ())), preferred_element_type=F32)


def _mod_kernel(c_ref, w_ref, b_ref, o_ref):
    sc = _silu(c_ref[...]).astype(BF16)
    o_ref[...] = _dot(sc, w_ref[...].astype(BF16)) + b_ref[...]


def _modulation(c, w_ada, b_ada):
    depth, d, e = w_ada.shape
    bsz = c.shape[0]
    nt = e // d
    return pl.pallas_call(
        _mod_kernel,
        grid=(depth, nt),
        in_specs=[
            pl.BlockSpec((bsz, d), lambda l, j: (0, 0)),
            pl.BlockSpec((None, d, d), lambda l, j: (l, 0, j)),
            pl.BlockSpec((None, 1, d), lambda l, j: (l, 0, j)),
        ],
        out_specs=pl.BlockSpec((None, bsz, d), lambda l, j: (l, 0, j)),
        out_shape=jax.ShapeDtypeStruct((depth, bsz, e), F32),
        compiler_params=pltpu.CompilerParams(
            dimension_semantics=("arbitrary", "arbitrary"), vmem_limit_bytes=VMEM_LIMIT),
        name="adaln_mod",
    )(c, w_ada, b_ada.reshape(depth, 1, e))


def _rope(v, cos, sin_signed):
    return v * cos + pltpu.roll(v, DIL_HD // 2, axis=1) * sin_signed


def _in_proj_kernel(x_ref, mod_ref, gpre_ref, wqa_ref, wka_ref, wva_ref, wza_ref, wlr_ref,
                    wqb_ref, wkb_ref, wvb_ref, wzb_ref, wg_ref, bg_ref, cos_ref, sin_ref,
                    qa_ref, ka_ref, va_ref, za_ref, la_ref, qb_ref, kb_ref, vb_ref, zb_ref, perm_ref):
    x = x_ref[...]
    shift = mod_ref[:, 0:D_MODEL]
    scale = mod_ref[:, D_MODEL:2 * D_MODEL]
    r = x * lax.rsqrt(jnp.mean(x * x, axis=-1, keepdims=True) + EPS) * gpre_ref[...]
    h = (r * (1.0 + scale) + shift).astype(BF16)

    qa_ref[...] = (_dot(h, wqa_ref[...]) * (GLA_DK ** -0.5)).astype(qa_ref.dtype)
    ka_ref[...] = _dot(h, wka_ref[...]).astype(ka_ref.dtype)
    va_ref[...] = _dot(h, wva_ref[...]).astype(va_ref.dtype)
    za_ref[...] = _dot(h, wza_ref[...]).astype(za_ref.dtype)

    lr = _dot(h, wlr_ref[...]).astype(BF16)
    z = _dot(lr, wg_ref[...]) + bg_ref[...]
    log_sig = jnp.minimum(z, 0.0) - jnp.log1p(jnp.exp(-jnp.abs(z)))
    la_ref[...] = log_sig / GLA_TAU

    def store_residue_order(out_ref, cols, slab, val):
        perm_ref[slab] = val
        n = PERM_TILE // PERM_D
        for r in range(PERM_D):
            out_ref[r * n:(r + 1) * n, cols] = perm_ref[slab, pl.ds(r, n, stride=PERM_D), :].astype(out_ref.dtype)

    cos = cos_ref[...]
    sin = sin_ref[...]
    q = _dot(h, wqb_ref[...])
    k = _dot(h, wkb_ref[...])
    v = _dot(h, wvb_ref[...])
    zg = _dot(h, wzb_ref[...])
    for hd in range(DIL_HEADS):
        cols = slice(hd * DIL_HD, (hd + 1) * DIL_HD)
        store_residue_order(qb_ref, cols, 4 * hd, _rope(q[:, cols], cos, sin) * (DIL_HD ** -0.5))
        store_residue_order(kb_ref, cols, 4 * hd + 1, _rope(k[:, cols], cos, sin))
        store_residue_order(vb_ref, cols, 4 * hd + 2, v[:, cols])
        store_residue_order(zb_ref, cols, 4 * hd + 3, zg[:, cols])


def _in_proj(x, mod_l, g_pre, w, cos, sin):
    bsz, seq, d = x.shape
    tm = PERM_TILE
    row = lambda b, i: (b, i, 0)
    const = lambda b, i: (0, 0)

    def full(a):
        return pl.BlockSpec(a.shape, const)

    out_cols = (("qa", GLA_QK, BF16), ("ka", GLA_QK, BF16), ("va", GLA_WIDTH, BF16),
                ("za", GLA_WIDTH, BF16), ("la", GLA_QK, F32), ("qb", DIL_WIDTH, F32),
                ("kb", DIL_WIDTH, F32), ("vb", DIL_WIDTH, F32), ("zb", DIL_WIDTH, BF16))
    weights = [w[k] for k in ("qa", "ka", "va", "za", "lr", "qb", "kb", "vb", "zb", "wg", "bg")]
    return pl.pallas_call(
        _in_proj_kernel,
        grid=(bsz, seq // tm),
        in_specs=[
            pl.BlockSpec((None, tm, d), row),
            pl.BlockSpec((None, 1, 3 * d), lambda b, i: (b, 0, 0)),
            full(g_pre),
            *[full(a) for a in weights],
            pl.BlockSpec((tm, DIL_HD), lambda b, i: (i, 0)),
            pl.BlockSpec((tm, DIL_HD), lambda b, i: (i, 0)),
        ],
        out_specs=[pl.BlockSpec((None, tm, n), row) for _, n, _ in out_cols],
        out_shape=[jax.ShapeDtypeStruct((bsz, seq, n), dt) for _, n, dt in out_cols],
        scratch_shapes=[pltpu.VMEM((4 * DIL_HEADS, tm, DIL_HD), F32)],
        compiler_params=pltpu.CompilerParams(
            dimension_semantics=("arbitrary", "arbitrary"), vmem_limit_bytes=VMEM_LIMIT),
        name="in_proj",
    )(x, mod_l, g_pre, *weights, cos, sin)


def _head_norm_gate(o, g, z):
    r = o * lax.rsqrt(jnp.mean(o * o, axis=-1, keepdims=True) + EPS)
    return r * g * _silu(z.astype(F32))


def _split2(v):
    h1 = v.astype(BF16)
    h2 = (v - h1.astype(F32)).astype(BF16)
    return h1, h2


def _gla_out_kernel(qa_ref, ka_ref, va_ref, za_ref, la_ref, g_ref, yb_ref, x_ref, mod_ref, wa_ref, wb_ref,
                    gpost_ref, o_ref, st_ref, ya_ref, *, chunks):
    @pl.when(pl.program_id(1) == 0)
    def _():
        st_ref[...] = jnp.zeros_like(st_ref)

    c_len = GLA_CHUNK
    grp = GLA_GROUP * c_len
    ri = lax.broadcasted_iota(jnp.int32, (grp, grp), 0)
    ci = lax.broadcasted_iota(jnp.int32, (grp, grp), 1)
    same_chunk = (ri & -c_len) == (ci & -c_len)
    tril_bd = ((ri >= ci) & same_chunk).astype(BF16)
    causal = (lax.broadcasted_iota(jnp.int32, (c_len, c_len), 0)
              >= lax.broadcasted_iota(jnp.int32, (c_len, c_len), 1))
    heads = range(GLA_HEADS)
    kcol = [slice(hd * GLA_DK, (hd + 1) * GLA_DK) for hd in heads]
    vcol = [slice(hd * GLA_DV, (hd + 1) * GLA_DV) for hd in heads]

    def prep(r0):
        rows = pl.ds(r0, grp)
        h1, h2 = _split2(la_ref[rows, :])
        b = _dot(tril_bd, h1) + _dot(tril_bd, h2)
        decay = [jnp.exp(b[(c + 1) * c_len - 1:(c + 1) * c_len, :]) for c in range(GLA_GROUP)]
        decay_rows = jnp.concatenate([jnp.broadcast_to(d, (c_len, GLA_QK)) for d in decay], axis=0)
        k_e32 = ka_ref[rows, :].astype(F32) * jnp.exp(-b)
        q_e = (qa_ref[rows, :].astype(F32) * jnp.exp(b)).astype(BF16)
        k_end = (k_e32 * decay_rows).astype(BF16)
        v = [[va_ref[pl.ds(r0 + c * c_len, c_len), vcol[hd]] for hd in heads] for c in range(GLA_GROUP)]
        return dict(r0=r0, q_e=q_e, k_e=k_e32.astype(BF16), k_end=k_end, decay=decay, v=v)

    crow = [slice(c * c_len, (c + 1) * c_len) for c in range(GLA_GROUP)]

    def intra(p):
        q_e, k_e, k_end, v = p["q_e"], p["k_e"], p["k_end"], p["v"]
        a = [[jnp.where(causal, _dot_nt(q_e[crow[c], kcol[hd]], k_e[crow[c], kcol[hd]]), 0.0).astype(BF16)
              for hd in heads] for c in range(GLA_GROUP)]
        p["inc"] = [[_dot_tn(v[c][hd], k_end[crow[c], kcol[hd]]) for hd in heads] for c in range(GLA_GROUP)]
        p["o"] = [[_dot(a[c][hd], v[c][hd]) for hd in heads] for c in range(GLA_GROUP)]

    def inter(p, st):
        for hd in heads:
            for c in range(GLA_GROUP):
                p["o"][c][hd] = p["o"][c][hd] + _dot_nt(p["q_e"][crow[c], kcol[hd]], st[hd].astype(BF16))
                st[hd] = st[hd] * p["decay"][c][:, kcol[hd]] + p["inc"][c][hd]

    def epilogue(p):
        for c in range(GLA_GROUP):
            for hd in heads:
                out_rows = pl.ds(p["r0"] + c * c_len, c_len)
                ya_ref[out_rows, vcol[hd]] = _head_norm_gate(
                    p["o"][c][hd], g_ref[:, vcol[hd]], za_ref[out_rows, vcol[hd]]).astype(ya_ref.dtype)

    def project(p):
        rows = pl.ds(p["r0"], grp)
        y = _dot(ya_ref[rows, :], wa_ref[...]) + _dot(yb_ref[rows, :], wb_ref[...])
        r = y * lax.rsqrt(jnp.mean(y * y, axis=-1, keepdims=True) + EPS) * gpost_ref[...]
        o_ref[rows, :] = x_ref[rows, :] + mod_ref[:, 2 * D_MODEL:3 * D_MODEL] * r

    st = [st_ref[hd] for hd in heads]
    groups = [prep(g * grp) for g in range(chunks // GLA_GROUP)]
    intra(groups[0])
    for g, p in enumerate(groups):
        inter(p, st)
        if g + 1 < len(groups):
            intra(groups[g + 1])
        epilogue(p)
        project(p)
    for hd in heads:
        st_ref[hd] = st[hd]


def _gla_out_proj(qa, ka, va, za, la, g_gla, yb, x, mod_l, w_out_a, w_out_b, g_post):
    bsz, seq, d = x.shape
    ts = ROW_TILE
    row = lambda b, i: (b, i, 0)
    const = lambda b, i: (0, 0)
    return pl.pallas_call(
        functools.partial(_gla_out_kernel, chunks=ts // GLA_CHUNK),
        grid=(bsz, seq // ts),
        in_specs=[
            pl.BlockSpec((None, ts, GLA_QK), row),
            pl.BlockSpec((None, ts, GLA_QK), row),
            pl.BlockSpec((None, ts, GLA_WIDTH), row),
            pl.BlockSpec((None, ts, GLA_WIDTH), row),
            pl.BlockSpec((None, ts, GLA_QK), row),
            pl.BlockSpec((1, GLA_WIDTH), const),
            pl.BlockSpec((None, ts, DIL_WIDTH), row),
            pl.BlockSpec((None, ts, d), row),
            pl.BlockSpec((None, 1, 3 * d), lambda b, i: (b, 0, 0)),
            pl.BlockSpec(w_out_a.shape, const),
            pl.BlockSpec(w_out_b.shape, const),
            pl.BlockSpec((1, d), const),
        ],
        out_specs=pl.BlockSpec((None, ts, d), row),
        out_shape=jax.ShapeDtypeStruct((bsz, seq, d), F32),
        scratch_shapes=[pltpu.VMEM((GLA_HEADS, GLA_DV, GLA_DK), F32), pltpu.VMEM((ts, GLA_WIDTH), BF16)],
        compiler_params=pltpu.CompilerParams(
            dimension_semantics=("arbitrary", "arbitrary"), vmem_limit_bytes=VMEM_LIMIT),
        name="gla_out_proj",
    )(qa, ka, va, za, la, g_gla, yb, x, mod_l, w_out_a, w_out_b, g_post)


def _dil_block_chunks(pattern, idx, seq):
    window, dil = DIL_PATTERNS[pattern]
    nb = seq // window
    lb = DIL_LB
    n_r = PERM_TILE // PERM_D
    if dil == PERM_D:
        return [((idx % nb) * PERM_TILE + (idx // nb) * n_r, 1)]
    if dil == 1:
        tile, part = idx // (PERM_TILE // lb), idx % (PERM_TILE // lb)
        rows = lb // PERM_D
        return [(tile * PERM_TILE + r * n_r + part * rows, 1) for r in range(PERM_D)]
    sub = dil // PERM_D
    res, n = idx // nb, idx % nb
    r4, c = res % PERM_D, res // PERM_D
    tiles = window // PERM_TILE
    return [(n * window + t * PERM_TILE + r4 * n_r + c, sub) for t in range(tiles)]


def _dil_kernel(q_ref, k_ref, v_ref, z_ref, g_ref, y_ref, qd, kd, vd, ynat,
                o1, o2, o3, m1, m2, m3, d1, d2, d3, *, seq):
    lb = DIL_LB
    n_blocks = seq // lb
    qi = lax.broadcasted_iota(jnp.int32, (lb, 2 * lb), 0)
    ki = lax.broadcasted_iota(jnp.int32, (lb, 2 * lb), 1)

    def biases(pos_in_block):
        dist = pos_in_block(qi) + lb - (pos_in_block(ki & (lb - 1)) + (ki & lb))
        band = (dist >= 0) & (dist <= lb)
        b_any = jnp.where(band, 0.0, NEG_INF)
        b_first = jnp.where(band & (ki >= lb), 0.0, NEG_INF)
        return b_any, b_first, b_first[:, lb:]

    step_order = biases(lambda a: a)
    rows_p1 = lb // PERM_D
    p1_order = biases(lambda a: PERM_D * (a % rows_p1) + a // rows_p1)

    kd[0] = jnp.zeros((DIL_HD, lb), BF16)
    vd[0:lb, :] = jnp.zeros((lb, DIL_HD), BF16)

    n_r = PERM_TILE // PERM_D

    def combine(t0):
        rows = pl.ds(t0, PERM_TILE)
        a1, a2, a3 = m1[rows, :], m2[rows, :], m3[rows, :]
        m = jnp.maximum(jnp.maximum(a1, a2), a3)
        e1, e2, e3 = jnp.exp(a1 - m), jnp.exp(a2 - m), jnp.exp(a3 - m)
        den = e1 * d1[rows, :] + e2 * d2[rows, :] + e3 * d3[rows, :]
        o = (e1 * o1[rows, :] + e2 * o2[rows, :] + e3 * o3[rows, :]) / den
        y = _head_norm_gate(o, g_ref[...], z_ref[rows, :])
        for r in range(PERM_D):
            ynat[pl.ds(t0 + r, n_r, stride=PERM_D), :] = y[r * n_r:(r + 1) * n_r, :]
        y_ref[rows, :] = ynat[rows, :].astype(y_ref.dtype)

    order = sorted(range(len(DIL_PATTERNS)), key=lambda i: -DIL_PATTERNS[i][1])
    assert DIL_PATTERNS[order[-1]][1] == 1
    for pat in order:
        window, dil = DIL_PATTERNS[pat]
        o_scr, m_scr, d_scr = (o1, o2, o3)[pat], (m1, m2, m3)[pat], (d1, d2, d3)[pat]
        nb = seq // window
        bias_any, bias_first, bias_cur = p1_order if dil == 1 else step_order

        def load_block(ref, idx, pat=pat):
            chunks = _dil_block_chunks(pat, idx, seq)
            rows = lb // len(chunks)
            parts = [ref[pl.ds(s0, rows) if st == 1 else pl.ds(s0, rows, stride=st), :] for s0, st in chunks]
            return parts[0] if len(parts) == 1 else jnp.concatenate(parts, axis=0)

        def store_block(ref, idx, val, pat=pat):
            chunks = _dil_block_chunks(pat, idx, seq)
            rows = lb // len(chunks)
            for i, (s0, st) in enumerate(chunks):
                dst = pl.ds(s0, rows) if st == 1 else pl.ds(s0, rows, stride=st)
                ref[dst, :] = val[i * rows:(i + 1) * rows, :]

        for idx in range(n_blocks):
            dst = pl.ds(lb + idx * lb, lb)
            qd[dst, :] = load_block(q_ref, idx).astype(BF16)
            kd[idx + 1] = load_block(k_ref, idx).T.astype(BF16)
            vd[dst, :] = load_block(v_ref, idx).astype(BF16)

        def scores(idx, nb=nb, bias_any=bias_any, bias_cur=bias_cur):
            q = qd[pl.ds(lb + idx * lb, lb), :]
            if idx % nb == 0:
                return _dot(q, kd[idx + 1]) + bias_cur, pl.ds(lb + idx * lb, lb)
            s = _dot(q, jnp.concatenate([kd[idx], kd[idx + 1]], axis=1)) + bias_any
            return s, pl.ds(idx * lb, 2 * lb)

        def softmax(s):
            m = jnp.max(s, axis=-1, keepdims=True)
            return jnp.exp((s - m).astype(BF16)), m

        def values(idx, p, m, keys, store_block=store_block, o_scr=o_scr, m_scr=m_scr, d_scr=d_scr):
            v_ones = jnp.concatenate([vd[keys, :], jnp.ones((p.shape[1], DIL_HD), BF16)], axis=1)
            acc = _dot(p, v_ones)
            store_block(o_scr, idx, acc[:, :DIL_HD])
            store_block(d_scr, idx, acc[:, DIL_HD:])
            store_block(m_scr, idx, jnp.broadcast_to(m, (lb, DIL_HD)))

        groups = [range(g * DIL_UNROLL, (g + 1) * DIL_UNROLL) for g in range(n_blocks // DIL_UNROLL)]
        pending = [scores(idx) for idx in groups[0]]
        for g, group in enumerate(groups):
            upcoming = [scores(idx) for idx in groups[g + 1]] if g + 1 < len(groups) else []
            probs = [softmax(s) for s, _ in pending]
            for idx, (p, m), (_, keys) in zip(group, probs, pending):
                values(idx, p, m, keys)
            pending = upcoming
            if dil == 1:
                for t0 in range(group[0] * lb, (group[-1] + 1) * lb, PERM_TILE):
                    combine(t0)


def _dilated(qb, kb, vb, zb, g_dil):
    bsz, seq, _ = qb.shape
    head = lambda b, h: (b, 0, h)
    blk = pl.BlockSpec((None, seq, DIL_HD), head)
    return pl.pallas_call(
        functools.partial(_dil_kernel, seq=seq),
        grid=(bsz, DIL_HEADS),
        in_specs=[blk, blk, blk, blk, pl.BlockSpec((1, DIL_HD), lambda b, h: (0, h))],
        out_specs=blk,
        out_shape=jax.ShapeDtypeStruct((bsz, seq, DIL_WIDTH), BF16),
        scratch_shapes=[pltpu.VMEM((seq + DIL_LB, DIL_HD), BF16),
                        pltpu.VMEM((seq // DIL_LB + 1, DIL_HD, DIL_LB), BF16),
                        pltpu.VMEM((seq + DIL_LB, DIL_HD), BF16)]
        + [pltpu.VMEM((seq, DIL_HD), F32) for _ in range(10)],
        compiler_params=pltpu.CompilerParams(
            dimension_semantics=("arbitrary", "arbitrary"), vmem_limit_bytes=VMEM_LIMIT),
        name="dilated",
    )(qb, kb, vb, zb, g_dil)


def _rope_tables(seq):
    inv_freq = ROPE_THETA ** (-jnp.arange(0, DIL_HD, 2, dtype=F32) / DIL_HD)
    ang = jnp.arange(seq, dtype=F32)[:, None] * inv_freq[None, :]
    cos, sin = jnp.cos(ang), jnp.sin(ang)
    return jnp.concatenate([cos, cos], axis=-1), jnp.concatenate([-sin, sin], axis=-1)


def _split_w_in(w_in_l, w_gate_up_l, b_gate_up_l):
    sizes = (GLA_QK, GLA_QK, GLA_WIDTH, GLA_WIDTH, GLA_LOWRANK,
             DIL_WIDTH, DIL_WIDTH, DIL_WIDTH, DIL_WIDTH)
    offs = np.cumsum((0,) + sizes)
    names = ("qa", "ka", "va", "za", "lr", "qb", "kb", "vb", "zb")
    w = {n: w_in_l[:, offs[i]:offs[i + 1]].astype(BF16) for i, n in enumerate(names)}
    w["lr"] = jnp.pad(w["lr"], ((0, 0), (0, LR_PAD - GLA_LOWRANK)))
    w["wg"] = jnp.pad(w_gate_up_l.astype(BF16), ((0, LR_PAD - GLA_LOWRANK), (0, 0)))
    w["bg"] = b_gate_up_l.reshape(1, GLA_QK)
    return w


def kernel(x, c, w_ada, b_ada, g_pre, w_in, w_gate_up, b_gate_up, g_gla, g_dil, w_out, g_post):
    bsz, seq, d = x.shape
    depth = w_ada.shape[0]
    mod = _modulation(c, w_ada, b_ada).reshape(depth, bsz, 1, 3 * d)
    cos, sin = _rope_tables(seq)
    for l in range(depth):
        w = _split_w_in(w_in[l], w_gate_up[l], b_gate_up[l])
        qa, ka, va, za, la, qb, kb, vb, zb = _in_proj(x, mod[l], g_pre[l].reshape(1, d), w, cos, sin)
        yb = _dilated(qb, kb, vb, zb, g_dil[l].reshape(1, DIL_WIDTH))
        w_o = w_out[l].astype(BF16)
        x = _gla_out_proj(qa, ka, va, za, la, g_gla[l].reshape(1, GLA_WIDTH), yb, x, mod[l],
                          w_o[:GLA_WIDTH], w_o[GLA_WIDTH:], g_post[l].reshape(1, d))
    return x
```

```python
import functools

import jax
import jax.numpy as jnp
import numpy as np
from jax import lax
from jax.experimental import pallas as pl
from jax.experimental.pallas import tpu as pltpu

F32 = jnp.float32
BF16 = jnp.bfloat16

D_MODEL = 1024
GLA_HEADS = 4
GLA_DK = 64
GLA_DV = 128
GLA_QK = GLA_HEADS * GLA_DK
GLA_WIDTH = GLA_HEADS * GLA_DV
GLA_LOWRANK = 16
GLA_TAU = 16.0
GLA_CHUNK = 64
GLA_GROUP = 4
DIL_HEADS = 4
DIL_HD = 128
DIL_WIDTH = DIL_HEADS * DIL_HD
DIL_PATTERNS = ((128, 1), (512, 4), (2048, 16))
DIL_LB = 128
PERM_TILE, PERM_D = DIL_PATTERNS[1]
ROPE_THETA = 10000.0
EPS = 1e-6
LANES = 128
LR_PAD = LANES

VMEM_LIMIT = 56 * 1024 * 1024
DIL_UNROLL = 8
ROW_TILE = 512
IN_PROJ_PARTS = 2
NEG_INF = float("-inf")


def _silu(v):
    return v * jax.nn.sigmoid(v)


def _dot(a, b):
    return jnp.dot(a, b, preferred_element_type=F32)


def _dot_nt(a, b):
    return lax.dot_general(a, b, (((1,), (1,)), ((), ())), preferred_element_type=F32)


def _dot_tn(a, b):
    return lax.dot_general(a, b, (((0,), (0,)), ((), ())), preferred_element_type=F32)


def _mod_kernel(c_ref, w_ref, b_ref, o_ref):
    sc = _silu(c_ref[...]).astype(BF16)
    o_ref[...] = _dot(sc, w_ref[...].astype(BF16)) + b_ref[...]


def _modulation(c, w_ada, b_ada):
    depth, d, e = w_ada.shape
    bsz = c.shape[0]
    nt = e // d
    return pl.pallas_call(
        _mod_kernel,
        grid=(depth, nt),
        in_specs=[
            pl.BlockSpec((bsz, d), lambda l, j: (0, 0)),
            pl.BlockSpec((None, d, d), lambda l, j: (l, 0, j)),
            pl.BlockSpec((None, 1, d), lambda l, j: (l, 0, j)),
        ],
        out_specs=pl.BlockSpec((None, bsz, d), lambda l, j: (l, 0, j)),
        out_shape=jax.ShapeDtypeStruct((depth, bsz, e), F32),
        compiler_params=pltpu.CompilerParams(
            dimension_semantics=("arbitrary", "arbitrary"), vmem_limit_bytes=VMEM_LIMIT),
        name="adaln_mod",
    )(c, w_ada, b_ada.reshape(depth, 1, e))


def _rope(v, cos, sin_signed):
    return v * cos + pltpu.roll(v, DIL_HD // 2, axis=1) * sin_signed


def _in_proj_kernel(x_ref, mod_ref, gpre_ref, wqa_ref, wka_ref, wva_ref, wza_ref, wlr_ref,
                    wqb_ref, wkb_ref, wvb_ref, wzb_ref, wg_ref, bg_ref, cos_ref, sin_ref,
                    qa_ref, ka_ref, va_ref, za_ref, la_ref, qb_ref, kb_ref, vb_ref, zb_ref, perm_ref):
    shift = mod_ref[:, 0:D_MODEL]
    gain = gpre_ref[...] * (1.0 + mod_ref[:, D_MODEL:2 * D_MODEL])
    n_res = PERM_TILE // PERM_D
    part = PERM_TILE // IN_PROJ_PARTS
    res_rows = part // PERM_D

    def store_residue_order(out_ref, hd, slab, p, val):
        perm_ref[slab] = val
        for r in range(PERM_D):
            dst = slice(r * n_res + p * res_rows, r * n_res + (p + 1) * res_rows)
            out_ref[hd, dst, :] = perm_ref[slab, pl.ds(r, res_rows, stride=PERM_D), :].astype(out_ref.dtype)

    for p in range(IN_PROJ_PARTS):
        rows = slice(p * part, (p + 1) * part)
        x = x_ref[rows, :]
        h = (x * lax.rsqrt(jnp.mean(x * x, axis=-1, keepdims=True) + EPS) * gain + shift).astype(BF16)

        qa_ref[rows, :] = (_dot(h, wqa_ref[...]) * (GLA_DK ** -0.5)).astype(qa_ref.dtype)
        ka_ref[rows, :] = _dot(h, wka_ref[...]).astype(ka_ref.dtype)
        va_ref[rows, :] = _dot(h, wva_ref[...]).astype(va_ref.dtype)
        za_ref[rows, :] = _dot(h, wza_ref[...]).astype(za_ref.dtype)

        lr = _dot(h, wlr_ref[...]).astype(BF16)
        z = _dot(lr, wg_ref[...]) + bg_ref[...]
        la_ref[rows, :] = (jnp.minimum(z, 0.0) - jnp.log1p(jnp.exp(-jnp.abs(z)))) / GLA_TAU

        cos = cos_ref[rows, :]
        sin = sin_ref[rows, :]
        q = _dot(h, wqb_ref[...])
        k = _dot(h, wkb_ref[...])
        v = _dot(h, wvb_ref[...])
        zg = _dot(h, wzb_ref[...])
        for hd in range(DIL_HEADS):
            cols = slice(hd * DIL_HD, (hd + 1) * DIL_HD)
            slab = (p * DIL_HEADS + hd) * 4
            store_residue_order(qb_ref, hd, slab, p, _rope(q[:, cols], cos, sin) * (DIL_HD ** -0.5))
            store_residue_order(kb_ref, hd, slab + 1, p, _rope(k[:, cols], cos, sin))
            store_residue_order(vb_ref, hd, slab + 2, p, v[:, cols])
            store_residue_order(zb_ref, hd, slab + 3, p, zg[:, cols])


def _in_proj(x, mod_l, g_pre, w, cos, sin):
    bsz, seq, d = x.shape
    tm = PERM_TILE
    row = lambda b, i: (b, i, 0)
    const = lambda b, i: (0, 0)

    def full(a):
        return pl.BlockSpec(a.shape, const)

    gla_cols = ((GLA_QK, BF16), (GLA_QK, BF16), (GLA_WIDTH, BF16), (GLA_WIDTH, BF16), (GLA_QK, F32))
    dil_types = (F32, F32, F32, BF16)
    weights = [w[k] for k in ("qa", "ka", "va", "za", "lr", "qb", "kb", "vb", "zb", "wg", "bg")]
    return pl.pallas_call(
        _in_proj_kernel,
        grid=(bsz, seq // tm),
        in_specs=[
            pl.BlockSpec((None, tm, d), row),
            pl.BlockSpec((None, 1, 3 * d), lambda b, i: (b, 0, 0)),
            full(g_pre),
            *[full(a) for a in weights],
            pl.BlockSpec((tm, DIL_HD), lambda b, i: (i, 0)),
            pl.BlockSpec((tm, DIL_HD), lambda b, i: (i, 0)),
        ],
        out_specs=[pl.BlockSpec((None, tm, n), row) for n, _ in gla_cols]
        + [pl.BlockSpec((None, DIL_HEADS, tm, DIL_HD), lambda b, i: (b, 0, i, 0)) for _ in dil_types],
        out_shape=[jax.ShapeDtypeStruct((bsz, seq, n), dt) for n, dt in gla_cols]
        + [jax.ShapeDtypeStruct((bsz, DIL_HEADS, seq, DIL_HD), dt) for dt in dil_types],
        scratch_shapes=[pltpu.VMEM((4 * DIL_HEADS * IN_PROJ_PARTS, tm // IN_PROJ_PARTS, DIL_HD), F32)],
        compiler_params=pltpu.CompilerParams(
            dimension_semantics=("arbitrary", "arbitrary"), vmem_limit_bytes=VMEM_LIMIT),
        name="in_proj",
    )(x, mod_l, g_pre, *weights, cos, sin)


def _head_norm_gate(o, g, z):
    r = o * lax.rsqrt(jnp.mean(o * o, axis=-1, keepdims=True) + EPS)
    return r * g * _silu(z.astype(F32))


def _split2(v):
    h1 = v.astype(BF16)
    h2 = (v - h1.astype(F32)).astype(BF16)
    return h1, h2


def _gla_out_kernel(qa_ref, ka_ref, va_ref, za_ref, la_ref, g_ref, yb_ref, x_ref, mod_ref, wa_ref, wb_ref,
                    gpost_ref, o_ref, st_ref, ya_ref, *, chunks):
    @pl.when(pl.program_id(1) == 0)
    def _():
        st_ref[...] = jnp.zeros_like(st_ref)

    c_len = GLA_CHUNK
    grp = GLA_GROUP * c_len
    ri = lax.broadcasted_iota(jnp.int32, (grp, grp), 0)
    ci = lax.broadcasted_iota(jnp.int32, (grp, grp), 1)
    same_chunk = (ri & -c_len) == (ci & -c_len)
    tril_bd = ((ri >= ci) & same_chunk).astype(BF16)
    causal = (lax.broadcasted_iota(jnp.int32, (c_len, c_len), 0)
              >= lax.broadcasted_iota(jnp.int32, (c_len, c_len), 1))
    heads = range(GLA_HEADS)
    kcol = [slice(hd * GLA_DK, (hd + 1) * GLA_DK) for hd in heads]
    vcol = [slice(hd * GLA_DV, (hd + 1) * GLA_DV) for hd in heads]

    def prep(r0):
        rows = pl.ds(r0, grp)
        h1, h2 = _split2(la_ref[rows, :])
        b = _dot(tril_bd, h1) + _dot(tril_bd, h2)
        decay = [jnp.exp(b[(c + 1) * c_len - 1:(c + 1) * c_len, :]) for c in range(GLA_GROUP)]
        decay_rows = jnp.concatenate([jnp.broadcast_to(d, (c_len, GLA_QK)) for d in decay], axis=0)
        k_e32 = ka_ref[rows, :].astype(F32) * jnp.exp(-b)
        q_e = (qa_ref[rows, :].astype(F32) * jnp.exp(b)).astype(BF16)
        k_end = (k_e32 * decay_rows).astype(BF16)
        v = [[va_ref[pl.ds(r0 + c * c_len, c_len), vcol[hd]] for hd in heads] for c in range(GLA_GROUP)]
        return dict(r0=r0, q_e=q_e, k_e=k_e32.astype(BF16), k_end=k_end, decay=decay, v=v)

    crow = [slice(c * c_len, (c + 1) * c_len) for c in range(GLA_GROUP)]

    def intra(p):
        q_e, k_e, k_end, v = p["q_e"], p["k_e"], p["k_end"], p["v"]
        a = [[jnp.where(causal, _dot_nt(q_e[crow[c], kcol[hd]], k_e[crow[c], kcol[hd]]), 0.0).astype(BF16)
              for hd in heads] for c in range(GLA_GROUP)]
        p["inc"] = [[_dot_tn(v[c][hd], k_end[crow[c], kcol[hd]]) for hd in heads] for c in range(GLA_GROUP)]
        p["o"] = [[_dot(a[c][hd], v[c][hd]) for hd in heads] for c in range(GLA_GROUP)]

    def inter(p, st):
        for hd in heads:
            for c in range(GLA_GROUP):
                p["o"][c][hd] = p["o"][c][hd] + _dot_nt(p["q_e"][crow[c], kcol[hd]], st[hd].astype(BF16))
                st[hd] = st[hd] * p["decay"][c][:, kcol[hd]] + p["inc"][c][hd]

    def epilogue(p):
        for c in range(GLA_GROUP):
            for hd in heads:
                out_rows = pl.ds(p["r0"] + c * c_len, c_len)
                ya_ref[out_rows, vcol[hd]] = _head_norm_gate(
                    p["o"][c][hd], g_ref[:, vcol[hd]], za_ref[out_rows, vcol[hd]]).astype(ya_ref.dtype)

    def project(p):
        rows = pl.ds(p["r0"], grp)
        y = _dot(ya_ref[rows, :], wa_ref[...]) + _dot(yb_ref[rows, :], wb_ref[...])
        r = y * lax.rsqrt(jnp.mean(y * y, axis=-1, keepdims=True) + EPS) * gpost_ref[...]
        o_ref[rows, :] = x_ref[rows, :] + mod_ref[:, 2 * D_MODEL:3 * D_MODEL] * r

    st = [st_ref[hd] for hd in heads]
    groups = [prep(g * grp) for g in range(chunks // GLA_GROUP)]
    intra(groups[0])
    for g, p in enumerate(groups):
        inter(p, st)
        if g + 1 < len(groups):
            intra(groups[g + 1])
        epilogue(p)
        project(p)
    for hd in heads:
        st_ref[hd] = st[hd]


def _gla_out_proj(qa, ka, va, za, la, g_gla, yb, x, mod_l, w_out_a, w_out_b, g_post):
    bsz, seq, d = x.shape
    ts = ROW_TILE
    row = lambda b, i: (b, i, 0)
    const = lambda b, i: (0, 0)
    return pl.pallas_call(
        functools.partial(_gla_out_kernel, chunks=ts // GLA_CHUNK),
        grid=(bsz, seq // ts),
        in_specs=[
            pl.BlockSpec((None, ts, GLA_QK), row),
            pl.BlockSpec((None, ts, GLA_QK), row),
            pl.BlockSpec((None, ts, GLA_WIDTH), row),
            pl.BlockSpec((None, ts, GLA_WIDTH), row),
            pl.BlockSpec((None, ts, GLA_QK), row),
            pl.BlockSpec((1, GLA_WIDTH), const),
            pl.BlockSpec((None, ts, DIL_WIDTH), row),
            pl.BlockSpec((None, ts, d), row),
            pl.BlockSpec((None, 1, 3 * d), lambda b, i: (b, 0, 0)),
            pl.BlockSpec(w_out_a.shape, const),
            pl.BlockSpec(w_out_b.shape, const),
            pl.BlockSpec((1, d), const),
        ],
        out_specs=pl.BlockSpec((None, ts, d), row),
        out_shape=jax.ShapeDtypeStruct((bsz, seq, d), F32),
        scratch_shapes=[pltpu.VMEM((GLA_HEADS, GLA_DV, GLA_DK), F32), pltpu.VMEM((ts, GLA_WIDTH), BF16)],
        compiler_params=pltpu.CompilerParams(
            dimension_semantics=("arbitrary", "arbitrary"), vmem_limit_bytes=VMEM_LIMIT),
        name="gla_out_proj",
    )(qa, ka, va, za, la, g_gla, yb, x, mod_l, w_out_a, w_out_b, g_post)


def _dil_block_chunks(pattern, idx, seq):
    window, dil = DIL_PATTERNS[pattern]
    nb = seq // window
    lb = DIL_LB
    n_r = PERM_TILE // PERM_D
    if dil == PERM_D:
        return [((idx % nb) * PERM_TILE + (idx // nb) * n_r, 1)]
    if dil == 1:
        tile, part = idx // (PERM_TILE // lb), idx % (PERM_TILE // lb)
        rows = lb // PERM_D
        return [(tile * PERM_TILE + r * n_r + part * rows, 1) for r in range(PERM_D)]
    sub = dil // PERM_D
    res, n = idx // nb, idx % nb
    r4, c = res % PERM_D, res // PERM_D
    tiles = window // PERM_TILE
    return [(n * window + t * PERM_TILE + r4 * n_r + c, sub) for t in range(tiles)]


def _dil_kernel(q_ref, k_ref, v_ref, z_ref, g_ref, y_ref, qd, kd, vd, ynat,
                o1, o2, o3, m1, m2, m3, d1, d2, d3, *, seq):
    lb = DIL_LB
    n_blocks = seq // lb
    qi = lax.broadcasted_iota(jnp.int32, (lb, 2 * lb), 0)
    ki = lax.broadcasted_iota(jnp.int32, (lb, 2 * lb), 1)

    def biases(pos_in_block):
        dist = pos_in_block(qi) + lb - (pos_in_block(ki & (lb - 1)) + (ki & lb))
        band = (dist >= 0) & (dist <= lb)
        b_any = jnp.where(band, 0.0, NEG_INF)
        b_first = jnp.where(band & (ki >= lb), 0.0, NEG_INF)
        return b_any, b_first, b_first[:, lb:]

    step_order = biases(lambda a: a)
    rows_p1 = lb // PERM_D
    p1_order = biases(lambda a: PERM_D * (a % rows_p1) + a // rows_p1)

    kd[0] = jnp.zeros((DIL_HD, lb), BF16)
    vd[0:lb, :] = jnp.zeros((lb, DIL_HD), BF16)

    n_r = PERM_TILE // PERM_D

    def combine(t0):
        rows = pl.ds(t0, PERM_TILE)
        a1, a2, a3 = m1[rows, :], m2[rows, :], m3[rows, :]
        m = jnp.maximum(jnp.maximum(a1, a2), a3)
        e1, e2, e3 = jnp.exp(a1 - m), jnp.exp(a2 - m), jnp.exp(a3 - m)
        den = e1 * d1[rows, :] + e2 * d2[rows, :] + e3 * d3[rows, :]
        o = (e1 * o1[rows, :] + e2 * o2[rows, :] + e3 * o3[rows, :]) / den
        y = _head_norm_gate(o, g_ref[...], z_ref[rows, :])
        for r in range(PERM_D):
            ynat[pl.ds(t0 + r, n_r, stride=PERM_D), :] = y[r * n_r:(r + 1) * n_r, :]
        y_ref[rows, :] = ynat[rows, :].astype(y_ref.dtype)

    order = sorted(range(len(DIL_PATTERNS)), key=lambda i: -DIL_PATTERNS[i][1])
    assert DIL_PATTERNS[order[-1]][1] == 1
    for pat in order:
        window, dil = DIL_PATTERNS[pat]
        o_scr, m_scr, d_scr = (o1, o2, o3)[pat], (m1, m2, m3)[pat], (d1, d2, d3)[pat]
        nb = seq // window
        bias_any, bias_first, bias_cur = p1_order if dil == 1 else step_order

        def load_block(ref, idx, pat=pat):
            chunks = _dil_block_chunks(pat, idx, seq)
            rows = lb // len(chunks)
            parts = [ref[pl.ds(s0, rows) if st == 1 else pl.ds(s0, rows, stride=st), :] for s0, st in chunks]
            return parts[0] if len(parts) == 1 else jnp.concatenate(parts, axis=0)

        def store_block(ref, idx, val, pat=pat):
            chunks = _dil_block_chunks(pat, idx, seq)
            rows = lb // len(chunks)
            for i, (s0, st) in enumerate(chunks):
                dst = pl.ds(s0, rows) if st == 1 else pl.ds(s0, rows, stride=st)
                ref[dst, :] = val[i * rows:(i + 1) * rows, :]

        for idx in range(n_blocks):
            dst = pl.ds(lb + idx * lb, lb)
            qd[dst, :] = load_block(q_ref, idx).astype(BF16)
            kd[idx + 1] = load_block(k_ref, idx).T.astype(BF16)
            vd[dst, :] = load_block(v_ref, idx).astype(BF16)

        def scores(idx, nb=nb, bias_any=bias_any, bias_cur=bias_cur):
            q = qd[pl.ds(lb + idx * lb, lb), :]
            if idx % nb == 0:
                return _dot(q, kd[idx + 1]) + bias_cur, pl.ds(lb + idx * lb, lb)
            s = _dot(q, jnp.concatenate([kd[idx], kd[idx + 1]], axis=1)) + bias_any
            return s, pl.ds(idx * lb, 2 * lb)

        def softmax(s):
            m = jnp.max(s, axis=-1, keepdims=True)
            return jnp.exp((s - m).astype(BF16)), m

        def values(idx, p, m, keys, store_block=store_block, o_scr=o_scr, m_scr=m_scr, d_scr=d_scr):
            v_ones = jnp.concatenate([vd[keys, :], jnp.ones((p.shape[1], DIL_HD), BF16)], axis=1)
            acc = _dot(p, v_ones)
            store_block(o_scr, idx, acc[:, :DIL_HD])
            store_block(d_scr, idx, acc[:, DIL_HD:])
            store_block(m_scr, idx, jnp.broadcast_to(m, (lb, DIL_HD)))

        groups = [range(g * DIL_UNROLL, (g + 1) * DIL_UNROLL) for g in range(n_blocks // DIL_UNROLL)]
        pending = [scores(idx) for idx in groups[0]]
        for g, group in enumerate(groups):
            upcoming = [scores(idx) for idx in groups[g + 1]] if g + 1 < len(groups) else []
            probs = [softmax(s) for s, _ in pending]
            for idx, (p, m), (_, keys) in zip(group, probs, pending):
                values(idx, p, m, keys)
            pending = upcoming
            if dil == 1:
                for t0 in range(group[0] * lb, (group[-1] + 1) * lb, PERM_TILE):
                    combine(t0)


def _dilated(qb, kb, vb, zb, g_dil):
    bsz, _, seq, _ = qb.shape
    blk = pl.BlockSpec((None, None, seq, DIL_HD), lambda b, h: (b, h, 0, 0))
    return pl.pallas_call(
        functools.partial(_dil_kernel, seq=seq),
        grid=(bsz, DIL_HEADS),
        in_specs=[blk, blk, blk, blk, pl.BlockSpec((1, DIL_HD), lambda b, h: (0, h))],
        out_specs=pl.BlockSpec((None, seq, DIL_HD), lambda b, h: (b, 0, h)),
        out_shape=jax.ShapeDtypeStruct((bsz, seq, DIL_WIDTH), BF16),
        scratch_shapes=[pltpu.VMEM((seq + DIL_LB, DIL_HD), BF16),
                        pltpu.VMEM((seq // DIL_LB + 1, DIL_HD, DIL_LB), BF16),
                        pltpu.VMEM((seq + DIL_LB, DIL_HD), BF16)]
        + [pltpu.VMEM((seq, DIL_HD), F32) for _ in range(10)],
        compiler_params=pltpu.CompilerParams(
            dimension_semantics=("arbitrary", "arbitrary"), vmem_limit_bytes=VMEM_LIMIT),
        name="dilated",
    )(qb, kb, vb, zb, g_dil)


def _rope_tables(seq):
    inv_freq = ROPE_THETA ** (-np.arange(0, DIL_HD, 2, dtype=np.float64) / DIL_HD)
    ang = np.arange(seq, dtype=np.float64)[:, None] * inv_freq[None, :]
    cos, sin = np.cos(ang), np.sin(ang)
    return (jnp.asarray(np.concatenate([cos, cos], axis=-1), F32),
            jnp.asarray(np.concatenate([-sin, sin], axis=-1), F32))


def _split_w_in(w_in_l, w_gate_up_l, b_gate_up_l):
    sizes = (GLA_QK, GLA_QK, GLA_WIDTH, GLA_WIDTH, GLA_LOWRANK,
             DIL_WIDTH, DIL_WIDTH, DIL_WIDTH, DIL_WIDTH)
    offs = np.cumsum((0,) + sizes)
    names = ("qa", "ka", "va", "za", "lr", "qb", "kb", "vb", "zb")
    w = {n: w_in_l[:, offs[i]:offs[i + 1]].astype(BF16) for i, n in enumerate(names)}
    w["lr"] = jnp.pad(w["lr"], ((0, 0), (0, LR_PAD - GLA_LOWRANK)))
    w["wg"] = jnp.pad(w_gate_up_l.astype(BF16), ((0, LR_PAD - GLA_LOWRANK), (0, 0)))
    w["bg"] = b_gate_up_l.reshape(1, GLA_QK)
    return w


def kernel(x, c, w_ada, b_ada, g_pre, w_in, w_gate_up, b_gate_up, g_gla, g_dil, w_out, g_post):
    bsz, seq, d = x.shape
    depth = w_ada.shape[0]
    mod = _modulation(c, w_ada, b_ada).reshape(depth, bsz, 1, 3 * d)
    cos, sin = _rope_tables(seq)
    for l in range(depth):
        w = _split_w_in(w_in[l], w_gate_up[l], b_gate_up[l])
        qa, ka, va, za, la, qb, kb, vb, zb = _in_proj(x, mod[l], g_pre[l].reshape(1, d), w, cos, sin)
        yb = _dilated(qb, kb, vb, zb, g_dil[l].reshape(1, DIL_WIDTH))
        w_o = w_out[l].astype(BF16)
        x = _gla_out_proj(qa, ka, va, za, la, g_gla[l].reshape(1, GLA_WIDTH), yb, x, mod[l],
                          w_o[:GLA_WIDTH], w_o[GLA_WIDTH:], g_post[l].reshape(1, d))
    return x
```

```python
import functools

import jax
import jax.numpy as jnp
import numpy as np
from jax import lax
from jax.experimental import pallas as pl
from jax.experimental.pallas import tpu as pltpu

F32 = jnp.float32
BF16 = jnp.bfloat16

D_MODEL = 1024
GLA_HEADS = 4
GLA_DK = 64
GLA_DV = 128
GLA_QK = GLA_HEADS * GLA_DK
GLA_WIDTH = GLA_HEADS * GLA_DV
GLA_LOWRANK = 16
GLA_TAU = 16.0
GLA_CHUNK = 64
GLA_GROUP = 4
DIL_HEADS = 4
DIL_HD = 128
DIL_WIDTH = DIL_HEADS * DIL_HD
DIL_PATTERNS = ((128, 1), (512, 4), (2048, 16))
DIL_LB = 128
PERM_TILE, PERM_D = DIL_PATTERNS[1]
ROPE_THETA = 10000.0
EPS = 1e-6
LANES = 128
LR_PAD = LANES

VMEM_LIMIT = 56 * 1024 * 1024
DIL_UNROLL = 8
ROW_TILE = 1024
NEG_INF = float("-inf")


def _silu(v):
    return v * jax.nn.sigmoid(v)


def _dot(a, b):
    return jnp.dot(a, b, preferred_element_type=F32)


def _dot_nt(a, b):
    return lax.dot_general(a, b, (((1,), (1,)), ((), ())), preferred_element_type=F32)


def _dot_tn(a, b):
    return lax.dot_general(a, b, (((0,), (0,)), ((), ())), preferred_element_type=F32)


def _mod_kernel(c_ref, w_ref, b_ref, o_ref):
    sc = _silu(c_ref[...]).astype(BF16)
    o_ref[...] = _dot(sc, w_ref[...].astype(BF16)) + b_ref[...]


def _modulation(c, w_ada, b_ada):
    depth, d, e = w_ada.shape
    bsz = c.shape[0]
    nt = e // d
    return pl.pallas_call(
        _mod_kernel,
        grid=(depth, nt),
        in_specs=[
            pl.BlockSpec((bsz, d), lambda l, j: (0, 0)),
            pl.BlockSpec((None, d, d), lambda l, j: (l, 0, j)),
            pl.BlockSpec((None, 1, d), lambda l, j: (l, 0, j)),
        ],
        out_specs=pl.BlockSpec((None, bsz, d), lambda l, j: (l, 0, j)),
        out_shape=jax.ShapeDtypeStruct((depth, bsz, e), F32),
        compiler_params=pltpu.CompilerParams(
            dimension_semantics=("arbitrary", "arbitrary"), vmem_limit_bytes=VMEM_LIMIT),
        name="adaln_mod",
    )(c, w_ada, b_ada.reshape(depth, 1, e))


def _rope(v, cos, sin_signed):
    return v * cos + pltpu.roll(v, DIL_HD // 2, axis=1) * sin_signed


def _in_proj_kernel(x_ref, mod_ref, gpre_ref, wqa_ref, wka_ref, wva_ref, wza_ref, wlr_ref,
                    wqb_ref, wkb_ref, wvb_ref, wzb_ref, wg_ref, bg_ref, cos_ref, sin_ref,
                    qa_ref, ka_ref, va_ref, za_ref, la_ref, qb_ref, kb_ref, vb_ref, zb_ref, perm_ref):
    shift = mod_ref[:, 0:D_MODEL]
    gain = gpre_ref[...] * (1.0 + mod_ref[:, D_MODEL:2 * D_MODEL])
    n_res = PERM_TILE // PERM_D

    def store_residue_order(out_ref, hd, slab, t0, val):
        perm_ref[slab] = val
        for r in range(PERM_D):
            dst = slice(t0 + r * n_res, t0 + (r + 1) * n_res)
            out_ref[hd, dst, :] = perm_ref[slab, pl.ds(r, n_res, stride=PERM_D), :].astype(out_ref.dtype)

    for t0 in range(0, x_ref.shape[0], PERM_TILE):
        rows = slice(t0, t0 + PERM_TILE)
        x = x_ref[rows, :]
        h = (x * lax.rsqrt(jnp.mean(x * x, axis=-1, keepdims=True) + EPS) * gain + shift).astype(BF16)

        qa_ref[rows, :] = (_dot(h, wqa_ref[...]) * (GLA_DK ** -0.5)).astype(qa_ref.dtype)
        ka_ref[rows, :] = _dot(h, wka_ref[...]).astype(ka_ref.dtype)
        va_ref[rows, :] = _dot(h, wva_ref[...]).astype(va_ref.dtype)
        za_ref[rows, :] = _dot(h, wza_ref[...]).astype(za_ref.dtype)

        lr = _dot(h, wlr_ref[...]).astype(BF16)
        z = _dot(lr, wg_ref[...]) + bg_ref[...]
        la_ref[rows, :] = (jnp.minimum(z, 0.0) - jnp.log1p(jnp.exp(-jnp.abs(z)))) / GLA_TAU

        cos = cos_ref[rows, :]
        sin = sin_ref[rows, :]
        q = _dot(h, wqb_ref[...])
        k = _dot(h, wkb_ref[...])
        v = _dot(h, wvb_ref[...])
        zg = _dot(h, wzb_ref[...])
        for hd in range(DIL_HEADS):
            cols = slice(hd * DIL_HD, (hd + 1) * DIL_HD)
            store_residue_order(qb_ref, hd, 4 * hd, t0, _rope(q[:, cols], cos, sin) * (DIL_HD ** -0.5))
            store_residue_order(kb_ref, hd, 4 * hd + 1, t0, _rope(k[:, cols], cos, sin))
            store_residue_order(vb_ref, hd, 4 * hd + 2, t0, v[:, cols])
            store_residue_order(zb_ref, hd, 4 * hd + 3, t0, zg[:, cols])


def _in_proj(x, mod_l, g_pre, w, cos, sin):
    bsz, seq, d = x.shape
    tm = ROW_TILE
    row = lambda b, i: (b, i, 0)
    const = lambda b, i: (0, 0)

    def full(a):
        return pl.BlockSpec(a.shape, const, pipeline_mode=pl.Buffered(1))

    gla_cols = ((GLA_QK, BF16), (GLA_QK, BF16), (GLA_WIDTH, BF16), (GLA_WIDTH, BF16), (GLA_QK, F32))
    dil_types = (F32, F32, F32, BF16)
    weights = [w[k] for k in ("qa", "ka", "va", "za", "lr", "qb", "kb", "vb", "zb", "wg", "bg")]
    return pl.pallas_call(
        _in_proj_kernel,
        grid=(bsz, seq // tm),
        in_specs=[
            pl.BlockSpec((None, tm, d), row),
            pl.BlockSpec((None, 1, 3 * d), lambda b, i: (b, 0, 0)),
            full(g_pre),
            *[full(a) for a in weights],
            pl.BlockSpec((tm, DIL_HD), lambda b, i: (i, 0)),
            pl.BlockSpec((tm, DIL_HD), lambda b, i: (i, 0)),
        ],
        out_specs=[pl.BlockSpec((None, tm, n), row) for n, _ in gla_cols]
        + [pl.BlockSpec((None, DIL_HEADS, tm, DIL_HD), lambda b, i: (b, 0, i, 0)) for _ in dil_types],
        out_shape=[jax.ShapeDtypeStruct((bsz, seq, n), dt) for n, dt in gla_cols]
        + [jax.ShapeDtypeStruct((bsz, DIL_HEADS, seq, DIL_HD), dt) for dt in dil_types],
        scratch_shapes=[pltpu.VMEM((4 * DIL_HEADS, PERM_TILE, DIL_HD), F32)],
        compiler_params=pltpu.CompilerParams(
            dimension_semantics=("arbitrary", "arbitrary"), vmem_limit_bytes=VMEM_LIMIT),
        name="in_proj",
    )(x, mod_l, g_pre, *weights, cos, sin)


def _head_norm_gate(o, g, z):
    r = o * lax.rsqrt(jnp.mean(o * o, axis=-1, keepdims=True) + EPS)
    return r * g * _silu(z.astype(F32))


def _split2(v):
    h1 = v.astype(BF16)
    h2 = (v - h1.astype(F32)).astype(BF16)
    return h1, h2


def _gla_out_kernel(qa_ref, ka_ref, va_ref, za_ref, la_ref, g_ref, yb_ref, x_ref, mod_ref, wa_ref, wb_ref,
                    gpost_ref, o_ref, st_ref, ya_ref, *, chunks):
    @pl.when(pl.program_id(1) == 0)
    def _():
        st_ref[...] = jnp.zeros_like(st_ref)

    c_len = GLA_CHUNK
    grp = GLA_GROUP * c_len
    ri = lax.broadcasted_iota(jnp.int32, (grp, grp), 0)
    ci = lax.broadcasted_iota(jnp.int32, (grp, grp), 1)
    same_chunk = (ri & -c_len) == (ci & -c_len)
    tril_bd = ((ri >= ci) & same_chunk).astype(BF16)
    causal = (lax.broadcasted_iota(jnp.int32, (c_len, c_len), 0)
              >= lax.broadcasted_iota(jnp.int32, (c_len, c_len), 1))
    heads = range(GLA_HEADS)
    kcol = [slice(hd * GLA_DK, (hd + 1) * GLA_DK) for hd in heads]
    vcol = [slice(hd * GLA_DV, (hd + 1) * GLA_DV) for hd in heads]

    def prep(r0):
        rows = pl.ds(r0, grp)
        h1, h2 = _split2(la_ref[rows, :])
        b = _dot(tril_bd, h1) + _dot(tril_bd, h2)
        decay = [jnp.exp(b[(c + 1) * c_len - 1:(c + 1) * c_len, :]) for c in range(GLA_GROUP)]
        decay_rows = jnp.concatenate([jnp.broadcast_to(d, (c_len, GLA_QK)) for d in decay], axis=0)
        k_e32 = ka_ref[rows, :].astype(F32) * jnp.exp(-b)
        q_e = (qa_ref[rows, :].astype(F32) * jnp.exp(b)).astype(BF16)
        k_end = (k_e32 * decay_rows).astype(BF16)
        v = [[va_ref[pl.ds(r0 + c * c_len, c_len), vcol[hd]] for hd in heads] for c in range(GLA_GROUP)]
        return dict(r0=r0, q_e=q_e, k_e=k_e32.astype(BF16), k_end=k_end, decay=decay, v=v)

    crow = [slice(c * c_len, (c + 1) * c_len) for c in range(GLA_GROUP)]

    def intra(p):
        q_e, k_e, k_end, v = p["q_e"], p["k_e"], p["k_end"], p["v"]
        a = [[jnp.where(causal, _dot_nt(q_e[crow[c], kcol[hd]], k_e[crow[c], kcol[hd]]), 0.0).astype(BF16)
              for hd in heads] for c in range(GLA_GROUP)]
        p["inc"] = [[_dot_tn(v[c][hd], k_end[crow[c], kcol[hd]]) for hd in heads] for c in range(GLA_GROUP)]
        p["o"] = [[_dot(a[c][hd], v[c][hd]) for hd in heads] for c in range(GLA_GROUP)]

    def inter(p, st):
        for hd in heads:
            for c in range(GLA_GROUP):
                p["o"][c][hd] = p["o"][c][hd] + _dot_nt(p["q_e"][crow[c], kcol[hd]], st[hd].astype(BF16))
                st[hd] = st[hd] * p["decay"][c][:, kcol[hd]] + p["inc"][c][hd]

    def epilogue(p):
        for c in range(GLA_GROUP):
            for hd in heads:
                out_rows = pl.ds(p["r0"] + c * c_len, c_len)
                ya_ref[out_rows, vcol[hd]] = _head_norm_gate(
                    p["o"][c][hd], g_ref[:, vcol[hd]], za_ref[out_rows, vcol[hd]]).astype(ya_ref.dtype)

    def project(p):
        rows = pl.ds(p["r0"], grp)
        y = _dot(ya_ref[rows, :], wa_ref[...]) + _dot(yb_ref[rows, :], wb_ref[...])
        r = y * lax.rsqrt(jnp.mean(y * y, axis=-1, keepdims=True) + EPS) * gpost_ref[...]
        o_ref[rows, :] = x_ref[rows, :] + mod_ref[:, 2 * D_MODEL:3 * D_MODEL] * r

    st = [st_ref[hd] for hd in heads]
    groups = [prep(g * grp) for g in range(chunks // GLA_GROUP)]
    intra(groups[0])
    for g, p in enumerate(groups):
        inter(p, st)
        if g + 1 < len(groups):
            intra(groups[g + 1])
        epilogue(p)
        project(p)
    for hd in heads:
        st_ref[hd] = st[hd]


def _gla_out_proj(qa, ka, va, za, la, g_gla, yb, x, mod_l, w_out_a, w_out_b, g_post):
    bsz, seq, d = x.shape
    ts = ROW_TILE
    row = lambda b, i: (b, i, 0)
    const = lambda b, i: (0, 0)
    return pl.pallas_call(
        functools.partial(_gla_out_kernel, chunks=ts // GLA_CHUNK),
        grid=(bsz, seq // ts),
        in_specs=[
            pl.BlockSpec((None, ts, GLA_QK), row),
            pl.BlockSpec((None, ts, GLA_QK), row),
            pl.BlockSpec((None, ts, GLA_WIDTH), row),
            pl.BlockSpec((None, ts, GLA_WIDTH), row),
            pl.BlockSpec((None, ts, GLA_QK), row),
            pl.BlockSpec((1, GLA_WIDTH), const),
            pl.BlockSpec((None, ts, DIL_WIDTH), row),
            pl.BlockSpec((None, ts, d), row),
            pl.BlockSpec((None, 1, 3 * d), lambda b, i: (b, 0, 0)),
            pl.BlockSpec(w_out_a.shape, const),
            pl.BlockSpec(w_out_b.shape, const),
            pl.BlockSpec((1, d), const),
        ],
        out_specs=pl.BlockSpec((None, ts, d), row),
        out_shape=jax.ShapeDtypeStruct((bsz, seq, d), F32),
        scratch_shapes=[pltpu.VMEM((GLA_HEADS, GLA_DV, GLA_DK), F32), pltpu.VMEM((ts, GLA_WIDTH), BF16)],
        compiler_params=pltpu.CompilerParams(
            dimension_semantics=("arbitrary", "arbitrary"), vmem_limit_bytes=VMEM_LIMIT),
        name="gla_out_proj",
    )(qa, ka, va, za, la, g_gla, yb, x, mod_l, w_out_a, w_out_b, g_post)


def _dil_block_chunks(pattern, idx, seq):
    window, dil = DIL_PATTERNS[pattern]
    nb = seq // window
    lb = DIL_LB
    n_r = PERM_TILE // PERM_D
    if dil == PERM_D:
        return [((idx % nb) * PERM_TILE + (idx // nb) * n_r, 1)]
    if dil == 1:
        tile, part = idx // (PERM_TILE // lb), idx % (PERM_TILE // lb)
        rows = lb // PERM_D
        return [(tile * PERM_TILE + r * n_r + part * rows, 1) for r in range(PERM_D)]
    sub = dil // PERM_D
    res, n = idx // nb, idx % nb
    r4, c = res % PERM_D, res // PERM_D
    tiles = window // PERM_TILE
    return [(n * window + t * PERM_TILE + r4 * n_r + c, sub) for t in range(tiles)]


def _dil_kernel(q_ref, k_ref, v_ref, z_ref, g_ref, y_ref, qd, kd, vd, ynat,
                o1, o2, o3, m1, m2, m3, d1, d2, d3, *, seq):
    lb = DIL_LB
    n_blocks = seq // lb
    qi = lax.broadcasted_iota(jnp.int32, (lb, 2 * lb), 0)
    ki = lax.broadcasted_iota(jnp.int32, (lb, 2 * lb), 1)

    def biases(pos_in_block):
        dist = pos_in_block(qi) + lb - (pos_in_block(ki & (lb - 1)) + (ki & lb))
        band = (dist >= 0) & (dist <= lb)
        b_any = jnp.where(band, 0.0, NEG_INF)
        b_first = jnp.where(band & (ki >= lb), 0.0, NEG_INF)
        return b_any, b_first, b_first[:, lb:]

    step_order = biases(lambda a: a)
    rows_p1 = lb // PERM_D
    p1_order = biases(lambda a: PERM_D * (a % rows_p1) + a // rows_p1)

    kd[0] = jnp.zeros((DIL_HD, lb), BF16)
    vd[0:lb, :] = jnp.zeros((lb, DIL_HD), BF16)

    n_r = PERM_TILE // PERM_D

    def combine(t0):
        rows = pl.ds(t0, PERM_TILE)
        a1, a2, a3 = m1[rows, :], m2[rows, :], m3[rows, :]
        m = jnp.maximum(jnp.maximum(a1, a2), a3)
        e1, e2, e3 = jnp.exp(a1 - m), jnp.exp(a2 - m), jnp.exp(a3 - m)
        den = e1 * d1[rows, :] + e2 * d2[rows, :] + e3 * d3[rows, :]
        o = (e1 * o1[rows, :] + e2 * o2[rows, :] + e3 * o3[rows, :]) / den
        y = _head_norm_gate(o, g_ref[...], z_ref[rows, :])
        for r in range(PERM_D):
            ynat[pl.ds(t0 + r, n_r, stride=PERM_D), :] = y[r * n_r:(r + 1) * n_r, :]
        y_ref[rows, :] = ynat[rows, :].astype(y_ref.dtype)

    order = sorted(range(len(DIL_PATTERNS)), key=lambda i: -DIL_PATTERNS[i][1])
    assert DIL_PATTERNS[order[-1]][1] == 1
    for pat in order:
        window, dil = DIL_PATTERNS[pat]
        o_scr, m_scr, d_scr = (o1, o2, o3)[pat], (m1, m2, m3)[pat], (d1, d2, d3)[pat]
        nb = seq // window
        bias_any, bias_first, bias_cur = p1_order if dil == 1 else step_order

        def load_block(ref, idx, pat=pat):
            chunks = _dil_block_chunks(pat, idx, seq)
            rows = lb // len(chunks)
            parts = [ref[pl.ds(s0, rows) if st == 1 else pl.ds(s0, rows, stride=st), :] for s0, st in chunks]
            return parts[0] if len(parts) == 1 else jnp.concatenate(parts, axis=0)

        def store_block(ref, idx, val, pat=pat):
            chunks = _dil_block_chunks(pat, idx, seq)
            rows = lb // len(chunks)
            for i, (s0, st) in enumerate(chunks):
                dst = pl.ds(s0, rows) if st == 1 else pl.ds(s0, rows, stride=st)
                ref[dst, :] = val[i * rows:(i + 1) * rows, :]

        for idx in range(n_blocks):
            dst = pl.ds(lb + idx * lb, lb)
            qd[dst, :] = load_block(q_ref, idx).astype(BF16)
            kd[idx + 1] = load_block(k_ref, idx).T.astype(BF16)
            vd[dst, :] = load_block(v_ref, idx).astype(BF16)

        def scores(idx, nb=nb, bias_any=bias_any, bias_cur=bias_cur):
            q = qd[pl.ds(lb + idx * lb, lb), :]
            if idx % nb == 0:
                return _dot(q, kd[idx + 1]) + bias_cur, pl.ds(lb + idx * lb, lb)
            s = _dot(q, jnp.concatenate([kd[idx], kd[idx + 1]], axis=1)) + bias_any
            return s, pl.ds(idx * lb, 2 * lb)

        def softmax(s):
            m = jnp.max(s, axis=-1, keepdims=True)
            return jnp.exp((s - m).astype(BF16)), m

        def values(idx, p, m, keys, store_block=store_block, o_scr=o_scr, m_scr=m_scr, d_scr=d_scr):
            v_ones = jnp.concatenate([vd[keys, :], jnp.ones((p.shape[1], DIL_HD), BF16)], axis=1)
            acc = _dot(p, v_ones)
            store_block(o_scr, idx, acc[:, :DIL_HD])
            store_block(d_scr, idx, acc[:, DIL_HD:])
            store_block(m_scr, idx, jnp.broadcast_to(m, (lb, DIL_HD)))

        groups = [range(g * DIL_UNROLL, (g + 1) * DIL_UNROLL) for g in range(n_blocks // DIL_UNROLL)]
        pending = [scores(idx) for idx in groups[0]]
        for g, group in enumerate(groups):
            upcoming = [scores(idx) for idx in groups[g + 1]] if g + 1 < len(groups) else []
            probs = [softmax(s) for s, _ in pending]
            for idx, (p, m), (_, keys) in zip(group, probs, pending):
                values(idx, p, m, keys)
            pending = upcoming
            if dil == 1:
                for t0 in range(group[0] * lb, (group[-1] + 1) * lb, PERM_TILE):
                    combine(t0)


def _dilated(qb, kb, vb, zb, g_dil):
    bsz, _, seq, _ = qb.shape
    blk = pl.BlockSpec((None, None, seq, DIL_HD), lambda b, h: (b, h, 0, 0))
    return pl.pallas_call(
        functools.partial(_dil_kernel, seq=seq),
        grid=(bsz, DIL_HEADS),
        in_specs=[blk, blk, blk, blk, pl.BlockSpec((1, DIL_HD), lambda b, h: (0, h))],
        out_specs=pl.BlockSpec((None, seq, DIL_HD), lambda b, h: (b, 0, h)),
        out_shape=jax.ShapeDtypeStruct((bsz, seq, DIL_WIDTH), BF16),
        scratch_shapes=[pltpu.VMEM((seq + DIL_LB, DIL_HD), BF16),
                        pltpu.VMEM((seq // DIL_LB + 1, DIL_HD, DIL_LB), BF16),
                        pltpu.VMEM((seq + DIL_LB, DIL_HD), BF16)]
        + [pltpu.VMEM((seq, DIL_HD), F32) for _ in range(10)],
        compiler_params=pltpu.CompilerParams(
            dimension_semantics=("arbitrary", "arbitrary"), vmem_limit_bytes=VMEM_LIMIT),
        name="dilated",
    )(qb, kb, vb, zb, g_dil)


def _rope_tables(seq):
    inv_freq = ROPE_THETA ** (-np.arange(0, DIL_HD, 2, dtype=np.float64) / DIL_HD)
    ang = np.arange(seq, dtype=np.float64)[:, None] * inv_freq[None, :]
    cos, sin = np.cos(ang), np.sin(ang)
    return (jnp.asarray(np.concatenate([cos, cos], axis=-1), F32),
            jnp.asarray(np.concatenate([-sin, sin], axis=-1), F32))


def _split_w_in(w_in_l, w_gate_up_l, b_gate_up_l):
    sizes = (GLA_QK, GLA_QK, GLA_WIDTH, GLA_WIDTH, GLA_LOWRANK,
             DIL_WIDTH, DIL_WIDTH, DIL_WIDTH, DIL_WIDTH)
    offs = np.cumsum((0,) + sizes)
    names = ("qa", "ka", "va", "za", "lr", "qb", "kb", "vb", "zb")
    w = {n: w_in_l[:, offs[i]:offs[i + 1]].astype(BF16) for i, n in enumerate(names)}
    w["lr"] = jnp.pad(w["lr"], ((0, 0), (0, LR_PAD - GLA_LOWRANK)))
    w["wg"] = jnp.pad(w_gate_up_l.astype(BF16), ((0, LR_PAD - GLA_LOWRANK), (0, 0)))
    w["bg"] = b_gate_up_l.reshape(1, GLA_QK)
    return w


def kernel(x, c, w_ada, b_ada, g_pre, w_in, w_gate_up, b_gate_up, g_gla, g_dil, w_out, g_post):
    bsz, seq, d = x.shape
    depth = w_ada.shape[0]
    mod = _modulation(c, w_ada, b_ada).reshape(depth, bsz, 1, 3 * d)
    cos, sin = _rope_tables(seq)
    for l in range(depth):
        w = _split_w_in(w_in[l], w_gate_up[l], b_gate_up[l])
        qa, ka, va, za, la, qb, kb, vb, zb = _in_proj(x, mod[l], g_pre[l].reshape(1, d), w, cos, sin)
        yb = _dilated(qb, kb, vb, zb, g_dil[l].reshape(1, DIL_WIDTH))
        w_o = w_out[l].astype(BF16)
        x = _gla_out_proj(qa, ka, va, za, la, g_gla[l].reshape(1, GLA_WIDTH), yb, x, mod[l],
                          w_o[:GLA_WIDTH], w_o[GLA_WIDTH:], g_post[l].reshape(1, d))
    return x
```

```python
import functools
import math

import jax
import jax.numpy as jnp
import numpy as np
from jax import lax
from jax.experimental import pallas as pl
from jax.experimental.pallas import tpu as pltpu

F32 = jnp.float32
BF16 = jnp.bfloat16

D_MODEL = 1024
GLA_HEADS = 4
GLA_DK = 64
GLA_DV = 128
GLA_QK = GLA_HEADS * GLA_DK
GLA_WIDTH = GLA_HEADS * GLA_DV
GLA_LOWRANK = 16
GLA_TAU = 16.0
GLA_CHUNK = 64
GLA_GROUP = 4
DIL_HEADS = 4
DIL_HD = 128
DIL_WIDTH = DIL_HEADS * DIL_HD
DIL_PATTERNS = ((128, 1), (512, 4), (2048, 16))
DIL_LB = 128
PERM_TILE, PERM_D = DIL_PATTERNS[1]
ROPE_THETA = 10000.0
EPS = 1e-6
LANES = 128
LR_PAD = LANES

VMEM_LIMIT = 56 * 1024 * 1024
DIL_UNROLL = 8
ROW_TILE = 1024
NEG_INF = float("-inf")
LOG2E = math.log2(math.e)


def _silu(v):
    return v * jax.nn.sigmoid(v)


def _dot(a, b):
    return jnp.dot(a, b, preferred_element_type=F32)


def _dot_nt(a, b):
    return lax.dot_general(a, b, (((1,), (1,)), ((), ())), preferred_element_type=F32)


def _dot_tn(a, b):
    return lax.dot_general(a, b, (((0,), (0,)), ((), ())), preferred_element_type=F32)


def _mod_kernel(c_ref, w_ref, b_ref, o_ref):
    sc = _silu(c_ref[...]).astype(BF16)
    o_ref[...] = _dot(sc, w_ref[...].astype(BF16)) + b_ref[...]


def _modulation(c, w_ada, b_ada):
    depth, d, e = w_ada.shape
    bsz = c.shape[0]
    nt = e // d
    return pl.pallas_call(
        _mod_kernel,
        grid=(depth, nt),
        in_specs=[
            pl.BlockSpec((bsz, d), lambda l, j: (0, 0)),
            pl.BlockSpec((None, d, d), lambda l, j: (l, 0, j)),
            pl.BlockSpec((None, 1, d), lambda l, j: (l, 0, j)),
        ],
        out_specs=pl.BlockSpec((None, bsz, d), lambda l, j: (l, 0, j)),
        out_shape=jax.ShapeDtypeStruct((depth, bsz, e), F32),
        compiler_params=pltpu.CompilerParams(
            dimension_semantics=("arbitrary", "arbitrary"), vmem_limit_bytes=VMEM_LIMIT),
        name="adaln_mod",
    )(c, w_ada, b_ada.reshape(depth, 1, e))


def _rope(v, cos, sin_signed):
    return v * cos + pltpu.roll(v, DIL_HD // 2, axis=1) * sin_signed


def _in_proj_kernel(x_ref, mod_ref, gpre_ref, wqa_ref, wka_ref, wva_ref, wza_ref, wlr_ref,
                    wqb_ref, wkb_ref, wvb_ref, wzb_ref, wg_ref, bg_ref, cos_ref, sin_ref,
                    qa_ref, ka_ref, va_ref, za_ref, la_ref, qb_ref, kb_ref, vb_ref, zb_ref, perm_ref):
    shift = mod_ref[:, 0:D_MODEL]
    gain = gpre_ref[...] * (1.0 + mod_ref[:, D_MODEL:2 * D_MODEL])
    n_res = PERM_TILE // PERM_D

    def store_residue_order(out_ref, hd, slab, t0, val):
        perm_ref[slab] = val
        for r in range(PERM_D):
            dst = slice(t0 + r * n_res, t0 + (r + 1) * n_res)
            out_ref[hd, dst, :] = perm_ref[slab, pl.ds(r, n_res, stride=PERM_D), :].astype(out_ref.dtype)

    for t0 in range(0, x_ref.shape[0], PERM_TILE):
        rows = slice(t0, t0 + PERM_TILE)
        x = x_ref[rows, :]
        h = (x * lax.rsqrt(jnp.mean(x * x, axis=-1, keepdims=True) + EPS) * gain + shift).astype(BF16)

        qa_ref[rows, :] = (_dot(h, wqa_ref[...]) * (GLA_DK ** -0.5)).astype(qa_ref.dtype)
        ka_ref[rows, :] = _dot(h, wka_ref[...]).astype(ka_ref.dtype)
        va_ref[rows, :] = _dot(h, wva_ref[...]).astype(va_ref.dtype)
        za_ref[rows, :] = _dot(h, wza_ref[...]).astype(za_ref.dtype)

        lr = _dot(h, wlr_ref[...]).astype(BF16)
        z = _dot(lr, wg_ref[...]) + bg_ref[...]
        la_ref[rows, :] = (jnp.minimum(z, 0.0) - jnp.log1p(jnp.exp(-jnp.abs(z)))) * (LOG2E / GLA_TAU)

        cos = cos_ref[rows, :]
        sin = sin_ref[rows, :]
        q = _dot(h, wqb_ref[...])
        k = _dot(h, wkb_ref[...])
        v = _dot(h, wvb_ref[...])
        zg = _dot(h, wzb_ref[...])
        for hd in range(DIL_HEADS):
            cols = slice(hd * DIL_HD, (hd + 1) * DIL_HD)
            store_residue_order(qb_ref, hd, 4 * hd, t0, _rope(q[:, cols], cos, sin) * (DIL_HD ** -0.5 * LOG2E))
            store_residue_order(kb_ref, hd, 4 * hd + 1, t0, _rope(k[:, cols], cos, sin))
            store_residue_order(vb_ref, hd, 4 * hd + 2, t0, v[:, cols])
            store_residue_order(zb_ref, hd, 4 * hd + 3, t0, zg[:, cols])


def _in_proj(x, mod_l, g_pre, w, cos, sin):
    bsz, seq, d = x.shape
    tm = ROW_TILE
    row = lambda b, i: (b, i, 0)
    const = lambda b, i: (0, 0)

    def full(a):
        return pl.BlockSpec(a.shape, const, pipeline_mode=pl.Buffered(1))

    gla_cols = ((GLA_QK, BF16), (GLA_QK, BF16), (GLA_WIDTH, BF16), (GLA_WIDTH, BF16), (GLA_QK, F32))
    dil_types = (F32, F32, F32, BF16)
    weights = [w[k] for k in ("qa", "ka", "va", "za", "lr", "qb", "kb", "vb", "zb", "wg", "bg")]
    return pl.pallas_call(
        _in_proj_kernel,
        grid=(bsz, seq // tm),
        in_specs=[
            pl.BlockSpec((None, tm, d), row),
            pl.BlockSpec((None, 1, 3 * d), lambda b, i: (b, 0, 0)),
            full(g_pre),
            *[full(a) for a in weights],
            pl.BlockSpec((tm, DIL_HD), lambda b, i: (i, 0)),
            pl.BlockSpec((tm, DIL_HD), lambda b, i: (i, 0)),
        ],
        out_specs=[pl.BlockSpec((None, tm, n), row) for n, _ in gla_cols]
        + [pl.BlockSpec((None, DIL_HEADS, tm, DIL_HD), lambda b, i: (b, 0, i, 0)) for _ in dil_types],
        out_shape=[jax.ShapeDtypeStruct((bsz, seq, n), dt) for n, dt in gla_cols]
        + [jax.ShapeDtypeStruct((bsz, DIL_HEADS, seq, DIL_HD), dt) for dt in dil_types],
        scratch_shapes=[pltpu.VMEM((4 * DIL_HEADS, PERM_TILE, DIL_HD), F32)],
        compiler_params=pltpu.CompilerParams(
            dimension_semantics=("arbitrary", "arbitrary"), vmem_limit_bytes=VMEM_LIMIT),
        name="in_proj",
    )(x, mod_l, g_pre, *weights, cos, sin)


def _head_norm_gate(o, g, z):
    r = o * lax.rsqrt(jnp.mean(o * o, axis=-1, keepdims=True) + EPS)
    return r * g * _silu(z).astype(F32)


def _split2(v):
    h1 = v.astype(BF16)
    h2 = (v - h1.astype(F32)).astype(BF16)
    return h1, h2


def _gla_out_kernel(qa_ref, ka_ref, va_ref, za_ref, la_ref, g_ref, yb_ref, x_ref, mod_ref, wa_ref, wb_ref,
                    gpost_ref, o_ref, st_ref, ya_ref, *, chunks):
    @pl.when(pl.program_id(1) == 0)
    def _():
        st_ref[...] = jnp.zeros_like(st_ref)

    c_len = GLA_CHUNK
    grp = GLA_GROUP * c_len
    ri = lax.broadcasted_iota(jnp.int32, (grp, grp), 0)
    ci = lax.broadcasted_iota(jnp.int32, (grp, grp), 1)
    same_chunk = (ri & -c_len) == (ci & -c_len)
    tril_bd = ((ri >= ci) & same_chunk).astype(BF16)
    causal = (lax.broadcasted_iota(jnp.int32, (c_len, c_len), 0)
              >= lax.broadcasted_iota(jnp.int32, (c_len, c_len), 1))
    heads = range(GLA_HEADS)
    kcol = [slice(hd * GLA_DK, (hd + 1) * GLA_DK) for hd in heads]
    vcol = [slice(hd * GLA_DV, (hd + 1) * GLA_DV) for hd in heads]

    def prep(r0):
        rows = pl.ds(r0, grp)
        h1, h2 = _split2(la_ref[rows, :])
        b = _dot(tril_bd, h1) + _dot(tril_bd, h2)
        decay = [jnp.exp2(b[(c + 1) * c_len - 1:(c + 1) * c_len, :]) for c in range(GLA_GROUP)]
        decay_rows = jnp.concatenate([jnp.broadcast_to(d, (c_len, GLA_QK)) for d in decay], axis=0)
        k_e32 = ka_ref[rows, :].astype(F32) * jnp.exp2(-b)
        q_e = (qa_ref[rows, :].astype(F32) * jnp.exp2(b)).astype(BF16)
        k_end = (k_e32 * decay_rows).astype(BF16)
        v = [[va_ref[pl.ds(r0 + c * c_len, c_len), vcol[hd]] for hd in heads] for c in range(GLA_GROUP)]
        return dict(r0=r0, q_e=q_e, k_e=k_e32.astype(BF16), k_end=k_end, decay=decay, v=v)

    crow = [slice(c * c_len, (c + 1) * c_len) for c in range(GLA_GROUP)]

    def intra(p):
        q_e, k_e, k_end, v = p["q_e"], p["k_e"], p["k_end"], p["v"]
        a = [[jnp.where(causal, _dot_nt(q_e[crow[c], kcol[hd]], k_e[crow[c], kcol[hd]]), 0.0).astype(BF16)
              for hd in heads] for c in range(GLA_GROUP)]
        p["inc"] = [[_dot_tn(v[c][hd], k_end[crow[c], kcol[hd]]) for hd in heads] for c in range(GLA_GROUP)]
        p["o"] = [[_dot(a[c][hd], v[c][hd]) for hd in heads] for c in range(GLA_GROUP)]

    def inter(p, st):
        for hd in heads:
            for c in range(GLA_GROUP):
                p["o"][c][hd] = p["o"][c][hd] + _dot_nt(p["q_e"][crow[c], kcol[hd]], st[hd].astype(BF16))
                st[hd] = st[hd] * p["decay"][c][:, kcol[hd]] + p["inc"][c][hd]

    def epilogue(p):
        for c in range(GLA_GROUP):
            for hd in heads:
                out_rows = pl.ds(p["r0"] + c * c_len, c_len)
                ya_ref[out_rows, vcol[hd]] = _head_norm_gate(
                    p["o"][c][hd], g_ref[:, vcol[hd]], za_ref[out_rows, vcol[hd]]).astype(ya_ref.dtype)

    def project(p):
        rows = pl.ds(p["r0"], grp)
        y = _dot(ya_ref[rows, :], wa_ref[...]) + _dot(yb_ref[rows, :], wb_ref[...])
        r = y * lax.rsqrt(jnp.mean(y * y, axis=-1, keepdims=True) + EPS) * gpost_ref[...]
        o_ref[rows, :] = x_ref[rows, :] + mod_ref[:, 2 * D_MODEL:3 * D_MODEL] * r

    st = [st_ref[hd] for hd in heads]
    groups = [prep(g * grp) for g in range(chunks // GLA_GROUP)]
    intra(groups[0])
    for g, p in enumerate(groups):
        inter(p, st)
        if g + 1 < len(groups):
            intra(groups[g + 1])
        epilogue(p)
        project(p)
    for hd in heads:
        st_ref[hd] = st[hd]


def _gla_out_proj(qa, ka, va, za, la, g_gla, yb, x, mod_l, w_out_a, w_out_b, g_post):
    bsz, seq, d = x.shape
    ts = ROW_TILE
    row = lambda b, i: (b, i, 0)
    const = lambda b, i: (0, 0)
    return pl.pallas_call(
        functools.partial(_gla_out_kernel, chunks=ts // GLA_CHUNK),
        grid=(bsz, seq // ts),
        in_specs=[
            pl.BlockSpec((None, ts, GLA_QK), row),
            pl.BlockSpec((None, ts, GLA_QK), row),
            pl.BlockSpec((None, ts, GLA_WIDTH), row),
            pl.BlockSpec((None, ts, GLA_WIDTH), row),
            pl.BlockSpec((None, ts, GLA_QK), row),
            pl.BlockSpec((1, GLA_WIDTH), const),
            pl.BlockSpec((None, ts, DIL_WIDTH), row),
            pl.BlockSpec((None, ts, d), row),
            pl.BlockSpec((None, 1, 3 * d), lambda b, i: (b, 0, 0)),
            pl.BlockSpec(w_out_a.shape, const),
            pl.BlockSpec(w_out_b.shape, const),
            pl.BlockSpec((1, d), const),
        ],
        out_specs=pl.BlockSpec((None, ts, d), row),
        out_shape=jax.ShapeDtypeStruct((bsz, seq, d), F32),
        scratch_shapes=[pltpu.VMEM((GLA_HEADS, GLA_DV, GLA_DK), F32), pltpu.VMEM((ts, GLA_WIDTH), BF16)],
        compiler_params=pltpu.CompilerParams(
            dimension_semantics=("arbitrary", "arbitrary"), vmem_limit_bytes=VMEM_LIMIT),
        name="gla_out_proj",
    )(qa, ka, va, za, la, g_gla, yb, x, mod_l, w_out_a, w_out_b, g_post)


def _dil_block_chunks(pattern, idx, seq):
    window, dil = DIL_PATTERNS[pattern]
    nb = seq // window
    lb = DIL_LB
    n_r = PERM_TILE // PERM_D
    if dil == PERM_D:
        return [((idx % nb) * PERM_TILE + (idx // nb) * n_r, 1)]
    if dil == 1:
        tile, part = idx // (PERM_TILE // lb), idx % (PERM_TILE // lb)
        rows = lb // PERM_D
        return [(tile * PERM_TILE + r * n_r + part * rows, 1) for r in range(PERM_D)]
    sub = dil // PERM_D
    res, n = idx // nb, idx % nb
    r4, c = res % PERM_D, res // PERM_D
    tiles = window // PERM_TILE
    return [(n * window + t * PERM_TILE + r4 * n_r + c, sub) for t in range(tiles)]


def _dil_kernel(q_ref, k_ref, v_ref, z_ref, g_ref, y_ref, qd, kd, vd, ynat,
                o1, o2, o3, m1, m2, m3, d1, d2, d3, *, seq):
    lb = DIL_LB
    n_blocks = seq // lb
    qi = lax.broadcasted_iota(jnp.int32, (lb, 2 * lb), 0)
    ki = lax.broadcasted_iota(jnp.int32, (lb, 2 * lb), 1)

    def biases(pos_in_block):
        dist = pos_in_block(qi) + lb - (pos_in_block(ki & (lb - 1)) + (ki & lb))
        band = (dist >= 0) & (dist <= lb)
        b_any = jnp.where(band, 0.0, NEG_INF)
        b_first = jnp.where(band & (ki >= lb), 0.0, NEG_INF)
        return b_any, b_first, b_first[:, lb:]

    step_order = biases(lambda a: a)
    rows_p1 = lb // PERM_D
    p1_order = biases(lambda a: PERM_D * (a % rows_p1) + a // rows_p1)

    kd[0] = jnp.zeros((DIL_HD, lb), BF16)
    vd[0:lb, :] = jnp.zeros((lb, DIL_HD), BF16)

    n_r = PERM_TILE // PERM_D

    def combine(t0):
        rows = pl.ds(t0, PERM_TILE)
        a1, a2, a3 = m1[rows, :], m2[rows, :], m3[rows, :]
        m = jnp.maximum(jnp.maximum(a1, a2), a3)
        e1, e2, e3 = jnp.exp2(a1 - m), jnp.exp2(a2 - m), jnp.exp2(a3 - m)
        den = e1 * d1[rows, :] + e2 * d2[rows, :] + e3 * d3[rows, :]
        o = (e1 * o1[rows, :] + e2 * o2[rows, :] + e3 * o3[rows, :]) / den
        y = _head_norm_gate(o, g_ref[...], z_ref[rows, :])
        for r in range(PERM_D):
            ynat[pl.ds(t0 + r, n_r, stride=PERM_D), :] = y[r * n_r:(r + 1) * n_r, :]
        y_ref[rows, :] = ynat[rows, :].astype(y_ref.dtype)

    order = sorted(range(len(DIL_PATTERNS)), key=lambda i: -DIL_PATTERNS[i][1])
    assert DIL_PATTERNS[order[-1]][1] == 1
    for pat in order:
        window, dil = DIL_PATTERNS[pat]
        o_scr, m_scr, d_scr = (o1, o2, o3)[pat], (m1, m2, m3)[pat], (d1, d2, d3)[pat]
        nb = seq // window
        bias_any, bias_first, bias_cur = p1_order if dil == 1 else step_order

        def load_block(ref, idx, pat=pat):
            chunks = _dil_block_chunks(pat, idx, seq)
            rows = lb // len(chunks)
            parts = [ref[pl.ds(s0, rows) if st == 1 else pl.ds(s0, rows, stride=st), :] for s0, st in chunks]
            return parts[0] if len(parts) == 1 else jnp.concatenate(parts, axis=0)

        def store_block(ref, idx, val, pat=pat):
            chunks = _dil_block_chunks(pat, idx, seq)
            rows = lb // len(chunks)
            for i, (s0, st) in enumerate(chunks):
                dst = pl.ds(s0, rows) if st == 1 else pl.ds(s0, rows, stride=st)
                ref[dst, :] = val[i * rows:(i + 1) * rows, :]

        for idx in range(n_blocks):
            dst = pl.ds(lb + idx * lb, lb)
            qd[dst, :] = load_block(q_ref, idx).astype(BF16)
            kd[idx + 1] = load_block(k_ref, idx).T.astype(BF16)
            vd[dst, :] = load_block(v_ref, idx).astype(BF16)

        def scores(idx, nb=nb, bias_any=bias_any, bias_cur=bias_cur):
            q = qd[pl.ds(lb + idx * lb, lb), :]
            if idx % nb == 0:
                return _dot(q, kd[idx + 1]) + bias_cur, pl.ds(lb + idx * lb, lb)
            s = _dot(q, jnp.concatenate([kd[idx], kd[idx + 1]], axis=1)) + bias_any
            return s, pl.ds(idx * lb, 2 * lb)

        def softmax(s):
            m = jnp.max(s, axis=-1, keepdims=True)
            return jnp.exp2((s - m).astype(BF16)), m

        def values(idx, p, m, keys, store_block=store_block, o_scr=o_scr, m_scr=m_scr, d_scr=d_scr):
            v_ones = jnp.concatenate([vd[keys, :], jnp.ones((p.shape[1], DIL_HD), BF16)], axis=1)
            acc = _dot(p, v_ones)
            store_block(o_scr, idx, acc[:, :DIL_HD])
            store_block(d_scr, idx, acc[:, DIL_HD:])
            store_block(m_scr, idx, jnp.broadcast_to(m, (lb, DIL_HD)))

        groups = [range(g * DIL_UNROLL, (g + 1) * DIL_UNROLL) for g in range(n_blocks // DIL_UNROLL)]
        pending = [scores(idx) for idx in groups[0]]
        for g, group in enumerate(groups):
            upcoming = [scores(idx) for idx in groups[g + 1]] if g + 1 < len(groups) else []
            probs = [softmax(s) for s, _ in pending]
            for idx, (p, m), (_, keys) in zip(group, probs, pending):
                values(idx, p, m, keys)
            pending = upcoming
            if dil == 1:
                for t0 in range(group[0] * lb, (group[-1] + 1) * lb, PERM_TILE):
                    combine(t0)


def _dilated(qb, kb, vb, zb, g_dil):
    bsz, _, seq, _ = qb.shape
    blk = pl.BlockSpec((None, None, seq, DIL_HD), lambda b, h: (b, h, 0, 0))
    return pl.pallas_call(
        functools.partial(_dil_kernel, seq=seq),
        grid=(bsz, DIL_HEADS),
        in_specs=[blk, blk, blk, blk, pl.BlockSpec((1, DIL_HD), lambda b, h: (0, h))],
        out_specs=pl.BlockSpec((None, seq, DIL_HD), lambda b, h: (b, 0, h)),
        out_shape=jax.ShapeDtypeStruct((bsz, seq, DIL_WIDTH), BF16),
        scratch_shapes=[pltpu.VMEM((seq + DIL_LB, DIL_HD), BF16),
                        pltpu.VMEM((seq // DIL_LB + 1, DIL_HD, DIL_LB), BF16),
                        pltpu.VMEM((seq + DIL_LB, DIL_HD), BF16)]
        + [pltpu.VMEM((seq, DIL_HD), F32) for _ in range(10)],
        compiler_params=pltpu.CompilerParams(
            dimension_semantics=("arbitrary", "arbitrary"), vmem_limit_bytes=VMEM_LIMIT),
        name="dilated",
    )(qb, kb, vb, zb, g_dil)


def _rope_tables(seq):
    inv_freq = ROPE_THETA ** (-np.arange(0, DIL_HD, 2, dtype=np.float64) / DIL_HD)
    ang = np.arange(seq, dtype=np.float64)[:, None] * inv_freq[None, :]
    cos, sin = np.cos(ang), np.sin(ang)
    return (jnp.asarray(np.concatenate([cos, cos], axis=-1), F32),
            jnp.asarray(np.concatenate([-sin, sin], axis=-1), F32))


def _split_w_in(w_in_l, w_gate_up_l, b_gate_up_l):
    sizes = (GLA_QK, GLA_QK, GLA_WIDTH, GLA_WIDTH, GLA_LOWRANK,
             DIL_WIDTH, DIL_WIDTH, DIL_WIDTH, DIL_WIDTH)
    offs = np.cumsum((0,) + sizes)
    names = ("qa", "ka", "va", "za", "lr", "qb", "kb", "vb", "zb")
    w = {n: w_in_l[:, offs[i]:offs[i + 1]].astype(BF16) for i, n in enumerate(names)}
    w["lr"] = jnp.pad(w["lr"], ((0, 0), (0, LR_PAD - GLA_LOWRANK)))
    w["wg"] = jnp.pad(w_gate_up_l.astype(BF16), ((0, LR_PAD - GLA_LOWRANK), (0, 0)))
    w["bg"] = b_gate_up_l.reshape(1, GLA_QK)
    return w


def kernel(x, c, w_ada, b_ada, g_pre, w_in, w_gate_up, b_gate_up, g_gla, g_dil, w_out, g_post):
    bsz, seq, d = x.shape
    depth = w_ada.shape[0]
    mod = _modulation(c, w_ada, b_ada).reshape(depth, bsz, 1, 3 * d)
    cos, sin = _rope_tables(seq)
    for l in range(depth):
        w = _split_w_in(w_in[l], w_gate_up[l], b_gate_up[l])
        qa, ka, va, za, la, qb, kb, vb, zb = _in_proj(x, mod[l], g_pre[l].reshape(1, d), w, cos, sin)
        yb = _dilated(qb, kb, vb, zb, g_dil[l].reshape(1, DIL_WIDTH))
        w_o = w_out[l].astype(BF16)
        x = _gla_out_proj(qa, ka, va, za, la, g_gla[l].reshape(1, GLA_WIDTH), yb, x, mod[l],
                          w_o[:GLA_WIDTH], w_o[GLA_WIDTH:], g_post[l].reshape(1, d))
    return x
```

```python
import functools
import math

import jax
import jax.numpy as jnp
import numpy as np
from jax import lax
from jax.experimental import pallas as pl
from jax.experimental.pallas import tpu as pltpu

F32 = jnp.float32
BF16 = jnp.bfloat16

D_MODEL = 1024
GLA_HEADS = 4
GLA_DK = 64
GLA_DV = 128
GLA_QK = GLA_HEADS * GLA_DK
GLA_WIDTH = GLA_HEADS * GLA_DV
GLA_LOWRANK = 16
GLA_TAU = 16.0
GLA_CHUNK = 64
GLA_GROUP = 4
DIL_HEADS = 4
DIL_HD = 128
DIL_WIDTH = DIL_HEADS * DIL_HD
DIL_PATTERNS = ((128, 1), (512, 4), (2048, 16))
DIL_LB = 128
PERM_TILE, PERM_D = DIL_PATTERNS[1]
ROPE_THETA = 10000.0
EPS = 1e-6
LANES = 128
LR_PAD = LANES

VMEM_LIMIT = 56 * 1024 * 1024
DIL_UNROLL = 8
ROW_TILE = 1024
NEG_INF = float("-inf")
LOG2E = math.log2(math.e)


def _silu(v):
    return v * jax.nn.sigmoid(v)


def _dot(a, b):
    return jnp.dot(a, b, preferred_element_type=F32)


def _dot_nt(a, b):
    return lax.dot_general(a, b, (((1,), (1,)), ((), ())), preferred_element_type=F32)


def _dot_tn(a, b):
    return lax.dot_general(a, b, (((0,), (0,)), ((), ())), preferred_element_type=F32)


def _mod_kernel(c_ref, w_ref, b_ref, o_ref):
    sc = _silu(c_ref[...]).astype(BF16)
    o_ref[...] = _dot(sc, w_ref[...].astype(BF16)) + b_ref[...]


def _modulation(c, w_ada, b_ada):
    depth, d, e = w_ada.shape
    bsz = c.shape[0]
    nt = e // d
    return pl.pallas_call(
        _mod_kernel,
        grid=(depth, nt),
        in_specs=[
            pl.BlockSpec((bsz, d), lambda l, j: (0, 0)),
            pl.BlockSpec((None, d, d), lambda l, j: (l, 0, j)),
            pl.BlockSpec((None, 1, d), lambda l, j: (l, 0, j)),
        ],
        out_specs=pl.BlockSpec((None, bsz, d), lambda l, j: (l, 0, j)),
        out_shape=jax.ShapeDtypeStruct((depth, bsz, e), F32),
        compiler_params=pltpu.CompilerParams(
            dimension_semantics=("arbitrary", "arbitrary"), vmem_limit_bytes=VMEM_LIMIT),
        name="adaln_mod",
    )(c, w_ada, b_ada.reshape(depth, 1, e))


def _rope(v, cos, sin_signed):
    return v * cos + pltpu.roll(v, DIL_HD // 2, axis=1) * sin_signed


def _in_proj_kernel(x_ref, mod_ref, gpre_ref, wqa_ref, wka_ref, wva_ref, wza_ref, wlr_ref,
                    wqb_ref, wkb_ref, wvb_ref, wzb_ref, wg_ref, bg_ref, cos_ref, sin_ref,
                    qa_ref, ka_ref, va_ref, za_ref, la_ref, qb_ref, kb_ref, vb_ref, zb_ref, perm_ref):
    shift = mod_ref[:, 0:D_MODEL]
    gain = gpre_ref[...] * (1.0 + mod_ref[:, D_MODEL:2 * D_MODEL])
    n_res = PERM_TILE // PERM_D

    def store_residue_order(out_ref, hd, slab, t0, val):
        perm_ref[slab] = val
        for r in range(PERM_D):
            dst = slice(t0 + r * n_res, t0 + (r + 1) * n_res)
            out_ref[hd, dst, :] = perm_ref[slab, pl.ds(r, n_res, stride=PERM_D), :].astype(out_ref.dtype)

    for t0 in range(0, x_ref.shape[0], PERM_TILE):
        rows = slice(t0, t0 + PERM_TILE)
        x = x_ref[rows, :]
        h = (x * lax.rsqrt(jnp.mean(x * x, axis=-1, keepdims=True) + EPS) * gain + shift).astype(BF16)

        qa_ref[rows, :] = (_dot(h, wqa_ref[...]) * (GLA_DK ** -0.5)).astype(qa_ref.dtype)
        ka_ref[rows, :] = _dot(h, wka_ref[...]).astype(ka_ref.dtype)
        va_ref[rows, :] = _dot(h, wva_ref[...]).astype(va_ref.dtype)
        za_ref[rows, :] = _dot(h, wza_ref[...]).astype(za_ref.dtype)

        lr = _dot(h, wlr_ref[...]).astype(BF16)
        z = _dot(lr, wg_ref[...]) + bg_ref[...]
        la_ref[rows, :] = (jnp.minimum(z, 0.0) - jnp.log1p(jnp.exp(-jnp.abs(z)))) / GLA_TAU

        cos = cos_ref[rows, :]
        sin = sin_ref[rows, :]
        q = _dot(h, wqb_ref[...])
        k = _dot(h, wkb_ref[...])
        v = _dot(h, wvb_ref[...])
        zg = _dot(h, wzb_ref[...])
        for hd in range(DIL_HEADS):
            cols = slice(hd * DIL_HD, (hd + 1) * DIL_HD)
            store_residue_order(qb_ref, hd, 4 * hd, t0, _rope(q[:, cols], cos, sin) * (DIL_HD ** -0.5 * LOG2E))
            store_residue_order(kb_ref, hd, 4 * hd + 1, t0, _rope(k[:, cols], cos, sin))
            store_residue_order(vb_ref, hd, 4 * hd + 2, t0, v[:, cols])
            store_residue_order(zb_ref, hd, 4 * hd + 3, t0, zg[:, cols])


def _in_proj(x, mod_l, g_pre, w, cos, sin):
    bsz, seq, d = x.shape
    tm = ROW_TILE
    row = lambda b, i: (b, i, 0)
    const = lambda b, i: (0, 0)

    def full(a):
        return pl.BlockSpec(a.shape, const, pipeline_mode=pl.Buffered(1))

    gla_cols = ((GLA_QK, BF16), (GLA_QK, BF16), (GLA_WIDTH, BF16), (GLA_WIDTH, BF16), (GLA_QK, F32))
    dil_types = (F32, F32, F32, BF16)
    weights = [w[k] for k in ("qa", "ka", "va", "za", "lr", "qb", "kb", "vb", "zb", "wg", "bg")]
    return pl.pallas_call(
        _in_proj_kernel,
        grid=(bsz, seq // tm),
        in_specs=[
            pl.BlockSpec((None, tm, d), row),
            pl.BlockSpec((None, 1, 3 * d), lambda b, i: (b, 0, 0)),
            full(g_pre),
            *[full(a) for a in weights],
            pl.BlockSpec((tm, DIL_HD), lambda b, i: (i, 0)),
            pl.BlockSpec((tm, DIL_HD), lambda b, i: (i, 0)),
        ],
        out_specs=[pl.BlockSpec((None, tm, n), row) for n, _ in gla_cols]
        + [pl.BlockSpec((None, DIL_HEADS, tm, DIL_HD), lambda b, i: (b, 0, i, 0)) for _ in dil_types],
        out_shape=[jax.ShapeDtypeStruct((bsz, seq, n), dt) for n, dt in gla_cols]
        + [jax.ShapeDtypeStruct((bsz, DIL_HEADS, seq, DIL_HD), dt) for dt in dil_types],
        scratch_shapes=[pltpu.VMEM((4 * DIL_HEADS, PERM_TILE, DIL_HD), F32)],
        compiler_params=pltpu.CompilerParams(
            dimension_semantics=("arbitrary", "arbitrary"), vmem_limit_bytes=VMEM_LIMIT),
        name="in_proj",
    )(x, mod_l, g_pre, *weights, cos, sin)


def _head_norm_gate(o, g, z):
    r = o * lax.rsqrt(jnp.mean(o * o, axis=-1, keepdims=True) + EPS)
    return r * g * _silu(z.astype(F32))


def _split2(v):
    h1 = v.astype(BF16)
    h2 = (v - h1.astype(F32)).astype(BF16)
    return h1, h2


def _gla_out_kernel(qa_ref, ka_ref, va_ref, za_ref, la_ref, g_ref, yb_ref, x_ref, mod_ref, wa_ref, wb_ref,
                    gpost_ref, o_ref, st_ref, ya_ref, *, chunks):
    @pl.when(pl.program_id(1) == 0)
    def _():
        st_ref[...] = jnp.zeros_like(st_ref)

    c_len = GLA_CHUNK
    grp = GLA_GROUP * c_len
    ri = lax.broadcasted_iota(jnp.int32, (grp, grp), 0)
    ci = lax.broadcasted_iota(jnp.int32, (grp, grp), 1)
    same_chunk = (ri & -c_len) == (ci & -c_len)
    tril_bd = ((ri >= ci) & same_chunk).astype(BF16)
    causal = (lax.broadcasted_iota(jnp.int32, (c_len, c_len), 0)
              >= lax.broadcasted_iota(jnp.int32, (c_len, c_len), 1))
    heads = range(GLA_HEADS)
    kcol = [slice(hd * GLA_DK, (hd + 1) * GLA_DK) for hd in heads]
    vcol = [slice(hd * GLA_DV, (hd + 1) * GLA_DV) for hd in heads]

    def prep(r0):
        rows = pl.ds(r0, grp)
        h1, h2 = _split2(la_ref[rows, :])
        b = _dot(tril_bd, h1) + _dot(tril_bd, h2)
        decay = [jnp.exp(b[(c + 1) * c_len - 1:(c + 1) * c_len, :]) for c in range(GLA_GROUP)]
        decay_rows = jnp.concatenate([jnp.broadcast_to(d, (c_len, GLA_QK)) for d in decay], axis=0)
        k_e32 = ka_ref[rows, :].astype(F32) * jnp.exp(-b)
        q_e = (qa_ref[rows, :].astype(F32) * jnp.exp(b)).astype(BF16)
        k_end = (k_e32 * decay_rows).astype(BF16)
        v = [[va_ref[pl.ds(r0 + c * c_len, c_len), vcol[hd]] for hd in heads] for c in range(GLA_GROUP)]
        return dict(r0=r0, q_e=q_e, k_e=k_e32.astype(BF16), k_end=k_end, decay=decay, v=v)

    crow = [slice(c * c_len, (c + 1) * c_len) for c in range(GLA_GROUP)]

    def intra(p):
        q_e, k_e, k_end, v = p["q_e"], p["k_e"], p["k_end"], p["v"]
        a = [[jnp.where(causal, _dot_nt(q_e[crow[c], kcol[hd]], k_e[crow[c], kcol[hd]]), 0.0).astype(BF16)
              for hd in heads] for c in range(GLA_GROUP)]
        p["inc"] = [[_dot_tn(v[c][hd], k_end[crow[c], kcol[hd]]) for hd in heads] for c in range(GLA_GROUP)]
        p["o"] = [[_dot(a[c][hd], v[c][hd]) for hd in heads] for c in range(GLA_GROUP)]

    def inter(p, st):
        for hd in heads:
            for c in range(GLA_GROUP):
                p["o"][c][hd] = p["o"][c][hd] + _dot_nt(p["q_e"][crow[c], kcol[hd]], st[hd].astype(BF16))
                st[hd] = st[hd] * p["decay"][c][:, kcol[hd]] + p["inc"][c][hd]

    def epilogue(p):
        for c in range(GLA_GROUP):
            for hd in heads:
                out_rows = pl.ds(p["r0"] + c * c_len, c_len)
                ya_ref[out_rows, vcol[hd]] = _head_norm_gate(
                    p["o"][c][hd], g_ref[:, vcol[hd]], za_ref[out_rows, vcol[hd]]).astype(ya_ref.dtype)

    def project(p):
        rows = pl.ds(p["r0"], grp)
        y = _dot(ya_ref[rows, :], wa_ref[...]) + _dot(yb_ref[rows, :], wb_ref[...])
        r = y * lax.rsqrt(jnp.mean(y * y, axis=-1, keepdims=True) + EPS) * gpost_ref[...]
        o_ref[rows, :] = x_ref[rows, :] + mod_ref[:, 2 * D_MODEL:3 * D_MODEL] * r

    st = [st_ref[hd] for hd in heads]
    groups = [prep(g * grp) for g in range(chunks // GLA_GROUP)]
    intra(groups[0])
    for g, p in enumerate(groups):
        inter(p, st)
        if g + 1 < len(groups):
            intra(groups[g + 1])
        epilogue(p)
        project(p)
    for hd in heads:
        st_ref[hd] = st[hd]


def _gla_out_proj(qa, ka, va, za, la, g_gla, yb, x, mod_l, w_out_a, w_out_b, g_post):
    bsz, seq, d = x.shape
    ts = ROW_TILE
    row = lambda b, i: (b, i, 0)
    const = lambda b, i: (0, 0)
    return pl.pallas_call(
        functools.partial(_gla_out_kernel, chunks=ts // GLA_CHUNK),
        grid=(bsz, seq // ts),
        in_specs=[
            pl.BlockSpec((None, ts, GLA_QK), row),
            pl.BlockSpec((None, ts, GLA_QK), row),
            pl.BlockSpec((None, ts, GLA_WIDTH), row),
            pl.BlockSpec((None, ts, GLA_WIDTH), row),
            pl.BlockSpec((None, ts, GLA_QK), row),
            pl.BlockSpec((1, GLA_WIDTH), const),
            pl.BlockSpec((None, ts, DIL_WIDTH), row),
            pl.BlockSpec((None, ts, d), row),
            pl.BlockSpec((None, 1, 3 * d), lambda b, i: (b, 0, 0)),
            pl.BlockSpec(w_out_a.shape, const),
            pl.BlockSpec(w_out_b.shape, const),
            pl.BlockSpec((1, d), const),
        ],
        out_specs=pl.BlockSpec((None, ts, d), row),
        out_shape=jax.ShapeDtypeStruct((bsz, seq, d), F32),
        scratch_shapes=[pltpu.VMEM((GLA_HEADS, GLA_DV, GLA_DK), F32), pltpu.VMEM((ts, GLA_WIDTH), BF16)],
        compiler_params=pltpu.CompilerParams(
            dimension_semantics=("arbitrary", "arbitrary"), vmem_limit_bytes=VMEM_LIMIT),
        name="gla_out_proj",
    )(qa, ka, va, za, la, g_gla, yb, x, mod_l, w_out_a, w_out_b, g_post)


def _dil_block_chunks(pattern, idx, seq):
    window, dil = DIL_PATTERNS[pattern]
    nb = seq // window
    lb = DIL_LB
    n_r = PERM_TILE // PERM_D
    if dil == PERM_D:
        return [((idx % nb) * PERM_TILE + (idx // nb) * n_r, 1)]
    if dil == 1:
        tile, part = idx // (PERM_TILE // lb), idx % (PERM_TILE // lb)
        rows = lb // PERM_D
        return [(tile * PERM_TILE + r * n_r + part * rows, 1) for r in range(PERM_D)]
    sub = dil // PERM_D
    res, n = idx // nb, idx % nb
    r4, c = res % PERM_D, res // PERM_D
    tiles = window // PERM_TILE
    return [(n * window + t * PERM_TILE + r4 * n_r + c, sub) for t in range(tiles)]


def _dil_kernel(q_ref, k_ref, v_ref, z_ref, g_ref, y_ref, qd, kd, vd, ynat,
                o1, o2, o3, m1, m2, m3, d1, d2, d3, *, seq):
    lb = DIL_LB
    n_blocks = seq // lb
    qi = lax.broadcasted_iota(jnp.int32, (lb, 2 * lb), 0)
    ki = lax.broadcasted_iota(jnp.int32, (lb, 2 * lb), 1)

    def biases(pos_in_block):
        dist = pos_in_block(qi) + lb - (pos_in_block(ki & (lb - 1)) + (ki & lb))
        band = (dist >= 0) & (dist <= lb)
        b_any = jnp.where(band, 0.0, NEG_INF)
        b_first = jnp.where(band & (ki >= lb), 0.0, NEG_INF)
        return b_any, b_first, b_first[:, lb:]

    step_order = biases(lambda a: a)
    rows_p1 = lb // PERM_D
    p1_order = biases(lambda a: PERM_D * (a % rows_p1) + a // rows_p1)

    kd[0] = jnp.zeros((DIL_HD, lb), BF16)
    vd[0:lb, :] = jnp.zeros((lb, DIL_HD), BF16)

    n_r = PERM_TILE // PERM_D

    def combine(t0):
        rows = pl.ds(t0, PERM_TILE)
        a1, a2, a3 = m1[rows, :], m2[rows, :], m3[rows, :]
        m = jnp.maximum(jnp.maximum(a1, a2), a3)
        e1, e2, e3 = jnp.exp2(a1 - m), jnp.exp2(a2 - m), jnp.exp2(a3 - m)
        den = e1 * d1[rows, :] + e2 * d2[rows, :] + e3 * d3[rows, :]
        o = (e1 * o1[rows, :] + e2 * o2[rows, :] + e3 * o3[rows, :]) / den
        y = _head_norm_gate(o, g_ref[...], z_ref[rows, :])
        for r in range(PERM_D):
            ynat[pl.ds(t0 + r, n_r, stride=PERM_D), :] = y[r * n_r:(r + 1) * n_r, :]
        y_ref[rows, :] = ynat[rows, :].astype(y_ref.dtype)

    order = sorted(range(len(DIL_PATTERNS)), key=lambda i: -DIL_PATTERNS[i][1])
    assert DIL_PATTERNS[order[-1]][1] == 1
    for pat in order:
        window, dil = DIL_PATTERNS[pat]
        o_scr, m_scr, d_scr = (o1, o2, o3)[pat], (m1, m2, m3)[pat], (d1, d2, d3)[pat]
        nb = seq // window
        bias_any, bias_first, bias_cur = p1_order if dil == 1 else step_order

        def load_block(ref, idx, pat=pat):
            chunks = _dil_block_chunks(pat, idx, seq)
            rows = lb // len(chunks)
            parts = [ref[pl.ds(s0, rows) if st == 1 else pl.ds(s0, rows, stride=st), :] for s0, st in chunks]
            return parts[0] if len(parts) == 1 else jnp.concatenate(parts, axis=0)

        def store_block(ref, idx, val, pat=pat):
            chunks = _dil_block_chunks(pat, idx, seq)
            rows = lb // len(chunks)
            for i, (s0, st) in enumerate(chunks):
                dst = pl.ds(s0, rows) if st == 1 else pl.ds(s0, rows, stride=st)
                ref[dst, :] = val[i * rows:(i + 1) * rows, :]

        for idx in range(n_blocks):
            dst = pl.ds(lb + idx * lb, lb)
            qd[dst, :] = load_block(q_ref, idx).astype(BF16)
            kd[idx + 1] = load_block(k_ref, idx).T.astype(BF16)
            vd[dst, :] = load_block(v_ref, idx).astype(BF16)

        def scores(idx, nb=nb, bias_any=bias_any, bias_cur=bias_cur):
            q = qd[pl.ds(lb + idx * lb, lb), :]
            if idx % nb == 0:
                return _dot(q, kd[idx + 1]) + bias_cur, pl.ds(lb + idx * lb, lb)
            s = _dot(q, jnp.concatenate([kd[idx], kd[idx + 1]], axis=1)) + bias_any
            return s, pl.ds(idx * lb, 2 * lb)

        def softmax(s):
            m = jnp.max(s, axis=-1, keepdims=True)
            return jnp.exp2((s - m).astype(BF16)), m

        def values(idx, p, m, keys, store_block=store_block, o_scr=o_scr, m_scr=m_scr, d_scr=d_scr):
            v_ones = jnp.concatenate([vd[keys, :], jnp.ones((p.shape[1], DIL_HD), BF16)], axis=1)
            acc = _dot(p, v_ones)
            store_block(o_scr, idx, acc[:, :DIL_HD])
            store_block(d_scr, idx, acc[:, DIL_HD:])
            store_block(m_scr, idx, jnp.broadcast_to(m, (lb, DIL_HD)))

        groups = [range(g * DIL_UNROLL, (g + 1) * DIL_UNROLL) for g in range(n_blocks // DIL_UNROLL)]
        pending = [scores(idx) for idx in groups[0]]
        for g, group in enumerate(groups):
            upcoming = [scores(idx) for idx in groups[g + 1]] if g + 1 < len(groups) else []
            probs = [softmax(s) for s, _ in pending]
            for idx, (p, m), (_, keys) in zip(group, probs, pending):
                values(idx, p, m, keys)
            pending = upcoming
            if dil == 1:
                for t0 in range(group[0] * lb, (group[-1] + 1) * lb, PERM_TILE):
                    combine(t0)


def _dilated(qb, kb, vb, zb, g_dil):
    bsz, _, seq, _ = qb.shape
    blk = pl.BlockSpec((None, None, seq, DIL_HD), lambda b, h: (b, h, 0, 0))
    return pl.pallas_call(
        functools.partial(_dil_kernel, seq=seq),
        grid=(bsz, DIL_HEADS),
        in_specs=[blk, blk, blk, blk, pl.BlockSpec((1, DIL_HD), lambda b, h: (0, h))],
        out_specs=pl.BlockSpec((None, seq, DIL_HD), lambda b, h: (b, 0, h)),
        out_shape=jax.ShapeDtypeStruct((bsz, seq, DIL_WIDTH), BF16),
        scratch_shapes=[pltpu.VMEM((seq + DIL_LB, DIL_HD), BF16),
                        pltpu.VMEM((seq // DIL_LB + 1, DIL_HD, DIL_LB), BF16),
                        pltpu.VMEM((seq + DIL_LB, DIL_HD), BF16)]
        + [pltpu.VMEM((seq, DIL_HD), F32) for _ in range(10)],
        compiler_params=pltpu.CompilerParams(
            dimension_semantics=("arbitrary", "arbitrary"), vmem_limit_bytes=VMEM_LIMIT),
        name="dilated",
    )(qb, kb, vb, zb, g_dil)


def _rope_tables(seq):
    inv_freq = ROPE_THETA ** (-np.arange(0, DIL_HD, 2, dtype=np.float64) / DIL_HD)
    ang = np.arange(seq, dtype=np.float64)[:, None] * inv_freq[None, :]
    cos, sin = np.cos(ang), np.sin(ang)
    return (jnp.asarray(np.concatenate([cos, cos], axis=-1), F32),
            jnp.asarray(np.concatenate([-sin, sin], axis=-1), F32))


def _split_w_in(w_in_l, w_gate_up_l, b_gate_up_l):
    sizes = (GLA_QK, GLA_QK, GLA_WIDTH, GLA_WIDTH, GLA_LOWRANK,
             DIL_WIDTH, DIL_WIDTH, DIL_WIDTH, DIL_WIDTH)
    offs = np.cumsum((0,) + sizes)
    names = ("qa", "ka", "va", "za", "lr", "qb", "kb", "vb", "zb")
    w = {n: w_in_l[:, offs[i]:offs[i + 1]].astype(BF16) for i, n in enumerate(names)}
    w["lr"] = jnp.pad(w["lr"], ((0, 0), (0, LR_PAD - GLA_LOWRANK)))
    w["wg"] = jnp.pad(w_gate_up_l.astype(BF16), ((0, LR_PAD - GLA_LOWRANK), (0, 0)))
    w["bg"] = b_gate_up_l.reshape(1, GLA_QK)
    return w


def kernel(x, c, w_ada, b_ada, g_pre, w_in, w_gate_up, b_gate_up, g_gla, g_dil, w_out, g_post):
    bsz, seq, d = x.shape
    depth = w_ada.shape[0]
    mod = _modulation(c, w_ada, b_ada).reshape(depth, bsz, 1, 3 * d)
    cos, sin = _rope_tables(seq)
    for l in range(depth):
        w = _split_w_in(w_in[l], w_gate_up[l], b_gate_up[l])
        qa, ka, va, za, la, qb, kb, vb, zb = _in_proj(x, mod[l], g_pre[l].reshape(1, d), w, cos, sin)
        yb = _dilated(qb, kb, vb, zb, g_dil[l].reshape(1, DIL_WIDTH))
        w_o = w_out[l].astype(BF16)
        x = _gla_out_proj(qa, ka, va, za, la, g_gla[l].reshape(1, GLA_WIDTH), yb, x, mod[l],
                          w_o[:GLA_WIDTH], w_o[GLA_WIDTH:], g_post[l].reshape(1, d))
    return x
```

```python
import functools
import math

import jax
import jax.numpy as jnp
import numpy as np
from jax import lax
from jax.experimental import pallas as pl
from jax.experimental.pallas import tpu as pltpu

F32 = jnp.float32
BF16 = jnp.bfloat16

D_MODEL = 1024
GLA_HEADS = 4
GLA_DK = 64
GLA_DV = 128
GLA_QK = GLA_HEADS * GLA_DK
GLA_WIDTH = GLA_HEADS * GLA_DV
GLA_LOWRANK = 16
GLA_TAU = 16.0
GLA_CHUNK = 64
GLA_GROUP = 4
DIL_HEADS = 4
DIL_HD = 128
DIL_WIDTH = DIL_HEADS * DIL_HD
DIL_PATTERNS = ((128, 1), (512, 4), (2048, 16))
DIL_LB = 128
PERM_TILE, PERM_D = DIL_PATTERNS[1]
ROPE_THETA = 10000.0
EPS = 1e-6
LANES = 128
LR_PAD = LANES
_W_IN_SIZES = (("qa", GLA_QK), ("ka", GLA_QK), ("va", GLA_WIDTH), ("za", GLA_WIDTH), ("lr", LR_PAD),
               ("qb", DIL_WIDTH), ("kb", DIL_WIDTH), ("vb", DIL_WIDTH), ("zb", DIL_WIDTH))
W_IN_COLS = {}
for _name, _size in _W_IN_SIZES:
    _lo = sum(n for _, n in _W_IN_SIZES[:len(W_IN_COLS)])
    W_IN_COLS[_name] = (_lo, _lo + _size)

VMEM_LIMIT = 56 * 1024 * 1024
DIL_UNROLL = 8
ROW_TILE = 1024
NEG_INF = float("-inf")
LOG2E = math.log2(math.e)


def _silu(v):
    return v * jax.nn.sigmoid(v)


def _dot(a, b):
    return jnp.dot(a, b, preferred_element_type=F32)


def _dot_nt(a, b):
    return lax.dot_general(a, b, (((1,), (1,)), ((), ())), preferred_element_type=F32)


def _dot_tn(a, b):
    return lax.dot_general(a, b, (((0,), (0,)), ((), ())), preferred_element_type=F32)


def _mod_kernel(c_ref, w_ref, b_ref, o_ref):
    sc = _silu(c_ref[...]).astype(BF16)
    o_ref[...] = _dot(sc, w_ref[...].astype(BF16)) + b_ref[...]


def _modulation(c, w_ada, b_ada):
    depth, d, e = w_ada.shape
    bsz = c.shape[0]
    return pl.pallas_call(
        _mod_kernel,
        grid=(depth,),
        in_specs=[
            pl.BlockSpec((bsz, d), lambda l: (0, 0)),
            pl.BlockSpec((None, d, e), lambda l: (l, 0, 0)),
            pl.BlockSpec((None, 1, e), lambda l: (l, 0, 0)),
        ],
        out_specs=pl.BlockSpec((None, bsz, e), lambda l: (l, 0, 0)),
        out_shape=jax.ShapeDtypeStruct((depth, bsz, e), F32),
        compiler_params=pltpu.CompilerParams(
            dimension_semantics=("arbitrary",), vmem_limit_bytes=VMEM_LIMIT),
        name="adaln_mod",
    )(c, w_ada, b_ada.reshape(depth, 1, e))


def _rope(v, cos, sin_signed):
    return v * cos + pltpu.roll(v, DIL_HD // 2, axis=1) * sin_signed


def _in_proj_kernel(x_ref, mod_ref, gpre_ref, w_ref, wg_ref, bg_ref, cos_ref, sin_ref,
                    qa_ref, ka_ref, va_ref, za_ref, la_ref, qb_ref, kb_ref, vb_ref, zb_ref, perm_ref):
    shift = mod_ref[:, 0:D_MODEL]
    gain = gpre_ref[...] * (1.0 + mod_ref[:, D_MODEL:2 * D_MODEL])
    n_res = PERM_TILE // PERM_D

    def store_residue_order(out_ref, hd, slab, t0, val):
        perm_ref[slab] = val
        for r in range(PERM_D):
            dst = slice(t0 + r * n_res, t0 + (r + 1) * n_res)
            out_ref[hd, dst, :] = perm_ref[slab, pl.ds(r, n_res, stride=PERM_D), :].astype(out_ref.dtype)

    for t0 in range(0, x_ref.shape[0], PERM_TILE):
        rows = slice(t0, t0 + PERM_TILE)
        x = x_ref[rows, :]
        h = (x * lax.rsqrt(jnp.mean(x * x, axis=-1, keepdims=True) + EPS) * gain + shift).astype(BF16)

        def proj(name):
            lo, hi = W_IN_COLS[name]
            return _dot(h, w_ref[:, lo:hi])

        qa_ref[rows, :] = (proj("qa") * (GLA_DK ** -0.5)).astype(qa_ref.dtype)
        ka_ref[rows, :] = proj("ka").astype(ka_ref.dtype)
        va_ref[rows, :] = proj("va").astype(va_ref.dtype)
        za_ref[rows, :] = proj("za").astype(za_ref.dtype)

        lr = proj("lr").astype(BF16)
        z = _dot(lr, wg_ref[...]) + bg_ref[...]
        la_ref[rows, :] = (jnp.minimum(z, 0.0) - jnp.log1p(jnp.exp(-jnp.abs(z)))) * (LOG2E / GLA_TAU)

        cos = cos_ref[rows, :]
        sin = sin_ref[rows, :]
        q, k, v, zg = proj("qb"), proj("kb"), proj("vb"), proj("zb")
        for hd in range(DIL_HEADS):
            cols = slice(hd * DIL_HD, (hd + 1) * DIL_HD)
            store_residue_order(qb_ref, hd, 4 * hd, t0, _rope(q[:, cols], cos, sin) * (DIL_HD ** -0.5 * LOG2E))
            store_residue_order(kb_ref, hd, 4 * hd + 1, t0, _rope(k[:, cols], cos, sin))
            store_residue_order(vb_ref, hd, 4 * hd + 2, t0, v[:, cols])
            store_residue_order(zb_ref, hd, 4 * hd + 3, t0, zg[:, cols])


def _in_proj(x, mod_l, g_pre, w_in_pad, w_gate_pad, b_gate, layer, cos, sin):
    bsz, seq, d = x.shape
    tm = ROW_TILE
    row = lambda b, i: (b, i, 0)
    const = lambda b, i: (0, 0)

    def of_layer(a):
        return pl.BlockSpec((None,) + a.shape[1:], lambda b, i: (layer,) + (0,) * (a.ndim - 1),
                            pipeline_mode=pl.Buffered(1))

    gla_cols = ((GLA_QK, BF16), (GLA_QK, BF16), (GLA_WIDTH, BF16), (GLA_WIDTH, BF16), (GLA_QK, F32))
    dil_types = (F32, F32, F32, BF16)
    return pl.pallas_call(
        _in_proj_kernel,
        grid=(bsz, seq // tm),
        in_specs=[
            pl.BlockSpec((None, tm, d), row),
            pl.BlockSpec((None, 1, 3 * d), lambda b, i: (b, 0, 0)),
            of_layer(g_pre), of_layer(w_in_pad), of_layer(w_gate_pad), of_layer(b_gate),
            pl.BlockSpec((tm, DIL_HD), lambda b, i: (i, 0)),
            pl.BlockSpec((tm, DIL_HD), lambda b, i: (i, 0)),
        ],
        out_specs=[pl.BlockSpec((None, tm, n), row) for n, _ in gla_cols]
        + [pl.BlockSpec((None, DIL_HEADS, tm, DIL_HD), lambda b, i: (b, 0, i, 0)) for _ in dil_types],
        out_shape=[jax.ShapeDtypeStruct((bsz, seq, n), dt) for n, dt in gla_cols]
        + [jax.ShapeDtypeStruct((bsz, DIL_HEADS, seq, DIL_HD), dt) for dt in dil_types],
        scratch_shapes=[pltpu.VMEM((4 * DIL_HEADS, PERM_TILE, DIL_HD), F32)],
        compiler_params=pltpu.CompilerParams(
            dimension_semantics=("arbitrary", "arbitrary"), vmem_limit_bytes=VMEM_LIMIT),
        name="in_proj",
    )(x, mod_l, g_pre, w_in_pad, w_gate_pad, b_gate, cos, sin)


def _head_norm_gate(o, g, z):
    r = o * lax.rsqrt(jnp.mean(o * o, axis=-1, keepdims=True) + EPS)
    return r * g * _silu(z.astype(F32))


def _split2(v):
    h1 = v.astype(BF16)
    h2 = (v - h1.astype(F32)).astype(BF16)
    return h1, h2


def _gla_out_kernel(qa_ref, ka_ref, va_ref, za_ref, la_ref, g_ref, yb_ref, x_ref, mod_ref, wa_ref, wb_ref,
                    gpost_ref, o_ref, st_ref, ya_ref, *, chunks):
    @pl.when(pl.program_id(1) == 0)
    def _():
        st_ref[...] = jnp.zeros_like(st_ref)

    c_len = GLA_CHUNK
    grp = GLA_GROUP * c_len
    ri = lax.broadcasted_iota(jnp.int32, (grp, grp), 0)
    ci = lax.broadcasted_iota(jnp.int32, (grp, grp), 1)
    same_chunk = (ri & -c_len) == (ci & -c_len)
    tril_bd = ((ri >= ci) & same_chunk).astype(BF16)
    causal = (lax.broadcasted_iota(jnp.int32, (c_len, c_len), 0)
              >= lax.broadcasted_iota(jnp.int32, (c_len, c_len), 1))
    heads = range(GLA_HEADS)
    kcol = [slice(hd * GLA_DK, (hd + 1) * GLA_DK) for hd in heads]
    vcol = [slice(hd * GLA_DV, (hd + 1) * GLA_DV) for hd in heads]

    def prep(r0):
        rows = pl.ds(r0, grp)
        h1, h2 = _split2(la_ref[rows, :])
        b = _dot(tril_bd, h1) + _dot(tril_bd, h2)
        decay = [jnp.exp2(b[(c + 1) * c_len - 1:(c + 1) * c_len, :]) for c in range(GLA_GROUP)]
        decay_rows = jnp.concatenate([jnp.broadcast_to(d, (c_len, GLA_QK)) for d in decay], axis=0)
        k_e32 = ka_ref[rows, :].astype(F32) * jnp.exp2(-b)
        q_e = (qa_ref[rows, :].astype(F32) * jnp.exp2(b)).astype(BF16)
        k_end = (k_e32 * decay_rows).astype(BF16)
        v = [[va_ref[pl.ds(r0 + c * c_len, c_len), vcol[hd]] for hd in heads] for c in range(GLA_GROUP)]
        return dict(r0=r0, q_e=q_e, k_e=k_e32.astype(BF16), k_end=k_end, decay=decay, v=v)

    crow = [slice(c * c_len, (c + 1) * c_len) for c in range(GLA_GROUP)]

    def intra(p):
        q_e, k_e, k_end, v = p["q_e"], p["k_e"], p["k_end"], p["v"]
        a = [[jnp.where(causal, _dot_nt(q_e[crow[c], kcol[hd]], k_e[crow[c], kcol[hd]]), 0.0).astype(BF16)
              for hd in heads] for c in range(GLA_GROUP)]
        p["inc"] = [[_dot_tn(v[c][hd], k_end[crow[c], kcol[hd]]) for hd in heads] for c in range(GLA_GROUP)]
        p["o"] = [[_dot(a[c][hd], v[c][hd]) for hd in heads] for c in range(GLA_GROUP)]

    def inter(p, st):
        for hd in heads:
            for c in range(GLA_GROUP):
                p["o"][c][hd] = p["o"][c][hd] + _dot_nt(p["q_e"][crow[c], kcol[hd]], st[hd].astype(BF16))
                st[hd] = st[hd] * p["decay"][c][:, kcol[hd]] + p["inc"][c][hd]

    def epilogue(p):
        for c in range(GLA_GROUP):
            for hd in heads:
                out_rows = pl.ds(p["r0"] + c * c_len, c_len)
                ya_ref[out_rows, vcol[hd]] = _head_norm_gate(
                    p["o"][c][hd], g_ref[:, vcol[hd]], za_ref[out_rows, vcol[hd]]).astype(ya_ref.dtype)

    def project(p):
        rows = pl.ds(p["r0"], grp)
        y = _dot(ya_ref[rows, :], wa_ref[...]) + _dot(yb_ref[rows, :], wb_ref[...])
        r = y * lax.rsqrt(jnp.mean(y * y, axis=-1, keepdims=True) + EPS) * gpost_ref[...]
        o_ref[rows, :] = x_ref[rows, :] + mod_ref[:, 2 * D_MODEL:3 * D_MODEL] * r

    st = [st_ref[hd] for hd in heads]
    groups = [prep(g * grp) for g in range(chunks // GLA_GROUP)]
    intra(groups[0])
    for g, p in enumerate(groups):
        inter(p, st)
        if g + 1 < len(groups):
            intra(groups[g + 1])
        epilogue(p)
        project(p)
    for hd in heads:
        st_ref[hd] = st[hd]


def _gla_out_proj(qa, ka, va, za, la, g_gla, yb, x, mod_l, w_out_a, w_out_b, g_post):
    bsz, seq, d = x.shape
    ts = ROW_TILE
    row = lambda b, i: (b, i, 0)
    const = lambda b, i: (0, 0)
    return pl.pallas_call(
        functools.partial(_gla_out_kernel, chunks=ts // GLA_CHUNK),
        grid=(bsz, seq // ts),
        in_specs=[
            pl.BlockSpec((None, ts, GLA_QK), row),
            pl.BlockSpec((None, ts, GLA_QK), row),
            pl.BlockSpec((None, ts, GLA_WIDTH), row),
            pl.BlockSpec((None, ts, GLA_WIDTH), row),
            pl.BlockSpec((None, ts, GLA_QK), row),
            pl.BlockSpec((1, GLA_WIDTH), const),
            pl.BlockSpec((None, ts, DIL_WIDTH), row),
            pl.BlockSpec((None, ts, d), row),
            pl.BlockSpec((None, 1, 3 * d), lambda b, i: (b, 0, 0)),
            pl.BlockSpec(w_out_a.shape, const),
            pl.BlockSpec(w_out_b.shape, const),
            pl.BlockSpec((1, d), const),
        ],
        out_specs=pl.BlockSpec((None, ts, d), row),
        out_shape=jax.ShapeDtypeStruct((bsz, seq, d), F32),
        scratch_shapes=[pltpu.VMEM((GLA_HEADS, GLA_DV, GLA_DK), F32), pltpu.VMEM((ts, GLA_WIDTH), BF16)],
        compiler_params=pltpu.CompilerParams(
            dimension_semantics=("arbitrary", "arbitrary"), vmem_limit_bytes=VMEM_LIMIT),
        name="gla_out_proj",
    )(qa, ka, va, za, la, g_gla, yb, x, mod_l, w_out_a, w_out_b, g_post)


def _dil_block_chunks(pattern, idx, seq):
    window, dil = DIL_PATTERNS[pattern]
    nb = seq // window
    lb = DIL_LB
    n_r = PERM_TILE // PERM_D
    if dil == PERM_D:
        return [((idx % nb) * PERM_TILE + (idx // nb) * n_r, 1)]
    if dil == 1:
        tile, part = idx // (PERM_TILE // lb), idx % (PERM_TILE // lb)
        rows = lb // PERM_D
        return [(tile * PERM_TILE + r * n_r + part * rows, 1) for r in range(PERM_D)]
    sub = dil // PERM_D
    res, n = idx // nb, idx % nb
    r4, c = res % PERM_D, res // PERM_D
    tiles = window // PERM_TILE
    return [(n * window + t * PERM_TILE + r4 * n_r + c, sub) for t in range(tiles)]


def _dil_kernel(q_ref, k_ref, v_ref, z_ref, g_ref, y_ref, qd, kd, vd, ynat,
                o1, o2, o3, m1, m2, m3, d1, d2, d3, *, seq):
    lb = DIL_LB
    n_blocks = seq // lb
    qi = lax.broadcasted_iota(jnp.int32, (lb, 2 * lb), 0)
    ki = lax.broadcasted_iota(jnp.int32, (lb, 2 * lb), 1)

    def biases(pos_in_block):
        dist = pos_in_block(qi) + lb - (pos_in_block(ki & (lb - 1)) + (ki & lb))
        band = (dist >= 0) & (dist <= lb)
        b_any = jnp.where(band, 0.0, NEG_INF)
        b_first = jnp.where(band & (ki >= lb), 0.0, NEG_INF)
        return b_any, b_first, b_first[:, lb:]

    step_order = biases(lambda a: a)
    rows_p1 = lb // PERM_D
    p1_order = biases(lambda a: PERM_D * (a % rows_p1) + a // rows_p1)

    kd[0] = jnp.zeros((DIL_HD, lb), BF16)
    vd[0:lb, :] = jnp.zeros((lb, DIL_HD), BF16)

    n_r = PERM_TILE // PERM_D

    def combine(t0):
        rows = pl.ds(t0, PERM_TILE)
        a1, a2, a3 = m1[rows, :], m2[rows, :], m3[rows, :]
        m = jnp.maximum(jnp.maximum(a1, a2), a3)
        e1, e2, e3 = jnp.exp2(a1 - m), jnp.exp2(a2 - m), jnp.exp2(a3 - m)
        den = e1 * d1[rows, :] + e2 * d2[rows, :] + e3 * d3[rows, :]
        o = (e1 * o1[rows, :] + e2 * o2[rows, :] + e3 * o3[rows, :]) / den
        y = _head_norm_gate(o, g_ref[...], z_ref[rows, :])
        for r in range(PERM_D):
            ynat[pl.ds(t0 + r, n_r, stride=PERM_D), :] = y[r * n_r:(r + 1) * n_r, :]
        y_ref[rows, :] = ynat[rows, :].astype(y_ref.dtype)

    order = sorted(range(len(DIL_PATTERNS)), key=lambda i: -DIL_PATTERNS[i][1])
    assert DIL_PATTERNS[order[-1]][1] == 1
    for pat in order:
        window, dil = DIL_PATTERNS[pat]
        o_scr, m_scr, d_scr = (o1, o2, o3)[pat], (m1, m2, m3)[pat], (d1, d2, d3)[pat]
        nb = seq // window
        bias_any, bias_first, bias_cur = p1_order if dil == 1 else step_order

        def load_block(ref, idx, pat=pat):
            chunks = _dil_block_chunks(pat, idx, seq)
            rows = lb // len(chunks)
            parts = [ref[pl.ds(s0, rows) if st == 1 else pl.ds(s0, rows, stride=st), :] for s0, st in chunks]
            return parts[0] if len(parts) == 1 else jnp.concatenate(parts, axis=0)

        def store_block(ref, idx, val, pat=pat):
            chunks = _dil_block_chunks(pat, idx, seq)
            rows = lb // len(chunks)
            for i, (s0, st) in enumerate(chunks):
                dst = pl.ds(s0, rows) if st == 1 else pl.ds(s0, rows, stride=st)
                ref[dst, :] = val[i * rows:(i + 1) * rows, :]

        for idx in range(n_blocks):
            dst = pl.ds(lb + idx * lb, lb)
            qd[dst, :] = load_block(q_ref, idx).astype(BF16)
            kd[idx + 1] = load_block(k_ref, idx).T.astype(BF16)
            vd[dst, :] = load_block(v_ref, idx).astype(BF16)

        def scores(idx, nb=nb, bias_any=bias_any, bias_cur=bias_cur):
            q = qd[pl.ds(lb + idx * lb, lb), :]
            if idx % nb == 0:
                return _dot(q, kd[idx + 1]) + bias_cur, pl.ds(lb + idx * lb, lb)
            s = _dot(q, jnp.concatenate([kd[idx], kd[idx + 1]], axis=1)) + bias_any
            return s, pl.ds(idx * lb, 2 * lb)

        def softmax(s):
            m = jnp.max(s, axis=-1, keepdims=True)
            return jnp.exp2((s - m).astype(BF16)), m

        def values(idx, p, m, keys, store_block=store_block, o_scr=o_scr, m_scr=m_scr, d_scr=d_scr):
            v_ones = jnp.concatenate([vd[keys, :], jnp.ones((p.shape[1], DIL_HD), BF16)], axis=1)
            acc = _dot(p, v_ones)
            store_block(o_scr, idx, acc[:, :DIL_HD])
            store_block(d_scr, idx, acc[:, DIL_HD:])
            store_block(m_scr, idx, jnp.broadcast_to(m, (lb, DIL_HD)))

        groups = [range(g * DIL_UNROLL, (g + 1) * DIL_UNROLL) for g in range(n_blocks // DIL_UNROLL)]
        pending = [scores(idx) for idx in groups[0]]
        for g, group in enumerate(groups):
            upcoming = [scores(idx) for idx in groups[g + 1]] if g + 1 < len(groups) else []
            probs = [softmax(s) for s, _ in pending]
            for idx, (p, m), (_, keys) in zip(group, probs, pending):
                values(idx, p, m, keys)
            pending = upcoming
            if dil == 1:
                for t0 in range(group[0] * lb, (group[-1] + 1) * lb, PERM_TILE):
                    combine(t0)


def _dilated(qb, kb, vb, zb, g_dil):
    bsz, _, seq, _ = qb.shape
    blk = pl.BlockSpec((None, None, seq, DIL_HD), lambda b, h: (b, h, 0, 0))
    return pl.pallas_call(
        functools.partial(_dil_kernel, seq=seq),
        grid=(bsz, DIL_HEADS),
        in_specs=[blk, blk, blk, blk, pl.BlockSpec((1, DIL_HD), lambda b, h: (0, h))],
        out_specs=pl.BlockSpec((None, seq, DIL_HD), lambda b, h: (b, 0, h)),
        out_shape=jax.ShapeDtypeStruct((bsz, seq, DIL_WIDTH), BF16),
        scratch_shapes=[pltpu.VMEM((seq + DIL_LB, DIL_HD), BF16),
                        pltpu.VMEM((seq // DIL_LB + 1, DIL_HD, DIL_LB), BF16),
                        pltpu.VMEM((seq + DIL_LB, DIL_HD), BF16)]
        + [pltpu.VMEM((seq, DIL_HD), F32) for _ in range(10)],
        compiler_params=pltpu.CompilerParams(
            dimension_semantics=("arbitrary", "arbitrary"), vmem_limit_bytes=VMEM_LIMIT),
        name="dilated",
    )(qb, kb, vb, zb, g_dil)


def _rope_tables(seq):
    inv_freq = ROPE_THETA ** (-np.arange(0, DIL_HD, 2, dtype=np.float64) / DIL_HD)
    ang = np.arange(seq, dtype=np.float64)[:, None] * inv_freq[None, :]
    cos, sin = np.cos(ang), np.sin(ang)
    return (jnp.asarray(np.concatenate([cos, cos], axis=-1), F32),
            jnp.asarray(np.concatenate([-sin, sin], axis=-1), F32))


def _pad_w_in(w_in):
    lr_end = W_IN_COLS["lr"][0] + GLA_LOWRANK
    depth, d, _ = w_in.shape
    zeros = jnp.zeros((depth, d, LR_PAD - GLA_LOWRANK), w_in.dtype)
    return jnp.concatenate([w_in[:, :, :lr_end], zeros, w_in[:, :, lr_end:]], axis=-1).astype(BF16)


def kernel(x, c, w_ada, b_ada, g_pre, w_in, w_gate_up, b_gate_up, g_gla, g_dil, w_out, g_post):
    bsz, seq, d = x.shape
    depth = w_ada.shape[0]
    mod = _modulation(c, w_ada, b_ada).reshape(depth, bsz, 1, 3 * d)
    cos, sin = _rope_tables(seq)
    w_in_pad = _pad_w_in(w_in)
    w_gate_pad = jnp.pad(w_gate_up.astype(BF16), ((0, 0), (0, LR_PAD - GLA_LOWRANK), (0, 0)))
    for l in range(depth):
        qa, ka, va, za, la, qb, kb, vb, zb = _in_proj(
            x, mod[l], g_pre.reshape(depth, 1, d), w_in_pad, w_gate_pad, b_gate_up.reshape(depth, 1, GLA_QK),
            l, cos, sin)
        yb = _dilated(qb, kb, vb, zb, g_dil[l].reshape(1, DIL_WIDTH))
        w_o = w_out[l].astype(BF16)
        x = _gla_out_proj(qa, ka, va, za, la, g_gla[l].reshape(1, GLA_WIDTH), yb, x, mod[l],
                          w_o[:GLA_WIDTH], w_o[GLA_WIDTH:], g_post[l].reshape(1, d))
    return x
```

```python
import functools
import math

import jax
import jax.numpy as jnp
import numpy as np
from jax import lax
from jax.experimental import pallas as pl
from jax.experimental.pallas import tpu as pltpu

F32 = jnp.float32
BF16 = jnp.bfloat16

D_MODEL = 1024
GLA_HEADS = 4
GLA_DK = 64
GLA_DV = 128
GLA_QK = GLA_HEADS * GLA_DK
GLA_WIDTH = GLA_HEADS * GLA_DV
GLA_LOWRANK = 16
GLA_TAU = 16.0
GLA_CHUNK = 64
GLA_GROUP = 4
DIL_HEADS = 4
DIL_HD = 128
DIL_WIDTH = DIL_HEADS * DIL_HD
DIL_PATTERNS = ((128, 1), (512, 4), (2048, 16))
DIL_LB = 128
PERM_TILE, PERM_D = DIL_PATTERNS[1]
ROPE_THETA = 10000.0
EPS = 1e-6
LANES = 128
LR_PAD = LANES
_GLA_SIZES = (("qa", GLA_QK), ("ka", GLA_QK), ("va", GLA_WIDTH), ("za", GLA_WIDTH), ("lr", LR_PAD))
W_GLA_COLS = {}
for _name, _size in _GLA_SIZES:
    _lo = sum(n for _, n in _GLA_SIZES[:len(W_GLA_COLS)])
    W_GLA_COLS[_name] = (_lo, _lo + _size)
W_DIL_START = W_GLA_COLS["lr"][0] + GLA_LOWRANK
W_DIL_NAMES = ("qb", "kb", "vb", "zb")

VMEM_LIMIT = 56 * 1024 * 1024
DIL_UNROLL = 8
ROW_TILE = 1024
NEG_INF = float("-inf")
LOG2E = math.log2(math.e)


def _silu(v):
    return v * jax.nn.sigmoid(v)


def _dot(a, b):
    return jnp.dot(a, b, preferred_element_type=F32)


def _dot_nt(a, b):
    return lax.dot_general(a, b, (((1,), (1,)), ((), ())), preferred_element_type=F32)


def _dot_tn(a, b):
    return lax.dot_general(a, b, (((0,), (0,)), ((), ())), preferred_element_type=F32)


def _mod_kernel(c_ref, w_ref, b_ref, o_ref):
    sc = _silu(c_ref[...]).astype(BF16)
    o_ref[...] = _dot(sc, w_ref[...].astype(BF16)) + b_ref[...]


def _modulation(c, w_ada, b_ada):
    depth, d, e = w_ada.shape
    bsz = c.shape[0]
    return pl.pallas_call(
        _mod_kernel,
        grid=(depth,),
        in_specs=[
            pl.BlockSpec((bsz, d), lambda l: (0, 0)),
            pl.BlockSpec((None, d, e), lambda l: (l, 0, 0)),
            pl.BlockSpec((None, 1, e), lambda l: (l, 0, 0)),
        ],
        out_specs=pl.BlockSpec((None, bsz, e), lambda l: (l, 0, 0)),
        out_shape=jax.ShapeDtypeStruct((depth, bsz, e), F32),
        compiler_params=pltpu.CompilerParams(
            dimension_semantics=("arbitrary",), vmem_limit_bytes=VMEM_LIMIT),
        name="adaln_mod",
    )(c, w_ada, b_ada.reshape(depth, 1, e))


def _rope(v, cos, sin_signed):
    return v * cos + pltpu.roll(v, DIL_HD // 2, axis=1) * sin_signed


def _in_proj_kernel(x_ref, mod_ref, gpre_ref, w_ref, wg_ref, bg_ref, cos_ref, sin_ref,
                    qa_ref, ka_ref, va_ref, za_ref, la_ref, qb_ref, kb_ref, vb_ref, zb_ref, perm_ref, wd_ref):
    @pl.when((pl.program_id(0) == 0) & (pl.program_id(1) == 0))
    def _():
        wd_ref[...] = w_ref[:, W_DIL_START:W_DIL_START + len(W_DIL_NAMES) * DIL_WIDTH]

    shift = mod_ref[:, 0:D_MODEL]
    gain = gpre_ref[...] * (1.0 + mod_ref[:, D_MODEL:2 * D_MODEL])
    n_res = PERM_TILE // PERM_D

    def store_residue_order(out_ref, hd, slab, t0, val):
        perm_ref[slab] = val
        for r in range(PERM_D):
            dst = slice(t0 + r * n_res, t0 + (r + 1) * n_res)
            out_ref[hd, dst, :] = perm_ref[slab, pl.ds(r, n_res, stride=PERM_D), :].astype(out_ref.dtype)

    for t0 in range(0, x_ref.shape[0], PERM_TILE):
        rows = slice(t0, t0 + PERM_TILE)
        x = x_ref[rows, :]
        h = (x * lax.rsqrt(jnp.mean(x * x, axis=-1, keepdims=True) + EPS) * gain + shift).astype(BF16)

        def proj(name):
            if name in W_DIL_NAMES:
                lo = W_DIL_NAMES.index(name) * DIL_WIDTH
                return _dot(h, wd_ref[:, lo:lo + DIL_WIDTH])
            lo, hi = W_GLA_COLS[name]
            return _dot(h, w_ref[:, lo:hi])

        qa_ref[rows, :] = (proj("qa") * (GLA_DK ** -0.5)).astype(qa_ref.dtype)
        ka_ref[rows, :] = proj("ka").astype(ka_ref.dtype)
        va_ref[rows, :] = proj("va").astype(va_ref.dtype)
        za_ref[rows, :] = proj("za").astype(za_ref.dtype)

        lr = proj("lr").astype(BF16)
        z = _dot(lr, wg_ref[...]) + bg_ref[...]
        la_ref[rows, :] = (jnp.minimum(z, 0.0) - jnp.log1p(jnp.exp(-jnp.abs(z)))) * (LOG2E / GLA_TAU)

        cos = cos_ref[rows, :]
        sin = sin_ref[rows, :]
        q, k, v, zg = proj("qb"), proj("kb"), proj("vb"), proj("zb")
        for hd in range(DIL_HEADS):
            cols = slice(hd * DIL_HD, (hd + 1) * DIL_HD)
            store_residue_order(qb_ref, hd, 4 * hd, t0, _rope(q[:, cols], cos, sin) * (DIL_HD ** -0.5 * LOG2E))
            store_residue_order(kb_ref, hd, 4 * hd + 1, t0, _rope(k[:, cols], cos, sin))
            store_residue_order(vb_ref, hd, 4 * hd + 2, t0, v[:, cols])
            store_residue_order(zb_ref, hd, 4 * hd + 3, t0, zg[:, cols])


def _in_proj(x, mod_l, g_pre, w_in_pad, w_gate_pad, b_gate, layer, cos, sin):
    bsz, seq, d = x.shape
    tm = ROW_TILE
    row = lambda b, i: (b, i, 0)
    const = lambda b, i: (0, 0)

    def of_layer(a):
        return pl.BlockSpec((None,) + a.shape[1:], lambda b, i: (layer,) + (0,) * (a.ndim - 1),
                            pipeline_mode=pl.Buffered(1))

    gla_cols = ((GLA_QK, BF16), (GLA_QK, BF16), (GLA_WIDTH, BF16), (GLA_WIDTH, BF16), (GLA_QK, F32))
    dil_types = (F32, F32, F32, BF16)
    return pl.pallas_call(
        _in_proj_kernel,
        grid=(bsz, seq // tm),
        in_specs=[
            pl.BlockSpec((None, tm, d), row),
            pl.BlockSpec((None, 1, 3 * d), lambda b, i: (b, 0, 0)),
            of_layer(g_pre), of_layer(w_in_pad), of_layer(w_gate_pad), of_layer(b_gate),
            pl.BlockSpec((tm, DIL_HD), lambda b, i: (i, 0)),
            pl.BlockSpec((tm, DIL_HD), lambda b, i: (i, 0)),
        ],
        out_specs=[pl.BlockSpec((None, tm, n), row) for n, _ in gla_cols]
        + [pl.BlockSpec((None, DIL_HEADS, tm, DIL_HD), lambda b, i: (b, 0, i, 0)) for _ in dil_types],
        out_shape=[jax.ShapeDtypeStruct((bsz, seq, n), dt) for n, dt in gla_cols]
        + [jax.ShapeDtypeStruct((bsz, DIL_HEADS, seq, DIL_HD), dt) for dt in dil_types],
        scratch_shapes=[pltpu.VMEM((4 * DIL_HEADS, PERM_TILE, DIL_HD), F32),
                        pltpu.VMEM((d, len(W_DIL_NAMES) * DIL_WIDTH), BF16)],
        compiler_params=pltpu.CompilerParams(
            dimension_semantics=("arbitrary", "arbitrary"), vmem_limit_bytes=VMEM_LIMIT),
        name="in_proj",
    )(x, mod_l, g_pre, w_in_pad, w_gate_pad, b_gate, cos, sin)


def _head_norm_gate(o, g, z):
    r = o * lax.rsqrt(jnp.mean(o * o, axis=-1, keepdims=True) + EPS)
    return r * g * _silu(z.astype(F32))


def _split2(v):
    h1 = v.astype(BF16)
    h2 = (v - h1.astype(F32)).astype(BF16)
    return h1, h2


def _gla_out_kernel(qa_ref, ka_ref, va_ref, za_ref, la_ref, g_ref, yb_ref, x_ref, mod_ref, wa_ref, wb_ref,
                    gpost_ref, o_ref, st_ref, ya_ref, *, chunks):
    @pl.when(pl.program_id(1) == 0)
    def _():
        st_ref[...] = jnp.zeros_like(st_ref)

    c_len = GLA_CHUNK
    grp = GLA_GROUP * c_len
    ri = lax.broadcasted_iota(jnp.int32, (grp, grp), 0)
    ci = lax.broadcasted_iota(jnp.int32, (grp, grp), 1)
    same_chunk = (ri & -c_len) == (ci & -c_len)
    tril_bd = ((ri >= ci) & same_chunk).astype(BF16)
    causal = (lax.broadcasted_iota(jnp.int32, (c_len, c_len), 0)
              >= lax.broadcasted_iota(jnp.int32, (c_len, c_len), 1))
    heads = range(GLA_HEADS)
    kcol = [slice(hd * GLA_DK, (hd + 1) * GLA_DK) for hd in heads]
    vcol = [slice(hd * GLA_DV, (hd + 1) * GLA_DV) for hd in heads]

    def prep(r0):
        rows = pl.ds(r0, grp)
        h1, h2 = _split2(la_ref[rows, :])
        b = _dot(tril_bd, h1) + _dot(tril_bd, h2)
        decay = [jnp.exp2(b[(c + 1) * c_len - 1:(c + 1) * c_len, :]) for c in range(GLA_GROUP)]
        decay_rows = jnp.concatenate([jnp.broadcast_to(d, (c_len, GLA_QK)) for d in decay], axis=0)
        k_e32 = ka_ref[rows, :].astype(F32) * jnp.exp2(-b)
        q_e = (qa_ref[rows, :].astype(F32) * jnp.exp2(b)).astype(BF16)
        k_end = (k_e32 * decay_rows).astype(BF16)
        v = [[va_ref[pl.ds(r0 + c * c_len, c_len), vcol[hd]] for hd in heads] for c in range(GLA_GROUP)]
        return dict(r0=r0, q_e=q_e, k_e=k_e32.astype(BF16), k_end=k_end, decay=decay, v=v)

    crow = [slice(c * c_len, (c + 1) * c_len) for c in range(GLA_GROUP)]

    def intra(p):
        q_e, k_e, k_end, v = p["q_e"], p["k_e"], p["k_end"], p["v"]
        a = [[jnp.where(causal, _dot_nt(q_e[crow[c], kcol[hd]], k_e[crow[c], kcol[hd]]), 0.0).astype(BF16)
              for hd in heads] for c in range(GLA_GROUP)]
        p["inc"] = [[_dot_tn(v[c][hd], k_end[crow[c], kcol[hd]]) for hd in heads] for c in range(GLA_GROUP)]
        p["o"] = [[_dot(a[c][hd], v[c][hd]) for hd in heads] for c in range(GLA_GROUP)]

    def inter(p, st):
        for hd in heads:
            for c in range(GLA_GROUP):
                p["o"][c][hd] = p["o"][c][hd] + _dot_nt(p["q_e"][crow[c], kcol[hd]], st[hd].astype(BF16))
                st[hd] = st[hd] * p["decay"][c][:, kcol[hd]] + p["inc"][c][hd]

    def epilogue(p):
        for c in range(GLA_GROUP):
            for hd in heads:
                out_rows = pl.ds(p["r0"] + c * c_len, c_len)
                ya_ref[out_rows, vcol[hd]] = _head_norm_gate(
                    p["o"][c][hd], g_ref[:, vcol[hd]], za_ref[out_rows, vcol[hd]]).astype(ya_ref.dtype)

    gated_gain = gpost_ref[...] * mod_ref[:, 2 * D_MODEL:3 * D_MODEL]

    def project(p):
        rows = pl.ds(p["r0"], grp)
        y = _dot(ya_ref[rows, :], wa_ref[...]) + _dot(yb_ref[rows, :], wb_ref[...])
        o_ref[rows, :] = x_ref[rows, :] + y * lax.rsqrt(jnp.mean(y * y, axis=-1, keepdims=True) + EPS) * gated_gain

    st = [st_ref[hd] for hd in heads]
    groups = [prep(g * grp) for g in range(chunks // GLA_GROUP)]
    intra(groups[0])
    for g, p in enumerate(groups):
        inter(p, st)
        if g + 1 < len(groups):
            intra(groups[g + 1])
        epilogue(p)
        project(p)
    for hd in heads:
        st_ref[hd] = st[hd]


def _gla_out_proj(qa, ka, va, za, la, g_gla, yb, x, mod_l, w_out_a, w_out_b, g_post):
    bsz, seq, d = x.shape
    ts = ROW_TILE
    row = lambda b, i: (b, i, 0)
    const = lambda b, i: (0, 0)
    return pl.pallas_call(
        functools.partial(_gla_out_kernel, chunks=ts // GLA_CHUNK),
        grid=(bsz, seq // ts),
        in_specs=[
            pl.BlockSpec((None, ts, GLA_QK), row),
            pl.BlockSpec((None, ts, GLA_QK), row),
            pl.BlockSpec((None, ts, GLA_WIDTH), row),
            pl.BlockSpec((None, ts, GLA_WIDTH), row),
            pl.BlockSpec((None, ts, GLA_QK), row),
            pl.BlockSpec((1, GLA_WIDTH), const),
            pl.BlockSpec((None, ts, DIL_WIDTH), row),
            pl.BlockSpec((None, ts, d), row),
            pl.BlockSpec((None, 1, 3 * d), lambda b, i: (b, 0, 0)),
            pl.BlockSpec(w_out_a.shape, const),
            pl.BlockSpec(w_out_b.shape, const),
            pl.BlockSpec((1, d), const),
        ],
        out_specs=pl.BlockSpec((None, ts, d), row),
        out_shape=jax.ShapeDtypeStruct((bsz, seq, d), F32),
        scratch_shapes=[pltpu.VMEM((GLA_HEADS, GLA_DV, GLA_DK), F32), pltpu.VMEM((ts, GLA_WIDTH), BF16)],
        compiler_params=pltpu.CompilerParams(
            dimension_semantics=("arbitrary", "arbitrary"), vmem_limit_bytes=VMEM_LIMIT),
        name="gla_out_proj",
    )(qa, ka, va, za, la, g_gla, yb, x, mod_l, w_out_a, w_out_b, g_post)


def _dil_block_chunks(pattern, idx, seq):
    window, dil = DIL_PATTERNS[pattern]
    nb = seq // window
    lb = DIL_LB
    n_r = PERM_TILE // PERM_D
    if dil == PERM_D:
        return [((idx % nb) * PERM_TILE + (idx // nb) * n_r, 1)]
    if dil == 1:
        tile, part = idx // (PERM_TILE // lb), idx % (PERM_TILE // lb)
        rows = lb // PERM_D
        return [(tile * PERM_TILE + r * n_r + part * rows, 1) for r in range(PERM_D)]
    sub = dil // PERM_D
    res, n = idx // nb, idx % nb
    r4, c = res % PERM_D, res // PERM_D
    tiles = window // PERM_TILE
    return [(n * window + t * PERM_TILE + r4 * n_r + c, sub) for t in range(tiles)]


def _dil_kernel(q_ref, k_ref, v_ref, z_ref, g_ref, y_ref, qd, kd, vd, ynat,
                o1, o2, o3, m1, m2, m3, d1, d2, d3, *, seq):
    lb = DIL_LB
    n_blocks = seq // lb
    qi = lax.broadcasted_iota(jnp.int32, (lb, 2 * lb), 0)
    ki = lax.broadcasted_iota(jnp.int32, (lb, 2 * lb), 1)

    def biases(pos_in_block):
        dist = pos_in_block(qi) + lb - (pos_in_block(ki & (lb - 1)) + (ki & lb))
        band = (dist >= 0) & (dist <= lb)
        b_any = jnp.where(band, 0.0, NEG_INF)
        b_first = jnp.where(band & (ki >= lb), 0.0, NEG_INF)
        return b_any, b_first, b_first[:, lb:]

    step_order = biases(lambda a: a)
    rows_p1 = lb // PERM_D
    p1_order = biases(lambda a: PERM_D * (a % rows_p1) + a // rows_p1)

    kd[0] = jnp.zeros((DIL_HD, lb), BF16)
    vd[0:lb, :] = jnp.zeros((lb, DIL_HD), BF16)

    n_r = PERM_TILE // PERM_D

    def combine(t0):
        rows = pl.ds(t0, PERM_TILE)
        a1, a2, a3 = m1[rows, :], m2[rows, :], m3[rows, :]
        m = jnp.maximum(jnp.maximum(a1, a2), a3)
        e1, e2, e3 = jnp.exp2(a1 - m), jnp.exp2(a2 - m), jnp.exp2(a3 - m)
        den = e1 * d1[rows, :] + e2 * d2[rows, :] + e3 * d3[rows, :]
        o = (e1 * o1[rows, :] + e2 * o2[rows, :] + e3 * o3[rows, :]) / den
        y = _head_norm_gate(o, g_ref[...], z_ref[rows, :])
        for r in range(PERM_D):
            ynat[pl.ds(t0 + r, n_r, stride=PERM_D), :] = y[r * n_r:(r + 1) * n_r, :]
        y_ref[rows, :] = ynat[rows, :].astype(y_ref.dtype)

    order = sorted(range(len(DIL_PATTERNS)), key=lambda i: -DIL_PATTERNS[i][1])
    assert DIL_PATTERNS[order[-1]][1] == 1
    for pat in order:
        window, dil = DIL_PATTERNS[pat]
        o_scr, m_scr, d_scr = (o1, o2, o3)[pat], (m1, m2, m3)[pat], (d1, d2, d3)[pat]
        nb = seq // window
        bias_any, bias_first, bias_cur = p1_order if dil == 1 else step_order

        def load_block(ref, idx, pat=pat):
            chunks = _dil_block_chunks(pat, idx, seq)
            rows = lb // len(chunks)
            parts = [ref[pl.ds(s0, rows) if st == 1 else pl.ds(s0, rows, stride=st), :] for s0, st in chunks]
            return parts[0] if len(parts) == 1 else jnp.concatenate(parts, axis=0)

        def store_block(ref, idx, val, pat=pat):
            chunks = _dil_block_chunks(pat, idx, seq)
            rows = lb // len(chunks)
            for i, (s0, st) in enumerate(chunks):
                dst = pl.ds(s0, rows) if st == 1 else pl.ds(s0, rows, stride=st)
                ref[dst, :] = val[i * rows:(i + 1) * rows, :]

        for idx in range(n_blocks):
            dst = pl.ds(lb + idx * lb, lb)
            qd[dst, :] = load_block(q_ref, idx).astype(BF16)
            kd[idx + 1] = load_block(k_ref, idx).T.astype(BF16)
            vd[dst, :] = load_block(v_ref, idx).astype(BF16)

        def scores(idx, nb=nb, bias_any=bias_any, bias_cur=bias_cur):
            q = qd[pl.ds(lb + idx * lb, lb), :]
            if idx % nb == 0:
                return _dot(q, kd[idx + 1]) + bias_cur, pl.ds(lb + idx * lb, lb)
            s = _dot(q, jnp.concatenate([kd[idx], kd[idx + 1]], axis=1)) + bias_any
            return s, pl.ds(idx * lb, 2 * lb)

        def softmax(s):
            m = jnp.max(s, axis=-1, keepdims=True)
            return jnp.exp2((s - m).astype(BF16)), m

        def values(idx, p, m, keys, store_block=store_block, o_scr=o_scr, m_scr=m_scr, d_scr=d_scr):
            v_ones = jnp.concatenate([vd[keys, :], jnp.ones((p.shape[1], DIL_HD), BF16)], axis=1)
            acc = _dot(p, v_ones)
            store_block(o_scr, idx, acc[:, :DIL_HD])
            store_block(d_scr, idx, acc[:, DIL_HD:])
            store_block(m_scr, idx, jnp.broadcast_to(m, (lb, DIL_HD)))

        groups = [range(g * DIL_UNROLL, (g + 1) * DIL_UNROLL) for g in range(n_blocks // DIL_UNROLL)]
        pending = [scores(idx) for idx in groups[0]]
        for g, group in enumerate(groups):
            upcoming = [scores(idx) for idx in groups[g + 1]] if g + 1 < len(groups) else []
            probs = [softmax(s) for s, _ in pending]
            for idx, (p, m), (_, keys) in zip(group, probs, pending):
                values(idx, p, m, keys)
            pending = upcoming
            if dil == 1:
                for t0 in range(group[0] * lb, (group[-1] + 1) * lb, PERM_TILE):
                    combine(t0)


def _dilated(qb, kb, vb, zb, g_dil):
    bsz, _, seq, _ = qb.shape
    blk = pl.BlockSpec((None, None, seq, DIL_HD), lambda b, h: (b, h, 0, 0))
    return pl.pallas_call(
        functools.partial(_dil_kernel, seq=seq),
        grid=(bsz, DIL_HEADS),
        in_specs=[blk, blk, blk, blk, pl.BlockSpec((1, DIL_HD), lambda b, h: (0, h))],
        out_specs=pl.BlockSpec((None, seq, DIL_HD), lambda b, h: (b, 0, h)),
        out_shape=jax.ShapeDtypeStruct((bsz, seq, DIL_WIDTH), BF16),
        scratch_shapes=[pltpu.VMEM((seq + DIL_LB, DIL_HD), BF16),
                        pltpu.VMEM((seq // DIL_LB + 1, DIL_HD, DIL_LB), BF16),
                        pltpu.VMEM((seq + DIL_LB, DIL_HD), BF16)]
        + [pltpu.VMEM((seq, DIL_HD), F32) for _ in range(10)],
        compiler_params=pltpu.CompilerParams(
            dimension_semantics=("arbitrary", "arbitrary"), vmem_limit_bytes=VMEM_LIMIT),
        name="dilated",
    )(qb, kb, vb, zb, g_dil)


def _rope_tables(seq):
    inv_freq = ROPE_THETA ** (-np.arange(0, DIL_HD, 2, dtype=np.float64) / DIL_HD)
    ang = np.arange(seq, dtype=np.float64)[:, None] * inv_freq[None, :]
    cos, sin = np.cos(ang), np.sin(ang)
    return (jnp.asarray(np.concatenate([cos, cos], axis=-1), F32),
            jnp.asarray(np.concatenate([-sin, sin], axis=-1), F32))


def kernel(x, c, w_ada, b_ada, g_pre, w_in, w_gate_up, b_gate_up, g_gla, g_dil, w_out, g_post):
    bsz, seq, d = x.shape
    depth = w_ada.shape[0]
    mod = _modulation(c, w_ada, b_ada).reshape(depth, bsz, 1, 3 * d)
    cos, sin = _rope_tables(seq)
    w_in_pad = w_in.astype(BF16)
    w_gate_pad = jnp.pad(w_gate_up.astype(BF16), ((0, 0), (0, LR_PAD - GLA_LOWRANK), (0, 0)))
    for l in range(depth):
        qa, ka, va, za, la, qb, kb, vb, zb = _in_proj(
            x, mod[l], g_pre.reshape(depth, 1, d), w_in_pad, w_gate_pad, b_gate_up.reshape(depth, 1, GLA_QK),
            l, cos, sin)
        yb = _dilated(qb, kb, vb, zb, g_dil[l].reshape(1, DIL_WIDTH))
        w_o = w_out[l].astype(BF16)
        x = _gla_out_proj(qa, ka, va, za, la, g_gla[l].reshape(1, GLA_WIDTH), yb, x, mod[l],
                          w_o[:GLA_WIDTH], w_o[GLA_WIDTH:], g_post[l].reshape(1, d))
    return x
```

```python
import functools
import math

import jax
import jax.numpy as jnp
import numpy as np
from jax import lax
from jax.experimental import pallas as pl
from jax.experimental.pallas import tpu as pltpu

F32 = jnp.float32
BF16 = jnp.bfloat16

D_MODEL = 1024
GLA_HEADS = 4
GLA_DK = 64
GLA_DV = 128
GLA_QK = GLA_HEADS * GLA_DK
GLA_WIDTH = GLA_HEADS * GLA_DV
GLA_LOWRANK = 16
GLA_TAU = 16.0
GLA_CHUNK = 64
GLA_GROUP = 4
DIL_HEADS = 4
DIL_HD = 128
DIL_WIDTH = DIL_HEADS * DIL_HD
DIL_PATTERNS = ((128, 1), (512, 4), (2048, 16))
DIL_LB = 128
PERM_TILE, PERM_D = DIL_PATTERNS[1]
ROPE_THETA = 10000.0
EPS = 1e-6
LANES = 128
LR_PAD = LANES
_GLA_SIZES = (("qa", GLA_QK), ("ka", GLA_QK), ("va", GLA_WIDTH), ("za", GLA_WIDTH), ("lr", LR_PAD))
W_GLA_COLS = {}
for _name, _size in _GLA_SIZES:
    _lo = sum(n for _, n in _GLA_SIZES[:len(W_GLA_COLS)])
    W_GLA_COLS[_name] = (_lo, _lo + _size)
W_DIL_START = W_GLA_COLS["lr"][0] + GLA_LOWRANK
W_DIL_NAMES = ("qb", "kb", "vb", "zb")

VMEM_LIMIT = 56 * 1024 * 1024
DIL_UNROLL = 8
ROW_TILE = 1024
NEG_INF = float("-inf")
LOG2E = math.log2(math.e)


def _silu(v):
    return v * jax.nn.sigmoid(v)


def _dot(a, b):
    return jnp.dot(a, b, preferred_element_type=F32)


def _dot_nt(a, b):
    return lax.dot_general(a, b, (((1,), (1,)), ((), ())), preferred_element_type=F32)


def _dot_tn(a, b):
    return lax.dot_general(a, b, (((0,), (0,)), ((), ())), preferred_element_type=F32)


def _mod_kernel(c_ref, w_ref, b_ref, o_ref):
    sc = _silu(c_ref[...]).astype(BF16)
    o_ref[...] = _dot(sc, w_ref[...].astype(BF16)) + b_ref[...]


def _modulation(c, w_ada, b_ada):
    depth, d, e = w_ada.shape
    bsz = c.shape[0]
    return pl.pallas_call(
        _mod_kernel,
        grid=(depth,),
        in_specs=[
            pl.BlockSpec((bsz, d), lambda l: (0, 0)),
            pl.BlockSpec((None, d, e), lambda l: (l, 0, 0)),
            pl.BlockSpec((None, 1, e), lambda l: (l, 0, 0)),
        ],
        out_specs=pl.BlockSpec((None, bsz, e), lambda l: (l, 0, 0)),
        out_shape=jax.ShapeDtypeStruct((depth, bsz, e), F32),
        compiler_params=pltpu.CompilerParams(
            dimension_semantics=("arbitrary",), vmem_limit_bytes=VMEM_LIMIT),
        name="adaln_mod",
    )(c, w_ada, b_ada.reshape(depth, 1, e))


def _rope(v, cos, sin_signed):
    return v * cos + pltpu.roll(v, DIL_HD // 2, axis=1) * sin_signed


def _in_proj_kernel(x_ref, mod_ref, gpre_ref, w_ref, wg_ref, bg_ref, cos_ref, sin_ref,
                    qa_ref, ka_ref, va_ref, za_ref, la_ref, qb_ref, kb_ref, vb_ref, zb_ref,
                    perm_ref, wa_ref, wd_ref):
    @pl.when((pl.program_id(0) == 0) & (pl.program_id(1) == 0))
    def _():
        wa_ref[...] = w_ref[:, 0:W_GLA_COLS["lr"][1]].astype(BF16)
        wd_ref[...] = w_ref[:, W_DIL_START:W_DIL_START + len(W_DIL_NAMES) * DIL_WIDTH].astype(BF16)

    shift = mod_ref[:, 0:D_MODEL]
    gain = gpre_ref[...] * (1.0 + mod_ref[:, D_MODEL:2 * D_MODEL])
    n_res = PERM_TILE // PERM_D

    def store_residue_order(out_ref, hd, slab, t0, val):
        perm_ref[slab] = val
        for r in range(PERM_D):
            dst = slice(t0 + r * n_res, t0 + (r + 1) * n_res)
            out_ref[hd, dst, :] = perm_ref[slab, pl.ds(r, n_res, stride=PERM_D), :].astype(out_ref.dtype)

    for t0 in range(0, x_ref.shape[0], PERM_TILE):
        rows = slice(t0, t0 + PERM_TILE)
        x = x_ref[rows, :]
        h = (x * lax.rsqrt(jnp.mean(x * x, axis=-1, keepdims=True) + EPS) * gain + shift).astype(BF16)

        def proj(name):
            if name in W_DIL_NAMES:
                lo = W_DIL_NAMES.index(name) * DIL_WIDTH
                return _dot(h, wd_ref[:, lo:lo + DIL_WIDTH])
            lo, hi = W_GLA_COLS[name]
            return _dot(h, wa_ref[:, lo:hi])

        qa_ref[rows, :] = (proj("qa") * (GLA_DK ** -0.5)).astype(qa_ref.dtype)
        ka_ref[rows, :] = proj("ka").astype(ka_ref.dtype)
        va_ref[rows, :] = proj("va").astype(va_ref.dtype)
        za_ref[rows, :] = proj("za").astype(za_ref.dtype)

        lr = proj("lr").astype(BF16)
        z = _dot(lr, wg_ref[...]) + bg_ref[...]
        la_ref[rows, :] = (jnp.minimum(z, 0.0) - jnp.log1p(jnp.exp(-jnp.abs(z)))) * (LOG2E / GLA_TAU)

        cos = cos_ref[rows, :]
        sin = sin_ref[rows, :]
        q, k, v, zg = proj("qb"), proj("kb"), proj("vb"), proj("zb")
        for hd in range(DIL_HEADS):
            cols = slice(hd * DIL_HD, (hd + 1) * DIL_HD)
            store_residue_order(qb_ref, hd, 4 * hd, t0, _rope(q[:, cols], cos, sin) * (DIL_HD ** -0.5 * LOG2E))
            store_residue_order(kb_ref, hd, 4 * hd + 1, t0, _rope(k[:, cols], cos, sin))
            store_residue_order(vb_ref, hd, 4 * hd + 2, t0, v[:, cols])
            store_residue_order(zb_ref, hd, 4 * hd + 3, t0, zg[:, cols])


def _in_proj(x, mod_l, g_pre, w_in, w_gate_pad, b_gate, layer, cos, sin):
    bsz, seq, d = x.shape
    tm = PERM_TILE
    row = lambda b, i: (b, i, 0)
    const = lambda b, i: (0, 0)

    def of_layer(a):
        return pl.BlockSpec((None,) + a.shape[1:], lambda b, i: (layer,) + (0,) * (a.ndim - 1),
                            pipeline_mode=pl.Buffered(1))

    gla_cols = ((GLA_QK, BF16), (GLA_QK, BF16), (GLA_WIDTH, BF16), (GLA_WIDTH, BF16), (GLA_QK, F32))
    dil_types = (F32, F32, F32, BF16)
    return pl.pallas_call(
        _in_proj_kernel,
        grid=(bsz, seq // tm),
        in_specs=[
            pl.BlockSpec((None, tm, d), row),
            pl.BlockSpec((None, 1, 3 * d), lambda b, i: (b, 0, 0)),
            of_layer(g_pre), of_layer(w_in), of_layer(w_gate_pad), of_layer(b_gate),
            pl.BlockSpec((tm, DIL_HD), lambda b, i: (i, 0)),
            pl.BlockSpec((tm, DIL_HD), lambda b, i: (i, 0)),
        ],
        out_specs=[pl.BlockSpec((None, tm, n), row) for n, _ in gla_cols]
        + [pl.BlockSpec((None, DIL_HEADS, tm, DIL_HD), lambda b, i: (b, 0, i, 0)) for _ in dil_types],
        out_shape=[jax.ShapeDtypeStruct((bsz, seq, n), dt) for n, dt in gla_cols]
        + [jax.ShapeDtypeStruct((bsz, DIL_HEADS, seq, DIL_HD), dt) for dt in dil_types],
        scratch_shapes=[pltpu.VMEM((4 * DIL_HEADS, PERM_TILE, DIL_HD), F32),
                        pltpu.VMEM((d, W_GLA_COLS["lr"][1]), BF16),
                        pltpu.VMEM((d, len(W_DIL_NAMES) * DIL_WIDTH), BF16)],
        compiler_params=pltpu.CompilerParams(
            dimension_semantics=("arbitrary", "arbitrary"), vmem_limit_bytes=VMEM_LIMIT),
        name="in_proj",
    )(x, mod_l, g_pre, w_in, w_gate_pad, b_gate, cos, sin)


def _head_norm_gate(o, g, z):
    r = o * lax.rsqrt(jnp.mean(o * o, axis=-1, keepdims=True) + EPS)
    return r * g * _silu(z.astype(F32))


def _split2(v):
    h1 = v.astype(BF16)
    h2 = (v - h1.astype(F32)).astype(BF16)
    return h1, h2


def _gla_out_kernel(qa_ref, ka_ref, va_ref, za_ref, la_ref, g_ref, yb_ref, x_ref, mod_ref, wo_ref,
                    gpost_ref, o_ref, st_ref, ya_ref, wo_bf, *, chunks):
    @pl.when((pl.program_id(0) == 0) & (pl.program_id(1) == 0))
    def _():
        wo_bf[...] = wo_ref[...].astype(BF16)

    @pl.when(pl.program_id(1) == 0)
    def _():
        st_ref[...] = jnp.zeros_like(st_ref)

    c_len = GLA_CHUNK
    grp = GLA_GROUP * c_len
    ri = lax.broadcasted_iota(jnp.int32, (grp, grp), 0)
    ci = lax.broadcasted_iota(jnp.int32, (grp, grp), 1)
    same_chunk = (ri & -c_len) == (ci & -c_len)
    tril_bd = ((ri >= ci) & same_chunk).astype(BF16)
    causal = (lax.broadcasted_iota(jnp.int32, (c_len, c_len), 0)
              >= lax.broadcasted_iota(jnp.int32, (c_len, c_len), 1))
    heads = range(GLA_HEADS)
    kcol = [slice(hd * GLA_DK, (hd + 1) * GLA_DK) for hd in heads]
    vcol = [slice(hd * GLA_DV, (hd + 1) * GLA_DV) for hd in heads]

    def prep(r0):
        rows = pl.ds(r0, grp)
        h1, h2 = _split2(la_ref[rows, :])
        b = _dot(tril_bd, h1) + _dot(tril_bd, h2)
        decay = [jnp.exp2(b[(c + 1) * c_len - 1:(c + 1) * c_len, :]) for c in range(GLA_GROUP)]
        decay_rows = jnp.concatenate([jnp.broadcast_to(d, (c_len, GLA_QK)) for d in decay], axis=0)
        k_e32 = ka_ref[rows, :].astype(F32) * jnp.exp2(-b)
        q_e = (qa_ref[rows, :].astype(F32) * jnp.exp2(b)).astype(BF16)
        k_end = (k_e32 * decay_rows).astype(BF16)
        v = [[va_ref[pl.ds(r0 + c * c_len, c_len), vcol[hd]] for hd in heads] for c in range(GLA_GROUP)]
        return dict(r0=r0, q_e=q_e, k_e=k_e32.astype(BF16), k_end=k_end, decay=decay, v=v)

    crow = [slice(c * c_len, (c + 1) * c_len) for c in range(GLA_GROUP)]

    def intra(p):
        q_e, k_e, k_end, v = p["q_e"], p["k_e"], p["k_end"], p["v"]
        a = [[jnp.where(causal, _dot_nt(q_e[crow[c], kcol[hd]], k_e[crow[c], kcol[hd]]), 0.0).astype(BF16)
              for hd in heads] for c in range(GLA_GROUP)]
        p["inc"] = [[_dot_tn(v[c][hd], k_end[crow[c], kcol[hd]]) for hd in heads] for c in range(GLA_GROUP)]
        p["o"] = [[_dot(a[c][hd], v[c][hd]) for hd in heads] for c in range(GLA_GROUP)]

    def inter(p, st):
        for hd in heads:
            for c in range(GLA_GROUP):
                p["o"][c][hd] = p["o"][c][hd] + _dot_nt(p["q_e"][crow[c], kcol[hd]], st[hd].astype(BF16))
                st[hd] = st[hd] * p["decay"][c][:, kcol[hd]] + p["inc"][c][hd]

    def epilogue(p):
        for c in range(GLA_GROUP):
            for hd in heads:
                out_rows = pl.ds(p["r0"] + c * c_len, c_len)
                ya_ref[out_rows, vcol[hd]] = _head_norm_gate(
                    p["o"][c][hd], g_ref[:, vcol[hd]], za_ref[out_rows, vcol[hd]]).astype(ya_ref.dtype)

    gated_gain = gpost_ref[...] * mod_ref[:, 2 * D_MODEL:3 * D_MODEL]

    def project(p):
        rows = pl.ds(p["r0"], grp)
        y = _dot(ya_ref[rows, :], wo_bf[0:GLA_WIDTH, :]) + _dot(yb_ref[rows, :], wo_bf[GLA_WIDTH:, :])
        o_ref[rows, :] = x_ref[rows, :] + y * lax.rsqrt(jnp.mean(y * y, axis=-1, keepdims=True) + EPS) * gated_gain

    st = [st_ref[hd] for hd in heads]
    groups = [prep(g * grp) for g in range(chunks // GLA_GROUP)]
    intra(groups[0])
    for g, p in enumerate(groups):
        inter(p, st)
        if g + 1 < len(groups):
            intra(groups[g + 1])
        epilogue(p)
        project(p)
    for hd in heads:
        st_ref[hd] = st[hd]


def _gla_out_proj(qa, ka, va, za, la, g_gla, yb, x, mod_l, w_out, g_post, layer):
    bsz, seq, d = x.shape
    ts = ROW_TILE
    row = lambda b, i: (b, i, 0)
    const = lambda b, i: (0, 0)
    return pl.pallas_call(
        functools.partial(_gla_out_kernel, chunks=ts // GLA_CHUNK),
        grid=(bsz, seq // ts),
        in_specs=[
            pl.BlockSpec((None, ts, GLA_QK), row),
            pl.BlockSpec((None, ts, GLA_QK), row),
            pl.BlockSpec((None, ts, GLA_WIDTH), row),
            pl.BlockSpec((None, ts, GLA_WIDTH), row),
            pl.BlockSpec((None, ts, GLA_QK), row),
            pl.BlockSpec((1, GLA_WIDTH), const),
            pl.BlockSpec((None, ts, DIL_WIDTH), row),
            pl.BlockSpec((None, ts, d), row),
            pl.BlockSpec((None, 1, 3 * d), lambda b, i: (b, 0, 0)),
            pl.BlockSpec((None,) + w_out.shape[1:], lambda b, i: (layer, 0, 0), pipeline_mode=pl.Buffered(1)),
            pl.BlockSpec((1, d), const),
        ],
        out_specs=pl.BlockSpec((None, ts, d), row),
        out_shape=jax.ShapeDtypeStruct((bsz, seq, d), F32),
        scratch_shapes=[pltpu.VMEM((GLA_HEADS, GLA_DV, GLA_DK), F32), pltpu.VMEM((ts, GLA_WIDTH), BF16),
                        pltpu.VMEM(w_out.shape[1:], BF16)],
        compiler_params=pltpu.CompilerParams(
            dimension_semantics=("arbitrary", "arbitrary"), vmem_limit_bytes=VMEM_LIMIT),
        name="gla_out_proj",
    )(qa, ka, va, za, la, g_gla, yb, x, mod_l, w_out, g_post)


def _dil_block_chunks(pattern, idx, seq):
    window, dil = DIL_PATTERNS[pattern]
    nb = seq // window
    lb = DIL_LB
    n_r = PERM_TILE // PERM_D
    if dil == PERM_D:
        return [((idx % nb) * PERM_TILE + (idx // nb) * n_r, 1)]
    if dil == 1:
        tile, part = idx // (PERM_TILE // lb), idx % (PERM_TILE // lb)
        rows = lb // PERM_D
        return [(tile * PERM_TILE + r * n_r + part * rows, 1) for r in range(PERM_D)]
    sub = dil // PERM_D
    res, n = idx // nb, idx % nb
    r4, c = res % PERM_D, res // PERM_D
    tiles = window // PERM_TILE
    return [(n * window + t * PERM_TILE + r4 * n_r + c, sub) for t in range(tiles)]


def _dil_kernel(q_ref, k_ref, v_ref, z_ref, g_ref, y_ref, qd, kd, vd, ynat,
                o1, o2, o3, m1, m2, m3, d1, d2, d3, *, seq):
    lb = DIL_LB
    n_blocks = seq // lb
    qi = lax.broadcasted_iota(jnp.int32, (lb, 2 * lb), 0)
    ki = lax.broadcasted_iota(jnp.int32, (lb, 2 * lb), 1)

    def biases(pos_in_block):
        dist = pos_in_block(qi) + lb - (pos_in_block(ki & (lb - 1)) + (ki & lb))
        band = (dist >= 0) & (dist <= lb)
        b_any = jnp.where(band, 0.0, NEG_INF)
        b_first = jnp.where(band & (ki >= lb), 0.0, NEG_INF)
        return b_any, b_first, b_first[:, lb:]

    step_order = biases(lambda a: a)
    rows_p1 = lb // PERM_D
    p1_order = biases(lambda a: PERM_D * (a % rows_p1) + a // rows_p1)

    kd[0] = jnp.zeros((DIL_HD, lb), BF16)
    vd[0:lb, :] = jnp.zeros((lb, DIL_HD), BF16)

    n_r = PERM_TILE // PERM_D

    def combine(t0):
        rows = pl.ds(t0, PERM_TILE)
        a1, a2, a3 = m1[rows, :], m2[rows, :], m3[rows, :]
        m = jnp.maximum(jnp.maximum(a1, a2), a3)
        e1, e2, e3 = jnp.exp2(a1 - m), jnp.exp2(a2 - m), jnp.exp2(a3 - m)
        den = e1 * d1[rows, :] + e2 * d2[rows, :] + e3 * d3[rows, :]
        o = (e1 * o1[rows, :] + e2 * o2[rows, :] + e3 * o3[rows, :]) / den
        y = _head_norm_gate(o, g_ref[...], z_ref[rows, :])
        for r in range(PERM_D):
            ynat[pl.ds(t0 + r, n_r, stride=PERM_D), :] = y[r * n_r:(r + 1) * n_r, :]
        y_ref[rows, :] = ynat[rows, :].astype(y_ref.dtype)

    order = sorted(range(len(DIL_PATTERNS)), key=lambda i: -DIL_PATTERNS[i][1])
    assert DIL_PATTERNS[order[-1]][1] == 1
    for pat in order:
        window, dil = DIL_PATTERNS[pat]
        o_scr, m_scr, d_scr = (o1, o2, o3)[pat], (m1, m2, m3)[pat], (d1, d2, d3)[pat]
        nb = seq // window
        bias_any, bias_first, bias_cur = p1_order if dil == 1 else step_order

        def load_block(ref, idx, pat=pat):
            chunks = _dil_block_chunks(pat, idx, seq)
            rows = lb // len(chunks)
            parts = [ref[pl.ds(s0, rows) if st == 1 else pl.ds(s0, rows, stride=st), :] for s0, st in chunks]
            return parts[0] if len(parts) == 1 else jnp.concatenate(parts, axis=0)

        def store_block(ref, idx, val, pat=pat):
            chunks = _dil_block_chunks(pat, idx, seq)
            rows = lb // len(chunks)
            for i, (s0, st) in enumerate(chunks):
                dst = pl.ds(s0, rows) if st == 1 else pl.ds(s0, rows, stride=st)
                ref[dst, :] = val[i * rows:(i + 1) * rows, :]

        for idx in range(n_blocks):
            dst = pl.ds(lb + idx * lb, lb)
            qd[dst, :] = load_block(q_ref, idx).astype(BF16)
            kd[idx + 1] = load_block(k_ref, idx).T.astype(BF16)
            vd[dst, :] = load_block(v_ref, idx).astype(BF16)

        def scores(idx, nb=nb, bias_any=bias_any, bias_cur=bias_cur):
            q = qd[pl.ds(lb + idx * lb, lb), :]
            if idx % nb == 0:
                return _dot(q, kd[idx + 1]) + bias_cur, pl.ds(lb + idx * lb, lb)
            s = _dot(q, jnp.concatenate([kd[idx], kd[idx + 1]], axis=1)) + bias_any
            return s, pl.ds(idx * lb, 2 * lb)

        def softmax(s):
            m = jnp.max(s, axis=-1, keepdims=True)
            return jnp.exp2((s - m).astype(BF16)), m

        def values(idx, p, m, keys, store_block=store_block, o_scr=o_scr, m_scr=m_scr, d_scr=d_scr):
            v_ones = jnp.concatenate([vd[keys, :], jnp.ones((p.shape[1], DIL_HD), BF16)], axis=1)
            acc = _dot(p, v_ones)
            store_block(o_scr, idx, acc[:, :DIL_HD])
            store_block(d_scr, idx, acc[:, DIL_HD:])
            store_block(m_scr, idx, jnp.broadcast_to(m, (lb, DIL_HD)))

        groups = [range(g * DIL_UNROLL, (g + 1) * DIL_UNROLL) for g in range(n_blocks // DIL_UNROLL)]
        pending = [scores(idx) for idx in groups[0]]
        for g, group in enumerate(groups):
            upcoming = [scores(idx) for idx in groups[g + 1]] if g + 1 < len(groups) else []
            probs = [softmax(s) for s, _ in pending]
            for idx, (p, m), (_, keys) in zip(group, probs, pending):
                values(idx, p, m, keys)
            pending = upcoming
            if dil == 1:
                for t0 in range(group[0] * lb, (group[-1] + 1) * lb, PERM_TILE):
                    combine(t0)


def _dilated(qb, kb, vb, zb, g_dil):
    bsz, _, seq, _ = qb.shape
    blk = pl.BlockSpec((None, None, seq, DIL_HD), lambda b, h: (b, h, 0, 0))
    return pl.pallas_call(
        functools.partial(_dil_kernel, seq=seq),
        grid=(bsz, DIL_HEADS),
        in_specs=[blk, blk, blk, blk, pl.BlockSpec((1, DIL_HD), lambda b, h: (0, h))],
        out_specs=pl.BlockSpec((None, seq, DIL_HD), lambda b, h: (b, 0, h)),
        out_shape=jax.ShapeDtypeStruct((bsz, seq, DIL_WIDTH), BF16),
        scratch_shapes=[pltpu.VMEM((seq + DIL_LB, DIL_HD), BF16),
                        pltpu.VMEM((seq // DIL_LB + 1, DIL_HD, DIL_LB), BF16),
                        pltpu.VMEM((seq + DIL_LB, DIL_HD), BF16)]
        + [pltpu.VMEM((seq, DIL_HD), F32) for _ in range(10)],
        compiler_params=pltpu.CompilerParams(
            dimension_semantics=("arbitrary", "arbitrary"), vmem_limit_bytes=VMEM_LIMIT),
        name="dilated",
    )(qb, kb, vb, zb, g_dil)


def _rope_tables(seq):
    inv_freq = ROPE_THETA ** (-np.arange(0, DIL_HD, 2, dtype=np.float64) / DIL_HD)
    ang = np.arange(seq, dtype=np.float64)[:, None] * inv_freq[None, :]
    cos, sin = np.cos(ang), np.sin(ang)
    return (jnp.asarray(np.concatenate([cos, cos], axis=-1), F32),
            jnp.asarray(np.concatenate([-sin, sin], axis=-1), F32))


def kernel(x, c, w_ada, b_ada, g_pre, w_in, w_gate_up, b_gate_up, g_gla, g_dil, w_out, g_post):
    bsz, seq, d = x.shape
    depth = w_ada.shape[0]
    mod = _modulation(c, w_ada, b_ada).reshape(depth, bsz, 1, 3 * d)
    cos, sin = _rope_tables(seq)
    w_gate_pad = jnp.pad(w_gate_up.astype(BF16), ((0, 0), (0, LR_PAD - GLA_LOWRANK), (0, 0)))
    for l in range(depth):
        qa, ka, va, za, la, qb, kb, vb, zb = _in_proj(
            x, mod[l], g_pre.reshape(depth, 1, d), w_in, w_gate_pad, b_gate_up.reshape(depth, 1, GLA_QK),
            l, cos, sin)
        yb = _dilated(qb, kb, vb, zb, g_dil[l].reshape(1, DIL_WIDTH))
        x = _gla_out_proj(qa, ka, va, za, la, g_gla[l].reshape(1, GLA_WIDTH), yb, x, mod[l],
                          w_out, g_post[l].reshape(1, d), l)
    return x
```

```python
import functools
import math

import jax
import jax.numpy as jnp
import numpy as np
from jax import lax
from jax.experimental import pallas as pl
from jax.experimental.pallas import tpu as pltpu

F32 = jnp.float32
BF16 = jnp.bfloat16

D_MODEL = 1024
GLA_HEADS = 4
GLA_DK = 64
GLA_DV = 128
GLA_QK = GLA_HEADS * GLA_DK
GLA_WIDTH = GLA_HEADS * GLA_DV
GLA_LOWRANK = 16
GLA_TAU = 16.0
GLA_CHUNK = 64
GLA_GROUP = 4
DIL_HEADS = 4
DIL_HD = 128
DIL_WIDTH = DIL_HEADS * DIL_HD
DIL_PATTERNS = ((128, 1), (512, 4), (2048, 16))
DIL_LB = 128
PERM_TILE, PERM_D = DIL_PATTERNS[1]
ROPE_THETA = 10000.0
EPS = 1e-6
LANES = 128
LR_PAD = LANES
_GLA_SIZES = (("qa", GLA_QK), ("ka", GLA_QK), ("va", GLA_WIDTH), ("za", GLA_WIDTH), ("lr", LR_PAD))
W_GLA_COLS = {}
for _name, _size in _GLA_SIZES:
    _lo = sum(n for _, n in _GLA_SIZES[:len(W_GLA_COLS)])
    W_GLA_COLS[_name] = (_lo, _lo + _size)
W_DIL_START = W_GLA_COLS["lr"][0] + GLA_LOWRANK
W_DIL_NAMES = ("qb", "kb", "vb", "zb")

VMEM_LIMIT = 56 * 1024 * 1024
DIL_UNROLL = 16
ROW_TILE = 1024
NEG_INF = float("-inf")
LOG2E = math.log2(math.e)


def _silu(v):
    return v * jax.nn.sigmoid(v)


def _dot(a, b):
    return jnp.dot(a, b, preferred_element_type=F32)


def _dot_nt(a, b):
    return lax.dot_general(a, b, (((1,), (1,)), ((), ())), preferred_element_type=F32)


def _dot_tn(a, b):
    return lax.dot_general(a, b, (((0,), (0,)), ((), ())), preferred_element_type=F32)


def _mod_kernel(c_ref, w_ref, b_ref, o_ref):
    sc = _silu(c_ref[...]).astype(BF16)
    o_ref[...] = _dot(sc, w_ref[...].astype(BF16)) + b_ref[...]


def _modulation(c, w_ada, b_ada):
    depth, d, e = w_ada.shape
    bsz = c.shape[0]
    return pl.pallas_call(
        _mod_kernel,
        grid=(depth,),
        in_specs=[
            pl.BlockSpec((bsz, d), lambda l: (0, 0)),
            pl.BlockSpec((None, d, e), lambda l: (l, 0, 0)),
            pl.BlockSpec((None, 1, e), lambda l: (l, 0, 0)),
        ],
        out_specs=pl.BlockSpec((None, bsz, e), lambda l: (l, 0, 0)),
        out_shape=jax.ShapeDtypeStruct((depth, bsz, e), F32),
        compiler_params=pltpu.CompilerParams(
            dimension_semantics=("arbitrary",), vmem_limit_bytes=VMEM_LIMIT),
        name="adaln_mod",
    )(c, w_ada, b_ada.reshape(depth, 1, e))


def _rope(v, cos, sin_signed):
    return v * cos + pltpu.roll(v, DIL_HD // 2, axis=1) * sin_signed


def _in_proj_kernel(x_ref, mod_ref, gpre_ref, w_ref, wg_ref, bg_ref, cos_ref, sin_ref,
                    qa_ref, ka_ref, va_ref, za_ref, la_ref, qb_ref, kb_ref, vb_ref, zb_ref, perm_ref, wd_ref):
    @pl.when((pl.program_id(0) == 0) & (pl.program_id(1) == 0))
    def _():
        wd_ref[...] = w_ref[:, W_DIL_START:W_DIL_START + len(W_DIL_NAMES) * DIL_WIDTH]

    shift = mod_ref[:, 0:D_MODEL]
    gain = gpre_ref[...] * (1.0 + mod_ref[:, D_MODEL:2 * D_MODEL])
    n_res = PERM_TILE // PERM_D

    def store_residue_order(out_ref, hd, slab, t0, val):
        perm_ref[slab] = val
        for r in range(PERM_D):
            dst = slice(t0 + r * n_res, t0 + (r + 1) * n_res)
            out_ref[hd, dst, :] = perm_ref[slab, pl.ds(r, n_res, stride=PERM_D), :].astype(out_ref.dtype)

    for t0 in range(0, x_ref.shape[0], PERM_TILE):
        rows = slice(t0, t0 + PERM_TILE)
        x = x_ref[rows, :]
        h = (x * lax.rsqrt(jnp.mean(x * x, axis=-1, keepdims=True) + EPS) * gain + shift).astype(BF16)

        def proj(name):
            if name in W_DIL_NAMES:
                lo = W_DIL_NAMES.index(name) * DIL_WIDTH
                return _dot(h, wd_ref[:, lo:lo + DIL_WIDTH])
            lo, hi = W_GLA_COLS[name]
            return _dot(h, w_ref[:, lo:hi])

        qa_ref[rows, :] = (proj("qa") * (GLA_DK ** -0.5)).astype(qa_ref.dtype)
        ka_ref[rows, :] = proj("ka").astype(ka_ref.dtype)
        va_ref[rows, :] = proj("va").astype(va_ref.dtype)
        za_ref[rows, :] = proj("za").astype(za_ref.dtype)

        lr = proj("lr").astype(BF16)
        z = _dot(lr, wg_ref[...]) + bg_ref[...]
        la_ref[rows, :] = (jnp.minimum(z, 0.0) - jnp.log1p(jnp.exp(-jnp.abs(z)))) * (LOG2E / GLA_TAU)

        cos = cos_ref[rows, :]
        sin = sin_ref[rows, :]
        q, k, v, zg = proj("qb"), proj("kb"), proj("vb"), proj("zb")
        for hd in range(DIL_HEADS):
            cols = slice(hd * DIL_HD, (hd + 1) * DIL_HD)
            store_residue_order(qb_ref, hd, 4 * hd, t0, _rope(q[:, cols], cos, sin) * (DIL_HD ** -0.5 * LOG2E))
            store_residue_order(kb_ref, hd, 4 * hd + 1, t0, _rope(k[:, cols], cos, sin))
            store_residue_order(vb_ref, hd, 4 * hd + 2, t0, v[:, cols])
            store_residue_order(zb_ref, hd, 4 * hd + 3, t0, zg[:, cols])


def _in_proj(x, mod_l, g_pre, w_in_bf, w_gate_pad, b_gate, layer, cos, sin):
    bsz, seq, d = x.shape
    tm = ROW_TILE
    row = lambda b, i: (b, i, 0)
    const = lambda b, i: (0, 0)

    def of_layer(a):
        return pl.BlockSpec((None,) + a.shape[1:], lambda b, i: (layer,) + (0,) * (a.ndim - 1),
                            pipeline_mode=pl.Buffered(1))

    gla_cols = ((GLA_QK, BF16), (GLA_QK, BF16), (GLA_WIDTH, BF16), (GLA_WIDTH, BF16), (GLA_QK, F32))
    dil_types = (F32, F32, F32, BF16)
    return pl.pallas_call(
        _in_proj_kernel,
        grid=(bsz, seq // tm),
        in_specs=[
            pl.BlockSpec((None, tm, d), row),
            pl.BlockSpec((None, 1, 3 * d), lambda b, i: (b, 0, 0)),
            of_layer(g_pre), of_layer(w_in_bf), of_layer(w_gate_pad), of_layer(b_gate),
            pl.BlockSpec((tm, DIL_HD), lambda b, i: (i, 0)),
            pl.BlockSpec((tm, DIL_HD), lambda b, i: (i, 0)),
        ],
        out_specs=[pl.BlockSpec((None, tm, n), row) for n, _ in gla_cols]
        + [pl.BlockSpec((None, DIL_HEADS, tm, DIL_HD), lambda b, i: (b, 0, i, 0)) for _ in dil_types],
        out_shape=[jax.ShapeDtypeStruct((bsz, seq, n), dt) for n, dt in gla_cols]
        + [jax.ShapeDtypeStruct((bsz, DIL_HEADS, seq, DIL_HD), dt) for dt in dil_types],
        scratch_shapes=[pltpu.VMEM((4 * DIL_HEADS, PERM_TILE, DIL_HD), F32),
                        pltpu.VMEM((d, len(W_DIL_NAMES) * DIL_WIDTH), BF16)],
        compiler_params=pltpu.CompilerParams(
            dimension_semantics=("arbitrary", "arbitrary"), vmem_limit_bytes=VMEM_LIMIT),
        name="in_proj",
    )(x, mod_l, g_pre, w_in_bf, w_gate_pad, b_gate, cos, sin)


def _head_norm_gate(o, g, z):
    r = o * lax.rsqrt(jnp.mean(o * o, axis=-1, keepdims=True) + EPS)
    return r * g * _silu(z.astype(F32))


def _split2(v):
    h1 = v.astype(BF16)
    h2 = (v - h1.astype(F32)).astype(BF16)
    return h1, h2


def _gla_out_kernel(qa_ref, ka_ref, va_ref, za_ref, la_ref, g_ref, yb_ref, x_ref, mod_ref, wo_ref,
                    gpost_ref, o_ref, st_ref, ya_ref, wo_bf, *, chunks):
    @pl.when((pl.program_id(0) == 0) & (pl.program_id(1) == 0))
    def _():
        wo_bf[...] = wo_ref[...].astype(BF16)

    @pl.when(pl.program_id(1) == 0)
    def _():
        st_ref[...] = jnp.zeros_like(st_ref)

    c_len = GLA_CHUNK
    grp = GLA_GROUP * c_len
    ri = lax.broadcasted_iota(jnp.int32, (grp, grp), 0)
    ci = lax.broadcasted_iota(jnp.int32, (grp, grp), 1)
    same_chunk = (ri & -c_len) == (ci & -c_len)
    tril_bd = ((ri >= ci) & same_chunk).astype(BF16)
    causal = (lax.broadcasted_iota(jnp.int32, (c_len, c_len), 0)
              >= lax.broadcasted_iota(jnp.int32, (c_len, c_len), 1))
    heads = range(GLA_HEADS)
    kcol = [slice(hd * GLA_DK, (hd + 1) * GLA_DK) for hd in heads]
    vcol = [slice(hd * GLA_DV, (hd + 1) * GLA_DV) for hd in heads]

    def prep(r0):
        rows = pl.ds(r0, grp)
        h1, h2 = _split2(la_ref[rows, :])
        b = _dot(tril_bd, h1) + _dot(tril_bd, h2)
        decay = [jnp.exp2(b[(c + 1) * c_len - 1:(c + 1) * c_len, :]) for c in range(GLA_GROUP)]
        decay_rows = jnp.concatenate([jnp.broadcast_to(d, (c_len, GLA_QK)) for d in decay], axis=0)
        k_e32 = ka_ref[rows, :].astype(F32) * jnp.exp2(-b)
        q_e = (qa_ref[rows, :].astype(F32) * jnp.exp2(b)).astype(BF16)
        k_end = (k_e32 * decay_rows).astype(BF16)
        v = [[va_ref[pl.ds(r0 + c * c_len, c_len), vcol[hd]] for hd in heads] for c in range(GLA_GROUP)]
        return dict(r0=r0, q_e=q_e, k_e=k_e32.astype(BF16), k_end=k_end, decay=decay, v=v)

    crow = [slice(c * c_len, (c + 1) * c_len) for c in range(GLA_GROUP)]

    def intra(p):
        q_e, k_e, k_end, v = p["q_e"], p["k_e"], p["k_end"], p["v"]
        a = [[jnp.where(causal, _dot_nt(q_e[crow[c], kcol[hd]], k_e[crow[c], kcol[hd]]), 0.0).astype(BF16)
              for hd in heads] for c in range(GLA_GROUP)]
        p["inc"] = [[_dot_tn(v[c][hd], k_end[crow[c], kcol[hd]]) for hd in heads] for c in range(GLA_GROUP)]
        p["o"] = [[_dot(a[c][hd], v[c][hd]) for hd in heads] for c in range(GLA_GROUP)]

    def inter(p, st):
        for hd in heads:
            for c in range(GLA_GROUP):
                p["o"][c][hd] = p["o"][c][hd] + _dot_nt(p["q_e"][crow[c], kcol[hd]], st[hd].astype(BF16))
                st[hd] = st[hd] * p["decay"][c][:, kcol[hd]] + p["inc"][c][hd]

    def epilogue(p):
        for c in range(GLA_GROUP):
            for hd in heads:
                out_rows = pl.ds(p["r0"] + c * c_len, c_len)
                ya_ref[out_rows, vcol[hd]] = _head_norm_gate(
                    p["o"][c][hd], g_ref[:, vcol[hd]], za_ref[out_rows, vcol[hd]]).astype(ya_ref.dtype)

    gated_gain = gpost_ref[...] * mod_ref[:, 2 * D_MODEL:3 * D_MODEL]

    def project(p):
        rows = pl.ds(p["r0"], grp)
        y = _dot(ya_ref[rows, :], wo_bf[0:GLA_WIDTH, :]) + _dot(yb_ref[rows, :], wo_bf[GLA_WIDTH:, :])
        o_ref[rows, :] = x_ref[rows, :] + y * lax.rsqrt(jnp.mean(y * y, axis=-1, keepdims=True) + EPS) * gated_gain

    st = [st_ref[hd] for hd in heads]
    groups = [prep(g * grp) for g in range(chunks // GLA_GROUP)]
    intra(groups[0])
    for g, p in enumerate(groups):
        inter(p, st)
        if g + 1 < len(groups):
            intra(groups[g + 1])
        epilogue(p)
        project(p)
    for hd in heads:
        st_ref[hd] = st[hd]


def _gla_out_proj(qa, ka, va, za, la, g_gla, yb, x, mod_l, w_out, g_post, layer):
    bsz, seq, d = x.shape
    ts = ROW_TILE
    row = lambda b, i: (b, i, 0)
    const = lambda b, i: (0, 0)
    return pl.pallas_call(
        functools.partial(_gla_out_kernel, chunks=ts // GLA_CHUNK),
        grid=(bsz, seq // ts),
        in_specs=[
            pl.BlockSpec((None, ts, GLA_QK), row),
            pl.BlockSpec((None, ts, GLA_QK), row),
            pl.BlockSpec((None, ts, GLA_WIDTH), row),
            pl.BlockSpec((None, ts, GLA_WIDTH), row),
            pl.BlockSpec((None, ts, GLA_QK), row),
            pl.BlockSpec((1, GLA_WIDTH), const),
            pl.BlockSpec((None, ts, DIL_WIDTH), row),
            pl.BlockSpec((None, ts, d), row),
            pl.BlockSpec((None, 1, 3 * d), lambda b, i: (b, 0, 0)),
            pl.BlockSpec((None,) + w_out.shape[1:], lambda b, i: (layer, 0, 0), pipeline_mode=pl.Buffered(1)),
            pl.BlockSpec((1, d), const),
        ],
        out_specs=pl.BlockSpec((None, ts, d), row),
        out_shape=jax.ShapeDtypeStruct((bsz, seq, d), F32),
        scratch_shapes=[pltpu.VMEM((GLA_HEADS, GLA_DV, GLA_DK), F32), pltpu.VMEM((ts, GLA_WIDTH), BF16),
                        pltpu.VMEM(w_out.shape[1:], BF16)],
        compiler_params=pltpu.CompilerParams(
            dimension_semantics=("arbitrary", "arbitrary"), vmem_limit_bytes=VMEM_LIMIT),
        name="gla_out_proj",
    )(qa, ka, va, za, la, g_gla, yb, x, mod_l, w_out, g_post)


def _dil_block_chunks(pattern, idx, seq):
    window, dil = DIL_PATTERNS[pattern]
    nb = seq // window
    lb = DIL_LB
    n_r = PERM_TILE // PERM_D
    if dil == PERM_D:
        return [((idx % nb) * PERM_TILE + (idx // nb) * n_r, 1)]
    if dil == 1:
        tile, part = idx // (PERM_TILE // lb), idx % (PERM_TILE // lb)
        rows = lb // PERM_D
        return [(tile * PERM_TILE + r * n_r + part * rows, 1) for r in range(PERM_D)]
    sub = dil // PERM_D
    res, n = idx // nb, idx % nb
    r4, c = res % PERM_D, res // PERM_D
    tiles = window // PERM_TILE
    return [(n * window + t * PERM_TILE + r4 * n_r + c, sub) for t in range(tiles)]


def _dil_kernel(q_ref, k_ref, v_ref, z_ref, g_ref, y_ref, qd, kd, vd, ynat,
                o1, o2, o3, m1, m2, m3, d1, d2, d3, *, seq):
    lb = DIL_LB
    n_blocks = seq // lb
    qi = lax.broadcasted_iota(jnp.int32, (lb, 2 * lb), 0)
    ki = lax.broadcasted_iota(jnp.int32, (lb, 2 * lb), 1)

    def biases(pos_in_block):
        dist = pos_in_block(qi) + lb - (pos_in_block(ki & (lb - 1)) + (ki & lb))
        band = (dist >= 0) & (dist <= lb)
        b_any = jnp.where(band, 0.0, NEG_INF)
        b_first = jnp.where(band & (ki >= lb), 0.0, NEG_INF)
        return b_any, b_first, b_first[:, lb:]

    step_order = biases(lambda a: a)
    rows_p1 = lb // PERM_D
    p1_order = biases(lambda a: PERM_D * (a % rows_p1) + a // rows_p1)

    kd[0] = jnp.zeros((DIL_HD, lb), BF16)
    vd[0:lb, :] = jnp.zeros((lb, DIL_HD), BF16)

    n_r = PERM_TILE // PERM_D

    def combine(t0):
        rows = pl.ds(t0, PERM_TILE)
        a1, a2, a3 = m1[rows, :], m2[rows, :], m3[rows, :]
        m = jnp.maximum(jnp.maximum(a1, a2), a3)
        e1, e2, e3 = jnp.exp2(a1 - m), jnp.exp2(a2 - m), jnp.exp2(a3 - m)
        den = e1 * d1[rows, :] + e2 * d2[rows, :] + e3 * d3[rows, :]
        o = (e1 * o1[rows, :] + e2 * o2[rows, :] + e3 * o3[rows, :]) / den
        y = _head_norm_gate(o, g_ref[...], z_ref[rows, :])
        for r in range(PERM_D):
            ynat[pl.ds(t0 + r, n_r, stride=PERM_D), :] = y[r * n_r:(r + 1) * n_r, :]
        y_ref[rows, :] = ynat[rows, :].astype(y_ref.dtype)

    order = sorted(range(len(DIL_PATTERNS)), key=lambda i: -DIL_PATTERNS[i][1])
    assert DIL_PATTERNS[order[-1]][1] == 1
    for pat in order:
        window, dil = DIL_PATTERNS[pat]
        o_scr, m_scr, d_scr = (o1, o2, o3)[pat], (m1, m2, m3)[pat], (d1, d2, d3)[pat]
        nb = seq // window
        bias_any, bias_first, bias_cur = p1_order if dil == 1 else step_order

        def load_block(ref, idx, pat=pat):
            chunks = _dil_block_chunks(pat, idx, seq)
            rows = lb // len(chunks)
            parts = [ref[pl.ds(s0, rows) if st == 1 else pl.ds(s0, rows, stride=st), :] for s0, st in chunks]
            return parts[0] if len(parts) == 1 else jnp.concatenate(parts, axis=0)

        def store_block(ref, idx, val, pat=pat):
            chunks = _dil_block_chunks(pat, idx, seq)
            rows = lb // len(chunks)
            for i, (s0, st) in enumerate(chunks):
                dst = pl.ds(s0, rows) if st == 1 else pl.ds(s0, rows, stride=st)
                ref[dst, :] = val[i * rows:(i + 1) * rows, :]

        for idx in range(n_blocks):
            dst = pl.ds(lb + idx * lb, lb)
            qd[dst, :] = load_block(q_ref, idx).astype(BF16)
            kd[idx + 1] = load_block(k_ref, idx).T.astype(BF16)
            vd[dst, :] = load_block(v_ref, idx).astype(BF16)

        def scores(idx, nb=nb, bias_any=bias_any, bias_cur=bias_cur):
            q = qd[pl.ds(lb + idx * lb, lb), :]
            if idx % nb == 0:
                return _dot(q, kd[idx + 1]) + bias_cur, pl.ds(lb + idx * lb, lb)
            s = _dot(q, jnp.concatenate([kd[idx], kd[idx + 1]], axis=1)) + bias_any
            return s, pl.ds(idx * lb, 2 * lb)

        def softmax(s):
            m = jnp.max(s, axis=-1, keepdims=True)
            return jnp.exp2((s - m).astype(BF16)), m

        def values(idx, p, m, keys, store_block=store_block, o_scr=o_scr, m_scr=m_scr, d_scr=d_scr):
            v_ones = jnp.concatenate([vd[keys, :], jnp.ones((p.shape[1], DIL_HD), BF16)], axis=1)
            acc = _dot(p, v_ones)
            store_block(o_scr, idx, acc[:, :DIL_HD])
            store_block(d_scr, idx, acc[:, DIL_HD:])
            store_block(m_scr, idx, jnp.broadcast_to(m, (lb, DIL_HD)))

        groups = [range(g * DIL_UNROLL, (g + 1) * DIL_UNROLL) for g in range(n_blocks // DIL_UNROLL)]
        pending = [scores(idx) for idx in groups[0]]
        for g, group in enumerate(groups):
            upcoming = [scores(idx) for idx in groups[g + 1]] if g + 1 < len(groups) else []
            probs = [softmax(s) for s, _ in pending]
            for idx, (p, m), (_, keys) in zip(group, probs, pending):
                values(idx, p, m, keys)
            pending = upcoming
            if dil == 1:
                for t0 in range(group[0] * lb, (group[-1] + 1) * lb, PERM_TILE):
                    combine(t0)


def _dilated(qb, kb, vb, zb, g_dil):
    bsz, _, seq, _ = qb.shape
    blk = pl.BlockSpec((None, None, seq, DIL_HD), lambda b, h: (b, h, 0, 0))
    return pl.pallas_call(
        functools.partial(_dil_kernel, seq=seq),
        grid=(bsz, DIL_HEADS),
        in_specs=[blk, blk, blk, blk, pl.BlockSpec((1, DIL_HD), lambda b, h: (0, h))],
        out_specs=pl.BlockSpec((None, seq, DIL_HD), lambda b, h: (b, 0, h)),
        out_shape=jax.ShapeDtypeStruct((bsz, seq, DIL_WIDTH), BF16),
        scratch_shapes=[pltpu.VMEM((seq + DIL_LB, DIL_HD), BF16),
                        pltpu.VMEM((seq // DIL_LB + 1, DIL_HD, DIL_LB), BF16),
                        pltpu.VMEM((seq + DIL_LB, DIL_HD), BF16)]
        + [pltpu.VMEM((seq, DIL_HD), F32) for _ in range(10)],
        compiler_params=pltpu.CompilerParams(
            dimension_semantics=("arbitrary", "arbitrary"), vmem_limit_bytes=VMEM_LIMIT),
        name="dilated",
    )(qb, kb, vb, zb, g_dil)


def _rope_tables(seq):
    inv_freq = ROPE_THETA ** (-np.arange(0, DIL_HD, 2, dtype=np.float64) / DIL_HD)
    ang = np.arange(seq, dtype=np.float64)[:, None] * inv_freq[None, :]
    cos, sin = np.cos(ang), np.sin(ang)
    return (jnp.asarray(np.concatenate([cos, cos], axis=-1), F32),
            jnp.asarray(np.concatenate([-sin, sin], axis=-1), F32))


def kernel(x, c, w_ada, b_ada, g_pre, w_in, w_gate_up, b_gate_up, g_gla, g_dil, w_out, g_post):
    bsz, seq, d = x.shape
    depth = w_ada.shape[0]
    mod = _modulation(c, w_ada, b_ada).reshape(depth, bsz, 1, 3 * d)
    cos, sin = _rope_tables(seq)
    w_in_bf = w_in.astype(BF16)
    w_gate_pad = jnp.pad(w_gate_up.astype(BF16), ((0, 0), (0, LR_PAD - GLA_LOWRANK), (0, 0)))
    for l in range(depth):
        qa, ka, va, za, la, qb, kb, vb, zb = _in_proj(
            x, mod[l], g_pre.reshape(depth, 1, d), w_in_bf, w_gate_pad, b_gate_up.reshape(depth, 1, GLA_QK),
            l, cos, sin)
        yb = _dilated(qb, kb, vb, zb, g_dil[l].reshape(1, DIL_WIDTH))
        x = _gla_out_proj(qa, ka, va, za, la, g_gla[l].reshape(1, GLA_WIDTH), yb, x, mod[l],
                          w_out, g_post[l].reshape(1, d), l)
    return x
```

```python
import functools
import math

import jax
import jax.numpy as jnp
import numpy as np
from jax import lax
from jax.experimental import pallas as pl
from jax.experimental.pallas import tpu as pltpu

F32 = jnp.float32
BF16 = jnp.bfloat16

D_MODEL = 1024
GLA_HEADS = 4
GLA_DK = 64
GLA_DV = 128
GLA_QK = GLA_HEADS * GLA_DK
GLA_WIDTH = GLA_HEADS * GLA_DV
GLA_LOWRANK = 16
GLA_TAU = 16.0
GLA_CHUNK = 64
GLA_GROUP = 4
DIL_HEADS = 4
DIL_HD = 128
DIL_WIDTH = DIL_HEADS * DIL_HD
DIL_PATTERNS = ((128, 1), (512, 4), (2048, 16))
DIL_LB = 128
PERM_TILE, PERM_D = DIL_PATTERNS[1]
ROPE_THETA = 10000.0
EPS = 1e-6
LANES = 128
LR_PAD = LANES
_GLA_SIZES = (("qa", GLA_QK), ("ka", GLA_QK), ("va", GLA_WIDTH), ("za", GLA_WIDTH), ("lr", LR_PAD))
W_GLA_COLS = {}
for _name, _size in _GLA_SIZES:
    _lo = sum(n for _, n in _GLA_SIZES[:len(W_GLA_COLS)])
    W_GLA_COLS[_name] = (_lo, _lo + _size)
W_DIL_START = W_GLA_COLS["lr"][0] + GLA_LOWRANK
W_DIL_NAMES = ("qb", "kb", "vb", "zb")

VMEM_LIMIT = 56 * 1024 * 1024
DIL_UNROLL = 16
ROW_TILE = 1024
NEG_INF = float("-inf")
LOG2E = math.log2(math.e)


def _silu(v):
    return v * jax.nn.sigmoid(v)


def _dot(a, b):
    return jnp.dot(a, b, preferred_element_type=F32)


def _dot_nt(a, b):
    return lax.dot_general(a, b, (((1,), (1,)), ((), ())), preferred_element_type=F32)


def _dot_tn(a, b):
    return lax.dot_general(a, b, (((0,), (0,)), ((), ())), preferred_element_type=F32)


def _mod_kernel(c_ref, w_ref, b_ref, o_ref):
    sc = _silu(c_ref[...]).astype(BF16)
    o_ref[...] = _dot(sc, w_ref[...].astype(BF16)) + b_ref[...]


def _modulation(c, w_ada, b_ada):
    depth, d, e = w_ada.shape
    bsz = c.shape[0]
    return pl.pallas_call(
        _mod_kernel,
        grid=(depth,),
        in_specs=[
            pl.BlockSpec((bsz, d), lambda l: (0, 0)),
            pl.BlockSpec((None, d, e), lambda l: (l, 0, 0)),
            pl.BlockSpec((None, 1, e), lambda l: (l, 0, 0)),
        ],
        out_specs=pl.BlockSpec((None, bsz, e), lambda l: (l, 0, 0)),
        out_shape=jax.ShapeDtypeStruct((depth, bsz, e), F32),
        compiler_params=pltpu.CompilerParams(
            dimension_semantics=("arbitrary",), vmem_limit_bytes=VMEM_LIMIT),
        name="adaln_mod",
    )(c, w_ada, b_ada.reshape(depth, 1, e))


def _rope(v, cos, sin_signed):
    return v * cos + pltpu.roll(v, DIL_HD // 2, axis=1) * sin_signed


def _in_proj_kernel(x_ref, mod_ref, gpre_ref, w_ref, wg_ref, bg_ref, cos_ref, sin_ref,
                    qa_ref, ka_ref, va_ref, za_ref, la_ref, qb_ref, kb_ref, vb_ref, zb_ref,
                    perm_ref, wd_ref, wz_ref):
    @pl.when((pl.program_id(0) == 0) & (pl.program_id(1) == 0))
    def _():
        wd_ref[...] = w_ref[:, W_DIL_START:W_DIL_START + len(W_DIL_NAMES) * DIL_WIDTH]
        lo, hi = W_GLA_COLS["lr"]
        wz_ref[...] = _dot(w_ref[:, lo:hi], wg_ref[...]).astype(BF16)

    shift = mod_ref[:, 0:D_MODEL]
    gain = gpre_ref[...] * (1.0 + mod_ref[:, D_MODEL:2 * D_MODEL])
    n_res = PERM_TILE // PERM_D

    def store_residue_order(out_ref, hd, slab, t0, val):
        perm_ref[slab] = val
        for r in range(PERM_D):
            dst = slice(t0 + r * n_res, t0 + (r + 1) * n_res)
            out_ref[hd, dst, :] = perm_ref[slab, pl.ds(r, n_res, stride=PERM_D), :].astype(out_ref.dtype)

    for t0 in range(0, x_ref.shape[0], PERM_TILE):
        rows = slice(t0, t0 + PERM_TILE)
        x = x_ref[rows, :]
        h = (x * lax.rsqrt(jnp.mean(x * x, axis=-1, keepdims=True) + EPS) * gain + shift).astype(BF16)

        def proj(name):
            if name in W_DIL_NAMES:
                lo = W_DIL_NAMES.index(name) * DIL_WIDTH
                return _dot(h, wd_ref[:, lo:lo + DIL_WIDTH])
            lo, hi = W_GLA_COLS[name]
            return _dot(h, w_ref[:, lo:hi])

        qa_ref[rows, :] = (proj("qa") * (GLA_DK ** -0.5)).astype(qa_ref.dtype)
        ka_ref[rows, :] = proj("ka").astype(ka_ref.dtype)
        va_ref[rows, :] = proj("va").astype(va_ref.dtype)
        za_ref[rows, :] = proj("za").astype(za_ref.dtype)

        z = _dot(h, wz_ref[...]) + bg_ref[...]
        la_ref[rows, :] = (jnp.minimum(z, 0.0) - jnp.log1p(jnp.exp(-jnp.abs(z)))) * (LOG2E / GLA_TAU)

        cos = cos_ref[rows, :]
        sin = sin_ref[rows, :]
        q, k, v, zg = proj("qb"), proj("kb"), proj("vb"), proj("zb")
        for hd in range(DIL_HEADS):
            cols = slice(hd * DIL_HD, (hd + 1) * DIL_HD)
            store_residue_order(qb_ref, hd, 4 * hd, t0, _rope(q[:, cols], cos, sin) * (DIL_HD ** -0.5 * LOG2E))
            store_residue_order(kb_ref, hd, 4 * hd + 1, t0, _rope(k[:, cols], cos, sin))
            store_residue_order(vb_ref, hd, 4 * hd + 2, t0, v[:, cols])
            store_residue_order(zb_ref, hd, 4 * hd + 3, t0, zg[:, cols])


def _in_proj(x, mod_l, g_pre, w_in_bf, w_gate_pad, b_gate, layer, cos, sin):
    bsz, seq, d = x.shape
    tm = ROW_TILE
    row = lambda b, i: (b, i, 0)
    const = lambda b, i: (0, 0)

    def of_layer(a):
        return pl.BlockSpec((None,) + a.shape[1:], lambda b, i: (layer,) + (0,) * (a.ndim - 1),
                            pipeline_mode=pl.Buffered(1))

    gla_cols = ((GLA_QK, BF16), (GLA_QK, BF16), (GLA_WIDTH, BF16), (GLA_WIDTH, BF16), (GLA_QK, F32))
    dil_types = (F32, F32, F32, BF16)
    return pl.pallas_call(
        _in_proj_kernel,
        grid=(bsz, seq // tm),
        in_specs=[
            pl.BlockSpec((None, tm, d), row),
            pl.BlockSpec((None, 1, 3 * d), lambda b, i: (b, 0, 0)),
            of_layer(g_pre), of_layer(w_in_bf), of_layer(w_gate_pad), of_layer(b_gate),
            pl.BlockSpec((tm, DIL_HD), lambda b, i: (i, 0)),
            pl.BlockSpec((tm, DIL_HD), lambda b, i: (i, 0)),
        ],
        out_specs=[pl.BlockSpec((None, tm, n), row) for n, _ in gla_cols]
        + [pl.BlockSpec((None, DIL_HEADS, tm, DIL_HD), lambda b, i: (b, 0, i, 0)) for _ in dil_types],
        out_shape=[jax.ShapeDtypeStruct((bsz, seq, n), dt) for n, dt in gla_cols]
        + [jax.ShapeDtypeStruct((bsz, DIL_HEADS, seq, DIL_HD), dt) for dt in dil_types],
        scratch_shapes=[pltpu.VMEM((4 * DIL_HEADS, PERM_TILE, DIL_HD), F32),
                        pltpu.VMEM((d, len(W_DIL_NAMES) * DIL_WIDTH), BF16),
                        pltpu.VMEM((d, GLA_QK), BF16)],
        compiler_params=pltpu.CompilerParams(
            dimension_semantics=("arbitrary", "arbitrary"), vmem_limit_bytes=VMEM_LIMIT),
        name="in_proj",
    )(x, mod_l, g_pre, w_in_bf, w_gate_pad, b_gate, cos, sin)


def _head_norm_gate(o, g, z):
    r = o * lax.rsqrt(jnp.mean(o * o, axis=-1, keepdims=True) + EPS)
    return r * g * _silu(z.astype(F32))


def _split2(v):
    h1 = v.astype(BF16)
    h2 = (v - h1.astype(F32)).astype(BF16)
    return h1, h2


def _gla_out_kernel(qa_ref, ka_ref, va_ref, za_ref, la_ref, g_ref, yb_ref, x_ref, mod_ref, wo_ref,
                    gpost_ref, o_ref, st_ref, ya_ref, wo_bf, *, chunks):
    @pl.when((pl.program_id(0) == 0) & (pl.program_id(1) == 0))
    def _():
        wo_bf[...] = wo_ref[...].astype(BF16)

    @pl.when(pl.program_id(1) == 0)
    def _():
        st_ref[...] = jnp.zeros_like(st_ref)

    c_len = GLA_CHUNK
    grp = GLA_GROUP * c_len
    ri = lax.broadcasted_iota(jnp.int32, (grp, grp), 0)
    ci = lax.broadcasted_iota(jnp.int32, (grp, grp), 1)
    same_chunk = (ri & -c_len) == (ci & -c_len)
    tril_bd = ((ri >= ci) & same_chunk).astype(BF16)
    causal = (lax.broadcasted_iota(jnp.int32, (c_len, c_len), 0)
              >= lax.broadcasted_iota(jnp.int32, (c_len, c_len), 1))
    heads = range(GLA_HEADS)
    kcol = [slice(hd * GLA_DK, (hd + 1) * GLA_DK) for hd in heads]
    vcol = [slice(hd * GLA_DV, (hd + 1) * GLA_DV) for hd in heads]

    def prep(r0):
        rows = pl.ds(r0, grp)
        h1, h2 = _split2(la_ref[rows, :])
        b = _dot(tril_bd, h1) + _dot(tril_bd, h2)
        decay = [jnp.exp2(b[(c + 1) * c_len - 1:(c + 1) * c_len, :]) for c in range(GLA_GROUP)]
        decay_rows = jnp.concatenate([jnp.broadcast_to(d, (c_len, GLA_QK)) for d in decay], axis=0)
        k_e32 = ka_ref[rows, :].astype(F32) * jnp.exp2(-b)
        q_e = (qa_ref[rows, :].astype(F32) * jnp.exp2(b)).astype(BF16)
        k_end = (k_e32 * decay_rows).astype(BF16)
        v = [[va_ref[pl.ds(r0 + c * c_len, c_len), vcol[hd]] for hd in heads] for c in range(GLA_GROUP)]
        return dict(r0=r0, q_e=q_e, k_e=k_e32.astype(BF16), k_end=k_end, decay=decay, v=v)

    crow = [slice(c * c_len, (c + 1) * c_len) for c in range(GLA_GROUP)]

    def intra(p):
        q_e, k_e, k_end, v = p["q_e"], p["k_e"], p["k_end"], p["v"]
        a = [[jnp.where(causal, _dot_nt(q_e[crow[c], kcol[hd]], k_e[crow[c], kcol[hd]]), 0.0).astype(BF16)
              for hd in heads] for c in range(GLA_GROUP)]
        p["inc"] = [[_dot_tn(v[c][hd], k_end[crow[c], kcol[hd]]) for hd in heads] for c in range(GLA_GROUP)]
        p["o"] = [[_dot(a[c][hd], v[c][hd]) for hd in heads] for c in range(GLA_GROUP)]

    def inter(p, st):
        for hd in heads:
            for c in range(GLA_GROUP):
                p["o"][c][hd] = p["o"][c][hd] + _dot_nt(p["q_e"][crow[c], kcol[hd]], st[hd].astype(BF16))
                st[hd] = st[hd] * p["decay"][c][:, kcol[hd]] + p["inc"][c][hd]

    def epilogue(p):
        for c in range(GLA_GROUP):
            for hd in heads:
                out_rows = pl.ds(p["r0"] + c * c_len, c_len)
                ya_ref[out_rows, vcol[hd]] = _head_norm_gate(
                    p["o"][c][hd], g_ref[:, vcol[hd]], za_ref[out_rows, vcol[hd]]).astype(ya_ref.dtype)

    gated_gain = gpost_ref[...] * mod_ref[:, 2 * D_MODEL:3 * D_MODEL]

    def project(p):
        rows = pl.ds(p["r0"], grp)
        y = _dot(ya_ref[rows, :], wo_bf[0:GLA_WIDTH, :]) + _dot(yb_ref[rows, :], wo_bf[GLA_WIDTH:, :])
        o_ref[rows, :] = x_ref[rows, :] + y * lax.rsqrt(jnp.mean(y * y, axis=-1, keepdims=True) + EPS) * gated_gain

    st = [st_ref[hd] for hd in heads]
    groups = [prep(g * grp) for g in range(chunks // GLA_GROUP)]
    intra(groups[0])
    for g, p in enumerate(groups):
        inter(p, st)
        if g + 1 < len(groups):
            intra(groups[g + 1])
        epilogue(p)
        project(p)
    for hd in heads:
        st_ref[hd] = st[hd]


def _gla_out_proj(qa, ka, va, za, la, g_gla, yb, x, mod_l, w_out, g_post, layer):
    bsz, seq, d = x.shape
    ts = ROW_TILE
    row = lambda b, i: (b, i, 0)
    const = lambda b, i: (0, 0)
    return pl.pallas_call(
        functools.partial(_gla_out_kernel, chunks=ts // GLA_CHUNK),
        grid=(bsz, seq // ts),
        in_specs=[
            pl.BlockSpec((None, ts, GLA_QK), row),
            pl.BlockSpec((None, ts, GLA_QK), row),
            pl.BlockSpec((None, ts, GLA_WIDTH), row),
            pl.BlockSpec((None, ts, GLA_WIDTH), row),
            pl.BlockSpec((None, ts, GLA_QK), row),
            pl.BlockSpec((1, GLA_WIDTH), const),
            pl.BlockSpec((None, ts, DIL_WIDTH), row),
            pl.BlockSpec((None, ts, d), row),
            pl.BlockSpec((None, 1, 3 * d), lambda b, i: (b, 0, 0)),
            pl.BlockSpec((None,) + w_out.shape[1:], lambda b, i: (layer, 0, 0), pipeline_mode=pl.Buffered(1)),
            pl.BlockSpec((1, d), const),
        ],
        out_specs=pl.BlockSpec((None, ts, d), row),
        out_shape=jax.ShapeDtypeStruct((bsz, seq, d), F32),
        scratch_shapes=[pltpu.VMEM((GLA_HEADS, GLA_DV, GLA_DK), F32), pltpu.VMEM((ts, GLA_WIDTH), BF16),
                        pltpu.VMEM(w_out.shape[1:], BF16)],
        compiler_params=pltpu.CompilerParams(
            dimension_semantics=("arbitrary", "arbitrary"), vmem_limit_bytes=VMEM_LIMIT),
        name="gla_out_proj",
    )(qa, ka, va, za, la, g_gla, yb, x, mod_l, w_out, g_post)


def _dil_block_chunks(pattern, idx, seq):
    window, dil = DIL_PATTERNS[pattern]
    nb = seq // window
    lb = DIL_LB
    n_r = PERM_TILE // PERM_D
    if dil == PERM_D:
        return [((idx % nb) * PERM_TILE + (idx // nb) * n_r, 1)]
    if dil == 1:
        tile, part = idx // (PERM_TILE // lb), idx % (PERM_TILE // lb)
        rows = lb // PERM_D
        return [(tile * PERM_TILE + r * n_r + part * rows, 1) for r in range(PERM_D)]
    sub = dil // PERM_D
    res, n = idx // nb, idx % nb
    r4, c = res % PERM_D, res // PERM_D
    tiles = window // PERM_TILE
    return [(n * window + t * PERM_TILE + r4 * n_r + c, sub) for t in range(tiles)]


def _dil_kernel(q_ref, k_ref, v_ref, z_ref, g_ref, y_ref, qd, kd, vd, ynat,
                o1, o2, o3, m1, m2, m3, d1, d2, d3, *, seq):
    lb = DIL_LB
    n_blocks = seq // lb
    qi = lax.broadcasted_iota(jnp.int32, (lb, 2 * lb), 0)
    ki = lax.broadcasted_iota(jnp.int32, (lb, 2 * lb), 1)

    def biases(pos_in_block):
        dist = pos_in_block(qi) + lb - (pos_in_block(ki & (lb - 1)) + (ki & lb))
        band = (dist >= 0) & (dist <= lb)
        b_any = jnp.where(band, 0.0, NEG_INF)
        b_first = jnp.where(band & (ki >= lb), 0.0, NEG_INF)
        return b_any, b_first, b_first[:, lb:]

    step_order = biases(lambda a: a)
    rows_p1 = lb // PERM_D
    p1_order = biases(lambda a: PERM_D * (a % rows_p1) + a // rows_p1)

    kd[0] = jnp.zeros((DIL_HD, lb), BF16)
    vd[0:lb, :] = jnp.zeros((lb, DIL_HD), BF16)

    n_r = PERM_TILE // PERM_D

    def combine(t0):
        rows = pl.ds(t0, PERM_TILE)
        a1, a2, a3 = m1[rows, :], m2[rows, :], m3[rows, :]
        m = jnp.maximum(jnp.maximum(a1, a2), a3)
        e1, e2, e3 = jnp.exp2(a1 - m), jnp.exp2(a2 - m), jnp.exp2(a3 - m)
        den = e1 * d1[rows, :] + e2 * d2[rows, :] + e3 * d3[rows, :]
        o = (e1 * o1[rows, :] + e2 * o2[rows, :] + e3 * o3[rows, :]) / den
        y = _head_norm_gate(o, g_ref[...], z_ref[rows, :])
        for r in range(PERM_D):
            ynat[pl.ds(t0 + r, n_r, stride=PERM_D), :] = y[r * n_r:(r + 1) * n_r, :]
        y_ref[rows, :] = ynat[rows, :].astype(y_ref.dtype)

    order = sorted(range(len(DIL_PATTERNS)), key=lambda i: -DIL_PATTERNS[i][1])
    assert DIL_PATTERNS[order[-1]][1] == 1
    for pat in order:
        window, dil = DIL_PATTERNS[pat]
        o_scr, m_scr, d_scr = (o1, o2, o3)[pat], (m1, m2, m3)[pat], (d1, d2, d3)[pat]
        nb = seq // window
        bias_any, bias_first, bias_cur = p1_order if dil == 1 else step_order

        def load_block(ref, idx, pat=pat):
            chunks = _dil_block_chunks(pat, idx, seq)
            rows = lb // len(chunks)
            parts = [ref[pl.ds(s0, rows) if st == 1 else pl.ds(s0, rows, stride=st), :] for s0, st in chunks]
            return parts[0] if len(parts) == 1 else jnp.concatenate(parts, axis=0)

        def store_block(ref, idx, val, pat=pat):
            chunks = _dil_block_chunks(pat, idx, seq)
            rows = lb // len(chunks)
            for i, (s0, st) in enumerate(chunks):
                dst = pl.ds(s0, rows) if st == 1 else pl.ds(s0, rows, stride=st)
                ref[dst, :] = val[i * rows:(i + 1) * rows, :]

        for idx in range(n_blocks):
            dst = pl.ds(lb + idx * lb, lb)
            qd[dst, :] = load_block(q_ref, idx).astype(BF16)
            kd[idx + 1] = load_block(k_ref, idx).T.astype(BF16)
            vd[dst, :] = load_block(v_ref, idx).astype(BF16)

        def scores(idx, nb=nb, bias_any=bias_any, bias_cur=bias_cur):
            q = qd[pl.ds(lb + idx * lb, lb), :]
            if idx % nb == 0:
                return _dot(q, kd[idx + 1]) + bias_cur, pl.ds(lb + idx * lb, lb)
            s = _dot(q, jnp.concatenate([kd[idx], kd[idx + 1]], axis=1)) + bias_any
            return s, pl.ds(idx * lb, 2 * lb)

        def softmax(s):
            m = jnp.max(s, axis=-1, keepdims=True)
            return jnp.exp2((s - m).astype(BF16)), m

        def values(idx, p, m, keys, store_block=store_block, o_scr=o_scr, m_scr=m_scr, d_scr=d_scr):
            v_ones = jnp.concatenate([vd[keys, :], jnp.ones((p.shape[1], DIL_HD), BF16)], axis=1)
            acc = _dot(p, v_ones)
            store_block(o_scr, idx, acc[:, :DIL_HD])
            store_block(d_scr, idx, acc[:, DIL_HD:])
            store_block(m_scr, idx, jnp.broadcast_to(m, (lb, DIL_HD)))

        groups = [range(g * DIL_UNROLL, (g + 1) * DIL_UNROLL) for g in range(n_blocks // DIL_UNROLL)]
        pending = [scores(idx) for idx in groups[0]]
        for g, group in enumerate(groups):
            upcoming = [scores(idx) for idx in groups[g + 1]] if g + 1 < len(groups) else []
            probs = [softmax(s) for s, _ in pending]
            for idx, (p, m), (_, keys) in zip(group, probs, pending):
                values(idx, p, m, keys)
            pending = upcoming
            if dil == 1:
                for t0 in range(group[0] * lb, (group[-1] + 1) * lb, PERM_TILE):
                    combine(t0)


def _dilated(qb, kb, vb, zb, g_dil):
    bsz, _, seq, _ = qb.shape
    blk = pl.BlockSpec((None, None, seq, DIL_HD), lambda b, h: (b, h, 0, 0))
    return pl.pallas_call(
        functools.partial(_dil_kernel, seq=seq),
        grid=(bsz, DIL_HEADS),
        in_specs=[blk, blk, blk, blk, pl.BlockSpec((1, DIL_HD), lambda b, h: (0, h))],
        out_specs=pl.BlockSpec((None, seq, DIL_HD), lambda b, h: (b, 0, h)),
        out_shape=jax.ShapeDtypeStruct((bsz, seq, DIL_WIDTH), BF16),
        scratch_shapes=[pltpu.VMEM((seq + DIL_LB, DIL_HD), BF16),
                        pltpu.VMEM((seq // DIL_LB + 1, DIL_HD, DIL_LB), BF16),
                        pltpu.VMEM((seq + DIL_LB, DIL_HD), BF16)]
        + [pltpu.VMEM((seq, DIL_HD), F32) for _ in range(10)],
        compiler_params=pltpu.CompilerParams(
            dimension_semantics=("arbitrary", "arbitrary"), vmem_limit_bytes=VMEM_LIMIT),
        name="dilated",
    )(qb, kb, vb, zb, g_dil)


def _rope_tables(seq):
    inv_freq = ROPE_THETA ** (-np.arange(0, DIL_HD, 2, dtype=np.float64) / DIL_HD)
    ang = np.arange(seq, dtype=np.float64)[:, None] * inv_freq[None, :]
    cos, sin = np.cos(ang), np.sin(ang)
    return (jnp.asarray(np.concatenate([cos, cos], axis=-1), F32),
            jnp.asarray(np.concatenate([-sin, sin], axis=-1), F32))


def kernel(x, c, w_ada, b_ada, g_pre, w_in, w_gate_up, b_gate_up, g_gla, g_dil, w_out, g_post):
    bsz, seq, d = x.shape
    depth = w_ada.shape[0]
    mod = _modulation(c, w_ada, b_ada).reshape(depth, bsz, 1, 3 * d)
    cos, sin = _rope_tables(seq)
    w_in_bf = w_in.astype(BF16)
    w_gate_pad = jnp.pad(w_gate_up.astype(BF16), ((0, 0), (0, LR_PAD - GLA_LOWRANK), (0, 0)))
    for l in range(depth):
        qa, ka, va, za, la, qb, kb, vb, zb = _in_proj(
            x, mod[l], g_pre.reshape(depth, 1, d), w_in_bf, w_gate_pad, b_gate_up.reshape(depth, 1, GLA_QK),
            l, cos, sin)
        yb = _dilated(qb, kb, vb, zb, g_dil[l].reshape(1, DIL_WIDTH))
        x = _gla_out_proj(qa, ka, va, za, la, g_gla[l].reshape(1, GLA_WIDTH), yb, x, mod[l],
                          w_out, g_post[l].reshape(1, d), l)
    return x
```

```python
import functools
import math

import jax
import jax.numpy as jnp
import numpy as np
from jax import lax
from jax.experimental import pallas as pl
from jax.experimental.pallas import tpu as pltpu

F32 = jnp.float32
BF16 = jnp.bfloat16

D_MODEL = 1024
GLA_HEADS = 4
GLA_DK = 64
GLA_DV = 128
GLA_QK = GLA_HEADS * GLA_DK
GLA_WIDTH = GLA_HEADS * GLA_DV
GLA_LOWRANK = 16
GLA_TAU = 16.0
GLA_CHUNK = 64
GLA_GROUP = 4
DIL_HEADS = 4
DIL_HD = 128
DIL_WIDTH = DIL_HEADS * DIL_HD
DIL_PATTERNS = ((128, 1), (512, 4), (2048, 16))
DIL_LB = 128
PERM_TILE, PERM_D = DIL_PATTERNS[1]
ROPE_THETA = 10000.0
EPS = 1e-6
LANES = 128
LR_PAD = LANES
_GLA_SIZES = (("qa", GLA_QK), ("ka", GLA_QK), ("va", GLA_WIDTH), ("za", GLA_WIDTH), ("lr", LR_PAD))
W_GLA_COLS = {}
for _name, _size in _GLA_SIZES:
    _lo = sum(n for _, n in _GLA_SIZES[:len(W_GLA_COLS)])
    W_GLA_COLS[_name] = (_lo, _lo + _size)
W_DIL_START = W_GLA_COLS["lr"][0] + GLA_LOWRANK
W_DIL_NAMES = ("qb", "kb", "vb", "zb")

VMEM_LIMIT = 56 * 1024 * 1024
DIL_UNROLL = 16
ROW_TILE = 1024
NEG_INF = float("-inf")
LOG2E = math.log2(math.e)


def _silu(v):
    return v * jax.nn.sigmoid(v)


def _dot(a, b):
    return jnp.dot(a, b, preferred_element_type=F32)


def _dot_nt(a, b):
    return lax.dot_general(a, b, (((1,), (1,)), ((), ())), preferred_element_type=F32)


def _dot_tn(a, b):
    return lax.dot_general(a, b, (((0,), (0,)), ((), ())), preferred_element_type=F32)


def _mod_kernel(c_ref, w_ref, b_ref, o_ref):
    sc = _silu(c_ref[...]).astype(BF16)
    o_ref[...] = _dot(sc, w_ref[...].astype(BF16)) + b_ref[...]


def _modulation(c, w_ada, b_ada):
    depth, d, e = w_ada.shape
    bsz = c.shape[0]
    return pl.pallas_call(
        _mod_kernel,
        grid=(depth,),
        in_specs=[
            pl.BlockSpec((bsz, d), lambda l: (0, 0)),
            pl.BlockSpec((None, d, e), lambda l: (l, 0, 0)),
            pl.BlockSpec((None, 1, e), lambda l: (l, 0, 0)),
        ],
        out_specs=pl.BlockSpec((None, bsz, e), lambda l: (l, 0, 0)),
        out_shape=jax.ShapeDtypeStruct((depth, bsz, e), F32),
        compiler_params=pltpu.CompilerParams(
            dimension_semantics=("arbitrary",), vmem_limit_bytes=VMEM_LIMIT),
        name="adaln_mod",
    )(c, w_ada, b_ada.reshape(depth, 1, e))


def _rope(v, cos, sin_signed):
    return v * cos + pltpu.roll(v, DIL_HD // 2, axis=1) * sin_signed


def _gla_decayed(q, k, la, tril_bd):
    c_len = GLA_CHUNK
    h1, h2 = _split2(la)
    b = _dot(tril_bd, h1) + _dot(tril_bd, h2)
    decay = [jnp.exp2(b[(c + 1) * c_len - 1:(c + 1) * c_len, :]) for c in range(GLA_GROUP)]
    decay_rows = jnp.concatenate([jnp.broadcast_to(d, (c_len, GLA_QK)) for d in decay], axis=0)
    k_e32 = k * jnp.exp2(-b)
    q_e = (q * jnp.exp2(b)).astype(BF16)
    return q_e, k_e32.astype(BF16), (k_e32 * decay_rows).astype(BF16), jnp.concatenate(decay, axis=0)


def _chunk_tril():
    grp = GLA_GROUP * GLA_CHUNK
    ri = lax.broadcasted_iota(jnp.int32, (grp, grp), 0)
    ci = lax.broadcasted_iota(jnp.int32, (grp, grp), 1)
    return ((ri >= ci) & ((ri & -GLA_CHUNK) == (ci & -GLA_CHUNK))).astype(BF16)


def _in_proj_kernel(x_ref, mod_ref, gpre_ref, w_ref, wg_ref, bg_ref, cos_ref, sin_ref,
                    qe_ref, ke_ref, kend_ref, dec_ref, va_ref, za_ref, qb_ref, kb_ref, vb_ref, zb_ref,
                    perm_ref, wd_ref, wz_ref):
    @pl.when((pl.program_id(0) == 0) & (pl.program_id(1) == 0))
    def _():
        wd_ref[...] = w_ref[:, W_DIL_START:W_DIL_START + len(W_DIL_NAMES) * DIL_WIDTH]
        lo, hi = W_GLA_COLS["lr"]
        wz_ref[...] = _dot(w_ref[:, lo:hi], wg_ref[...]).astype(BF16)

    shift = mod_ref[:, 0:D_MODEL]
    gain = gpre_ref[...] * (1.0 + mod_ref[:, D_MODEL:2 * D_MODEL])
    tril_bd = _chunk_tril()
    n_res = PERM_TILE // PERM_D

    def store_residue_order(out_ref, hd, slab, t0, val):
        perm_ref[slab] = val
        for r in range(PERM_D):
            dst = slice(t0 + r * n_res, t0 + (r + 1) * n_res)
            out_ref[hd, dst, :] = perm_ref[slab, pl.ds(r, n_res, stride=PERM_D), :].astype(out_ref.dtype)

    for t0 in range(0, x_ref.shape[0], PERM_TILE):
        rows = slice(t0, t0 + PERM_TILE)
        x = x_ref[rows, :]
        h = (x * lax.rsqrt(jnp.mean(x * x, axis=-1, keepdims=True) + EPS) * gain + shift).astype(BF16)

        def proj(name):
            if name in W_DIL_NAMES:
                lo = W_DIL_NAMES.index(name) * DIL_WIDTH
                return _dot(h, wd_ref[:, lo:lo + DIL_WIDTH])
            lo, hi = W_GLA_COLS[name]
            return _dot(h, w_ref[:, lo:hi])

        z = _dot(h, wz_ref[...]) + bg_ref[...]
        la = (jnp.minimum(z, 0.0) - jnp.log1p(jnp.exp(-jnp.abs(z)))) * (LOG2E / GLA_TAU)
        q_a = proj("qa") * (GLA_DK ** -0.5)
        k_a = proj("ka")
        grp = GLA_GROUP * GLA_CHUNK
        for g0 in range(0, PERM_TILE, grp):
            gr = slice(g0, g0 + grp)
            q_e, k_e, k_end, decay = _gla_decayed(q_a[gr, :], k_a[gr, :], la[gr, :], tril_bd)
            out = slice(t0 + g0, t0 + g0 + grp)
            qe_ref[out, :] = q_e
            ke_ref[out, :] = k_e
            kend_ref[out, :] = k_end
            c0 = (t0 + g0) // GLA_CHUNK
            dec_ref[c0:c0 + GLA_GROUP, :] = decay
        va_ref[rows, :] = proj("va").astype(va_ref.dtype)
        za_ref[rows, :] = proj("za").astype(za_ref.dtype)

        cos = cos_ref[rows, :]
        sin = sin_ref[rows, :]
        q, k, v, zg = proj("qb"), proj("kb"), proj("vb"), proj("zb")
        for hd in range(DIL_HEADS):
            cols = slice(hd * DIL_HD, (hd + 1) * DIL_HD)
            store_residue_order(qb_ref, hd, 4 * hd, t0, _rope(q[:, cols], cos, sin) * (DIL_HD ** -0.5 * LOG2E))
            store_residue_order(kb_ref, hd, 4 * hd + 1, t0, _rope(k[:, cols], cos, sin))
            store_residue_order(vb_ref, hd, 4 * hd + 2, t0, v[:, cols])
            store_residue_order(zb_ref, hd, 4 * hd + 3, t0, zg[:, cols])


def _in_proj(x, mod_l, g_pre, w_in_bf, w_gate_pad, b_gate, layer, cos, sin):
    bsz, seq, d = x.shape
    tm = ROW_TILE
    row = lambda b, i: (b, i, 0)
    const = lambda b, i: (0, 0)

    def of_layer(a):
        return pl.BlockSpec((None,) + a.shape[1:], lambda b, i: (layer,) + (0,) * (a.ndim - 1),
                            pipeline_mode=pl.Buffered(1))

    gla_cols = ((GLA_QK, BF16), (GLA_QK, BF16), (GLA_QK, BF16))
    gate_cols = ((GLA_WIDTH, BF16), (GLA_WIDTH, BF16))
    dil_types = (F32, F32, F32, BF16)
    return pl.pallas_call(
        _in_proj_kernel,
        grid=(bsz, seq // tm),
        in_specs=[
            pl.BlockSpec((None, tm, d), row),
            pl.BlockSpec((None, 1, 3 * d), lambda b, i: (b, 0, 0)),
            of_layer(g_pre), of_layer(w_in_bf), of_layer(w_gate_pad), of_layer(b_gate),
            pl.BlockSpec((tm, DIL_HD), lambda b, i: (i, 0)),
            pl.BlockSpec((tm, DIL_HD), lambda b, i: (i, 0)),
        ],
        out_specs=[pl.BlockSpec((None, tm, n), row) for n, _ in gla_cols]
        + [pl.BlockSpec((None, tm // GLA_CHUNK, GLA_QK), row)]
        + [pl.BlockSpec((None, tm, n), row) for n, _ in gate_cols]
        + [pl.BlockSpec((None, DIL_HEADS, tm, DIL_HD), lambda b, i: (b, 0, i, 0)) for _ in dil_types],
        out_shape=[jax.ShapeDtypeStruct((bsz, seq, n), dt) for n, dt in gla_cols]
        + [jax.ShapeDtypeStruct((bsz, seq // GLA_CHUNK, GLA_QK), F32)]
        + [jax.ShapeDtypeStruct((bsz, seq, n), dt) for n, dt in gate_cols]
        + [jax.ShapeDtypeStruct((bsz, DIL_HEADS, seq, DIL_HD), dt) for dt in dil_types],
        scratch_shapes=[pltpu.VMEM((4 * DIL_HEADS, PERM_TILE, DIL_HD), F32),
                        pltpu.VMEM((d, len(W_DIL_NAMES) * DIL_WIDTH), BF16),
                        pltpu.VMEM((d, GLA_QK), BF16)],
        compiler_params=pltpu.CompilerParams(
            dimension_semantics=("arbitrary", "arbitrary"), vmem_limit_bytes=VMEM_LIMIT),
        name="in_proj",
    )(x, mod_l, g_pre, w_in_bf, w_gate_pad, b_gate, cos, sin)


def _head_norm_gate(o, g, z):
    r = o * lax.rsqrt(jnp.mean(o * o, axis=-1, keepdims=True) + EPS)
    return r * g * _silu(z.astype(F32))


def _split2(v):
    h1 = v.astype(BF16)
    h2 = (v - h1.astype(F32)).astype(BF16)
    return h1, h2


def _gla_out_kernel(qe_ref, ke_ref, kend_ref, dec_ref, va_ref, za_ref, g_ref, yb_ref, x_ref, mod_ref, wo_ref,
                    gpost_ref, o_ref, st_ref, ya_ref, wo_bf, *, chunks):
    @pl.when((pl.program_id(0) == 0) & (pl.program_id(1) == 0))
    def _():
        wo_bf[...] = wo_ref[...].astype(BF16)

    @pl.when(pl.program_id(1) == 0)
    def _():
        st_ref[...] = jnp.zeros_like(st_ref)

    c_len = GLA_CHUNK
    grp = GLA_GROUP * c_len
    causal = (lax.broadcasted_iota(jnp.int32, (c_len, c_len), 0)
              >= lax.broadcasted_iota(jnp.int32, (c_len, c_len), 1))
    heads = range(GLA_HEADS)
    kcol = [slice(hd * GLA_DK, (hd + 1) * GLA_DK) for hd in heads]
    vcol = [slice(hd * GLA_DV, (hd + 1) * GLA_DV) for hd in heads]

    def prep(r0):
        rows = pl.ds(r0, grp)
        c0 = r0 // c_len
        decay = [dec_ref[c0 + c:c0 + c + 1, :] for c in range(GLA_GROUP)]
        v = [[va_ref[pl.ds(r0 + c * c_len, c_len), vcol[hd]] for hd in heads] for c in range(GLA_GROUP)]
        return dict(r0=r0, q_e=qe_ref[rows, :], k_e=ke_ref[rows, :], k_end=kend_ref[rows, :], decay=decay, v=v)

    crow = [slice(c * c_len, (c + 1) * c_len) for c in range(GLA_GROUP)]

    def intra(p):
        q_e, k_e, k_end, v = p["q_e"], p["k_e"], p["k_end"], p["v"]
        a = [[jnp.where(causal, _dot_nt(q_e[crow[c], kcol[hd]], k_e[crow[c], kcol[hd]]), 0.0).astype(BF16)
              for hd in heads] for c in range(GLA_GROUP)]
        p["inc"] = [[_dot_tn(v[c][hd], k_end[crow[c], kcol[hd]]) for hd in heads] for c in range(GLA_GROUP)]
        p["o"] = [[_dot(a[c][hd], v[c][hd]) for hd in heads] for c in range(GLA_GROUP)]

    def inter(p, st):
        for hd in heads:
            for c in range(GLA_GROUP):
                p["o"][c][hd] = p["o"][c][hd] + _dot_nt(p["q_e"][crow[c], kcol[hd]], st[hd].astype(BF16))
                st[hd] = st[hd] * p["decay"][c][:, kcol[hd]] + p["inc"][c][hd]

    def epilogue(p):
        for c in range(GLA_GROUP):
            for hd in heads:
                out_rows = pl.ds(p["r0"] + c * c_len, c_len)
                ya_ref[out_rows, vcol[hd]] = _head_norm_gate(
                    p["o"][c][hd], g_ref[:, vcol[hd]], za_ref[out_rows, vcol[hd]]).astype(ya_ref.dtype)

    gated_gain = gpost_ref[...] * mod_ref[:, 2 * D_MODEL:3 * D_MODEL]

    def project(p):
        rows = pl.ds(p["r0"], grp)
        y = _dot(ya_ref[rows, :], wo_bf[0:GLA_WIDTH, :]) + _dot(yb_ref[rows, :], wo_bf[GLA_WIDTH:, :])
        o_ref[rows, :] = x_ref[rows, :] + y * lax.rsqrt(jnp.mean(y * y, axis=-1, keepdims=True) + EPS) * gated_gain

    st = [st_ref[hd] for hd in heads]
    groups = [prep(g * grp) for g in range(chunks // GLA_GROUP)]
    intra(groups[0])
    for g, p in enumerate(groups):
        inter(p, st)
        if g + 1 < len(groups):
            intra(groups[g + 1])
        epilogue(p)
        project(p)
    for hd in heads:
        st_ref[hd] = st[hd]


def _gla_out_proj(qe, ke, kend, dec, va, za, g_gla, yb, x, mod_l, w_out, g_post, layer):
    bsz, seq, d = x.shape
    ts = ROW_TILE
    row = lambda b, i: (b, i, 0)
    const = lambda b, i: (0, 0)
    return pl.pallas_call(
        functools.partial(_gla_out_kernel, chunks=ts // GLA_CHUNK),
        grid=(bsz, seq // ts),
        in_specs=[
            pl.BlockSpec((None, ts, GLA_QK), row),
            pl.BlockSpec((None, ts, GLA_QK), row),
            pl.BlockSpec((None, ts, GLA_QK), row),
            pl.BlockSpec((None, ts // GLA_CHUNK, GLA_QK), row),
            pl.BlockSpec((None, ts, GLA_WIDTH), row),
            pl.BlockSpec((None, ts, GLA_WIDTH), row),
            pl.BlockSpec((1, GLA_WIDTH), const),
            pl.BlockSpec((None, ts, DIL_WIDTH), row),
            pl.BlockSpec((None, ts, d), row),
            pl.BlockSpec((None, 1, 3 * d), lambda b, i: (b, 0, 0)),
            pl.BlockSpec((None,) + w_out.shape[1:], lambda b, i: (layer, 0, 0), pipeline_mode=pl.Buffered(1)),
            pl.BlockSpec((1, d), const),
        ],
        out_specs=pl.BlockSpec((None, ts, d), row),
        out_shape=jax.ShapeDtypeStruct((bsz, seq, d), F32),
        scratch_shapes=[pltpu.VMEM((GLA_HEADS, GLA_DV, GLA_DK), F32), pltpu.VMEM((ts, GLA_WIDTH), BF16),
                        pltpu.VMEM(w_out.shape[1:], BF16)],
        compiler_params=pltpu.CompilerParams(
            dimension_semantics=("arbitrary", "arbitrary"), vmem_limit_bytes=VMEM_LIMIT),
        name="gla_out_proj",
    )(qe, ke, kend, dec, va, za, g_gla, yb, x, mod_l, w_out, g_post)


def _dil_block_chunks(pattern, idx, seq):
    window, dil = DIL_PATTERNS[pattern]
    nb = seq // window
    lb = DIL_LB
    n_r = PERM_TILE // PERM_D
    if dil == PERM_D:
        return [((idx % nb) * PERM_TILE + (idx // nb) * n_r, 1)]
    if dil == 1:
        tile, part = idx // (PERM_TILE // lb), idx % (PERM_TILE // lb)
        rows = lb // PERM_D
        return [(tile * PERM_TILE + r * n_r + part * rows, 1) for r in range(PERM_D)]
    sub = dil // PERM_D
    res, n = idx // nb, idx % nb
    r4, c = res % PERM_D, res // PERM_D
    tiles = window // PERM_TILE
    return [(n * window + t * PERM_TILE + r4 * n_r + c, sub) for t in range(tiles)]


def _dil_kernel(q_ref, k_ref, v_ref, z_ref, g_ref, y_ref, qd, kd, vd, ynat,
                o1, o2, o3, m1, m2, m3, d1, d2, d3, *, seq):
    lb = DIL_LB
    n_blocks = seq // lb
    qi = lax.broadcasted_iota(jnp.int32, (lb, 2 * lb), 0)
    ki = lax.broadcasted_iota(jnp.int32, (lb, 2 * lb), 1)

    def biases(pos_in_block):
        dist = pos_in_block(qi) + lb - (pos_in_block(ki & (lb - 1)) + (ki & lb))
        band = (dist >= 0) & (dist <= lb)
        b_any = jnp.where(band, 0.0, NEG_INF)
        b_first = jnp.where(band & (ki >= lb), 0.0, NEG_INF)
        return b_any, b_first, b_first[:, lb:]

    step_order = biases(lambda a: a)
    rows_p1 = lb // PERM_D
    p1_order = biases(lambda a: PERM_D * (a % rows_p1) + a // rows_p1)

    kd[0] = jnp.zeros((DIL_HD, lb), BF16)
    vd[0:lb, :] = jnp.zeros((lb, DIL_HD), BF16)

    n_r = PERM_TILE // PERM_D

    def combine(t0):
        rows = pl.ds(t0, PERM_TILE)
        a1, a2, a3 = m1[rows, :], m2[rows, :], m3[rows, :]
        m = jnp.maximum(jnp.maximum(a1, a2), a3)
        e1, e2, e3 = jnp.exp2(a1 - m), jnp.exp2(a2 - m), jnp.exp2(a3 - m)
        den = e1 * d1[rows, :] + e2 * d2[rows, :] + e3 * d3[rows, :]
        o = (e1 * o1[rows, :] + e2 * o2[rows, :] + e3 * o3[rows, :]) / den
        y = _head_norm_gate(o, g_ref[...], z_ref[rows, :])
        for r in range(PERM_D):
            ynat[pl.ds(t0 + r, n_r, stride=PERM_D), :] = y[r * n_r:(r + 1) * n_r, :]
        y_ref[rows, :] = ynat[rows, :].astype(y_ref.dtype)

    order = sorted(range(len(DIL_PATTERNS)), key=lambda i: -DIL_PATTERNS[i][1])
    assert DIL_PATTERNS[order[-1]][1] == 1
    for pat in order:
        window, dil = DIL_PATTERNS[pat]
        o_scr, m_scr, d_scr = (o1, o2, o3)[pat], (m1, m2, m3)[pat], (d1, d2, d3)[pat]
        nb = seq // window
        bias_any, bias_first, bias_cur = p1_order if dil == 1 else step_order

        def load_block(ref, idx, pat=pat):
            chunks = _dil_block_chunks(pat, idx, seq)
            rows = lb // len(chunks)
            parts = [ref[pl.ds(s0, rows) if st == 1 else pl.ds(s0, rows, stride=st), :] for s0, st in chunks]
            return parts[0] if len(parts) == 1 else jnp.concatenate(parts, axis=0)

        def store_block(ref, idx, val, pat=pat):
            chunks = _dil_block_chunks(pat, idx, seq)
            rows = lb // len(chunks)
            for i, (s0, st) in enumerate(chunks):
                dst = pl.ds(s0, rows) if st == 1 else pl.ds(s0, rows, stride=st)
                ref[dst, :] = val[i * rows:(i + 1) * rows, :]

        for idx in range(n_blocks):
            dst = pl.ds(lb + idx * lb, lb)
            qd[dst, :] = load_block(q_ref, idx).astype(BF16)
            kd[idx + 1] = load_block(k_ref, idx).T.astype(BF16)
            vd[dst, :] = load_block(v_ref, idx).astype(BF16)

        def scores(idx, nb=nb, bias_any=bias_any, bias_cur=bias_cur):
            q = qd[pl.ds(lb + idx * lb, lb), :]
            if idx % nb == 0:
                return _dot(q, kd[idx + 1]) + bias_cur, pl.ds(lb + idx * lb, lb)
            s = _dot(q, jnp.concatenate([kd[idx], kd[idx + 1]], axis=1)) + bias_any
            return s, pl.ds(idx * lb, 2 * lb)

        def softmax(s):
            m = jnp.max(s, axis=-1, keepdims=True)
            return jnp.exp2((s - m).astype(BF16)), m

        def values(idx, p, m, keys, store_block=store_block, o_scr=o_scr, m_scr=m_scr, d_scr=d_scr):
            v_ones = jnp.concatenate([vd[keys, :], jnp.ones((p.shape[1], DIL_HD), BF16)], axis=1)
            acc = _dot(p, v_ones)
            store_block(o_scr, idx, acc[:, :DIL_HD])
            store_block(d_scr, idx, acc[:, DIL_HD:])
            store_block(m_scr, idx, jnp.broadcast_to(m, (lb, DIL_HD)))

        groups = [range(g * DIL_UNROLL, (g + 1) * DIL_UNROLL) for g in range(n_blocks // DIL_UNROLL)]
        pending = [scores(idx) for idx in groups[0]]
        for g, group in enumerate(groups):
            upcoming = [scores(idx) for idx in groups[g + 1]] if g + 1 < len(groups) else []
            probs = [softmax(s) for s, _ in pending]
            for idx, (p, m), (_, keys) in zip(group, probs, pending):
                values(idx, p, m, keys)
            pending = upcoming
            if dil == 1:
                for t0 in range(group[0] * lb, (group[-1] + 1) * lb, PERM_TILE):
                    combine(t0)


def _dilated(qb, kb, vb, zb, g_dil):
    bsz, _, seq, _ = qb.shape
    blk = pl.BlockSpec((None, None, seq, DIL_HD), lambda b, h: (b, h, 0, 0))
    return pl.pallas_call(
        functools.partial(_dil_kernel, seq=seq),
        grid=(bsz, DIL_HEADS),
        in_specs=[blk, blk, blk, blk, pl.BlockSpec((1, DIL_HD), lambda b, h: (0, h))],
        out_specs=pl.BlockSpec((None, seq, DIL_HD), lambda b, h: (b, 0, h)),
        out_shape=jax.ShapeDtypeStruct((bsz, seq, DIL_WIDTH), BF16),
        scratch_shapes=[pltpu.VMEM((seq + DIL_LB, DIL_HD), BF16),
                        pltpu.VMEM((seq // DIL_LB + 1, DIL_HD, DIL_LB), BF16),
                        pltpu.VMEM((seq + DIL_LB, DIL_HD), BF16)]
        + [pltpu.VMEM((seq, DIL_HD), F32) for _ in range(10)],
        compiler_params=pltpu.CompilerParams(
            dimension_semantics=("arbitrary", "arbitrary"), vmem_limit_bytes=VMEM_LIMIT),
        name="dilated",
    )(qb, kb, vb, zb, g_dil)


def _rope_tables(seq):
    inv_freq = ROPE_THETA ** (-np.arange(0, DIL_HD, 2, dtype=np.float64) / DIL_HD)
    ang = np.arange(seq, dtype=np.float64)[:, None] * inv_freq[None, :]
    cos, sin = np.cos(ang), np.sin(ang)
    return (jnp.asarray(np.concatenate([cos, cos], axis=-1), F32),
            jnp.asarray(np.concatenate([-sin, sin], axis=-1), F32))


def kernel(x, c, w_ada, b_ada, g_pre, w_in, w_gate_up, b_gate_up, g_gla, g_dil, w_out, g_post):
    bsz, seq, d = x.shape
    depth = w_ada.shape[0]
    mod = _modulation(c, w_ada, b_ada).reshape(depth, bsz, 1, 3 * d)
    cos, sin = _rope_tables(seq)
    w_in_bf = w_in.astype(BF16)
    w_gate_pad = jnp.pad(w_gate_up.astype(BF16), ((0, 0), (0, LR_PAD - GLA_LOWRANK), (0, 0)))
    for l in range(depth):
        qe, ke, kend, dec, va, za, qb, kb, vb, zb = _in_proj(
            x, mod[l], g_pre.reshape(depth, 1, d), w_in_bf, w_gate_pad, b_gate_up.reshape(depth, 1, GLA_QK),
            l, cos, sin)
        yb = _dilated(qb, kb, vb, zb, g_dil[l].reshape(1, DIL_WIDTH))
        x = _gla_out_proj(qe, ke, kend, dec, va, za, g_gla[l].reshape(1, GLA_WIDTH), yb, x, mod[l],
                          w_out, g_post[l].reshape(1, d), l)
    return x
```

```python
import functools
import math

import jax
import jax.numpy as jnp
import numpy as np
from jax import lax
from jax.experimental import pallas as pl
from jax.experimental.pallas import tpu as pltpu

F32 = jnp.float32
BF16 = jnp.bfloat16

D_MODEL = 1024
GLA_HEADS = 4
GLA_DK = 64
GLA_DV = 128
GLA_QK = GLA_HEADS * GLA_DK
GLA_WIDTH = GLA_HEADS * GLA_DV
GLA_LOWRANK = 16
GLA_TAU = 16.0
GLA_CHUNK = 64
GLA_GROUP = 4
DIL_HEADS = 4
DIL_HD = 128
DIL_WIDTH = DIL_HEADS * DIL_HD
DIL_PATTERNS = ((128, 1), (512, 4), (2048, 16))
DIL_LB = 128
PERM_TILE, PERM_D = DIL_PATTERNS[1]
ROPE_THETA = 10000.0
EPS = 1e-6
LANES = 128
LR_PAD = LANES
_GLA_SIZES = (("qa", GLA_QK), ("ka", GLA_QK), ("va", GLA_WIDTH), ("za", GLA_WIDTH), ("lr", LR_PAD))
W_GLA_COLS = {}
for _name, _size in _GLA_SIZES:
    _lo = sum(n for _, n in _GLA_SIZES[:len(W_GLA_COLS)])
    W_GLA_COLS[_name] = (_lo, _lo + _size)
W_DIL_START = W_GLA_COLS["lr"][0] + GLA_LOWRANK
W_DIL_NAMES = ("qb", "kb", "vb", "zb")

VMEM_LIMIT = 56 * 1024 * 1024
DIL_UNROLL = 16
ROW_TILE = 1024
NEG_INF = float("-inf")
LOG2E = math.log2(math.e)


def _silu(v):
    return v * jax.nn.sigmoid(v)


def _dot(a, b):
    return jnp.dot(a, b, preferred_element_type=F32)


def _dot_nt(a, b):
    return lax.dot_general(a, b, (((1,), (1,)), ((), ())), preferred_element_type=F32)


def _dot_tn(a, b):
    return lax.dot_general(a, b, (((0,), (0,)), ((), ())), preferred_element_type=F32)


def _mod_kernel(c_ref, w_ref, b_ref, o_ref):
    sc = _silu(c_ref[...]).astype(BF16)
    o_ref[...] = _dot(sc, w_ref[...].astype(BF16)) + b_ref[...]


def _modulation(c, w_ada, b_ada):
    depth, d, e = w_ada.shape
    bsz = c.shape[0]
    return pl.pallas_call(
        _mod_kernel,
        grid=(depth,),
        in_specs=[
            pl.BlockSpec((bsz, d), lambda l: (0, 0)),
            pl.BlockSpec((None, d, e), lambda l: (l, 0, 0)),
            pl.BlockSpec((None, 1, e), lambda l: (l, 0, 0)),
        ],
        out_specs=pl.BlockSpec((None, bsz, e), lambda l: (l, 0, 0)),
        out_shape=jax.ShapeDtypeStruct((depth, bsz, e), F32),
        compiler_params=pltpu.CompilerParams(
            dimension_semantics=("arbitrary",), vmem_limit_bytes=VMEM_LIMIT),
        name="adaln_mod",
    )(c, w_ada, b_ada.reshape(depth, 1, e))


def _rope(v, cos, sin_signed):
    return v * cos + pltpu.roll(v, DIL_HD // 2, axis=1) * sin_signed


def _gla_decayed(q, k, la, tril_bd):
    c_len = GLA_CHUNK
    h1, h2 = _split2(la)
    b = _dot(tril_bd, h1) + _dot(tril_bd, h2)
    decay = [jnp.exp2(b[(c + 1) * c_len - 1:(c + 1) * c_len, :]) for c in range(GLA_GROUP)]
    decay_rows = jnp.concatenate([jnp.broadcast_to(d, (c_len, GLA_QK)) for d in decay], axis=0)
    k_e32 = k * jnp.exp2(-b)
    q_e = (q * jnp.exp2(b)).astype(BF16)
    return q_e, k_e32.astype(BF16), (k_e32 * decay_rows).astype(BF16), jnp.concatenate(decay, axis=0)


def _chunk_tril():
    grp = GLA_GROUP * GLA_CHUNK
    ri = lax.broadcasted_iota(jnp.int32, (grp, grp), 0)
    ci = lax.broadcasted_iota(jnp.int32, (grp, grp), 1)
    return ((ri >= ci) & ((ri & -GLA_CHUNK) == (ci & -GLA_CHUNK))).astype(BF16)


def _in_proj_kernel(x_ref, mod_ref, gpre_ref, w_ref, wg_ref, bg_ref, cos_ref, sin_ref,
                    qe_ref, ke_ref, kend_ref, dec_ref, va_ref, za_ref, qb_ref, kb_ref, vb_ref, zb_ref,
                    perm_ref, wa_ref, wd_ref, wz_ref):
    @pl.when((pl.program_id(0) == 0) & (pl.program_id(1) == 0))
    def _():
        for c0 in range(0, wa_ref.shape[1], LANES):
            wa_ref[:, c0:c0 + LANES] = w_ref[c0:c0 + LANES, :].T.astype(BF16)
        for c0 in range(0, wd_ref.shape[1], LANES):
            wd_ref[:, c0:c0 + LANES] = w_ref[W_DIL_START + c0:W_DIL_START + c0 + LANES, :].T.astype(BF16)
        lo, hi = W_GLA_COLS["lr"]
        wz_ref[...] = _dot(wa_ref[:, lo:hi], wg_ref[...]).astype(BF16)

    shift = mod_ref[:, 0:D_MODEL]
    gain = gpre_ref[...] * (1.0 + mod_ref[:, D_MODEL:2 * D_MODEL])
    tril_bd = _chunk_tril()
    n_res = PERM_TILE // PERM_D

    def store_residue_order(out_ref, hd, slab, t0, val):
        perm_ref[slab] = val
        for r in range(PERM_D):
            dst = slice(t0 + r * n_res, t0 + (r + 1) * n_res)
            out_ref[hd, dst, :] = perm_ref[slab, pl.ds(r, n_res, stride=PERM_D), :].astype(out_ref.dtype)

    for t0 in range(0, x_ref.shape[0], PERM_TILE):
        rows = slice(t0, t0 + PERM_TILE)
        x = x_ref[rows, :]
        h = (x * lax.rsqrt(jnp.mean(x * x, axis=-1, keepdims=True) + EPS) * gain + shift).astype(BF16)

        def proj(name):
            if name in W_DIL_NAMES:
                lo = W_DIL_NAMES.index(name) * DIL_WIDTH
                return _dot(h, wd_ref[:, lo:lo + DIL_WIDTH])
            lo, hi = W_GLA_COLS[name]
            return _dot(h, wa_ref[:, lo:hi])

        z = _dot(h, wz_ref[...]) + bg_ref[...]
        la = (jnp.minimum(z, 0.0) - jnp.log1p(jnp.exp(-jnp.abs(z)))) * (LOG2E / GLA_TAU)
        q_a = proj("qa") * (GLA_DK ** -0.5)
        k_a = proj("ka")
        grp = GLA_GROUP * GLA_CHUNK
        for g0 in range(0, PERM_TILE, grp):
            gr = slice(g0, g0 + grp)
            q_e, k_e, k_end, decay = _gla_decayed(q_a[gr, :], k_a[gr, :], la[gr, :], tril_bd)
            out = slice(t0 + g0, t0 + g0 + grp)
            qe_ref[out, :] = q_e
            ke_ref[out, :] = k_e
            kend_ref[out, :] = k_end
            c0 = (t0 + g0) // GLA_CHUNK
            dec_ref[c0:c0 + GLA_GROUP, :] = decay
        va_ref[rows, :] = proj("va").astype(va_ref.dtype)
        za_ref[rows, :] = proj("za").astype(za_ref.dtype)

        cos = cos_ref[rows, :]
        sin = sin_ref[rows, :]
        q, k, v, zg = proj("qb"), proj("kb"), proj("vb"), proj("zb")
        for hd in range(DIL_HEADS):
            cols = slice(hd * DIL_HD, (hd + 1) * DIL_HD)
            store_residue_order(qb_ref, hd, 4 * hd, t0, _rope(q[:, cols], cos, sin) * (DIL_HD ** -0.5 * LOG2E))
            store_residue_order(kb_ref, hd, 4 * hd + 1, t0, _rope(k[:, cols], cos, sin))
            store_residue_order(vb_ref, hd, 4 * hd + 2, t0, v[:, cols])
            store_residue_order(zb_ref, hd, 4 * hd + 3, t0, zg[:, cols])


def _in_proj(x, mod_l, g_pre, w_in_t, w_gate_pad, b_gate, layer, cos, sin):
    bsz, seq, d = x.shape
    tm = PERM_TILE
    row = lambda b, i: (b, i, 0)
    const = lambda b, i: (0, 0)

    def of_layer(a):
        return pl.BlockSpec((None,) + a.shape[1:], lambda b, i: (layer,) + (0,) * (a.ndim - 1),
                            pipeline_mode=pl.Buffered(1))

    gla_cols = ((GLA_QK, BF16), (GLA_QK, BF16), (GLA_QK, BF16))
    gate_cols = ((GLA_WIDTH, BF16), (GLA_WIDTH, BF16))
    dil_types = (F32, F32, F32, BF16)
    return pl.pallas_call(
        _in_proj_kernel,
        grid=(bsz, seq // tm),
        in_specs=[
            pl.BlockSpec((None, tm, d), row),
            pl.BlockSpec((None, 1, 3 * d), lambda b, i: (b, 0, 0)),
            of_layer(g_pre), of_layer(w_in_t), of_layer(w_gate_pad), of_layer(b_gate),
            pl.BlockSpec((tm, DIL_HD), lambda b, i: (i, 0)),
            pl.BlockSpec((tm, DIL_HD), lambda b, i: (i, 0)),
        ],
        out_specs=[pl.BlockSpec((None, tm, n), row) for n, _ in gla_cols]
        + [pl.BlockSpec((None, tm // GLA_CHUNK, GLA_QK), row)]
        + [pl.BlockSpec((None, tm, n), row) for n, _ in gate_cols]
        + [pl.BlockSpec((None, DIL_HEADS, tm, DIL_HD), lambda b, i: (b, 0, i, 0)) for _ in dil_types],
        out_shape=[jax.ShapeDtypeStruct((bsz, seq, n), dt) for n, dt in gla_cols]
        + [jax.ShapeDtypeStruct((bsz, seq // GLA_CHUNK, GLA_QK), F32)]
        + [jax.ShapeDtypeStruct((bsz, seq, n), dt) for n, dt in gate_cols]
        + [jax.ShapeDtypeStruct((bsz, DIL_HEADS, seq, DIL_HD), dt) for dt in dil_types],
        scratch_shapes=[pltpu.VMEM((4 * DIL_HEADS, PERM_TILE, DIL_HD), F32),
                        pltpu.VMEM((d, W_GLA_COLS["lr"][1]), BF16),
                        pltpu.VMEM((d, len(W_DIL_NAMES) * DIL_WIDTH), BF16),
                        pltpu.VMEM((d, GLA_QK), BF16)],
        compiler_params=pltpu.CompilerParams(
            dimension_semantics=("arbitrary", "arbitrary"), vmem_limit_bytes=VMEM_LIMIT),
        name="in_proj",
    )(x, mod_l, g_pre, w_in_t, w_gate_pad, b_gate, cos, sin)


def _head_norm_gate(o, g, z):
    r = o * lax.rsqrt(jnp.mean(o * o, axis=-1, keepdims=True) + EPS)
    return r * g * _silu(z.astype(F32))


def _split2(v):
    h1 = v.astype(BF16)
    h2 = (v - h1.astype(F32)).astype(BF16)
    return h1, h2


def _gla_out_kernel(qe_ref, ke_ref, kend_ref, dec_ref, va_ref, za_ref, g_ref, yb_ref, x_ref, mod_ref, wo_ref,
                    gpost_ref, o_ref, st_ref, ya_ref, wo_bf, *, chunks):
    @pl.when((pl.program_id(0) == 0) & (pl.program_id(1) == 0))
    def _():
        wo_bf[...] = wo_ref[...].astype(BF16)

    @pl.when(pl.program_id(1) == 0)
    def _():
        st_ref[...] = jnp.zeros_like(st_ref)

    c_len = GLA_CHUNK
    grp = GLA_GROUP * c_len
    causal = (lax.broadcasted_iota(jnp.int32, (c_len, c_len), 0)
              >= lax.broadcasted_iota(jnp.int32, (c_len, c_len), 1))
    heads = range(GLA_HEADS)
    kcol = [slice(hd * GLA_DK, (hd + 1) * GLA_DK) for hd in heads]
    vcol = [slice(hd * GLA_DV, (hd + 1) * GLA_DV) for hd in heads]

    def prep(r0):
        rows = pl.ds(r0, grp)
        c0 = r0 // c_len
        decay = [dec_ref[c0 + c:c0 + c + 1, :] for c in range(GLA_GROUP)]
        v = [[va_ref[pl.ds(r0 + c * c_len, c_len), vcol[hd]] for hd in heads] for c in range(GLA_GROUP)]
        return dict(r0=r0, q_e=qe_ref[rows, :], k_e=ke_ref[rows, :], k_end=kend_ref[rows, :], decay=decay, v=v)

    crow = [slice(c * c_len, (c + 1) * c_len) for c in range(GLA_GROUP)]

    def intra(p):
        q_e, k_e, k_end, v = p["q_e"], p["k_e"], p["k_end"], p["v"]
        a = [[jnp.where(causal, _dot_nt(q_e[crow[c], kcol[hd]], k_e[crow[c], kcol[hd]]), 0.0).astype(BF16)
              for hd in heads] for c in range(GLA_GROUP)]
        p["inc"] = [[_dot_tn(v[c][hd], k_end[crow[c], kcol[hd]]) for hd in heads] for c in range(GLA_GROUP)]
        p["o"] = [[_dot(a[c][hd], v[c][hd]) for hd in heads] for c in range(GLA_GROUP)]

    def inter(p, st):
        for hd in heads:
            for c in range(GLA_GROUP):
                p["o"][c][hd] = p["o"][c][hd] + _dot_nt(p["q_e"][crow[c], kcol[hd]], st[hd].astype(BF16))
                st[hd] = st[hd] * p["decay"][c][:, kcol[hd]] + p["inc"][c][hd]

    def epilogue(p):
        for c in range(GLA_GROUP):
            for hd in heads:
                out_rows = pl.ds(p["r0"] + c * c_len, c_len)
                ya_ref[out_rows, vcol[hd]] = _head_norm_gate(
                    p["o"][c][hd], g_ref[:, vcol[hd]], za_ref[out_rows, vcol[hd]]).astype(ya_ref.dtype)

    gated_gain = gpost_ref[...] * mod_ref[:, 2 * D_MODEL:3 * D_MODEL]

    def project(p):
        rows = pl.ds(p["r0"], grp)
        y = _dot(ya_ref[rows, :], wo_bf[0:GLA_WIDTH, :]) + _dot(yb_ref[rows, :], wo_bf[GLA_WIDTH:, :])
        o_ref[rows, :] = x_ref[rows, :] + y * lax.rsqrt(jnp.mean(y * y, axis=-1, keepdims=True) + EPS) * gated_gain

    st = [st_ref[hd] for hd in heads]
    groups = [prep(g * grp) for g in range(chunks // GLA_GROUP)]
    intra(groups[0])
    for g, p in enumerate(groups):
        inter(p, st)
        if g + 1 < len(groups):
            intra(groups[g + 1])
        epilogue(p)
        project(p)
    for hd in heads:
        st_ref[hd] = st[hd]


def _gla_out_proj(qe, ke, kend, dec, va, za, g_gla, yb, x, mod_l, w_out, g_post, layer):
    bsz, seq, d = x.shape
    ts = ROW_TILE
    row = lambda b, i: (b, i, 0)
    const = lambda b, i: (0, 0)
    return pl.pallas_call(
        functools.partial(_gla_out_kernel, chunks=ts // GLA_CHUNK),
        grid=(bsz, seq // ts),
        in_specs=[
            pl.BlockSpec((None, ts, GLA_QK), row),
            pl.BlockSpec((None, ts, GLA_QK), row),
            pl.BlockSpec((None, ts, GLA_QK), row),
            pl.BlockSpec((None, ts // GLA_CHUNK, GLA_QK), row),
            pl.BlockSpec((None, ts, GLA_WIDTH), row),
            pl.BlockSpec((None, ts, GLA_WIDTH), row),
            pl.BlockSpec((1, GLA_WIDTH), const),
            pl.BlockSpec((None, ts, DIL_WIDTH), row),
            pl.BlockSpec((None, ts, d), row),
            pl.BlockSpec((None, 1, 3 * d), lambda b, i: (b, 0, 0)),
            pl.BlockSpec((None,) + w_out.shape[1:], lambda b, i: (layer, 0, 0), pipeline_mode=pl.Buffered(1)),
            pl.BlockSpec((1, d), const),
        ],
        out_specs=pl.BlockSpec((None, ts, d), row),
        out_shape=jax.ShapeDtypeStruct((bsz, seq, d), F32),
        scratch_shapes=[pltpu.VMEM((GLA_HEADS, GLA_DV, GLA_DK), F32), pltpu.VMEM((ts, GLA_WIDTH), BF16),
                        pltpu.VMEM(w_out.shape[1:], BF16)],
        compiler_params=pltpu.CompilerParams(
            dimension_semantics=("arbitrary", "arbitrary"), vmem_limit_bytes=VMEM_LIMIT),
        name="gla_out_proj",
    )(qe, ke, kend, dec, va, za, g_gla, yb, x, mod_l, w_out, g_post)


def _dil_block_chunks(pattern, idx, seq):
    window, dil = DIL_PATTERNS[pattern]
    nb = seq // window
    lb = DIL_LB
    n_r = PERM_TILE // PERM_D
    if dil == PERM_D:
        return [((idx % nb) * PERM_TILE + (idx // nb) * n_r, 1)]
    if dil == 1:
        tile, part = idx // (PERM_TILE // lb), idx % (PERM_TILE // lb)
        rows = lb // PERM_D
        return [(tile * PERM_TILE + r * n_r + part * rows, 1) for r in range(PERM_D)]
    sub = dil // PERM_D
    res, n = idx // nb, idx % nb
    r4, c = res % PERM_D, res // PERM_D
    tiles = window // PERM_TILE
    return [(n * window + t * PERM_TILE + r4 * n_r + c, sub) for t in range(tiles)]


def _dil_kernel(q_ref, k_ref, v_ref, z_ref, g_ref, y_ref, qd, kd, vd, ynat,
                o1, o2, o3, m1, m2, m3, d1, d2, d3, *, seq):
    lb = DIL_LB
    n_blocks = seq // lb
    qi = lax.broadcasted_iota(jnp.int32, (lb, 2 * lb), 0)
    ki = lax.broadcasted_iota(jnp.int32, (lb, 2 * lb), 1)

    def biases(pos_in_block):
        dist = pos_in_block(qi) + lb - (pos_in_block(ki & (lb - 1)) + (ki & lb))
        band = (dist >= 0) & (dist <= lb)
        b_any = jnp.where(band, 0.0, NEG_INF)
        b_first = jnp.where(band & (ki >= lb), 0.0, NEG_INF)
        return b_any, b_first, b_first[:, lb:]

    step_order = biases(lambda a: a)
    rows_p1 = lb // PERM_D
    p1_order = biases(lambda a: PERM_D * (a % rows_p1) + a // rows_p1)

    kd[0] = jnp.zeros((DIL_HD, lb), BF16)
    vd[0:lb, :] = jnp.zeros((lb, DIL_HD), BF16)

    n_r = PERM_TILE // PERM_D

    def combine(t0):
        rows = pl.ds(t0, PERM_TILE)
        a1, a2, a3 = m1[rows, :], m2[rows, :], m3[rows, :]
        m = jnp.maximum(jnp.maximum(a1, a2), a3)
        e1, e2, e3 = jnp.exp2(a1 - m), jnp.exp2(a2 - m), jnp.exp2(a3 - m)
        den = e1 * d1[rows, :] + e2 * d2[rows, :] + e3 * d3[rows, :]
        o = (e1 * o1[rows, :] + e2 * o2[rows, :] + e3 * o3[rows, :]) / den
        y = _head_norm_gate(o, g_ref[...], z_ref[rows, :])
        for r in range(PERM_D):
            ynat[pl.ds(t0 + r, n_r, stride=PERM_D), :] = y[r * n_r:(r + 1) * n_r, :]
        y_ref[rows, :] = ynat[rows, :].astype(y_ref.dtype)

    order = sorted(range(len(DIL_PATTERNS)), key=lambda i: -DIL_PATTERNS[i][1])
    assert DIL_PATTERNS[order[-1]][1] == 1
    for pat in order:
        window, dil = DIL_PATTERNS[pat]
        o_scr, m_scr, d_scr = (o1, o2, o3)[pat], (m1, m2, m3)[pat], (d1, d2, d3)[pat]
        nb = seq // window
        bias_any, bias_first, bias_cur = p1_order if dil == 1 else step_order

        def load_block(ref, idx, pat=pat):
            chunks = _dil_block_chunks(pat, idx, seq)
            rows = lb // len(chunks)
            parts = [ref[pl.ds(s0, rows) if st == 1 else pl.ds(s0, rows, stride=st), :] for s0, st in chunks]
            return parts[0] if len(parts) == 1 else jnp.concatenate(parts, axis=0)

        def store_block(ref, idx, val, pat=pat):
            chunks = _dil_block_chunks(pat, idx, seq)
            rows = lb // len(chunks)
            for i, (s0, st) in enumerate(chunks):
                dst = pl.ds(s0, rows) if st == 1 else pl.ds(s0, rows, stride=st)
                ref[dst, :] = val[i * rows:(i + 1) * rows, :]

        for idx in range(n_blocks):
            dst = pl.ds(lb + idx * lb, lb)
            qd[dst, :] = load_block(q_ref, idx).astype(BF16)
            kd[idx + 1] = load_block(k_ref, idx).T.astype(BF16)
            vd[dst, :] = load_block(v_ref, idx).astype(BF16)

        def scores(idx, nb=nb, bias_any=bias_any, bias_cur=bias_cur):
            q = qd[pl.ds(lb + idx * lb, lb), :]
            if idx % nb == 0:
                return _dot(q, kd[idx + 1]) + bias_cur, pl.ds(lb + idx * lb, lb)
            s = _dot(q, jnp.concatenate([kd[idx], kd[idx + 1]], axis=1)) + bias_any
            return s, pl.ds(idx * lb, 2 * lb)

        def softmax(s):
            m = jnp.max(s, axis=-1, keepdims=True)
            return jnp.exp2((s - m).astype(BF16)), m

        def values(idx, p, m, keys, store_block=store_block, o_scr=o_scr, m_scr=m_scr, d_scr=d_scr):
            v_ones = jnp.concatenate([vd[keys, :], jnp.ones((p.shape[1], DIL_HD), BF16)], axis=1)
            acc = _dot(p, v_ones)
            store_block(o_scr, idx, acc[:, :DIL_HD])
            store_block(d_scr, idx, acc[:, DIL_HD:])
            store_block(m_scr, idx, jnp.broadcast_to(m, (lb, DIL_HD)))

        groups = [range(g * DIL_UNROLL, (g + 1) * DIL_UNROLL) for g in range(n_blocks // DIL_UNROLL)]
        pending = [scores(idx) for idx in groups[0]]
        for g, group in enumerate(groups):
            upcoming = [scores(idx) for idx in groups[g + 1]] if g + 1 < len(groups) else []
            probs = [softmax(s) for s, _ in pending]
            for idx, (p, m), (_, keys) in zip(group, probs, pending):
                values(idx, p, m, keys)
            pending = upcoming
            if dil == 1:
                for t0 in range(group[0] * lb, (group[-1] + 1) * lb, PERM_TILE):
                    combine(t0)


def _dilated(qb, kb, vb, zb, g_dil):
    bsz, _, seq, _ = qb.shape
    blk = pl.BlockSpec((None, None, seq, DIL_HD), lambda b, h: (b, h, 0, 0))
    return pl.pallas_call(
        functools.partial(_dil_kernel, seq=seq),
        grid=(bsz, DIL_HEADS),
        in_specs=[blk, blk, blk, blk, pl.BlockSpec((1, DIL_HD), lambda b, h: (0, h))],
        out_specs=pl.BlockSpec((None, seq, DIL_HD), lambda b, h: (b, 0, h)),
        out_shape=jax.ShapeDtypeStruct((bsz, seq, DIL_WIDTH), BF16),
        scratch_shapes=[pltpu.VMEM((seq + DIL_LB, DIL_HD), BF16),
                        pltpu.VMEM((seq // DIL_LB + 1, DIL_HD, DIL_LB), BF16),
                        pltpu.VMEM((seq + DIL_LB, DIL_HD), BF16)]
        + [pltpu.VMEM((seq, DIL_HD), F32) for _ in range(10)],
        compiler_params=pltpu.CompilerParams(
            dimension_semantics=("arbitrary", "arbitrary"), vmem_limit_bytes=VMEM_LIMIT),
        name="dilated",
    )(qb, kb, vb, zb, g_dil)


def _rope_tables(seq):
    inv_freq = ROPE_THETA ** (-np.arange(0, DIL_HD, 2, dtype=np.float64) / DIL_HD)
    ang = np.arange(seq, dtype=np.float64)[:, None] * inv_freq[None, :]
    cos, sin = np.cos(ang), np.sin(ang)
    return (jnp.asarray(np.concatenate([cos, cos], axis=-1), F32),
            jnp.asarray(np.concatenate([-sin, sin], axis=-1), F32))


def kernel(x, c, w_ada, b_ada, g_pre, w_in, w_gate_up, b_gate_up, g_gla, g_dil, w_out, g_post):
    bsz, seq, d = x.shape
    depth = w_ada.shape[0]
    mod = _modulation(c, w_ada, b_ada).reshape(depth, bsz, 1, 3 * d)
    cos, sin = _rope_tables(seq)
    w_in_t = jnp.swapaxes(w_in, 1, 2)
    w_gate_pad = jnp.pad(w_gate_up.astype(BF16), ((0, 0), (0, LR_PAD - GLA_LOWRANK), (0, 0)))
    for l in range(depth):
        qe, ke, kend, dec, va, za, qb, kb, vb, zb = _in_proj(
            x, mod[l], g_pre.reshape(depth, 1, d), w_in_t, w_gate_pad, b_gate_up.reshape(depth, 1, GLA_QK),
            l, cos, sin)
        yb = _dilated(qb, kb, vb, zb, g_dil[l].reshape(1, DIL_WIDTH))
        x = _gla_out_proj(qe, ke, kend, dec, va, za, g_gla[l].reshape(1, GLA_WIDTH), yb, x, mod[l],
                          w_out, g_post[l].reshape(1, d), l)
    return x
```

```python
import functools
import math

import jax
import jax.numpy as jnp
import numpy as np
from jax import lax
from jax.experimental import pallas as pl
from jax.experimental.pallas import tpu as pltpu

F32 = jnp.float32
BF16 = jnp.bfloat16

D_MODEL = 1024
GLA_HEADS = 4
GLA_DK = 64
GLA_DV = 128
GLA_QK = GLA_HEADS * GLA_DK
GLA_WIDTH = GLA_HEADS * GLA_DV
GLA_LOWRANK = 16
GLA_TAU = 16.0
GLA_CHUNK = 64
GLA_GROUP = 4
DIL_HEADS = 4
DIL_HD = 128
DIL_WIDTH = DIL_HEADS * DIL_HD
DIL_PATTERNS = ((128, 1), (512, 4), (2048, 16))
DIL_LB = 128
PERM_TILE, PERM_D = DIL_PATTERNS[1]
ROPE_THETA = 10000.0
EPS = 1e-6
LANES = 128
LR_PAD = LANES
_GLA_SIZES = (("qa", GLA_QK), ("ka", GLA_QK), ("va", GLA_WIDTH), ("za", GLA_WIDTH), ("lr", LR_PAD))
W_GLA_COLS = {}
for _name, _size in _GLA_SIZES:
    _lo = sum(n for _, n in _GLA_SIZES[:len(W_GLA_COLS)])
    W_GLA_COLS[_name] = (_lo, _lo + _size)
W_DIL_START = W_GLA_COLS["lr"][0] + GLA_LOWRANK
W_DIL_NAMES = ("qb", "kb", "vb", "zb")

VMEM_LIMIT = 56 * 1024 * 1024
DIL_UNROLL = 16
ROW_TILE = 1024
NEG_INF = float("-inf")
LOG2E = math.log2(math.e)


def _silu(v):
    return v * jax.nn.sigmoid(v)


def _dot(a, b):
    return jnp.dot(a, b, preferred_element_type=F32)


def _dot_nt(a, b):
    return lax.dot_general(a, b, (((1,), (1,)), ((), ())), preferred_element_type=F32)


def _dot_tn(a, b):
    return lax.dot_general(a, b, (((0,), (0,)), ((), ())), preferred_element_type=F32)


def _mod_kernel(c_ref, w_ref, b_ref, o_ref):
    sc = _silu(c_ref[...]).astype(BF16)
    o_ref[...] = _dot(sc, w_ref[...].astype(BF16)) + b_ref[pl.ds(pl.program_id(0), 1), :]


def _modulation(c, w_ada, b_ada):
    depth, d, e = w_ada.shape
    bsz = c.shape[0]
    return pl.pallas_call(
        _mod_kernel,
        grid=(depth,),
        in_specs=[
            pl.BlockSpec((bsz, d), lambda l: (0, 0)),
            pl.BlockSpec((None, d, e), lambda l: (l, 0, 0)),
            pl.BlockSpec((depth, e), lambda l: (0, 0)),
        ],
        out_specs=pl.BlockSpec((None, bsz, e), lambda l: (l, 0, 0)),
        out_shape=jax.ShapeDtypeStruct((depth, bsz, e), F32),
        compiler_params=pltpu.CompilerParams(
            dimension_semantics=("arbitrary",), vmem_limit_bytes=VMEM_LIMIT),
        name="adaln_mod",
    )(c, w_ada, b_ada)


def _rope(v, cos, sin_signed):
    return v * cos + pltpu.roll(v, DIL_HD // 2, axis=1) * sin_signed


def _gla_decayed(q, k, la, tril_bd):
    c_len = GLA_CHUNK
    h1, h2 = _split2(la)
    b = _dot(tril_bd, h1) + _dot(tril_bd, h2)
    decay = [jnp.exp2(b[(c + 1) * c_len - 1:(c + 1) * c_len, :]) for c in range(GLA_GROUP)]
    decay_rows = jnp.concatenate([jnp.broadcast_to(d, (c_len, GLA_QK)) for d in decay], axis=0)
    k_e32 = k * jnp.exp2(-b)
    q_e = (q * jnp.exp2(b)).astype(BF16)
    return q_e, k_e32.astype(BF16), (k_e32 * decay_rows).astype(BF16), jnp.concatenate(decay, axis=0)


def _chunk_tril():
    grp = GLA_GROUP * GLA_CHUNK
    ri = lax.broadcasted_iota(jnp.int32, (grp, grp), 0)
    ci = lax.broadcasted_iota(jnp.int32, (grp, grp), 1)
    return ((ri >= ci) & ((ri & -GLA_CHUNK) == (ci & -GLA_CHUNK))).astype(BF16)


def _in_proj_kernel(x_ref, mod_ref, gpre_ref, w_ref, wg_ref, bg_ref, cos_ref, sin_ref,
                    qe_ref, ke_ref, kend_ref, dec_ref, va_ref, za_ref, qb_ref, kb_ref, vb_ref, zb_ref,
                    perm_ref, wd_ref, wz_ref, *, layer):
    @pl.when((pl.program_id(0) == 0) & (pl.program_id(1) == 0))
    def _():
        wd_ref[...] = w_ref[:, W_DIL_START:W_DIL_START + len(W_DIL_NAMES) * DIL_WIDTH]
        lo, hi = W_GLA_COLS["lr"]
        wz_ref[...] = _dot(w_ref[:, lo:hi], wg_ref[...]).astype(BF16)

    shift = mod_ref[:, 0:D_MODEL]
    gain = gpre_ref[layer:layer + 1, :] * (1.0 + mod_ref[:, D_MODEL:2 * D_MODEL])
    tril_bd = _chunk_tril()
    n_res = PERM_TILE // PERM_D

    def store_residue_order(out_ref, hd, slab, t0, val):
        perm_ref[slab] = val
        for r in range(PERM_D):
            dst = slice(t0 + r * n_res, t0 + (r + 1) * n_res)
            out_ref[hd, dst, :] = perm_ref[slab, pl.ds(r, n_res, stride=PERM_D), :].astype(out_ref.dtype)

    for t0 in range(0, x_ref.shape[0], PERM_TILE):
        rows = slice(t0, t0 + PERM_TILE)
        x = x_ref[rows, :]
        h = (x * lax.rsqrt(jnp.mean(x * x, axis=-1, keepdims=True) + EPS) * gain + shift).astype(BF16)

        def proj(name):
            if name in W_DIL_NAMES:
                lo = W_DIL_NAMES.index(name) * DIL_WIDTH
                return _dot(h, wd_ref[:, lo:lo + DIL_WIDTH])
            lo, hi = W_GLA_COLS[name]
            return _dot(h, w_ref[:, lo:hi])

        z = _dot(h, wz_ref[...]) + bg_ref[layer:layer + 1, :]
        la = (jnp.minimum(z, 0.0) - jnp.log1p(jnp.exp(-jnp.abs(z)))) * (LOG2E / GLA_TAU)
        q_a = proj("qa") * (GLA_DK ** -0.5)
        k_a = proj("ka")
        grp = GLA_GROUP * GLA_CHUNK
        for g0 in range(0, PERM_TILE, grp):
            gr = slice(g0, g0 + grp)
            q_e, k_e, k_end, decay = _gla_decayed(q_a[gr, :], k_a[gr, :], la[gr, :], tril_bd)
            out = slice(t0 + g0, t0 + g0 + grp)
            qe_ref[out, :] = q_e
            ke_ref[out, :] = k_e
            kend_ref[out, :] = k_end
            c0 = (t0 + g0) // GLA_CHUNK
            dec_ref[c0:c0 + GLA_GROUP, :] = decay
        va_ref[rows, :] = proj("va").astype(va_ref.dtype)
        za_ref[rows, :] = proj("za").astype(za_ref.dtype)

        cos = cos_ref[rows, :]
        sin = sin_ref[rows, :]
        q, k, v, zg = proj("qb"), proj("kb"), proj("vb"), proj("zb")
        for hd in range(DIL_HEADS):
            cols = slice(hd * DIL_HD, (hd + 1) * DIL_HD)
            store_residue_order(qb_ref, hd, 4 * hd, t0, _rope(q[:, cols], cos, sin) * (DIL_HD ** -0.5 * LOG2E))
            store_residue_order(kb_ref, hd, 4 * hd + 1, t0, _rope(k[:, cols], cos, sin))
            store_residue_order(vb_ref, hd, 4 * hd + 2, t0, v[:, cols])
            store_residue_order(zb_ref, hd, 4 * hd + 3, t0, zg[:, cols])


def _in_proj(x, mod, g_pre, w_in_bf, w_gate_pad, b_gate, layer, cos, sin):
    bsz, seq, d = x.shape
    tm = ROW_TILE
    row = lambda b, i: (b, i, 0)
    const = lambda b, i: (0, 0)

    def of_layer(a):
        return pl.BlockSpec((None,) + a.shape[1:], lambda b, i: (layer,) + (0,) * (a.ndim - 1),
                            pipeline_mode=pl.Buffered(1))

    gla_cols = ((GLA_QK, BF16), (GLA_QK, BF16), (GLA_QK, BF16))
    gate_cols = ((GLA_WIDTH, BF16), (GLA_WIDTH, BF16))
    dil_types = (F32, F32, F32, BF16)
    return pl.pallas_call(
        functools.partial(_in_proj_kernel, layer=layer),
        grid=(bsz, seq // tm),
        in_specs=[
            pl.BlockSpec((None, tm, d), row),
            pl.BlockSpec((None, None, 1, 3 * d), lambda b, i: (layer, b, 0, 0)),
            pl.BlockSpec(g_pre.shape, const), of_layer(w_in_bf), of_layer(w_gate_pad),
            pl.BlockSpec(b_gate.shape, const),
            pl.BlockSpec((tm, DIL_HD), lambda b, i: (i, 0)),
            pl.BlockSpec((tm, DIL_HD), lambda b, i: (i, 0)),
        ],
        out_specs=[pl.BlockSpec((None, tm, n), row) for n, _ in gla_cols]
        + [pl.BlockSpec((None, tm // GLA_CHUNK, GLA_QK), row)]
        + [pl.BlockSpec((None, tm, n), row) for n, _ in gate_cols]
        + [pl.BlockSpec((None, DIL_HEADS, tm, DIL_HD), lambda b, i: (b, 0, i, 0)) for _ in dil_types],
        out_shape=[jax.ShapeDtypeStruct((bsz, seq, n), dt) for n, dt in gla_cols]
        + [jax.ShapeDtypeStruct((bsz, seq // GLA_CHUNK, GLA_QK), F32)]
        + [jax.ShapeDtypeStruct((bsz, seq, n), dt) for n, dt in gate_cols]
        + [jax.ShapeDtypeStruct((bsz, DIL_HEADS, seq, DIL_HD), dt) for dt in dil_types],
        scratch_shapes=[pltpu.VMEM((4 * DIL_HEADS, PERM_TILE, DIL_HD), F32),
                        pltpu.VMEM((d, len(W_DIL_NAMES) * DIL_WIDTH), BF16),
                        pltpu.VMEM((d, GLA_QK), BF16)],
        compiler_params=pltpu.CompilerParams(
            dimension_semantics=("arbitrary", "arbitrary"), vmem_limit_bytes=VMEM_LIMIT),
        name="in_proj",
    )(x, mod, g_pre, w_in_bf, w_gate_pad, b_gate, cos, sin)


def _head_norm_gate(o, g, z):
    r = o * lax.rsqrt(jnp.mean(o * o, axis=-1, keepdims=True) + EPS)
    return r * g * _silu(z.astype(F32))


def _split2(v):
    h1 = v.astype(BF16)
    h2 = (v - h1.astype(F32)).astype(BF16)
    return h1, h2


def _gla_out_kernel(qe_ref, ke_ref, kend_ref, dec_ref, va_ref, za_ref, g_ref, yb_ref, x_ref, mod_ref, wo_ref,
                    gpost_ref, o_ref, st_ref, ya_ref, wo_bf, *, chunks, layer):
    @pl.when((pl.program_id(0) == 0) & (pl.program_id(1) == 0))
    def _():
        wo_bf[...] = wo_ref[...].astype(BF16)

    @pl.when(pl.program_id(1) == 0)
    def _():
        st_ref[...] = jnp.zeros_like(st_ref)

    c_len = GLA_CHUNK
    grp = GLA_GROUP * c_len
    causal = (lax.broadcasted_iota(jnp.int32, (c_len, c_len), 0)
              >= lax.broadcasted_iota(jnp.int32, (c_len, c_len), 1))
    heads = range(GLA_HEADS)
    kcol = [slice(hd * GLA_DK, (hd + 1) * GLA_DK) for hd in heads]
    vcol = [slice(hd * GLA_DV, (hd + 1) * GLA_DV) for hd in heads]

    def prep(r0):
        rows = pl.ds(r0, grp)
        c0 = r0 // c_len
        decay = [dec_ref[c0 + c:c0 + c + 1, :] for c in range(GLA_GROUP)]
        v = [[va_ref[pl.ds(r0 + c * c_len, c_len), vcol[hd]] for hd in heads] for c in range(GLA_GROUP)]
        return dict(r0=r0, q_e=qe_ref[rows, :], k_e=ke_ref[rows, :], k_end=kend_ref[rows, :], decay=decay, v=v)

    crow = [slice(c * c_len, (c + 1) * c_len) for c in range(GLA_GROUP)]

    def intra(p):
        q_e, k_e, k_end, v = p["q_e"], p["k_e"], p["k_end"], p["v"]
        a = [[jnp.where(causal, _dot_nt(q_e[crow[c], kcol[hd]], k_e[crow[c], kcol[hd]]), 0.0).astype(BF16)
              for hd in heads] for c in range(GLA_GROUP)]
        p["inc"] = [[_dot_tn(v[c][hd], k_end[crow[c], kcol[hd]]) for hd in heads] for c in range(GLA_GROUP)]
        p["o"] = [[_dot(a[c][hd], v[c][hd]) for hd in heads] for c in range(GLA_GROUP)]

    def inter(p, st):
        for hd in heads:
            for c in range(GLA_GROUP):
                p["o"][c][hd] = p["o"][c][hd] + _dot_nt(p["q_e"][crow[c], kcol[hd]], st[hd].astype(BF16))
                st[hd] = st[hd] * p["decay"][c][:, kcol[hd]] + p["inc"][c][hd]

    def epilogue(p):
        for c in range(GLA_GROUP):
            for hd in heads:
                out_rows = pl.ds(p["r0"] + c * c_len, c_len)
                ya_ref[out_rows, vcol[hd]] = _head_norm_gate(
                    p["o"][c][hd], g_ref[layer:layer + 1, vcol[hd]], za_ref[out_rows, vcol[hd]]).astype(ya_ref.dtype)

    gated_gain = gpost_ref[layer:layer + 1, :] * mod_ref[:, 2 * D_MODEL:3 * D_MODEL]

    def project(p):
        rows = pl.ds(p["r0"], grp)
        y = _dot(ya_ref[rows, :], wo_bf[0:GLA_WIDTH, :]) + _dot(yb_ref[rows, :], wo_bf[GLA_WIDTH:, :])
        o_ref[rows, :] = x_ref[rows, :] + y * lax.rsqrt(jnp.mean(y * y, axis=-1, keepdims=True) + EPS) * gated_gain

    st = [st_ref[hd] for hd in heads]
    groups = [prep(g * grp) for g in range(chunks // GLA_GROUP)]
    intra(groups[0])
    for g, p in enumerate(groups):
        inter(p, st)
        if g + 1 < len(groups):
            intra(groups[g + 1])
        epilogue(p)
        project(p)
    for hd in heads:
        st_ref[hd] = st[hd]


def _gla_out_proj(qe, ke, kend, dec, va, za, g_gla, yb, x, mod, w_out, g_post, layer):
    bsz, seq, d = x.shape
    ts = ROW_TILE
    row = lambda b, i: (b, i, 0)
    const = lambda b, i: (0, 0)
    return pl.pallas_call(
        functools.partial(_gla_out_kernel, chunks=ts // GLA_CHUNK, layer=layer),
        grid=(bsz, seq // ts),
        in_specs=[
            pl.BlockSpec((None, ts, GLA_QK), row),
            pl.BlockSpec((None, ts, GLA_QK), row),
            pl.BlockSpec((None, ts, GLA_QK), row),
            pl.BlockSpec((None, ts // GLA_CHUNK, GLA_QK), row),
            pl.BlockSpec((None, ts, GLA_WIDTH), row),
            pl.BlockSpec((None, ts, GLA_WIDTH), row),
            pl.BlockSpec(g_gla.shape, const),
            pl.BlockSpec((None, ts, DIL_WIDTH), row),
            pl.BlockSpec((None, ts, d), row),
            pl.BlockSpec((None, None, 1, 3 * d), lambda b, i: (layer, b, 0, 0)),
            pl.BlockSpec((None,) + w_out.shape[1:], lambda b, i: (layer, 0, 0), pipeline_mode=pl.Buffered(1)),
            pl.BlockSpec(g_post.shape, const),
        ],
        out_specs=pl.BlockSpec((None, ts, d), row),
        out_shape=jax.ShapeDtypeStruct((bsz, seq, d), F32),
        scratch_shapes=[pltpu.VMEM((GLA_HEADS, GLA_DV, GLA_DK), F32), pltpu.VMEM((ts, GLA_WIDTH), BF16),
                        pltpu.VMEM(w_out.shape[1:], BF16)],
        compiler_params=pltpu.CompilerParams(
            dimension_semantics=("arbitrary", "arbitrary"), vmem_limit_bytes=VMEM_LIMIT),
        name="gla_out_proj",
    )(qe, ke, kend, dec, va, za, g_gla, yb, x, mod, w_out, g_post)


def _dil_block_chunks(pattern, idx, seq):
    window, dil = DIL_PATTERNS[pattern]
    nb = seq // window
    lb = DIL_LB
    n_r = PERM_TILE // PERM_D
    if dil == PERM_D:
        return [((idx % nb) * PERM_TILE + (idx // nb) * n_r, 1)]
    if dil == 1:
        tile, part = idx // (PERM_TILE // lb), idx % (PERM_TILE // lb)
        rows = lb // PERM_D
        return [(tile * PERM_TILE + r * n_r + part * rows, 1) for r in range(PERM_D)]
    sub = dil // PERM_D
    res, n = idx // nb, idx % nb
    r4, c = res % PERM_D, res // PERM_D
    tiles = window // PERM_TILE
    return [(n * window + t * PERM_TILE + r4 * n_r + c, sub) for t in range(tiles)]


def _dil_kernel(q_ref, k_ref, v_ref, z_ref, g_ref, y_ref, qd, kd, vd, ynat,
                o1, o2, o3, m1, m2, m3, d1, d2, d3, *, seq, layer):
    lb = DIL_LB
    n_blocks = seq // lb
    qi = lax.broadcasted_iota(jnp.int32, (lb, 2 * lb), 0)
    ki = lax.broadcasted_iota(jnp.int32, (lb, 2 * lb), 1)

    def biases(pos_in_block):
        dist = pos_in_block(qi) + lb - (pos_in_block(ki & (lb - 1)) + (ki & lb))
        band = (dist >= 0) & (dist <= lb)
        b_any = jnp.where(band, 0.0, NEG_INF)
        b_first = jnp.where(band & (ki >= lb), 0.0, NEG_INF)
        return b_any, b_first, b_first[:, lb:]

    step_order = biases(lambda a: a)
    rows_p1 = lb // PERM_D
    p1_order = biases(lambda a: PERM_D * (a % rows_p1) + a // rows_p1)

    kd[0] = jnp.zeros((DIL_HD, lb), BF16)
    vd[0:lb, :] = jnp.zeros((lb, DIL_HD), BF16)

    n_r = PERM_TILE // PERM_D

    def combine(t0):
        rows = pl.ds(t0, PERM_TILE)
        a1, a2, a3 = m1[rows, :], m2[rows, :], m3[rows, :]
        m = jnp.maximum(jnp.maximum(a1, a2), a3)
        e1, e2, e3 = jnp.exp2(a1 - m), jnp.exp2(a2 - m), jnp.exp2(a3 - m)
        den = e1 * d1[rows, :] + e2 * d2[rows, :] + e3 * d3[rows, :]
        o = (e1 * o1[rows, :] + e2 * o2[rows, :] + e3 * o3[rows, :]) / den
        y = _head_norm_gate(o, g_ref[layer:layer + 1, :], z_ref[rows, :])
        for r in range(PERM_D):
            ynat[pl.ds(t0 + r, n_r, stride=PERM_D), :] = y[r * n_r:(r + 1) * n_r, :]
        y_ref[rows, :] = ynat[rows, :].astype(y_ref.dtype)

    order = sorted(range(len(DIL_PATTERNS)), key=lambda i: -DIL_PATTERNS[i][1])
    assert DIL_PATTERNS[order[-1]][1] == 1
    for pat in order:
        window, dil = DIL_PATTERNS[pat]
        o_scr, m_scr, d_scr = (o1, o2, o3)[pat], (m1, m2, m3)[pat], (d1, d2, d3)[pat]
        nb = seq // window
        bias_any, bias_first, bias_cur = p1_order if dil == 1 else step_order

        def load_block(ref, idx, pat=pat):
            chunks = _dil_block_chunks(pat, idx, seq)
            rows = lb // len(chunks)
            parts = [ref[pl.ds(s0, rows) if st == 1 else pl.ds(s0, rows, stride=st), :] for s0, st in chunks]
            return parts[0] if len(parts) == 1 else jnp.concatenate(parts, axis=0)

        def store_block(ref, idx, val, pat=pat):
            chunks = _dil_block_chunks(pat, idx, seq)
            rows = lb // len(chunks)
            for i, (s0, st) in enumerate(chunks):
                dst = pl.ds(s0, rows) if st == 1 else pl.ds(s0, rows, stride=st)
                ref[dst, :] = val[i * rows:(i + 1) * rows, :]

        for idx in range(n_blocks):
            dst = pl.ds(lb + idx * lb, lb)
            qd[dst, :] = load_block(q_ref, idx).astype(BF16)
            kd[idx + 1] = load_block(k_ref, idx).T.astype(BF16)
            vd[dst, :] = load_block(v_ref, idx).astype(BF16)

        def scores(idx, nb=nb, bias_any=bias_any, bias_cur=bias_cur):
            q = qd[pl.ds(lb + idx * lb, lb), :]
            if idx % nb == 0:
                return _dot(q, kd[idx + 1]) + bias_cur, pl.ds(lb + idx * lb, lb)
            s = _dot(q, jnp.concatenate([kd[idx], kd[idx + 1]], axis=1)) + bias_any
            return s, pl.ds(idx * lb, 2 * lb)

        def softmax(s):
            m = jnp.max(s, axis=-1, keepdims=True)
            return jnp.exp2((s - m).astype(BF16)), m

        def values(idx, p, m, keys, store_block=store_block, o_scr=o_scr, m_scr=m_scr, d_scr=d_scr):
            v_ones = jnp.concatenate([vd[keys, :], jnp.ones((p.shape[1], DIL_HD), BF16)], axis=1)
            acc = _dot(p, v_ones)
            store_block(o_scr, idx, acc[:, :DIL_HD])
            store_block(d_scr, idx, acc[:, DIL_HD:])
            store_block(m_scr, idx, jnp.broadcast_to(m, (lb, DIL_HD)))

        groups = [range(g * DIL_UNROLL, (g + 1) * DIL_UNROLL) for g in range(n_blocks // DIL_UNROLL)]
        pending = [scores(idx) for idx in groups[0]]
        for g, group in enumerate(groups):
            upcoming = [scores(idx) for idx in groups[g + 1]] if g + 1 < len(groups) else []
            probs = [softmax(s) for s, _ in pending]
            for idx, (p, m), (_, keys) in zip(group, probs, pending):
                values(idx, p, m, keys)
            pending = upcoming
            if dil == 1:
                for t0 in range(group[0] * lb, (group[-1] + 1) * lb, PERM_TILE):
                    combine(t0)


def _dilated(qb, kb, vb, zb, g_dil, layer):
    bsz, _, seq, _ = qb.shape
    blk = pl.BlockSpec((None, None, seq, DIL_HD), lambda b, h: (b, h, 0, 0))
    return pl.pallas_call(
        functools.partial(_dil_kernel, seq=seq, layer=layer),
        grid=(bsz, DIL_HEADS),
        in_specs=[blk, blk, blk, blk, pl.BlockSpec((g_dil.shape[0], DIL_HD), lambda b, h: (0, h))],
        out_specs=pl.BlockSpec((None, seq, DIL_HD), lambda b, h: (b, 0, h)),
        out_shape=jax.ShapeDtypeStruct((bsz, seq, DIL_WIDTH), BF16),
        scratch_shapes=[pltpu.VMEM((seq + DIL_LB, DIL_HD), BF16),
                        pltpu.VMEM((seq // DIL_LB + 1, DIL_HD, DIL_LB), BF16),
                        pltpu.VMEM((seq + DIL_LB, DIL_HD), BF16)]
        + [pltpu.VMEM((seq, DIL_HD), F32) for _ in range(10)],
        compiler_params=pltpu.CompilerParams(
            dimension_semantics=("arbitrary", "arbitrary"), vmem_limit_bytes=VMEM_LIMIT),
        name="dilated",
    )(qb, kb, vb, zb, g_dil)


def _rope_tables(seq):
    inv_freq = ROPE_THETA ** (-np.arange(0, DIL_HD, 2, dtype=np.float64) / DIL_HD)
    ang = np.arange(seq, dtype=np.float64)[:, None] * inv_freq[None, :]
    cos, sin = np.cos(ang), np.sin(ang)
    return (jnp.asarray(np.concatenate([cos, cos], axis=-1), F32),
            jnp.asarray(np.concatenate([-sin, sin], axis=-1), F32))


def kernel(x, c, w_ada, b_ada, g_pre, w_in, w_gate_up, b_gate_up, g_gla, g_dil, w_out, g_post):
    bsz, seq, d = x.shape
    depth = w_ada.shape[0]
    mod = _modulation(c, w_ada, b_ada).reshape(depth, bsz, 1, 3 * d)
    cos, sin = _rope_tables(seq)
    w_in_bf = w_in.astype(BF16)
    w_gate_pad = jnp.pad(w_gate_up.astype(BF16), ((0, 0), (0, LR_PAD - GLA_LOWRANK), (0, 0)))
    for l in range(depth):
        qe, ke, kend, dec, va, za, qb, kb, vb, zb = _in_proj(
            x, mod, g_pre, w_in_bf, w_gate_pad, b_gate_up, l, cos, sin)
        yb = _dilated(qb, kb, vb, zb, g_dil, l)
        x = _gla_out_proj(qe, ke, kend, dec, va, za, g_gla, yb, x, mod, w_out, g_post, l)
    return x
```

```python
import functools
import math

import jax
import jax.numpy as jnp
import numpy as np
from jax import lax
from jax.experimental import pallas as pl
from jax.experimental.pallas import tpu as pltpu

F32 = jnp.float32
BF16 = jnp.bfloat16

D_MODEL = 1024
GLA_HEADS = 4
GLA_DK = 64
GLA_DV = 128
GLA_QK = GLA_HEADS * GLA_DK
GLA_WIDTH = GLA_HEADS * GLA_DV
GLA_LOWRANK = 16
GLA_TAU = 16.0
GLA_CHUNK = 64
GLA_GROUP = 4
DIL_HEADS = 4
DIL_HD = 128
DIL_WIDTH = DIL_HEADS * DIL_HD
DIL_PATTERNS = ((128, 1), (512, 4), (2048, 16))
DIL_LB = 128
PERM_TILE, PERM_D = DIL_PATTERNS[1]
ROPE_THETA = 10000.0
EPS = 1e-6
LANES = 128
LR_PAD = LANES
_GLA_SIZES = (("qa", GLA_QK), ("ka", GLA_QK), ("va", GLA_WIDTH), ("za", GLA_WIDTH), ("lr", LR_PAD))
W_GLA_COLS = {}
for _name, _size in _GLA_SIZES:
    _lo = sum(n for _, n in _GLA_SIZES[:len(W_GLA_COLS)])
    W_GLA_COLS[_name] = (_lo, _lo + _size)
W_DIL_START = W_GLA_COLS["lr"][0] + GLA_LOWRANK
W_DIL_NAMES = ("qb", "kb", "vb", "zb")

VMEM_LIMIT = 56 * 1024 * 1024
DIL_UNROLL = 16
ROW_TILE = 1024
NEG_INF = float("-inf")
LOG2E = math.log2(math.e)


def _silu(v):
    return v * jax.nn.sigmoid(v)


def _dot(a, b):
    return jnp.dot(a, b, preferred_element_type=F32)


def _dot_nt(a, b):
    return lax.dot_general(a, b, (((1,), (1,)), ((), ())), preferred_element_type=F32)


def _dot_tn(a, b):
    return lax.dot_general(a, b, (((0,), (0,)), ((), ())), preferred_element_type=F32)


def _mod_kernel(c_ref, w_ref, b_ref, o_ref):
    sc = _silu(c_ref[...]).astype(BF16)
    o_ref[...] = _dot(sc, w_ref[...].astype(BF16)) + b_ref[pl.ds(pl.program_id(0), 1), :]


def _modulation(c, w_ada, b_ada):
    depth, d, e = w_ada.shape
    bsz = c.shape[0]
    return pl.pallas_call(
        _mod_kernel,
        grid=(depth,),
        in_specs=[
            pl.BlockSpec((bsz, d), lambda l: (0, 0)),
            pl.BlockSpec((None, d, e), lambda l: (l, 0, 0)),
            pl.BlockSpec((depth, e), lambda l: (0, 0)),
        ],
        out_specs=pl.BlockSpec((None, bsz, e), lambda l: (l, 0, 0)),
        out_shape=jax.ShapeDtypeStruct((depth, bsz, e), F32),
        compiler_params=pltpu.CompilerParams(
            dimension_semantics=("arbitrary",), vmem_limit_bytes=VMEM_LIMIT),
        name="adaln_mod",
    )(c, w_ada, b_ada)


def _rope(v, cos, sin_signed):
    return v * cos + pltpu.roll(v, DIL_HD // 2, axis=1) * sin_signed


def _gla_decayed(q, k, la, tril_bd):
    c_len = GLA_CHUNK
    h1, h2 = _split2(la)
    b = _dot(tril_bd, h1) + _dot(tril_bd, h2)
    decay = [jnp.exp2(b[(c + 1) * c_len - 1:(c + 1) * c_len, :]) for c in range(GLA_GROUP)]
    decay_rows = jnp.concatenate([jnp.broadcast_to(d, (c_len, GLA_QK)) for d in decay], axis=0)
    k_e32 = k * jnp.exp2(-b)
    q_e = (q * jnp.exp2(b)).astype(BF16)
    return q_e, k_e32.astype(BF16), (k_e32 * decay_rows).astype(BF16), jnp.concatenate(decay, axis=0)


def _chunk_tril():
    grp = GLA_GROUP * GLA_CHUNK
    ri = lax.broadcasted_iota(jnp.int32, (grp, grp), 0)
    ci = lax.broadcasted_iota(jnp.int32, (grp, grp), 1)
    return ((ri >= ci) & ((ri & -GLA_CHUNK) == (ci & -GLA_CHUNK))).astype(BF16)


def _in_proj_kernel(x_ref, mod_ref, gpre_ref, w_ref, wg_ref, bg_ref, cos_ref, sin_ref,
                    qe_ref, ke_ref, kend_ref, dec_ref, va_ref, za_ref, qb_ref, kb_ref, vb_ref, zb_ref,
                    perm_ref, wd_ref, wz_ref, *, layer):
    @pl.when((pl.program_id(0) == 0) & (pl.program_id(1) == 0))
    def _():
        wd_ref[...] = w_ref[:, W_DIL_START:W_DIL_START + len(W_DIL_NAMES) * DIL_WIDTH]
        lo, hi = W_GLA_COLS["lr"]
        wz_ref[...] = _dot(w_ref[:, lo:hi], wg_ref[...]).astype(BF16)

    shift = mod_ref[:, 0:D_MODEL]
    gain = gpre_ref[layer:layer + 1, :] * (1.0 + mod_ref[:, D_MODEL:2 * D_MODEL])
    tril_bd = _chunk_tril()
    n_res = PERM_TILE // PERM_D

    def store_residue_order(out_ref, hd, slab, t0, val):
        perm_ref[slab] = val
        for r in range(PERM_D):
            dst = slice(t0 + r * n_res, t0 + (r + 1) * n_res)
            out_ref[hd, dst, :] = perm_ref[slab, pl.ds(r, n_res, stride=PERM_D), :].astype(out_ref.dtype)

    for t0 in range(0, x_ref.shape[0], PERM_TILE):
        rows = slice(t0, t0 + PERM_TILE)
        x = x_ref[rows, :]
        h = (x * lax.rsqrt(jnp.mean(x * x, axis=-1, keepdims=True) + EPS) * gain + shift).astype(BF16)

        def proj(name):
            if name in W_DIL_NAMES:
                lo = W_DIL_NAMES.index(name) * DIL_WIDTH
                return _dot(h, wd_ref[:, lo:lo + DIL_WIDTH])
            lo, hi = W_GLA_COLS[name]
            return _dot(h, w_ref[:, lo:hi])

        z = _dot(h, wz_ref[...]) + bg_ref[layer:layer + 1, :]
        la = (jnp.minimum(z, 0.0) - jnp.log1p(jnp.exp(-jnp.abs(z)))) * (LOG2E / GLA_TAU)
        q_a = proj("qa") * (GLA_DK ** -0.5)
        k_a = proj("ka")
        va_ref[rows, :] = proj("va").astype(va_ref.dtype)
        za_ref[rows, :] = proj("za").astype(za_ref.dtype)
        q, k, v, zg = proj("qb"), proj("kb"), proj("vb"), proj("zb")

        grp = GLA_GROUP * GLA_CHUNK
        for g0 in range(0, PERM_TILE, grp):
            gr = slice(g0, g0 + grp)
            q_e, k_e, k_end, decay = _gla_decayed(q_a[gr, :], k_a[gr, :], la[gr, :], tril_bd)
            out = slice(t0 + g0, t0 + g0 + grp)
            qe_ref[out, :] = q_e
            ke_ref[out, :] = k_e
            kend_ref[out, :] = k_end
            c0 = (t0 + g0) // GLA_CHUNK
            dec_ref[c0:c0 + GLA_GROUP, :] = decay

        cos = cos_ref[rows, :]
        sin = sin_ref[rows, :]
        for hd in range(DIL_HEADS):
            cols = slice(hd * DIL_HD, (hd + 1) * DIL_HD)
            store_residue_order(qb_ref, hd, 4 * hd, t0, _rope(q[:, cols], cos, sin) * (DIL_HD ** -0.5 * LOG2E))
            store_residue_order(kb_ref, hd, 4 * hd + 1, t0, _rope(k[:, cols], cos, sin))
            store_residue_order(vb_ref, hd, 4 * hd + 2, t0, v[:, cols])
            store_residue_order(zb_ref, hd, 4 * hd + 3, t0, zg[:, cols])


def _in_proj(x, mod, g_pre, w_in_bf, w_gate_pad, b_gate, layer, cos, sin):
    bsz, seq, d = x.shape
    tm = ROW_TILE
    row = lambda b, i: (b, i, 0)
    const = lambda b, i: (0, 0)

    def of_layer(a):
        return pl.BlockSpec((None,) + a.shape[1:], lambda b, i: (layer,) + (0,) * (a.ndim - 1),
                            pipeline_mode=pl.Buffered(1))

    gla_cols = ((GLA_QK, BF16), (GLA_QK, BF16), (GLA_QK, BF16))
    gate_cols = ((GLA_WIDTH, BF16), (GLA_WIDTH, BF16))
    dil_types = (F32, F32, F32, BF16)
    return pl.pallas_call(
        functools.partial(_in_proj_kernel, layer=layer),
        grid=(bsz, seq // tm),
        in_specs=[
            pl.BlockSpec((None, tm, d), row),
            pl.BlockSpec((None, None, 1, 3 * d), lambda b, i: (layer, b, 0, 0)),
            pl.BlockSpec(g_pre.shape, const), of_layer(w_in_bf), of_layer(w_gate_pad),
            pl.BlockSpec(b_gate.shape, const),
            pl.BlockSpec((tm, DIL_HD), lambda b, i: (i, 0)),
            pl.BlockSpec((tm, DIL_HD), lambda b, i: (i, 0)),
        ],
        out_specs=[pl.BlockSpec((None, tm, n), row) for n, _ in gla_cols]
        + [pl.BlockSpec((None, tm // GLA_CHUNK, GLA_QK), row)]
        + [pl.BlockSpec((None, tm, n), row) for n, _ in gate_cols]
        + [pl.BlockSpec((None, DIL_HEADS, tm, DIL_HD), lambda b, i: (b, 0, i, 0)) for _ in dil_types],
        out_shape=[jax.ShapeDtypeStruct((bsz, seq, n), dt) for n, dt in gla_cols]
        + [jax.ShapeDtypeStruct((bsz, seq // GLA_CHUNK, GLA_QK), F32)]
        + [jax.ShapeDtypeStruct((bsz, seq, n), dt) for n, dt in gate_cols]
        + [jax.ShapeDtypeStruct((bsz, DIL_HEADS, seq, DIL_HD), dt) for dt in dil_types],
        scratch_shapes=[pltpu.VMEM((4 * DIL_HEADS, PERM_TILE, DIL_HD), F32),
                        pltpu.VMEM((d, len(W_DIL_NAMES) * DIL_WIDTH), BF16),
                        pltpu.VMEM((d, GLA_QK), BF16)],
        compiler_params=pltpu.CompilerParams(
            dimension_semantics=("arbitrary", "arbitrary"), vmem_limit_bytes=VMEM_LIMIT),
        name="in_proj",
    )(x, mod, g_pre, w_in_bf, w_gate_pad, b_gate, cos, sin)


def _head_norm_gate(o, g, z):
    r = o * lax.rsqrt(jnp.mean(o * o, axis=-1, keepdims=True) + EPS)
    return r * g * _silu(z.astype(F32))


def _split2(v):
    h1 = v.astype(BF16)
    h2 = (v - h1.astype(F32)).astype(BF16)
    return h1, h2


def _gla_out_kernel(qe_ref, ke_ref, kend_ref, dec_ref, va_ref, za_ref, g_ref, yb_ref, x_ref, mod_ref, wo_ref,
                    gpost_ref, o_ref, st_ref, ya_ref, wo_bf, *, chunks, layer):
    @pl.when((pl.program_id(0) == 0) & (pl.program_id(1) == 0))
    def _():
        wo_bf[...] = wo_ref[...].astype(BF16)

    @pl.when(pl.program_id(1) == 0)
    def _():
        st_ref[...] = jnp.zeros_like(st_ref)

    c_len = GLA_CHUNK
    grp = GLA_GROUP * c_len
    causal = (lax.broadcasted_iota(jnp.int32, (c_len, c_len), 0)
              >= lax.broadcasted_iota(jnp.int32, (c_len, c_len), 1))
    heads = range(GLA_HEADS)
    kcol = [slice(hd * GLA_DK, (hd + 1) * GLA_DK) for hd in heads]
    vcol = [slice(hd * GLA_DV, (hd + 1) * GLA_DV) for hd in heads]

    def prep(r0):
        rows = pl.ds(r0, grp)
        c0 = r0 // c_len
        decay = [dec_ref[c0 + c:c0 + c + 1, :] for c in range(GLA_GROUP)]
        v = [[va_ref[pl.ds(r0 + c * c_len, c_len), vcol[hd]] for hd in heads] for c in range(GLA_GROUP)]
        return dict(r0=r0, q_e=qe_ref[rows, :], k_e=ke_ref[rows, :], k_end=kend_ref[rows, :], decay=decay, v=v)

    crow = [slice(c * c_len, (c + 1) * c_len) for c in range(GLA_GROUP)]

    def intra(p):
        q_e, k_e, k_end, v = p["q_e"], p["k_e"], p["k_end"], p["v"]
        a = [[jnp.where(causal, _dot_nt(q_e[crow[c], kcol[hd]], k_e[crow[c], kcol[hd]]), 0.0).astype(BF16)
              for hd in heads] for c in range(GLA_GROUP)]
        p["inc"] = [[_dot_tn(v[c][hd], k_end[crow[c], kcol[hd]]) for hd in heads] for c in range(GLA_GROUP)]
        p["o"] = [[_dot(a[c][hd], v[c][hd]) for hd in heads] for c in range(GLA_GROUP)]

    def inter(p, st):
        for hd in heads:
            for c in range(GLA_GROUP):
                p["o"][c][hd] = p["o"][c][hd] + _dot_nt(p["q_e"][crow[c], kcol[hd]], st[hd].astype(BF16))
                st[hd] = st[hd] * p["decay"][c][:, kcol[hd]] + p["inc"][c][hd]

    def epilogue(p):
        for c in range(GLA_GROUP):
            for hd in heads:
                out_rows = pl.ds(p["r0"] + c * c_len, c_len)
                ya_ref[out_rows, vcol[hd]] = _head_norm_gate(
                    p["o"][c][hd], g_ref[layer:layer + 1, vcol[hd]], za_ref[out_rows, vcol[hd]]).astype(ya_ref.dtype)

    gated_gain = gpost_ref[layer:layer + 1, :] * mod_ref[:, 2 * D_MODEL:3 * D_MODEL]

    def project(p):
        rows = pl.ds(p["r0"], grp)
        y = _dot(ya_ref[rows, :], wo_bf[0:GLA_WIDTH, :]) + _dot(yb_ref[rows, :], wo_bf[GLA_WIDTH:, :])
        o_ref[rows, :] = x_ref[rows, :] + y * lax.rsqrt(jnp.mean(y * y, axis=-1, keepdims=True) + EPS) * gated_gain

    st = [st_ref[hd] for hd in heads]
    groups = [prep(g * grp) for g in range(chunks // GLA_GROUP)]
    intra(groups[0])
    for g, p in enumerate(groups):
        inter(p, st)
        if g + 1 < len(groups):
            intra(groups[g + 1])
        if g > 0:
            project(groups[g - 1])
        epilogue(p)
    project(groups[-1])
    for hd in heads:
        st_ref[hd] = st[hd]


def _gla_out_proj(qe, ke, kend, dec, va, za, g_gla, yb, x, mod, w_out, g_post, layer):
    bsz, seq, d = x.shape
    ts = ROW_TILE
    row = lambda b, i: (b, i, 0)
    const = lambda b, i: (0, 0)
    return pl.pallas_call(
        functools.partial(_gla_out_kernel, chunks=ts // GLA_CHUNK, layer=layer),
        grid=(bsz, seq // ts),
        in_specs=[
            pl.BlockSpec((None, ts, GLA_QK), row),
            pl.BlockSpec((None, ts, GLA_QK), row),
            pl.BlockSpec((None, ts, GLA_QK), row),
            pl.BlockSpec((None, ts // GLA_CHUNK, GLA_QK), row),
            pl.BlockSpec((None, ts, GLA_WIDTH), row),
            pl.BlockSpec((None, ts, GLA_WIDTH), row),
            pl.BlockSpec(g_gla.shape, const),
            pl.BlockSpec((None, ts, DIL_WIDTH), row),
            pl.BlockSpec((None, ts, d), row),
            pl.BlockSpec((None, None, 1, 3 * d), lambda b, i: (layer, b, 0, 0)),
            pl.BlockSpec((None,) + w_out.shape[1:], lambda b, i: (layer, 0, 0), pipeline_mode=pl.Buffered(1)),
            pl.BlockSpec(g_post.shape, const),
        ],
        out_specs=pl.BlockSpec((None, ts, d), row),
        out_shape=jax.ShapeDtypeStruct((bsz, seq, d), F32),
        scratch_shapes=[pltpu.VMEM((GLA_HEADS, GLA_DV, GLA_DK), F32), pltpu.VMEM((ts, GLA_WIDTH), BF16),
                        pltpu.VMEM(w_out.shape[1:], BF16)],
        compiler_params=pltpu.CompilerParams(
            dimension_semantics=("arbitrary", "arbitrary"), vmem_limit_bytes=VMEM_LIMIT),
        name="gla_out_proj",
    )(qe, ke, kend, dec, va, za, g_gla, yb, x, mod, w_out, g_post)


def _dil_block_chunks(pattern, idx, seq):
    window, dil = DIL_PATTERNS[pattern]
    nb = seq // window
    lb = DIL_LB
    n_r = PERM_TILE // PERM_D
    if dil == PERM_D:
        return [((idx % nb) * PERM_TILE + (idx // nb) * n_r, 1)]
    if dil == 1:
        tile, part = idx // (PERM_TILE // lb), idx % (PERM_TILE // lb)
        rows = lb // PERM_D
        return [(tile * PERM_TILE + r * n_r + part * rows, 1) for r in range(PERM_D)]
    sub = dil // PERM_D
    res, n = idx // nb, idx % nb
    r4, c = res % PERM_D, res // PERM_D
    tiles = window // PERM_TILE
    return [(n * window + t * PERM_TILE + r4 * n_r + c, sub) for t in range(tiles)]


def _dil_kernel(q_ref, k_ref, v_ref, z_ref, g_ref, y_ref, qd, kd, vd, ynat,
                o1, o2, o3, m1, m2, m3, d1, d2, d3, *, seq, layer):
    lb = DIL_LB
    n_blocks = seq // lb
    qi = lax.broadcasted_iota(jnp.int32, (lb, 2 * lb), 0)
    ki = lax.broadcasted_iota(jnp.int32, (lb, 2 * lb), 1)

    def biases(pos_in_block):
        dist = pos_in_block(qi) + lb - (pos_in_block(ki & (lb - 1)) + (ki & lb))
        band = (dist >= 0) & (dist <= lb)
        b_any = jnp.where(band, 0.0, NEG_INF)
        b_first = jnp.where(band & (ki >= lb), 0.0, NEG_INF)
        return b_any, b_first, b_first[:, lb:]

    step_order = biases(lambda a: a)
    rows_p1 = lb // PERM_D
    p1_order = biases(lambda a: PERM_D * (a % rows_p1) + a // rows_p1)

    kd[0] = jnp.zeros((DIL_HD, lb), BF16)
    vd[0:lb, :] = jnp.zeros((lb, DIL_HD), BF16)

    n_r = PERM_TILE // PERM_D

    def combine(t0):
        rows = pl.ds(t0, PERM_TILE)
        a1, a2, a3 = m1[rows, :], m2[rows, :], m3[rows, :]
        m = jnp.maximum(jnp.maximum(a1, a2), a3)
        e1, e2, e3 = jnp.exp2(a1 - m), jnp.exp2(a2 - m), jnp.exp2(a3 - m)
        den = e1 * d1[rows, :] + e2 * d2[rows, :] + e3 * d3[rows, :]
        o = (e1 * o1[rows, :] + e2 * o2[rows, :] + e3 * o3[rows, :]) / den
        y = _head_norm_gate(o, g_ref[layer:layer + 1, :], z_ref[rows, :])
        for r in range(PERM_D):
            ynat[pl.ds(t0 + r, n_r, stride=PERM_D), :] = y[r * n_r:(r + 1) * n_r, :]
        y_ref[rows, :] = ynat[rows, :].astype(y_ref.dtype)

    order = sorted(range(len(DIL_PATTERNS)), key=lambda i: -DIL_PATTERNS[i][1])
    assert DIL_PATTERNS[order[-1]][1] == 1
    for pat in order:
        window, dil = DIL_PATTERNS[pat]
        o_scr, m_scr, d_scr = (o1, o2, o3)[pat], (m1, m2, m3)[pat], (d1, d2, d3)[pat]
        nb = seq // window
        bias_any, bias_first, bias_cur = p1_order if dil == 1 else step_order

        def load_block(ref, idx, pat=pat):
            chunks = _dil_block_chunks(pat, idx, seq)
            rows = lb // len(chunks)
            parts = [ref[pl.ds(s0, rows) if st == 1 else pl.ds(s0, rows, stride=st), :] for s0, st in chunks]
            return parts[0] if len(parts) == 1 else jnp.concatenate(parts, axis=0)

        def store_block(ref, idx, val, pat=pat):
            chunks = _dil_block_chunks(pat, idx, seq)
            rows = lb // len(chunks)
            for i, (s0, st) in enumerate(chunks):
                dst = pl.ds(s0, rows) if st == 1 else pl.ds(s0, rows, stride=st)
                ref[dst, :] = val[i * rows:(i + 1) * rows, :]

        for idx in range(n_blocks):
            dst = pl.ds(lb + idx * lb, lb)
            qd[dst, :] = load_block(q_ref, idx).astype(BF16)
            kd[idx + 1] = load_block(k_ref, idx).T.astype(BF16)
            vd[dst, :] = load_block(v_ref, idx).astype(BF16)

        def scores(idx, nb=nb, bias_any=bias_any, bias_cur=bias_cur):
            q = qd[pl.ds(lb + idx * lb, lb), :]
            if idx % nb == 0:
                return _dot(q, kd[idx + 1]) + bias_cur, pl.ds(lb + idx * lb, lb)
            s = _dot(q, jnp.concatenate([kd[idx], kd[idx + 1]], axis=1)) + bias_any
            return s, pl.ds(idx * lb, 2 * lb)

        def softmax(s):
            m = jnp.max(s, axis=-1, keepdims=True)
            return jnp.exp2((s - m).astype(BF16)), m

        def values(idx, p, m, keys, store_block=store_block, o_scr=o_scr, m_scr=m_scr, d_scr=d_scr):
            v_ones = jnp.concatenate([vd[keys, :], jnp.ones((p.shape[1], DIL_HD), BF16)], axis=1)
            acc = _dot(p, v_ones)
            store_block(o_scr, idx, acc[:, :DIL_HD])
            store_block(d_scr, idx, acc[:, DIL_HD:])
            store_block(m_scr, idx, jnp.broadcast_to(m, (lb, DIL_HD)))

        groups = [range(g * DIL_UNROLL, (g + 1) * DIL_UNROLL) for g in range(n_blocks // DIL_UNROLL)]
        pending = [scores(idx) for idx in groups[0]]
        for g, group in enumerate(groups):
            upcoming = [scores(idx) for idx in groups[g + 1]] if g + 1 < len(groups) else []
            probs = [softmax(s) for s, _ in pending]
            for idx, (p, m), (_, keys) in zip(group, probs, pending):
                values(idx, p, m, keys)
            pending = upcoming
            if dil == 1:
                for t0 in range(group[0] * lb, (group[-1] + 1) * lb, PERM_TILE):
                    combine(t0)


def _dilated(qb, kb, vb, zb, g_dil, layer):
    bsz, _, seq, _ = qb.shape
    blk = pl.BlockSpec((None, None, seq, DIL_HD), lambda b, h: (b, h, 0, 0))
    return pl.pallas_call(
        functools.partial(_dil_kernel, seq=seq, layer=layer),
        grid=(bsz, DIL_HEADS),
        in_specs=[blk, blk, blk, blk, pl.BlockSpec((g_dil.shape[0], DIL_HD), lambda b, h: (0, h))],
        out_specs=pl.BlockSpec((None, seq, DIL_HD), lambda b, h: (b, 0, h)),
        out_shape=jax.ShapeDtypeStruct((bsz, seq, DIL_WIDTH), BF16),
        scratch_shapes=[pltpu.VMEM((seq + DIL_LB, DIL_HD), BF16),
                        pltpu.VMEM((seq // DIL_LB + 1, DIL_HD, DIL_LB), BF16),
                        pltpu.VMEM((seq + DIL_LB, DIL_HD), BF16)]
        + [pltpu.VMEM((seq, DIL_HD), F32) for _ in range(10)],
        compiler_params=pltpu.CompilerParams(
            dimension_semantics=("arbitrary", "arbitrary"), vmem_limit_bytes=VMEM_LIMIT),
        name="dilated",
    )(qb, kb, vb, zb, g_dil)


def _rope_tables(seq):
    inv_freq = ROPE_THETA ** (-np.arange(0, DIL_HD, 2, dtype=np.float64) / DIL_HD)
    ang = np.arange(seq, dtype=np.float64)[:, None] * inv_freq[None, :]
    cos, sin = np.cos(ang), np.sin(ang)
    return (jnp.asarray(np.concatenate([cos, cos], axis=-1), F32),
            jnp.asarray(np.concatenate([-sin, sin], axis=-1), F32))


def kernel(x, c, w_ada, b_ada, g_pre, w_in, w_gate_up, b_gate_up, g_gla, g_dil, w_out, g_post):
    bsz, seq, d = x.shape
    depth = w_ada.shape[0]
    mod = _modulation(c, w_ada, b_ada).reshape(depth, bsz, 1, 3 * d)
    cos, sin = _rope_tables(seq)
    w_in_bf = w_in.astype(BF16)
    w_gate_pad = jnp.pad(w_gate_up.astype(BF16), ((0, 0), (0, LR_PAD - GLA_LOWRANK), (0, 0)))
    for l in range(depth):
        qe, ke, kend, dec, va, za, qb, kb, vb, zb = _in_proj(
            x, mod, g_pre, w_in_bf, w_gate_pad, b_gate_up, l, cos, sin)
        yb = _dilated(qb, kb, vb, zb, g_dil, l)
        x = _gla_out_proj(qe, ke, kend, dec, va, za, g_gla, yb, x, mod, w_out, g_post, l)
    return x
```

```python
import functools
import math

import jax
import jax.numpy as jnp
import numpy as np
from jax import lax
from jax.experimental import pallas as pl
from jax.experimental.pallas import tpu as pltpu

F32 = jnp.float32
BF16 = jnp.bfloat16

D_MODEL = 1024
GLA_HEADS = 4
GLA_DK = 64
GLA_DV = 128
GLA_QK = GLA_HEADS * GLA_DK
GLA_WIDTH = GLA_HEADS * GLA_DV
GLA_LOWRANK = 16
GLA_TAU = 16.0
GLA_CHUNK = 64
GLA_GROUP = 4
DIL_HEADS = 4
DIL_HD = 128
DIL_WIDTH = DIL_HEADS * DIL_HD
DIL_PATTERNS = ((128, 1), (512, 4), (2048, 16))
DIL_LB = 128
PERM_TILE, PERM_D = DIL_PATTERNS[1]
ROPE_THETA = 10000.0
EPS = 1e-6
LANES = 128
LR_PAD = LANES
_GLA_SIZES = (("qa", GLA_QK), ("ka", GLA_QK), ("va", GLA_WIDTH), ("za", GLA_WIDTH), ("lr", LR_PAD))
W_GLA_COLS = {}
for _name, _size in _GLA_SIZES:
    _lo = sum(n for _, n in _GLA_SIZES[:len(W_GLA_COLS)])
    W_GLA_COLS[_name] = (_lo, _lo + _size)
W_DIL_START = W_GLA_COLS["lr"][0] + GLA_LOWRANK
W_DIL_NAMES = ("qb", "kb", "vb", "zb")

VMEM_LIMIT = 56 * 1024 * 1024
DIL_UNROLL = 16
ROW_TILE = 1024
NEG_INF = float("-inf")
LOG2E = math.log2(math.e)


def _silu(v):
    return v * jax.nn.sigmoid(v)


def _dot(a, b):
    return jnp.dot(a, b, preferred_element_type=F32)


def _dot_nt(a, b):
    return lax.dot_general(a, b, (((1,), (1,)), ((), ())), preferred_element_type=F32)


def _dot_tn(a, b):
    return lax.dot_general(a, b, (((0,), (0,)), ((), ())), preferred_element_type=F32)


def _mod_kernel(c_ref, w_ref, b_ref, o_ref):
    sc = _silu(c_ref[...]).astype(BF16)
    o_ref[...] = _dot(sc, w_ref[...].astype(BF16)) + b_ref[pl.ds(pl.program_id(0), 1), :]


def _modulation(c, w_ada, b_ada):
    depth, d, e = w_ada.shape
    bsz = c.shape[0]
    return pl.pallas_call(
        _mod_kernel,
        grid=(depth,),
        in_specs=[
            pl.BlockSpec((bsz, d), lambda l: (0, 0)),
            pl.BlockSpec((None, d, e), lambda l: (l, 0, 0)),
            pl.BlockSpec((depth, e), lambda l: (0, 0)),
        ],
        out_specs=pl.BlockSpec((None, bsz, e), lambda l: (l, 0, 0)),
        out_shape=jax.ShapeDtypeStruct((depth, bsz, e), F32),
        compiler_params=pltpu.CompilerParams(
            dimension_semantics=("arbitrary",), vmem_limit_bytes=VMEM_LIMIT),
        name="adaln_mod",
    )(c, w_ada, b_ada)


def _rope(v, cos, sin_signed):
    return v * cos + pltpu.roll(v, DIL_HD // 2, axis=1) * sin_signed


def _gla_decayed(q, k, la, tril_bd):
    c_len = GLA_CHUNK
    h1, h2 = _split2(la)
    b = _dot(tril_bd, h1) + _dot(tril_bd, h2)
    decay = [jnp.exp2(b[(c + 1) * c_len - 1:(c + 1) * c_len, :]) for c in range(GLA_GROUP)]
    decay_rows = jnp.concatenate([jnp.broadcast_to(d, (c_len, GLA_QK)) for d in decay], axis=0)
    k_e32 = k * jnp.exp2(-b)
    q_e = (q * jnp.exp2(b)).astype(BF16)
    return q_e, k_e32.astype(BF16), (k_e32 * decay_rows).astype(BF16), jnp.concatenate(decay, axis=0)


def _chunk_tril():
    grp = GLA_GROUP * GLA_CHUNK
    ri = lax.broadcasted_iota(jnp.int32, (grp, grp), 0)
    ci = lax.broadcasted_iota(jnp.int32, (grp, grp), 1)
    return ((ri >= ci) & ((ri & -GLA_CHUNK) == (ci & -GLA_CHUNK))).astype(BF16)


def _in_proj_kernel(x_ref, mod_ref, gpre_ref, w_ref, wg_ref, bg_ref, cos_ref, sin_ref,
                    qe_ref, ke_ref, kend_ref, dec_ref, va_ref, za_ref, qb_ref, kb_ref, vb_ref, zb_ref,
                    perm_ref, wd_ref, wz_ref, *, layer):
    @pl.when((pl.program_id(0) == 0) & (pl.program_id(1) == 0))
    def _():
        wd_ref[...] = w_ref[:, W_DIL_START:W_DIL_START + len(W_DIL_NAMES) * DIL_WIDTH]
        lo, hi = W_GLA_COLS["lr"]
        wz_ref[...] = _dot(w_ref[:, lo:hi], wg_ref[...]).astype(BF16)

    shift = mod_ref[:, 0:D_MODEL]
    gain = gpre_ref[layer:layer + 1, :] * (1.0 + mod_ref[:, D_MODEL:2 * D_MODEL])
    tril_bd = _chunk_tril()
    n_res = PERM_TILE // PERM_D

    def store_residue_order(out_ref, hd, slab, t0, val):
        perm_ref[slab] = val
        for r in range(PERM_D):
            dst = slice(t0 + r * n_res, t0 + (r + 1) * n_res)
            out_ref[hd, dst, :] = perm_ref[slab, pl.ds(r, n_res, stride=PERM_D), :].astype(out_ref.dtype)

    for t0 in range(0, x_ref.shape[0], PERM_TILE):
        rows = slice(t0, t0 + PERM_TILE)
        x = x_ref[rows, :]
        h = (x * lax.rsqrt(jnp.mean(x * x, axis=-1, keepdims=True) + EPS) * gain + shift).astype(BF16)

        def proj(name):
            if name in W_DIL_NAMES:
                lo = W_DIL_NAMES.index(name) * DIL_WIDTH
                return _dot(h, wd_ref[:, lo:lo + DIL_WIDTH])
            lo, hi = W_GLA_COLS[name]
            return _dot(h, w_ref[:, lo:hi])

        z = _dot(h, wz_ref[...]) + bg_ref[layer:layer + 1, :]
        la = (jnp.minimum(z, 0.0) - jnp.log1p(jnp.exp(-jnp.abs(z)))) * (LOG2E / GLA_TAU)
        va_ref[rows, :] = proj("va").astype(va_ref.dtype)
        za_ref[rows, :] = proj("za").astype(za_ref.dtype)
        q, k, v, zg = proj("qb"), proj("kb"), proj("vb"), proj("zb")
        q_a = proj("qa") * (GLA_DK ** -0.5)
        k_a = proj("ka")

        grp = GLA_GROUP * GLA_CHUNK
        for g0 in range(0, PERM_TILE, grp):
            gr = slice(g0, g0 + grp)
            q_e, k_e, k_end, decay = _gla_decayed(q_a[gr, :], k_a[gr, :], la[gr, :], tril_bd)
            out = slice(t0 + g0, t0 + g0 + grp)
            qe_ref[out, :] = q_e
            ke_ref[out, :] = k_e
            kend_ref[out, :] = k_end
            c0 = (t0 + g0) // GLA_CHUNK
            dec_ref[c0:c0 + GLA_GROUP, :] = decay

        cos = cos_ref[rows, :]
        sin = sin_ref[rows, :]
        for hd in range(DIL_HEADS):
            cols = slice(hd * DIL_HD, (hd + 1) * DIL_HD)
            store_residue_order(qb_ref, hd, 4 * hd, t0, _rope(q[:, cols], cos, sin) * (DIL_HD ** -0.5 * LOG2E))
            store_residue_order(kb_ref, hd, 4 * hd + 1, t0, _rope(k[:, cols], cos, sin))
            store_residue_order(vb_ref, hd, 4 * hd + 2, t0, v[:, cols])
            store_residue_order(zb_ref, hd, 4 * hd + 3, t0, zg[:, cols])


def _in_proj(x, mod, g_pre, w_in_bf, w_gate_pad, b_gate, layer, cos, sin):
    bsz, seq, d = x.shape
    tm = ROW_TILE
    row = lambda b, i: (b, i, 0)
    const = lambda b, i: (0, 0)

    def of_layer(a):
        return pl.BlockSpec((None,) + a.shape[1:], lambda b, i: (layer,) + (0,) * (a.ndim - 1),
                            pipeline_mode=pl.Buffered(1))

    gla_cols = ((GLA_QK, BF16), (GLA_QK, BF16), (GLA_QK, BF16))
    gate_cols = ((GLA_WIDTH, BF16), (GLA_WIDTH, BF16))
    dil_types = (F32, F32, F32, BF16)
    return pl.pallas_call(
        functools.partial(_in_proj_kernel, layer=layer),
        grid=(bsz, seq // tm),
        in_specs=[
            pl.BlockSpec((None, tm, d), row),
            pl.BlockSpec((None, None, 1, 3 * d), lambda b, i: (layer, b, 0, 0)),
            pl.BlockSpec(g_pre.shape, const), of_layer(w_in_bf), of_layer(w_gate_pad),
            pl.BlockSpec(b_gate.shape, const),
            pl.BlockSpec((tm, DIL_HD), lambda b, i: (i, 0)),
            pl.BlockSpec((tm, DIL_HD), lambda b, i: (i, 0)),
        ],
        out_specs=[pl.BlockSpec((None, tm, n), row) for n, _ in gla_cols]
        + [pl.BlockSpec((None, tm // GLA_CHUNK, GLA_QK), row)]
        + [pl.BlockSpec((None, tm, n), row) for n, _ in gate_cols]
        + [pl.BlockSpec((None, DIL_HEADS, tm, DIL_HD), lambda b, i: (b, 0, i, 0)) for _ in dil_types],
        out_shape=[jax.ShapeDtypeStruct((bsz, seq, n), dt) for n, dt in gla_cols]
        + [jax.ShapeDtypeStruct((bsz, seq // GLA_CHUNK, GLA_QK), F32)]
        + [jax.ShapeDtypeStruct((bsz, seq, n), dt) for n, dt in gate_cols]
        + [jax.ShapeDtypeStruct((bsz, DIL_HEADS, seq, DIL_HD), dt) for dt in dil_types],
        scratch_shapes=[pltpu.VMEM((4 * DIL_HEADS, PERM_TILE, DIL_HD), F32),
                        pltpu.VMEM((d, len(W_DIL_NAMES) * DIL_WIDTH), BF16),
                        pltpu.VMEM((d, GLA_QK), BF16)],
        compiler_params=pltpu.CompilerParams(
            dimension_semantics=("arbitrary", "arbitrary"), vmem_limit_bytes=VMEM_LIMIT),
        name="in_proj",
    )(x, mod, g_pre, w_in_bf, w_gate_pad, b_gate, cos, sin)


def _head_norm_gate(o, g, z):
    r = o * lax.rsqrt(jnp.mean(o * o, axis=-1, keepdims=True) + EPS)
    return r * g * _silu(z.astype(F32))


def _split2(v):
    h1 = v.astype(BF16)
    h2 = (v - h1.astype(F32)).astype(BF16)
    return h1, h2


def _gla_out_kernel(qe_ref, ke_ref, kend_ref, dec_ref, va_ref, za_ref, g_ref, yb_ref, x_ref, mod_ref, wo_ref,
                    gpost_ref, o_ref, st_ref, ya_ref, wo_bf, *, chunks, layer):
    @pl.when((pl.program_id(0) == 0) & (pl.program_id(1) == 0))
    def _():
        wo_bf[...] = wo_ref[...].astype(BF16)

    @pl.when(pl.program_id(1) == 0)
    def _():
        st_ref[...] = jnp.zeros_like(st_ref)

    c_len = GLA_CHUNK
    grp = GLA_GROUP * c_len
    causal = (lax.broadcasted_iota(jnp.int32, (c_len, c_len), 0)
              >= lax.broadcasted_iota(jnp.int32, (c_len, c_len), 1))
    heads = range(GLA_HEADS)
    kcol = [slice(hd * GLA_DK, (hd + 1) * GLA_DK) for hd in heads]
    vcol = [slice(hd * GLA_DV, (hd + 1) * GLA_DV) for hd in heads]

    def prep(r0):
        rows = pl.ds(r0, grp)
        c0 = r0 // c_len
        decay = [dec_ref[c0 + c:c0 + c + 1, :] for c in range(GLA_GROUP)]
        v = [[va_ref[pl.ds(r0 + c * c_len, c_len), vcol[hd]] for hd in heads] for c in range(GLA_GROUP)]
        return dict(r0=r0, q_e=qe_ref[rows, :], k_e=ke_ref[rows, :], k_end=kend_ref[rows, :], decay=decay, v=v)

    crow = [slice(c * c_len, (c + 1) * c_len) for c in range(GLA_GROUP)]

    def intra(p):
        q_e, k_e, k_end, v = p["q_e"], p["k_e"], p["k_end"], p["v"]
        a = [[jnp.where(causal, _dot_nt(q_e[crow[c], kcol[hd]], k_e[crow[c], kcol[hd]]), 0.0).astype(BF16)
              for hd in heads] for c in range(GLA_GROUP)]
        p["inc"] = [[_dot_tn(v[c][hd], k_end[crow[c], kcol[hd]]) for hd in heads] for c in range(GLA_GROUP)]
        p["o"] = [[_dot(a[c][hd], v[c][hd]) for hd in heads] for c in range(GLA_GROUP)]

    def inter(p, st):
        for hd in heads:
            for c in range(GLA_GROUP):
                p["o"][c][hd] = p["o"][c][hd] + _dot_nt(p["q_e"][crow[c], kcol[hd]], st[hd].astype(BF16))
                st[hd] = st[hd] * p["decay"][c][:, kcol[hd]] + p["inc"][c][hd]

    def epilogue(p):
        for c in range(GLA_GROUP):
            for hd in heads:
                out_rows = pl.ds(p["r0"] + c * c_len, c_len)
                ya_ref[out_rows, vcol[hd]] = _head_norm_gate(
                    p["o"][c][hd], g_ref[layer:layer + 1, vcol[hd]], za_ref[out_rows, vcol[hd]]).astype(ya_ref.dtype)

    gated_gain = gpost_ref[layer:layer + 1, :] * mod_ref[:, 2 * D_MODEL:3 * D_MODEL]

    def project(p):
        rows = pl.ds(p["r0"], grp)
        y = _dot(ya_ref[rows, :], wo_bf[0:GLA_WIDTH, :]) + _dot(yb_ref[rows, :], wo_bf[GLA_WIDTH:, :])
        o_ref[rows, :] = x_ref[rows, :] + y * lax.rsqrt(jnp.mean(y * y, axis=-1, keepdims=True) + EPS) * gated_gain

    st = [st_ref[hd] for hd in heads]
    groups = [prep(g * grp) for g in range(chunks // GLA_GROUP)]
    intra(groups[0])
    for g, p in enumerate(groups):
        inter(p, st)
        if g + 1 < len(groups):
            intra(groups[g + 1])
        epilogue(p)
        project(p)
    for hd in heads:
        st_ref[hd] = st[hd]


def _gla_out_proj(qe, ke, kend, dec, va, za, g_gla, yb, x, mod, w_out, g_post, layer):
    bsz, seq, d = x.shape
    ts = ROW_TILE
    row = lambda b, i: (b, i, 0)
    const = lambda b, i: (0, 0)
    return pl.pallas_call(
        functools.partial(_gla_out_kernel, chunks=ts // GLA_CHUNK, layer=layer),
        grid=(bsz, seq // ts),
        in_specs=[
            pl.BlockSpec((None, ts, GLA_QK), row),
            pl.BlockSpec((None, ts, GLA_QK), row),
            pl.BlockSpec((None, ts, GLA_QK), row),
            pl.BlockSpec((None, ts // GLA_CHUNK, GLA_QK), row),
            pl.BlockSpec((None, ts, GLA_WIDTH), row),
            pl.BlockSpec((None, ts, GLA_WIDTH), row),
            pl.BlockSpec(g_gla.shape, const),
            pl.BlockSpec((None, ts, DIL_WIDTH), row),
            pl.BlockSpec((None, ts, d), row),
            pl.BlockSpec((None, None, 1, 3 * d), lambda b, i: (layer, b, 0, 0)),
            pl.BlockSpec((None,) + w_out.shape[1:], lambda b, i: (layer, 0, 0), pipeline_mode=pl.Buffered(1)),
            pl.BlockSpec(g_post.shape, const),
        ],
        out_specs=pl.BlockSpec((None, ts, d), row),
        out_shape=jax.ShapeDtypeStruct((bsz, seq, d), F32),
        scratch_shapes=[pltpu.VMEM((GLA_HEADS, GLA_DV, GLA_DK), F32), pltpu.VMEM((ts, GLA_WIDTH), BF16),
                        pltpu.VMEM(w_out.shape[1:], BF16)],
        compiler_params=pltpu.CompilerParams(
            dimension_semantics=("arbitrary", "arbitrary"), vmem_limit_bytes=VMEM_LIMIT),
        name="gla_out_proj",
    )(qe, ke, kend, dec, va, za, g_gla, yb, x, mod, w_out, g_post)


def _dil_block_chunks(pattern, idx, seq):
    window, dil = DIL_PATTERNS[pattern]
    nb = seq // window
    lb = DIL_LB
    n_r = PERM_TILE // PERM_D
    if dil == PERM_D:
        return [((idx % nb) * PERM_TILE + (idx // nb) * n_r, 1)]
    if dil == 1:
        tile, part = idx // (PERM_TILE // lb), idx % (PERM_TILE // lb)
        rows = lb // PERM_D
        return [(tile * PERM_TILE + r * n_r + part * rows, 1) for r in range(PERM_D)]
    sub = dil // PERM_D
    res, n = idx // nb, idx % nb
    r4, c = res % PERM_D, res // PERM_D
    tiles = window // PERM_TILE
    return [(n * window + t * PERM_TILE + r4 * n_r + c, sub) for t in range(tiles)]


def _dil_kernel(q_ref, k_ref, v_ref, z_ref, g_ref, y_ref, qd, kd, vd, ynat,
                o1, o2, o3, m1, m2, m3, d1, d2, d3, *, seq, layer):
    lb = DIL_LB
    n_blocks = seq // lb
    qi = lax.broadcasted_iota(jnp.int32, (lb, 2 * lb), 0)
    ki = lax.broadcasted_iota(jnp.int32, (lb, 2 * lb), 1)

    def biases(pos_in_block):
        dist = pos_in_block(qi) + lb - (pos_in_block(ki & (lb - 1)) + (ki & lb))
        band = (dist >= 0) & (dist <= lb)
        b_any = jnp.where(band, 0.0, NEG_INF)
        b_first = jnp.where(band & (ki >= lb), 0.0, NEG_INF)
        return b_any, b_first, b_first[:, lb:]

    step_order = biases(lambda a: a)
    rows_p1 = lb // PERM_D
    p1_order = biases(lambda a: PERM_D * (a % rows_p1) + a // rows_p1)

    kd[0] = jnp.zeros((DIL_HD, lb), BF16)
    vd[0:lb, :] = jnp.zeros((lb, DIL_HD), BF16)

    n_r = PERM_TILE // PERM_D

    def combine(t0):
        rows = pl.ds(t0, PERM_TILE)
        a1, a2, a3 = m1[rows, :], m2[rows, :], m3[rows, :]
        m = jnp.maximum(jnp.maximum(a1, a2), a3)
        e1, e2, e3 = jnp.exp2(a1 - m), jnp.exp2(a2 - m), jnp.exp2(a3 - m)
        den = e1 * d1[rows, :] + e2 * d2[rows, :] + e3 * d3[rows, :]
        o = (e1 * o1[rows, :] + e2 * o2[rows, :] + e3 * o3[rows, :]) / den
        y = _head_norm_gate(o, g_ref[layer:layer + 1, :], z_ref[rows, :])
        for r in range(PERM_D):
            ynat[pl.ds(t0 + r, n_r, stride=PERM_D), :] = y[r * n_r:(r + 1) * n_r, :]
        y_ref[rows, :] = ynat[rows, :].astype(y_ref.dtype)

    order = sorted(range(len(DIL_PATTERNS)), key=lambda i: -DIL_PATTERNS[i][1])
    assert DIL_PATTERNS[order[-1]][1] == 1
    for pat in order:
        window, dil = DIL_PATTERNS[pat]
        o_scr, m_scr, d_scr = (o1, o2, o3)[pat], (m1, m2, m3)[pat], (d1, d2, d3)[pat]
        nb = seq // window
        bias_any, bias_first, bias_cur = p1_order if dil == 1 else step_order

        def load_block(ref, idx, pat=pat):
            chunks = _dil_block_chunks(pat, idx, seq)
            rows = lb // len(chunks)
            parts = [ref[pl.ds(s0, rows) if st == 1 else pl.ds(s0, rows, stride=st), :] for s0, st in chunks]
            return parts[0] if len(parts) == 1 else jnp.concatenate(parts, axis=0)

        def store_block(ref, idx, val, pat=pat):
            chunks = _dil_block_chunks(pat, idx, seq)
            rows = lb // len(chunks)
            for i, (s0, st) in enumerate(chunks):
                dst = pl.ds(s0, rows) if st == 1 else pl.ds(s0, rows, stride=st)
                ref[dst, :] = val[i * rows:(i + 1) * rows, :]

        for idx in range(n_blocks):
            dst = pl.ds(lb + idx * lb, lb)
            qd[dst, :] = load_block(q_ref, idx).astype(BF16)
            kd[idx + 1] = load_block(k_ref, idx).T.astype(BF16)
            vd[dst, :] = load_block(v_ref, idx).astype(BF16)

        def scores(idx, nb=nb, bias_any=bias_any, bias_cur=bias_cur):
            q = qd[pl.ds(lb + idx * lb, lb), :]
            if idx % nb == 0:
                return _dot(q, kd[idx + 1]) + bias_cur, pl.ds(lb + idx * lb, lb)
            s = _dot(q, jnp.concatenate([kd[idx], kd[idx + 1]], axis=1)) + bias_any
            return s, pl.ds(idx * lb, 2 * lb)

        def softmax(s):
            m = jnp.max(s, axis=-1, keepdims=True)
            return jnp.exp2((s - m).astype(BF16)), m

        def values(idx, p, m, keys, store_block=store_block, o_scr=o_scr, m_scr=m_scr, d_scr=d_scr):
            v_ones = jnp.concatenate([vd[keys, :], jnp.ones((p.shape[1], DIL_HD), BF16)], axis=1)
            acc = _dot(p, v_ones)
            store_block(o_scr, idx, acc[:, :DIL_HD])
            store_block(d_scr, idx, acc[:, DIL_HD:])
            store_block(m_scr, idx, jnp.broadcast_to(m, (lb, DIL_HD)))

        groups = [range(g * DIL_UNROLL, (g + 1) * DIL_UNROLL) for g in range(n_blocks // DIL_UNROLL)]
        pending = [scores(idx) for idx in groups[0]]
        for g, group in enumerate(groups):
            upcoming = [scores(idx) for idx in groups[g + 1]] if g + 1 < len(groups) else []
            probs = [softmax(s) for s, _ in pending]
            for idx, (p, m), (_, keys) in zip(group, probs, pending):
                values(idx, p, m, keys)
            pending = upcoming
            if dil == 1:
                for t0 in range(group[0] * lb, (group[-1] + 1) * lb, PERM_TILE):
                    combine(t0)


def _dilated(qb, kb, vb, zb, g_dil, layer):
    bsz, _, seq, _ = qb.shape
    blk = pl.BlockSpec((None, None, seq, DIL_HD), lambda b, h: (b, h, 0, 0))
    return pl.pallas_call(
        functools.partial(_dil_kernel, seq=seq, layer=layer),
        grid=(bsz, DIL_HEADS),
        in_specs=[blk, blk, blk, blk, pl.BlockSpec((g_dil.shape[0], DIL_HD), lambda b, h: (0, h))],
        out_specs=pl.BlockSpec((None, seq, DIL_HD), lambda b, h: (b, 0, h)),
        out_shape=jax.ShapeDtypeStruct((bsz, seq, DIL_WIDTH), BF16),
        scratch_shapes=[pltpu.VMEM((seq + DIL_LB, DIL_HD), BF16),
                        pltpu.VMEM((seq // DIL_LB + 1, DIL_HD, DIL_LB), BF16),
                        pltpu.VMEM((seq + DIL_LB, DIL_HD), BF16)]
        + [pltpu.VMEM((seq, DIL_HD), F32) for _ in range(10)],
        compiler_params=pltpu.CompilerParams(
            dimension_semantics=("arbitrary", "arbitrary"), vmem_limit_bytes=VMEM_LIMIT),
        name="dilated",
    )(qb, kb, vb, zb, g_dil)


def _rope_tables(seq):
    inv_freq = ROPE_THETA ** (-np.arange(0, DIL_HD, 2, dtype=np.float64) / DIL_HD)
    ang = np.arange(seq, dtype=np.float64)[:, None] * inv_freq[None, :]
    cos, sin = np.cos(ang), np.sin(ang)
    return (jnp.asarray(np.concatenate([cos, cos], axis=-1), F32),
            jnp.asarray(np.concatenate([-sin, sin], axis=-1), F32))


def kernel(x, c, w_ada, b_ada, g_pre, w_in, w_gate_up, b_gate_up, g_gla, g_dil, w_out, g_post):
    bsz, seq, d = x.shape
    depth = w_ada.shape[0]
    mod = _modulation(c, w_ada, b_ada).reshape(depth, bsz, 1, 3 * d)
    cos, sin = _rope_tables(seq)
    w_in_bf = w_in.astype(BF16)
    w_gate_pad = jnp.pad(w_gate_up.astype(BF16), ((0, 0), (0, LR_PAD - GLA_LOWRANK), (0, 0)))
    for l in range(depth):
        qe, ke, kend, dec, va, za, qb, kb, vb, zb = _in_proj(
            x, mod, g_pre, w_in_bf, w_gate_pad, b_gate_up, l, cos, sin)
        yb = _dilated(qb, kb, vb, zb, g_dil, l)
        x = _gla_out_proj(qe, ke, kend, dec, va, za, g_gla, yb, x, mod, w_out, g_post, l)
    return x
```

```python
import functools
import math

import jax
import jax.numpy as jnp
import numpy as np
from jax import lax
from jax.experimental import pallas as pl
from jax.experimental.pallas import tpu as pltpu

F32 = jnp.float32
BF16 = jnp.bfloat16

D_MODEL = 1024
GLA_HEADS = 4
GLA_DK = 64
GLA_DV = 128
GLA_QK = GLA_HEADS * GLA_DK
GLA_WIDTH = GLA_HEADS * GLA_DV
GLA_LOWRANK = 16
GLA_TAU = 16.0
GLA_CHUNK = 64
GLA_GROUP = 4
DIL_HEADS = 4
DIL_HD = 128
DIL_WIDTH = DIL_HEADS * DIL_HD
DIL_PATTERNS = ((128, 1), (512, 4), (2048, 16))
DIL_LB = 128
PERM_TILE, PERM_D = DIL_PATTERNS[1]
ROPE_THETA = 10000.0
EPS = 1e-6
LANES = 128
LR_PAD = LANES
_GLA_SIZES = (("qa", GLA_QK), ("ka", GLA_QK), ("va", GLA_WIDTH), ("za", GLA_WIDTH), ("lr", LR_PAD))
W_GLA_COLS = {}
for _name, _size in _GLA_SIZES:
    _lo = sum(n for _, n in _GLA_SIZES[:len(W_GLA_COLS)])
    W_GLA_COLS[_name] = (_lo, _lo + _size)
W_DIL_START = W_GLA_COLS["lr"][0] + GLA_LOWRANK
W_DIL_NAMES = ("qb", "kb", "vb", "zb")

VMEM_LIMIT = 56 * 1024 * 1024
DIL_UNROLL = 16
ROW_TILE = 1024
NEG_INF = float("-inf")
LOG2E = math.log2(math.e)


def _silu(v):
    return v * jax.nn.sigmoid(v)


def _dot(a, b):
    return jnp.dot(a, b, preferred_element_type=F32)


def _dot_nt(a, b):
    return lax.dot_general(a, b, (((1,), (1,)), ((), ())), preferred_element_type=F32)


def _dot_tn(a, b):
    return lax.dot_general(a, b, (((0,), (0,)), ((), ())), preferred_element_type=F32)


def _mod_kernel(c_ref, w_ref, b_ref, o_ref):
    sc = _silu(c_ref[...]).astype(BF16)
    o_ref[...] = _dot(sc, w_ref[...].astype(BF16)) + b_ref[pl.ds(pl.program_id(0), 1), :]


def _modulation(c, w_ada, b_ada):
    depth, d, e = w_ada.shape
    bsz = c.shape[0]
    return pl.pallas_call(
        _mod_kernel,
        grid=(depth,),
        in_specs=[
            pl.BlockSpec((bsz, d), lambda l: (0, 0)),
            pl.BlockSpec((None, d, e), lambda l: (l, 0, 0)),
            pl.BlockSpec((depth, e), lambda l: (0, 0)),
        ],
        out_specs=pl.BlockSpec((None, bsz, e), lambda l: (l, 0, 0)),
        out_shape=jax.ShapeDtypeStruct((depth, bsz, e), F32),
        compiler_params=pltpu.CompilerParams(
            dimension_semantics=("arbitrary",), vmem_limit_bytes=VMEM_LIMIT),
        name="adaln_mod",
    )(c, w_ada, b_ada)


def _rope(v, cos, sin_signed):
    return v * cos + pltpu.roll(v, DIL_HD // 2, axis=1) * sin_signed


def _gla_decayed(q, k, la, tril_bd):
    c_len = GLA_CHUNK
    h1, h2 = _split2(la)
    b = _dot(tril_bd, h1) + _dot(tril_bd, h2)
    decay = [jnp.exp2(b[(c + 1) * c_len - 1:(c + 1) * c_len, :]) for c in range(GLA_GROUP)]
    decay_rows = jnp.concatenate([jnp.broadcast_to(d, (c_len, GLA_QK)) for d in decay], axis=0)
    k_e32 = k * jnp.exp2(-b)
    q_e = (q * jnp.exp2(b)).astype(BF16)
    return q_e, k_e32.astype(BF16), (k_e32 * decay_rows).astype(BF16), jnp.concatenate(decay, axis=0)


def _chunk_tril():
    grp = GLA_GROUP * GLA_CHUNK
    ri = lax.broadcasted_iota(jnp.int32, (grp, grp), 0)
    ci = lax.broadcasted_iota(jnp.int32, (grp, grp), 1)
    return ((ri >= ci) & ((ri & -GLA_CHUNK) == (ci & -GLA_CHUNK))).astype(BF16)


def _in_proj_kernel(x_ref, mod_ref, gpre_ref, w_ref, wg_ref, bg_ref, cos_ref, sin_ref,
                    qe_ref, ke_ref, kend_ref, dec_ref, va_ref, za_ref, qb_ref, kb_ref, vb_ref, zb_ref,
                    perm_ref, wd_ref, wz_ref, *, layer):
    @pl.when((pl.program_id(0) == 0) & (pl.program_id(1) == 0))
    def _():
        wd_ref[...] = w_ref[:, W_DIL_START:W_DIL_START + len(W_DIL_NAMES) * DIL_WIDTH]
        lo, hi = W_GLA_COLS["lr"]
        wz_ref[...] = _dot(w_ref[:, lo:hi], wg_ref[...]).astype(BF16)

    shift = mod_ref[:, 0:D_MODEL]
    gain = gpre_ref[layer:layer + 1, :] * (1.0 + mod_ref[:, D_MODEL:2 * D_MODEL])
    tril_bd = _chunk_tril()
    n_res = PERM_TILE // PERM_D

    def store_residue_order(out_ref, hd, slab, t0, val):
        perm_ref[slab] = val
        for r in range(PERM_D):
            dst = slice(t0 + r * n_res, t0 + (r + 1) * n_res)
            out_ref[hd, dst, :] = perm_ref[slab, pl.ds(r, n_res, stride=PERM_D), :].astype(out_ref.dtype)

    for t0 in range(0, x_ref.shape[0], PERM_TILE):
        rows = slice(t0, t0 + PERM_TILE)
        x = x_ref[rows, :]
        h = (x * lax.rsqrt(jnp.mean(x * x, axis=-1, keepdims=True) + EPS) * gain + shift).astype(BF16)

        def proj(name):
            if name in W_DIL_NAMES:
                lo = W_DIL_NAMES.index(name) * DIL_WIDTH
                return _dot(h, wd_ref[:, lo:lo + DIL_WIDTH])
            lo, hi = W_GLA_COLS[name]
            return _dot(h, w_ref[:, lo:hi])

        z = _dot(h, wz_ref[...]) + bg_ref[layer:layer + 1, :]
        la = (jnp.minimum(z, 0.0) - jnp.log1p(jnp.exp(-jnp.abs(z)))) * (LOG2E / GLA_TAU)
        va_ref[rows, :] = proj("va").astype(va_ref.dtype)
        za_ref[rows, :] = proj("za").astype(za_ref.dtype)
        q, k, v, zg = proj("qb"), proj("kb"), proj("vb"), proj("zb")
        q_a = proj("qa") * (GLA_DK ** -0.5)
        k_a = proj("ka")

        grp = GLA_GROUP * GLA_CHUNK
        for g0 in range(0, PERM_TILE, grp):
            gr = slice(g0, g0 + grp)
            q_e, k_e, k_end, decay = _gla_decayed(q_a[gr, :], k_a[gr, :], la[gr, :], tril_bd)
            out = slice(t0 + g0, t0 + g0 + grp)
            qe_ref[out, :] = q_e
            ke_ref[out, :] = k_e
            kend_ref[out, :] = k_end
            c0 = (t0 + g0) // GLA_CHUNK
            dec_ref[c0:c0 + GLA_GROUP, :] = decay

        cos = cos_ref[rows, :]
        sin = sin_ref[rows, :]
        for hd in range(DIL_HEADS):
            cols = slice(hd * DIL_HD, (hd + 1) * DIL_HD)
            store_residue_order(qb_ref, hd, 4 * hd, t0, _rope(q[:, cols], cos, sin) * (DIL_HD ** -0.5 * LOG2E))
            store_residue_order(kb_ref, hd, 4 * hd + 1, t0, _rope(k[:, cols], cos, sin))
            store_residue_order(vb_ref, hd, 4 * hd + 2, t0, v[:, cols])
            store_residue_order(zb_ref, hd, 4 * hd + 3, t0, zg[:, cols])


def _in_proj(x, mod, g_pre, w_in_bf, w_gate_pad, b_gate, layer, cos, sin):
    bsz, seq, d = x.shape
    tm = ROW_TILE
    row = lambda b, i: (b, i, 0)
    const = lambda b, i: (0, 0)

    def of_layer(a):
        return pl.BlockSpec((None,) + a.shape[1:], lambda b, i: (layer,) + (0,) * (a.ndim - 1),
                            pipeline_mode=pl.Buffered(1))

    gla_cols = ((GLA_QK, BF16), (GLA_QK, BF16), (GLA_QK, BF16))
    gate_cols = ((GLA_WIDTH, BF16), (GLA_WIDTH, BF16))
    dil_types = (F32, F32, F32, BF16)
    return pl.pallas_call(
        functools.partial(_in_proj_kernel, layer=layer),
        grid=(bsz, seq // tm),
        in_specs=[
            pl.BlockSpec((None, tm, d), row),
            pl.BlockSpec((None, None, 1, 3 * d), lambda b, i: (layer, b, 0, 0)),
            pl.BlockSpec(g_pre.shape, const), of_layer(w_in_bf), of_layer(w_gate_pad),
            pl.BlockSpec(b_gate.shape, const),
            pl.BlockSpec((tm, DIL_HD), lambda b, i: (i, 0)),
            pl.BlockSpec((tm, DIL_HD), lambda b, i: (i, 0)),
        ],
        out_specs=[pl.BlockSpec((None, tm, n), row) for n, _ in gla_cols]
        + [pl.BlockSpec((None, tm // GLA_CHUNK, GLA_QK), row)]
        + [pl.BlockSpec((None, tm, n), row) for n, _ in gate_cols]
        + [pl.BlockSpec((None, DIL_HEADS, tm, DIL_HD), lambda b, i: (b, 0, i, 0)) for _ in dil_types],
        out_shape=[jax.ShapeDtypeStruct((bsz, seq, n), dt) for n, dt in gla_cols]
        + [jax.ShapeDtypeStruct((bsz, seq // GLA_CHUNK, GLA_QK), F32)]
        + [jax.ShapeDtypeStruct((bsz, seq, n), dt) for n, dt in gate_cols]
        + [jax.ShapeDtypeStruct((bsz, DIL_HEADS, seq, DIL_HD), dt) for dt in dil_types],
        scratch_shapes=[pltpu.VMEM((4 * DIL_HEADS, PERM_TILE, DIL_HD), F32),
                        pltpu.VMEM((d, len(W_DIL_NAMES) * DIL_WIDTH), BF16),
                        pltpu.VMEM((d, GLA_QK), BF16)],
        compiler_params=pltpu.CompilerParams(
            dimension_semantics=("arbitrary", "arbitrary"), vmem_limit_bytes=VMEM_LIMIT),
        name="in_proj",
    )(x, mod, g_pre, w_in_bf, w_gate_pad, b_gate, cos, sin)


def _head_norm_gate(o, g, z):
    r = o * lax.rsqrt(jnp.mean(o * o, axis=-1, keepdims=True) + EPS)
    return r * g * _silu(z.astype(F32))


def _split2(v):
    h1 = v.astype(BF16)
    h2 = (v - h1.astype(F32)).astype(BF16)
    return h1, h2


def _gla_out_kernel(qe_ref, ke_ref, kend_ref, dec_ref, va_ref, za_ref, g_ref, yb_ref, x_ref, mod_ref, wo_ref,
                    gpost_ref, o_ref, st_ref, ya_ref, wo_bf, *, chunks, layer):
    @pl.when((pl.program_id(0) == 0) & (pl.program_id(1) == 0))
    def _():
        wo_bf[...] = wo_ref[...].astype(BF16)

    @pl.when(pl.program_id(1) == 0)
    def _():
        st_ref[...] = jnp.zeros_like(st_ref)

    c_len = GLA_CHUNK
    grp = GLA_GROUP * c_len
    causal = (lax.broadcasted_iota(jnp.int32, (c_len, c_len), 0)
              >= lax.broadcasted_iota(jnp.int32, (c_len, c_len), 1))
    heads = range(GLA_HEADS)
    kcol = [slice(hd * GLA_DK, (hd + 1) * GLA_DK) for hd in heads]
    vcol = [slice(hd * GLA_DV, (hd + 1) * GLA_DV) for hd in heads]

    def prep(r0):
        rows = pl.ds(r0, grp)
        c0 = r0 // c_len
        decay = [jnp.broadcast_to(dec_ref[c0 + c:c0 + c + 1, :], (GLA_DV, GLA_QK)).T for c in range(GLA_GROUP)]
        v = [[va_ref[pl.ds(r0 + c * c_len, c_len), vcol[hd]] for hd in heads] for c in range(GLA_GROUP)]
        return dict(r0=r0, q_e=qe_ref[rows, :], k_e=ke_ref[rows, :], k_end=kend_ref[rows, :], decay=decay, v=v)

    crow = [slice(c * c_len, (c + 1) * c_len) for c in range(GLA_GROUP)]

    def intra(p):
        q_e, k_e, k_end, v = p["q_e"], p["k_e"], p["k_end"], p["v"]
        a = [[jnp.where(causal, _dot_nt(q_e[crow[c], kcol[hd]], k_e[crow[c], kcol[hd]]), 0.0).astype(BF16)
              for hd in heads] for c in range(GLA_GROUP)]
        p["inc"] = [[_dot_tn(k_end[crow[c], kcol[hd]], v[c][hd]) for hd in heads] for c in range(GLA_GROUP)]
        p["o"] = [[_dot(a[c][hd], v[c][hd]) for hd in heads] for c in range(GLA_GROUP)]

    def inter(p, st):
        for hd in heads:
            for c in range(GLA_GROUP):
                p["o"][c][hd] = p["o"][c][hd] + _dot(p["q_e"][crow[c], kcol[hd]], st[hd].astype(BF16))
                st[hd] = st[hd] * p["decay"][c][kcol[hd], :] + p["inc"][c][hd]

    def epilogue(p):
        for c in range(GLA_GROUP):
            for hd in heads:
                out_rows = pl.ds(p["r0"] + c * c_len, c_len)
                ya_ref[out_rows, vcol[hd]] = _head_norm_gate(
                    p["o"][c][hd], g_ref[layer:layer + 1, vcol[hd]], za_ref[out_rows, vcol[hd]]).astype(ya_ref.dtype)

    gated_gain = gpost_ref[layer:layer + 1, :] * mod_ref[:, 2 * D_MODEL:3 * D_MODEL]

    def project(p):
        rows = pl.ds(p["r0"], grp)
        y = _dot(ya_ref[rows, :], wo_bf[0:GLA_WIDTH, :]) + _dot(yb_ref[rows, :], wo_bf[GLA_WIDTH:, :])
        o_ref[rows, :] = x_ref[rows, :] + y * lax.rsqrt(jnp.mean(y * y, axis=-1, keepdims=True) + EPS) * gated_gain

    st = [st_ref[hd] for hd in heads]
    groups = [prep(g * grp) for g in range(chunks // GLA_GROUP)]
    intra(groups[0])
    for g, p in enumerate(groups):
        inter(p, st)
        if g + 1 < len(groups):
            intra(groups[g + 1])
        epilogue(p)
        project(p)
    for hd in heads:
        st_ref[hd] = st[hd]


def _gla_out_proj(qe, ke, kend, dec, va, za, g_gla, yb, x, mod, w_out, g_post, layer):
    bsz, seq, d = x.shape
    ts = ROW_TILE
    row = lambda b, i: (b, i, 0)
    const = lambda b, i: (0, 0)
    return pl.pallas_call(
        functools.partial(_gla_out_kernel, chunks=ts // GLA_CHUNK, layer=layer),
        grid=(bsz, seq // ts),
        in_specs=[
            pl.BlockSpec((None, ts, GLA_QK), row),
            pl.BlockSpec((None, ts, GLA_QK), row),
            pl.BlockSpec((None, ts, GLA_QK), row),
            pl.BlockSpec((None, ts // GLA_CHUNK, GLA_QK), row),
            pl.BlockSpec((None, ts, GLA_WIDTH), row),
            pl.BlockSpec((None, ts, GLA_WIDTH), row),
            pl.BlockSpec(g_gla.shape, const),
            pl.BlockSpec((None, ts, DIL_WIDTH), row),
            pl.BlockSpec((None, ts, d), row),
            pl.BlockSpec((None, None, 1, 3 * d), lambda b, i: (layer, b, 0, 0)),
            pl.BlockSpec((None,) + w_out.shape[1:], lambda b, i: (layer, 0, 0), pipeline_mode=pl.Buffered(1)),
            pl.BlockSpec(g_post.shape, const),
        ],
        out_specs=pl.BlockSpec((None, ts, d), row),
        out_shape=jax.ShapeDtypeStruct((bsz, seq, d), F32),
        scratch_shapes=[pltpu.VMEM((GLA_HEADS, GLA_DK, GLA_DV), F32), pltpu.VMEM((ts, GLA_WIDTH), BF16),
                        pltpu.VMEM(w_out.shape[1:], BF16)],
        compiler_params=pltpu.CompilerParams(
            dimension_semantics=("arbitrary", "arbitrary"), vmem_limit_bytes=VMEM_LIMIT),
        name="gla_out_proj",
    )(qe, ke, kend, dec, va, za, g_gla, yb, x, mod, w_out, g_post)


def _dil_block_chunks(pattern, idx, seq):
    window, dil = DIL_PATTERNS[pattern]
    nb = seq // window
    lb = DIL_LB
    n_r = PERM_TILE // PERM_D
    if dil == PERM_D:
        return [((idx % nb) * PERM_TILE + (idx // nb) * n_r, 1)]
    if dil == 1:
        tile, part = idx // (PERM_TILE // lb), idx % (PERM_TILE // lb)
        rows = lb // PERM_D
        return [(tile * PERM_TILE + r * n_r + part * rows, 1) for r in range(PERM_D)]
    sub = dil // PERM_D
    res, n = idx // nb, idx % nb
    r4, c = res % PERM_D, res // PERM_D
    tiles = window // PERM_TILE
    return [(n * window + t * PERM_TILE + r4 * n_r + c, sub) for t in range(tiles)]


def _dil_kernel(q_ref, k_ref, v_ref, z_ref, g_ref, y_ref, qd, kd, vd, ynat,
                o1, o2, o3, m1, m2, m3, d1, d2, d3, *, seq, layer):
    lb = DIL_LB
    n_blocks = seq // lb
    qi = lax.broadcasted_iota(jnp.int32, (lb, 2 * lb), 0)
    ki = lax.broadcasted_iota(jnp.int32, (lb, 2 * lb), 1)

    def biases(pos_in_block):
        dist = pos_in_block(qi) + lb - (pos_in_block(ki & (lb - 1)) + (ki & lb))
        band = (dist >= 0) & (dist <= lb)
        b_any = jnp.where(band, 0.0, NEG_INF)
        b_first = jnp.where(band & (ki >= lb), 0.0, NEG_INF)
        return b_any, b_first, b_first[:, lb:]

    step_order = biases(lambda a: a)
    rows_p1 = lb // PERM_D
    p1_order = biases(lambda a: PERM_D * (a % rows_p1) + a // rows_p1)

    kd[0] = jnp.zeros((DIL_HD, lb), BF16)
    vd[0:lb, :] = jnp.zeros((lb, DIL_HD), BF16)

    n_r = PERM_TILE // PERM_D

    def combine(t0):
        rows = pl.ds(t0, PERM_TILE)
        a1, a2, a3 = m1[rows, :], m2[rows, :], m3[rows, :]
        m = jnp.maximum(jnp.maximum(a1, a2), a3)
        e1, e2, e3 = jnp.exp2(a1 - m), jnp.exp2(a2 - m), jnp.exp2(a3 - m)
        den = e1 * d1[rows, :] + e2 * d2[rows, :] + e3 * d3[rows, :]
        o = (e1 * o1[rows, :] + e2 * o2[rows, :] + e3 * o3[rows, :]) / den
        y = _head_norm_gate(o, g_ref[layer:layer + 1, :], z_ref[rows, :])
        for r in range(PERM_D):
            ynat[pl.ds(t0 + r, n_r, stride=PERM_D), :] = y[r * n_r:(r + 1) * n_r, :]
        y_ref[rows, :] = ynat[rows, :].astype(y_ref.dtype)

    order = sorted(range(len(DIL_PATTERNS)), key=lambda i: -DIL_PATTERNS[i][1])
    assert DIL_PATTERNS[order[-1]][1] == 1
    for pat in order:
        window, dil = DIL_PATTERNS[pat]
        o_scr, m_scr, d_scr = (o1, o2, o3)[pat], (m1, m2, m3)[pat], (d1, d2, d3)[pat]
        nb = seq // window
        bias_any, bias_first, bias_cur = p1_order if dil == 1 else step_order

        def load_block(ref, idx, pat=pat):
            chunks = _dil_block_chunks(pat, idx, seq)
            rows = lb // len(chunks)
            parts = [ref[pl.ds(s0, rows) if st == 1 else pl.ds(s0, rows, stride=st), :] for s0, st in chunks]
            return parts[0] if len(parts) == 1 else jnp.concatenate(parts, axis=0)

        def store_block(ref, idx, val, pat=pat):
            chunks = _dil_block_chunks(pat, idx, seq)
            rows = lb // len(chunks)
            for i, (s0, st) in enumerate(chunks):
                dst = pl.ds(s0, rows) if st == 1 else pl.ds(s0, rows, stride=st)
                ref[dst, :] = val[i * rows:(i + 1) * rows, :]

        for idx in range(n_blocks):
            dst = pl.ds(lb + idx * lb, lb)
            qd[dst, :] = load_block(q_ref, idx).astype(BF16)
            kd[idx + 1] = load_block(k_ref, idx).T.astype(BF16)
            vd[dst, :] = load_block(v_ref, idx).astype(BF16)

        def scores(idx, nb=nb, bias_any=bias_any, bias_cur=bias_cur):
            q = qd[pl.ds(lb + idx * lb, lb), :]
            if idx % nb == 0:
                return _dot(q, kd[idx + 1]) + bias_cur, pl.ds(lb + idx * lb, lb)
            s = _dot(q, jnp.concatenate([kd[idx], kd[idx + 1]], axis=1)) + bias_any
            return s, pl.ds(idx * lb, 2 * lb)

        def softmax(s):
            m = jnp.max(s, axis=-1, keepdims=True)
            return jnp.exp2((s - m).astype(BF16)), m

        def values(idx, p, m, keys, store_block=store_block, o_scr=o_scr, m_scr=m_scr, d_scr=d_scr):
            v_ones = jnp.concatenate([vd[keys, :], jnp.ones((p.shape[1], DIL_HD), BF16)], axis=1)
            acc = _dot(p, v_ones)
            store_block(o_scr, idx, acc[:, :DIL_HD])
            store_block(d_scr, idx, acc[:, DIL_HD:])
            store_block(m_scr, idx, jnp.broadcast_to(m, (lb, DIL_HD)))

        groups = [range(g * DIL_UNROLL, (g + 1) * DIL_UNROLL) for g in range(n_blocks // DIL_UNROLL)]
        pending = [scores(idx) for idx in groups[0]]
        for g, group in enumerate(groups):
            upcoming = [scores(idx) for idx in groups[g + 1]] if g + 1 < len(groups) else []
            probs = [softmax(s) for s, _ in pending]
            for idx, (p, m), (_, keys) in zip(group, probs, pending):
                values(idx, p, m, keys)
            pending = upcoming
            if dil == 1:
                for t0 in range(group[0] * lb, (group[-1] + 1) * lb, PERM_TILE):
                    combine(t0)


def _dilated(qb, kb, vb, zb, g_dil, layer):
    bsz, _, seq, _ = qb.shape
    blk = pl.BlockSpec((None, None, seq, DIL_HD), lambda b, h: (b, h, 0, 0))
    return pl.pallas_call(
        functools.partial(_dil_kernel, seq=seq, layer=layer),
        grid=(bsz, DIL_HEADS),
        in_specs=[blk, blk, blk, blk, pl.BlockSpec((g_dil.shape[0], DIL_HD), lambda b, h: (0, h))],
        out_specs=pl.BlockSpec((None, seq, DIL_HD), lambda b, h: (b, 0, h)),
        out_shape=jax.ShapeDtypeStruct((bsz, seq, DIL_WIDTH), BF16),
        scratch_shapes=[pltpu.VMEM((seq + DIL_LB, DIL_HD), BF16),
                        pltpu.VMEM((seq // DIL_LB + 1, DIL_HD, DIL_LB), BF16),
                        pltpu.VMEM((seq + DIL_LB, DIL_HD), BF16)]
        + [pltpu.VMEM((seq, DIL_HD), F32) for _ in range(10)],
        compiler_params=pltpu.CompilerParams(
            dimension_semantics=("arbitrary", "arbitrary"), vmem_limit_bytes=VMEM_LIMIT),
        name="dilated",
    )(qb, kb, vb, zb, g_dil)


def _rope_tables(seq):
    inv_freq = ROPE_THETA ** (-np.arange(0, DIL_HD, 2, dtype=np.float64) / DIL_HD)
    ang = np.arange(seq, dtype=np.float64)[:, None] * inv_freq[None, :]
    cos, sin = np.cos(ang), np.sin(ang)
    return (jnp.asarray(np.concatenate([cos, cos], axis=-1), F32),
            jnp.asarray(np.concatenate([-sin, sin], axis=-1), F32))


def kernel(x, c, w_ada, b_ada, g_pre, w_in, w_gate_up, b_gate_up, g_gla, g_dil, w_out, g_post):
    bsz, seq, d = x.shape
    depth = w_ada.shape[0]
    mod = _modulation(c, w_ada, b_ada).reshape(depth, bsz, 1, 3 * d)
    cos, sin = _rope_tables(seq)
    w_in_bf = w_in.astype(BF16)
    w_gate_pad = jnp.pad(w_gate_up.astype(BF16), ((0, 0), (0, LR_PAD - GLA_LOWRANK), (0, 0)))
    for l in range(depth):
        qe, ke, kend, dec, va, za, qb, kb, vb, zb = _in_proj(
            x, mod, g_pre, w_in_bf, w_gate_pad, b_gate_up, l, cos, sin)
        yb = _dilated(qb, kb, vb, zb, g_dil, l)
        x = _gla_out_proj(qe, ke, kend, dec, va, za, g_gla, yb, x, mod, w_out, g_post, l)
    return x
```

```python
import functools
import math

import jax
import jax.numpy as jnp
import numpy as np
from jax import lax
from jax.experimental import pallas as pl
from jax.experimental.pallas import tpu as pltpu

F32 = jnp.float32
BF16 = jnp.bfloat16

D_MODEL = 1024
GLA_HEADS = 4
GLA_DK = 64
GLA_DV = 128
GLA_QK = GLA_HEADS * GLA_DK
GLA_WIDTH = GLA_HEADS * GLA_DV
GLA_LOWRANK = 16
GLA_TAU = 16.0
GLA_CHUNK = 64
GLA_GROUP = 4
DIL_HEADS = 4
DIL_HD = 128
DIL_WIDTH = DIL_HEADS * DIL_HD
DIL_PATTERNS = ((128, 1), (512, 4), (2048, 16))
DIL_LB = 128
PERM_TILE, PERM_D = DIL_PATTERNS[1]
ROPE_THETA = 10000.0
EPS = 1e-6
LANES = 128
LR_PAD = LANES
_GLA_SIZES = (("qa", GLA_QK), ("ka", GLA_QK), ("va", GLA_WIDTH), ("za", GLA_WIDTH), ("lr", LR_PAD))
W_GLA_COLS = {}
for _name, _size in _GLA_SIZES:
    _lo = sum(n for _, n in _GLA_SIZES[:len(W_GLA_COLS)])
    W_GLA_COLS[_name] = (_lo, _lo + _size)
W_DIL_START = W_GLA_COLS["lr"][0] + GLA_LOWRANK
W_DIL_NAMES = ("qb", "kb", "vb", "zb")

VMEM_LIMIT = 56 * 1024 * 1024
DIL_UNROLL = 16
ROW_TILE = 1024
NEG_INF = float("-inf")
LOG2E = math.log2(math.e)


def _silu(v):
    return v * jax.nn.sigmoid(v)


def _dot(a, b):
    return jnp.dot(a, b, preferred_element_type=F32)


def _dot_nt(a, b):
    return lax.dot_general(a, b, (((1,), (1,)), ((), ())), preferred_element_type=F32)


def _dot_tn(a, b):
    return lax.dot_general(a, b, (((0,), (0,)), ((), ())), preferred_element_type=F32)


def _mod_kernel(c_ref, w_ref, b_ref, o_ref):
    sc = _silu(c_ref[...]).astype(BF16)
    o_ref[...] = _dot(sc, w_ref[...].astype(BF16)) + b_ref[pl.ds(pl.program_id(0), 1), :]


def _modulation(c, w_ada, b_ada):
    depth, d, e = w_ada.shape
    bsz = c.shape[0]
    return pl.pallas_call(
        _mod_kernel,
        grid=(depth,),
        in_specs=[
            pl.BlockSpec((bsz, d), lambda l: (0, 0)),
            pl.BlockSpec((None, d, e), lambda l: (l, 0, 0)),
            pl.BlockSpec((depth, e), lambda l: (0, 0)),
        ],
        out_specs=pl.BlockSpec((None, bsz, e), lambda l: (l, 0, 0)),
        out_shape=jax.ShapeDtypeStruct((depth, bsz, e), F32),
        compiler_params=pltpu.CompilerParams(
            dimension_semantics=("arbitrary",), vmem_limit_bytes=VMEM_LIMIT),
        name="adaln_mod",
    )(c, w_ada, b_ada)


def _rope(v, cos, sin_signed):
    return v * cos + pltpu.roll(v, DIL_HD // 2, axis=1) * sin_signed


def _gla_decayed(q, k, la, tril_bd):
    c_len = GLA_CHUNK
    h1, h2 = _split2(la)
    b = _dot(tril_bd, h1) + _dot(tril_bd, h2)
    decay = [jnp.exp2(b[(c + 1) * c_len - 1:(c + 1) * c_len, :]) for c in range(GLA_GROUP)]
    decay_rows = jnp.concatenate([jnp.broadcast_to(d, (c_len, GLA_QK)) for d in decay], axis=0)
    k_e32 = k * jnp.exp2(-b)
    q_e = (q * jnp.exp2(b)).astype(BF16)
    return q_e, k_e32.astype(BF16), (k_e32 * decay_rows).astype(BF16), jnp.concatenate(decay, axis=0)


def _chunk_tril():
    grp = GLA_GROUP * GLA_CHUNK
    ri = lax.broadcasted_iota(jnp.int32, (grp, grp), 0)
    ci = lax.broadcasted_iota(jnp.int32, (grp, grp), 1)
    return ((ri >= ci) & ((ri & -GLA_CHUNK) == (ci & -GLA_CHUNK))).astype(BF16)


def _in_proj_kernel(x_ref, mod_ref, gpre_ref, w_ref, wg_ref, bg_ref, cos_ref, sin_ref,
                    qe_ref, ke_ref, kend_ref, dec_ref, va_ref, za_ref, qb_ref, kb_ref, vb_ref, zb_ref,
                    perm_ref, wz_ref, *, layer):
    @pl.when((pl.program_id(0) == 0) & (pl.program_id(1) == 0))
    def _():
        lo, hi = W_GLA_COLS["lr"]
        wz_ref[...] = _dot_tn(w_ref[lo:hi, :], wg_ref[...]).astype(BF16)

    shift = mod_ref[:, 0:D_MODEL]
    gain = gpre_ref[layer:layer + 1, :] * (1.0 + mod_ref[:, D_MODEL:2 * D_MODEL])
    tril_bd = _chunk_tril()
    n_res = PERM_TILE // PERM_D

    def store_residue_order(out_ref, hd, slab, t0, val):
        perm_ref[slab] = val
        for r in range(PERM_D):
            dst = slice(t0 + r * n_res, t0 + (r + 1) * n_res)
            out_ref[hd, dst, :] = perm_ref[slab, pl.ds(r, n_res, stride=PERM_D), :].astype(out_ref.dtype)

    for t0 in range(0, x_ref.shape[0], PERM_TILE):
        rows = slice(t0, t0 + PERM_TILE)
        x = x_ref[rows, :]
        h = (x * lax.rsqrt(jnp.mean(x * x, axis=-1, keepdims=True) + EPS) * gain + shift).astype(BF16)

        def proj(name):
            if name in W_DIL_NAMES:
                lo = W_DIL_START + W_DIL_NAMES.index(name) * DIL_WIDTH
                return _dot_nt(h, w_ref[lo:lo + DIL_WIDTH, :])
            lo, hi = W_GLA_COLS[name]
            return _dot_nt(h, w_ref[lo:hi, :])

        z = _dot(h, wz_ref[...]) + bg_ref[layer:layer + 1, :]
        la = (jnp.minimum(z, 0.0) - jnp.log1p(jnp.exp(-jnp.abs(z)))) * (LOG2E / GLA_TAU)
        va_ref[rows, :] = proj("va").astype(va_ref.dtype)
        za_ref[rows, :] = proj("za").astype(za_ref.dtype)
        q, k, v, zg = proj("qb"), proj("kb"), proj("vb"), proj("zb")
        q_a = proj("qa") * (GLA_DK ** -0.5)
        k_a = proj("ka")

        grp = GLA_GROUP * GLA_CHUNK
        for g0 in range(0, PERM_TILE, grp):
            gr = slice(g0, g0 + grp)
            q_e, k_e, k_end, decay = _gla_decayed(q_a[gr, :], k_a[gr, :], la[gr, :], tril_bd)
            out = slice(t0 + g0, t0 + g0 + grp)
            qe_ref[out, :] = q_e
            ke_ref[out, :] = k_e
            kend_ref[out, :] = k_end
            c0 = (t0 + g0) // GLA_CHUNK
            dec_ref[c0:c0 + GLA_GROUP, :] = decay

        cos = cos_ref[rows, :]
        sin = sin_ref[rows, :]
        for hd in range(DIL_HEADS):
            cols = slice(hd * DIL_HD, (hd + 1) * DIL_HD)
            store_residue_order(qb_ref, hd, 4 * hd, t0, _rope(q[:, cols], cos, sin) * (DIL_HD ** -0.5 * LOG2E))
            store_residue_order(kb_ref, hd, 4 * hd + 1, t0, _rope(k[:, cols], cos, sin))
            store_residue_order(vb_ref, hd, 4 * hd + 2, t0, v[:, cols])
            store_residue_order(zb_ref, hd, 4 * hd + 3, t0, zg[:, cols])


def _in_proj(x, mod, g_pre, w_in_t, w_gate_pad, b_gate, layer, cos, sin):
    bsz, seq, d = x.shape
    tm = ROW_TILE
    row = lambda b, i: (b, i, 0)
    const = lambda b, i: (0, 0)

    def of_layer(a):
        return pl.BlockSpec((None,) + a.shape[1:], lambda b, i: (layer,) + (0,) * (a.ndim - 1),
                            pipeline_mode=pl.Buffered(1))

    gla_cols = ((GLA_QK, BF16), (GLA_QK, BF16), (GLA_QK, BF16))
    gate_cols = ((GLA_WIDTH, BF16), (GLA_WIDTH, BF16))
    dil_types = (F32, F32, F32, BF16)
    return pl.pallas_call(
        functools.partial(_in_proj_kernel, layer=layer),
        grid=(bsz, seq // tm),
        in_specs=[
            pl.BlockSpec((None, tm, d), row),
            pl.BlockSpec((None, None, 1, 3 * d), lambda b, i: (layer, b, 0, 0)),
            pl.BlockSpec(g_pre.shape, const), of_layer(w_in_t), of_layer(w_gate_pad),
            pl.BlockSpec(b_gate.shape, const),
            pl.BlockSpec((tm, DIL_HD), lambda b, i: (i, 0)),
            pl.BlockSpec((tm, DIL_HD), lambda b, i: (i, 0)),
        ],
        out_specs=[pl.BlockSpec((None, tm, n), row) for n, _ in gla_cols]
        + [pl.BlockSpec((None, tm // GLA_CHUNK, GLA_QK), row)]
        + [pl.BlockSpec((None, tm, n), row) for n, _ in gate_cols]
        + [pl.BlockSpec((None, DIL_HEADS, tm, DIL_HD), lambda b, i: (b, 0, i, 0)) for _ in dil_types],
        out_shape=[jax.ShapeDtypeStruct((bsz, seq, n), dt) for n, dt in gla_cols]
        + [jax.ShapeDtypeStruct((bsz, seq // GLA_CHUNK, GLA_QK), F32)]
        + [jax.ShapeDtypeStruct((bsz, seq, n), dt) for n, dt in gate_cols]
        + [jax.ShapeDtypeStruct((bsz, DIL_HEADS, seq, DIL_HD), dt) for dt in dil_types],
        scratch_shapes=[pltpu.VMEM((4 * DIL_HEADS, PERM_TILE, DIL_HD), F32),
                        pltpu.VMEM((d, GLA_QK), BF16)],
        compiler_params=pltpu.CompilerParams(
            dimension_semantics=("arbitrary", "arbitrary"), vmem_limit_bytes=VMEM_LIMIT),
        name="in_proj",
    )(x, mod, g_pre, w_in_t, w_gate_pad, b_gate, cos, sin)


def _head_norm_gate(o, g, z):
    r = o * lax.rsqrt(jnp.mean(o * o, axis=-1, keepdims=True) + EPS)
    return r * g * _silu(z.astype(F32))


def _split2(v):
    h1 = v.astype(BF16)
    h2 = (v - h1.astype(F32)).astype(BF16)
    return h1, h2


def _gla_out_kernel(qe_ref, ke_ref, kend_ref, dec_ref, va_ref, za_ref, g_ref, yb_ref, x_ref, mod_ref, wo_ref,
                    gpost_ref, o_ref, st_ref, ya_ref, wo_bf, *, chunks, layer):
    @pl.when((pl.program_id(0) == 0) & (pl.program_id(1) == 0))
    def _():
        wo_bf[...] = wo_ref[...].astype(BF16)

    @pl.when(pl.program_id(1) == 0)
    def _():
        st_ref[...] = jnp.zeros_like(st_ref)

    c_len = GLA_CHUNK
    grp = GLA_GROUP * c_len
    causal = (lax.broadcasted_iota(jnp.int32, (c_len, c_len), 0)
              >= lax.broadcasted_iota(jnp.int32, (c_len, c_len), 1))
    heads = range(GLA_HEADS)
    kcol = [slice(hd * GLA_DK, (hd + 1) * GLA_DK) for hd in heads]
    vcol = [slice(hd * GLA_DV, (hd + 1) * GLA_DV) for hd in heads]

    def prep(r0):
        rows = pl.ds(r0, grp)
        c0 = r0 // c_len
        decay = [jnp.broadcast_to(dec_ref[c0 + c:c0 + c + 1, :], (GLA_DV, GLA_QK)).T for c in range(GLA_GROUP)]
        v = [[va_ref[pl.ds(r0 + c * c_len, c_len), vcol[hd]] for hd in heads] for c in range(GLA_GROUP)]
        return dict(r0=r0, q_e=qe_ref[rows, :], k_e=ke_ref[rows, :], k_end=kend_ref[rows, :], decay=decay, v=v)

    crow = [slice(c * c_len, (c + 1) * c_len) for c in range(GLA_GROUP)]

    def intra(p):
        q_e, k_e, k_end, v = p["q_e"], p["k_e"], p["k_end"], p["v"]
        a = [[jnp.where(causal, _dot_nt(q_e[crow[c], kcol[hd]], k_e[crow[c], kcol[hd]]), 0.0).astype(BF16)
              for hd in heads] for c in range(GLA_GROUP)]
        p["inc"] = [[_dot_tn(k_end[crow[c], kcol[hd]], v[c][hd]) for hd in heads] for c in range(GLA_GROUP)]
        p["o"] = [[_dot(a[c][hd], v[c][hd]) for hd in heads] for c in range(GLA_GROUP)]

    def inter(p, st):
        for hd in heads:
            for c in range(GLA_GROUP):
                p["o"][c][hd] = p["o"][c][hd] + _dot(p["q_e"][crow[c], kcol[hd]], st[hd].astype(BF16))
                st[hd] = st[hd] * p["decay"][c][kcol[hd], :] + p["inc"][c][hd]

    def epilogue(p):
        for c in range(GLA_GROUP):
            for hd in heads:
                out_rows = pl.ds(p["r0"] + c * c_len, c_len)
                ya_ref[out_rows, vcol[hd]] = _head_norm_gate(
                    p["o"][c][hd], g_ref[layer:layer + 1, vcol[hd]], za_ref[out_rows, vcol[hd]]).astype(ya_ref.dtype)

    gated_gain = gpost_ref[layer:layer + 1, :] * mod_ref[:, 2 * D_MODEL:3 * D_MODEL]

    def project(p):
        rows = pl.ds(p["r0"], grp)
        y = _dot(ya_ref[rows, :], wo_bf[0:GLA_WIDTH, :]) + _dot(yb_ref[rows, :], wo_bf[GLA_WIDTH:, :])
        o_ref[rows, :] = x_ref[rows, :] + y * lax.rsqrt(jnp.mean(y * y, axis=-1, keepdims=True) + EPS) * gated_gain

    st = [st_ref[hd] for hd in heads]
    groups = [prep(g * grp) for g in range(chunks // GLA_GROUP)]
    intra(groups[0])
    for g, p in enumerate(groups):
        inter(p, st)
        if g + 1 < len(groups):
            intra(groups[g + 1])
        epilogue(p)
        project(p)
    for hd in heads:
        st_ref[hd] = st[hd]


def _gla_out_proj(qe, ke, kend, dec, va, za, g_gla, yb, x, mod, w_out, g_post, layer):
    bsz, seq, d = x.shape
    ts = ROW_TILE
    row = lambda b, i: (b, i, 0)
    const = lambda b, i: (0, 0)
    return pl.pallas_call(
        functools.partial(_gla_out_kernel, chunks=ts // GLA_CHUNK, layer=layer),
        grid=(bsz, seq // ts),
        in_specs=[
            pl.BlockSpec((None, ts, GLA_QK), row),
            pl.BlockSpec((None, ts, GLA_QK), row),
            pl.BlockSpec((None, ts, GLA_QK), row),
            pl.BlockSpec((None, ts // GLA_CHUNK, GLA_QK), row),
            pl.BlockSpec((None, ts, GLA_WIDTH), row),
            pl.BlockSpec((None, ts, GLA_WIDTH), row),
            pl.BlockSpec(g_gla.shape, const),
            pl.BlockSpec((None, ts, DIL_WIDTH), row),
            pl.BlockSpec((None, ts, d), row),
            pl.BlockSpec((None, None, 1, 3 * d), lambda b, i: (layer, b, 0, 0)),
            pl.BlockSpec((None,) + w_out.shape[1:], lambda b, i: (layer, 0, 0), pipeline_mode=pl.Buffered(1)),
            pl.BlockSpec(g_post.shape, const),
        ],
        out_specs=pl.BlockSpec((None, ts, d), row),
        out_shape=jax.ShapeDtypeStruct((bsz, seq, d), F32),
        scratch_shapes=[pltpu.VMEM((GLA_HEADS, GLA_DK, GLA_DV), F32), pltpu.VMEM((ts, GLA_WIDTH), BF16),
                        pltpu.VMEM(w_out.shape[1:], BF16)],
        compiler_params=pltpu.CompilerParams(
            dimension_semantics=("arbitrary", "arbitrary"), vmem_limit_bytes=VMEM_LIMIT),
        name="gla_out_proj",
    )(qe, ke, kend, dec, va, za, g_gla, yb, x, mod, w_out, g_post)


def _dil_block_chunks(pattern, idx, seq):
    window, dil = DIL_PATTERNS[pattern]
    nb = seq // window
    lb = DIL_LB
    n_r = PERM_TILE // PERM_D
    if dil == PERM_D:
        return [((idx % nb) * PERM_TILE + (idx // nb) * n_r, 1)]
    if dil == 1:
        tile, part = idx // (PERM_TILE // lb), idx % (PERM_TILE // lb)
        rows = lb // PERM_D
        return [(tile * PERM_TILE + r * n_r + part * rows, 1) for r in range(PERM_D)]
    sub = dil // PERM_D
    res, n = idx // nb, idx % nb
    r4, c = res % PERM_D, res // PERM_D
    tiles = window // PERM_TILE
    return [(n * window + t * PERM_TILE + r4 * n_r + c, sub) for t in range(tiles)]


def _dil_kernel(q_ref, k_ref, v_ref, z_ref, g_ref, y_ref, qd, kd, vd, ynat,
                o1, o2, o3, m1, m2, m3, d1, d2, d3, *, seq, layer):
    lb = DIL_LB
    n_blocks = seq // lb
    qi = lax.broadcasted_iota(jnp.int32, (lb, 2 * lb), 0)
    ki = lax.broadcasted_iota(jnp.int32, (lb, 2 * lb), 1)

    def biases(pos_in_block):
        dist = pos_in_block(qi) + lb - (pos_in_block(ki & (lb - 1)) + (ki & lb))
        band = (dist >= 0) & (dist <= lb)
        b_any = jnp.where(band, 0.0, NEG_INF)
        b_first = jnp.where(band & (ki >= lb), 0.0, NEG_INF)
        return b_any, b_first, b_first[:, lb:]

    step_order = biases(lambda a: a)
    rows_p1 = lb // PERM_D
    p1_order = biases(lambda a: PERM_D * (a % rows_p1) + a // rows_p1)

    kd[0] = jnp.zeros((DIL_HD, lb), BF16)
    vd[0:lb, :] = jnp.zeros((lb, DIL_HD), BF16)

    n_r = PERM_TILE // PERM_D

    def combine(t0):
        rows = pl.ds(t0, PERM_TILE)
        a1, a2, a3 = m1[rows, :], m2[rows, :], m3[rows, :]
        m = jnp.maximum(jnp.maximum(a1, a2), a3)
        e1, e2, e3 = jnp.exp2(a1 - m), jnp.exp2(a2 - m), jnp.exp2(a3 - m)
        den = e1 * d1[rows, :] + e2 * d2[rows, :] + e3 * d3[rows, :]
        o = (e1 * o1[rows, :] + e2 * o2[rows, :] + e3 * o3[rows, :]) / den
        y = _head_norm_gate(o, g_ref[layer:layer + 1, :], z_ref[rows, :])
        for r in range(PERM_D):
            ynat[pl.ds(t0 + r, n_r, stride=PERM_D), :] = y[r * n_r:(r + 1) * n_r, :]
        y_ref[rows, :] = ynat[rows, :].astype(y_ref.dtype)

    order = sorted(range(len(DIL_PATTERNS)), key=lambda i: -DIL_PATTERNS[i][1])
    assert DIL_PATTERNS[order[-1]][1] == 1
    for pat in order:
        window, dil = DIL_PATTERNS[pat]
        o_scr, m_scr, d_scr = (o1, o2, o3)[pat], (m1, m2, m3)[pat], (d1, d2, d3)[pat]
        nb = seq // window
        bias_any, bias_first, bias_cur = p1_order if dil == 1 else step_order

        def load_block(ref, idx, pat=pat):
            chunks = _dil_block_chunks(pat, idx, seq)
            rows = lb // len(chunks)
            parts = [ref[pl.ds(s0, rows) if st == 1 else pl.ds(s0, rows, stride=st), :] for s0, st in chunks]
            return parts[0] if len(parts) == 1 else jnp.concatenate(parts, axis=0)

        def store_block(ref, idx, val, pat=pat):
            chunks = _dil_block_chunks(pat, idx, seq)
            rows = lb // len(chunks)
            for i, (s0, st) in enumerate(chunks):
                dst = pl.ds(s0, rows) if st == 1 else pl.ds(s0, rows, stride=st)
                ref[dst, :] = val[i * rows:(i + 1) * rows, :]

        for idx in range(n_blocks):
            dst = pl.ds(lb + idx * lb, lb)
            qd[dst, :] = load_block(q_ref, idx).astype(BF16)
            kd[idx + 1] = load_block(k_ref, idx).T.astype(BF16)
            vd[dst, :] = load_block(v_ref, idx).astype(BF16)

        def scores(idx, nb=nb, bias_any=bias_any, bias_cur=bias_cur):
            q = qd[pl.ds(lb + idx * lb, lb), :]
            if idx % nb == 0:
                return _dot(q, kd[idx + 1]) + bias_cur, pl.ds(lb + idx * lb, lb)
            s = _dot(q, jnp.concatenate([kd[idx], kd[idx + 1]], axis=1)) + bias_any
            return s, pl.ds(idx * lb, 2 * lb)

        def softmax(s):
            m = jnp.max(s, axis=-1, keepdims=True)
            return jnp.exp2((s - m).astype(BF16)), m

        def values(idx, p, m, keys, store_block=store_block, o_scr=o_scr, m_scr=m_scr, d_scr=d_scr):
            v_ones = jnp.concatenate([vd[keys, :], jnp.ones((p.shape[1], DIL_HD), BF16)], axis=1)
            acc = _dot(p, v_ones)
            store_block(o_scr, idx, acc[:, :DIL_HD])
            store_block(d_scr, idx, acc[:, DIL_HD:])
            store_block(m_scr, idx, jnp.broadcast_to(m, (lb, DIL_HD)))

        groups = [range(g * DIL_UNROLL, (g + 1) * DIL_UNROLL) for g in range(n_blocks // DIL_UNROLL)]
        pending = [scores(idx) for idx in groups[0]]
        for g, group in enumerate(groups):
            upcoming = [scores(idx) for idx in groups[g + 1]] if g + 1 < len(groups) else []
            probs = [softmax(s) for s, _ in pending]
            for idx, (p, m), (_, keys) in zip(group, probs, pending):
                values(idx, p, m, keys)
            pending = upcoming
            if dil == 1:
                for t0 in range(group[0] * lb, (group[-1] + 1) * lb, PERM_TILE):
                    combine(t0)


def _dilated(qb, kb, vb, zb, g_dil, layer):
    bsz, _, seq, _ = qb.shape
    blk = pl.BlockSpec((None, None, seq, DIL_HD), lambda b, h: (b, h, 0, 0))
    return pl.pallas_call(
        functools.partial(_dil_kernel, seq=seq, layer=layer),
        grid=(bsz, DIL_HEADS),
        in_specs=[blk, blk, blk, blk, pl.BlockSpec((g_dil.shape[0], DIL_HD), lambda b, h: (0, h))],
        out_specs=pl.BlockSpec((None, seq, DIL_HD), lambda b, h: (b, 0, h)),
        out_shape=jax.ShapeDtypeStruct((bsz, seq, DIL_WIDTH), BF16),
        scratch_shapes=[pltpu.VMEM((seq + DIL_LB, DIL_HD), BF16),
                        pltpu.VMEM((seq // DIL_LB + 1, DIL_HD, DIL_LB), BF16),
                        pltpu.VMEM((seq + DIL_LB, DIL_HD), BF16)]
        + [pltpu.VMEM((seq, DIL_HD), F32) for _ in range(10)],
        compiler_params=pltpu.CompilerParams(
            dimension_semantics=("arbitrary", "arbitrary"), vmem_limit_bytes=VMEM_LIMIT),
        name="dilated",
    )(qb, kb, vb, zb, g_dil)


def _rope_tables(seq):
    inv_freq = ROPE_THETA ** (-np.arange(0, DIL_HD, 2, dtype=np.float64) / DIL_HD)
    ang = np.arange(seq, dtype=np.float64)[:, None] * inv_freq[None, :]
    cos, sin = np.cos(ang), np.sin(ang)
    return (jnp.asarray(np.concatenate([cos, cos], axis=-1), F32),
            jnp.asarray(np.concatenate([-sin, sin], axis=-1), F32))


def kernel(x, c, w_ada, b_ada, g_pre, w_in, w_gate_up, b_gate_up, g_gla, g_dil, w_out, g_post):
    bsz, seq, d = x.shape
    depth = w_ada.shape[0]
    mod = _modulation(c, w_ada, b_ada).reshape(depth, bsz, 1, 3 * d)
    cos, sin = _rope_tables(seq)
    w_in_t = jnp.swapaxes(w_in, 1, 2).astype(BF16)
    w_gate_pad = jnp.pad(w_gate_up.astype(BF16), ((0, 0), (0, LR_PAD - GLA_LOWRANK), (0, 0)))
    for l in range(depth):
        qe, ke, kend, dec, va, za, qb, kb, vb, zb = _in_proj(
            x, mod, g_pre, w_in_t, w_gate_pad, b_gate_up, l, cos, sin)
        yb = _dilated(qb, kb, vb, zb, g_dil, l)
        x = _gla_out_proj(qe, ke, kend, dec, va, za, g_gla, yb, x, mod, w_out, g_post, l)
    return x
```

```python
import functools
import math

import jax
import jax.numpy as jnp
import numpy as np
from jax import lax
from jax.experimental import pallas as pl
from jax.experimental.pallas import tpu as pltpu

F32 = jnp.float32
BF16 = jnp.bfloat16

D_MODEL = 1024
GLA_HEADS = 4
GLA_DK = 64
GLA_DV = 128
GLA_QK = GLA_HEADS * GLA_DK
GLA_WIDTH = GLA_HEADS * GLA_DV
GLA_LOWRANK = 16
GLA_TAU = 16.0
GLA_CHUNK = 64
GLA_GROUP = 4
GLA_STAGE_CHUNKS = 8
DIL_HEADS = 4
DIL_HD = 128
DIL_WIDTH = DIL_HEADS * DIL_HD
DIL_PATTERNS = ((128, 1), (512, 4), (2048, 16))
DIL_LB = 128
PERM_TILE, PERM_D = DIL_PATTERNS[1]
ROPE_THETA = 10000.0
EPS = 1e-6
LANES = 128
LR_PAD = LANES
_GLA_SIZES = (("qa", GLA_QK), ("ka", GLA_QK), ("va", GLA_WIDTH), ("za", GLA_WIDTH), ("lr", LR_PAD))
W_GLA_COLS = {}
for _name, _size in _GLA_SIZES:
    _lo = sum(n for _, n in _GLA_SIZES[:len(W_GLA_COLS)])
    W_GLA_COLS[_name] = (_lo, _lo + _size)
W_DIL_START = W_GLA_COLS["lr"][0] + GLA_LOWRANK
W_DIL_NAMES = ("qb", "kb", "vb", "zb")

VMEM_LIMIT = 56 * 1024 * 1024
DIL_UNROLL = 16
ROW_TILE = 1024
NEG_INF = float("-inf")
LOG2E = math.log2(math.e)


def _silu(v):
    return v * jax.nn.sigmoid(v)


def _dot(a, b):
    return jnp.dot(a, b, preferred_element_type=F32)


def _dot_nt(a, b):
    return lax.dot_general(a, b, (((1,), (1,)), ((), ())), preferred_element_type=F32)


def _dot_tn(a, b):
    return lax.dot_general(a, b, (((0,), (0,)), ((), ())), preferred_element_type=F32)


def _mod_kernel(c_ref, w_ref, b_ref, o_ref):
    sc = _silu(c_ref[...]).astype(BF16)
    o_ref[...] = _dot(sc, w_ref[...].astype(BF16)) + b_ref[pl.ds(pl.program_id(0), 1), :]


def _modulation(c, w_ada, b_ada):
    depth, d, e = w_ada.shape
    bsz = c.shape[0]
    return pl.pallas_call(
        _mod_kernel,
        grid=(depth,),
        in_specs=[
            pl.BlockSpec((bsz, d), lambda l: (0, 0)),
            pl.BlockSpec((None, d, e), lambda l: (l, 0, 0)),
            pl.BlockSpec((depth, e), lambda l: (0, 0)),
        ],
        out_specs=pl.BlockSpec((None, bsz, e), lambda l: (l, 0, 0)),
        out_shape=jax.ShapeDtypeStruct((depth, bsz, e), F32),
        compiler_params=pltpu.CompilerParams(
            dimension_semantics=("arbitrary",), vmem_limit_bytes=VMEM_LIMIT),
        name="adaln_mod",
    )(c, w_ada, b_ada)


def _rope(v, cos, sin_signed):
    return v * cos + pltpu.roll(v, DIL_HD // 2, axis=1) * sin_signed


def _gla_decayed(q, k, la, tril_bd):
    c_len = GLA_CHUNK
    h1, h2 = _split2(la)
    b = _dot(tril_bd, h1) + _dot(tril_bd, h2)
    decay = [jnp.exp2(b[(c + 1) * c_len - 1:(c + 1) * c_len, :]) for c in range(GLA_GROUP)]
    decay_rows = jnp.concatenate([jnp.broadcast_to(d, (c_len, GLA_QK)) for d in decay], axis=0)
    k_e32 = k * jnp.exp2(-b)
    q_e = (q * jnp.exp2(b)).astype(BF16)
    return q_e, k_e32.astype(BF16), (k_e32 * decay_rows).astype(BF16), jnp.concatenate(decay, axis=0)


def _chunk_tril():
    grp = GLA_GROUP * GLA_CHUNK
    ri = lax.broadcasted_iota(jnp.int32, (grp, grp), 0)
    ci = lax.broadcasted_iota(jnp.int32, (grp, grp), 1)
    return ((ri >= ci) & ((ri & -GLA_CHUNK) == (ci & -GLA_CHUNK))).astype(BF16)


def _in_proj_kernel(x_ref, mod_ref, gpre_ref, w_ref, wg_ref, bg_ref, cos_ref, sin_ref,
                    qe_ref, ke_ref, kend_ref, dec_ref, va_ref, za_ref, qb_ref, kb_ref, vb_ref, zb_ref,
                    perm_ref, wz_ref, *, layer):
    @pl.when((pl.program_id(0) == 0) & (pl.program_id(1) == 0))
    def _():
        lo, hi = W_GLA_COLS["lr"]
        wz_ref[...] = _dot_tn(w_ref[lo:hi, :], wg_ref[...]).astype(BF16)

    shift = mod_ref[:, 0:D_MODEL]
    gain = gpre_ref[layer:layer + 1, :] * (1.0 + mod_ref[:, D_MODEL:2 * D_MODEL])
    tril_bd = _chunk_tril()
    n_res = PERM_TILE // PERM_D

    def store_residue_order(out_ref, hd, slab, t0, val):
        perm_ref[slab] = val
        for r in range(PERM_D):
            dst = slice(t0 + r * n_res, t0 + (r + 1) * n_res)
            out_ref[hd, dst, :] = perm_ref[slab, pl.ds(r, n_res, stride=PERM_D), :].astype(out_ref.dtype)

    for t0 in range(0, x_ref.shape[0], PERM_TILE):
        rows = slice(t0, t0 + PERM_TILE)
        x = x_ref[rows, :]
        h = (x * lax.rsqrt(jnp.mean(x * x, axis=-1, keepdims=True) + EPS) * gain + shift).astype(BF16)

        def proj(name):
            if name in W_DIL_NAMES:
                lo = W_DIL_START + W_DIL_NAMES.index(name) * DIL_WIDTH
                return _dot_nt(h, w_ref[lo:lo + DIL_WIDTH, :])
            lo, hi = W_GLA_COLS[name]
            return _dot_nt(h, w_ref[lo:hi, :])

        z = _dot(h, wz_ref[...]) + bg_ref[layer:layer + 1, :]
        la = (jnp.minimum(z, 0.0) - jnp.log1p(jnp.exp(-jnp.abs(z)))) * (LOG2E / GLA_TAU)
        va_ref[rows, :] = proj("va").astype(va_ref.dtype)
        za_ref[rows, :] = proj("za").astype(za_ref.dtype)
        q, k, v, zg = proj("qb"), proj("kb"), proj("vb"), proj("zb")
        q_a = proj("qa") * (GLA_DK ** -0.5)
        k_a = proj("ka")

        grp = GLA_GROUP * GLA_CHUNK
        for g0 in range(0, PERM_TILE, grp):
            gr = slice(g0, g0 + grp)
            q_e, k_e, k_end, decay = _gla_decayed(q_a[gr, :], k_a[gr, :], la[gr, :], tril_bd)
            out = slice(t0 + g0, t0 + g0 + grp)
            qe_ref[out, :] = q_e
            ke_ref[out, :] = k_e
            kend_ref[out, :] = k_end
            c0 = (t0 + g0) // GLA_CHUNK
            dec_ref[c0:c0 + GLA_GROUP, :] = decay

        cos = cos_ref[rows, :]
        sin = sin_ref[rows, :]
        for hd in range(DIL_HEADS):
            cols = slice(hd * DIL_HD, (hd + 1) * DIL_HD)
            store_residue_order(qb_ref, hd, 4 * hd, t0, _rope(q[:, cols], cos, sin) * (DIL_HD ** -0.5 * LOG2E))
            store_residue_order(kb_ref, hd, 4 * hd + 1, t0, _rope(k[:, cols], cos, sin))
            store_residue_order(vb_ref, hd, 4 * hd + 2, t0, v[:, cols])
            store_residue_order(zb_ref, hd, 4 * hd + 3, t0, zg[:, cols])


def _in_proj(x, mod, g_pre, w_in_t, w_gate_pad, b_gate, layer, cos, sin):
    bsz, seq, d = x.shape
    tm = ROW_TILE
    row = lambda b, i: (b, i, 0)
    const = lambda b, i: (0, 0)

    def of_layer(a):
        return pl.BlockSpec((None,) + a.shape[1:], lambda b, i: (layer,) + (0,) * (a.ndim - 1),
                            pipeline_mode=pl.Buffered(1))

    gla_cols = ((GLA_QK, BF16), (GLA_QK, BF16), (GLA_QK, BF16))
    gate_cols = ((GLA_WIDTH, BF16), (GLA_WIDTH, BF16))
    dil_types = (F32, F32, F32, BF16)
    return pl.pallas_call(
        functools.partial(_in_proj_kernel, layer=layer),
        grid=(bsz, seq // tm),
        in_specs=[
            pl.BlockSpec((None, tm, d), row),
            pl.BlockSpec((None, None, 1, 3 * d), lambda b, i: (layer, b, 0, 0)),
            pl.BlockSpec(g_pre.shape, const), of_layer(w_in_t), of_layer(w_gate_pad),
            pl.BlockSpec(b_gate.shape, const),
            pl.BlockSpec((tm, DIL_HD), lambda b, i: (i, 0)),
            pl.BlockSpec((tm, DIL_HD), lambda b, i: (i, 0)),
        ],
        out_specs=[pl.BlockSpec((None, tm, n), row) for n, _ in gla_cols]
        + [pl.BlockSpec((None, tm // GLA_CHUNK, GLA_QK), row)]
        + [pl.BlockSpec((None, tm, n), row) for n, _ in gate_cols]
        + [pl.BlockSpec((None, DIL_HEADS, tm, DIL_HD), lambda b, i: (b, 0, i, 0)) for _ in dil_types],
        out_shape=[jax.ShapeDtypeStruct((bsz, seq, n), dt) for n, dt in gla_cols]
        + [jax.ShapeDtypeStruct((bsz, seq // GLA_CHUNK, GLA_QK), F32)]
        + [jax.ShapeDtypeStruct((bsz, seq, n), dt) for n, dt in gate_cols]
        + [jax.ShapeDtypeStruct((bsz, DIL_HEADS, seq, DIL_HD), dt) for dt in dil_types],
        scratch_shapes=[pltpu.VMEM((4 * DIL_HEADS, PERM_TILE, DIL_HD), F32),
                        pltpu.VMEM((d, GLA_QK), BF16)],
        compiler_params=pltpu.CompilerParams(
            dimension_semantics=("arbitrary", "arbitrary"), vmem_limit_bytes=VMEM_LIMIT),
        name="in_proj",
    )(x, mod, g_pre, w_in_t, w_gate_pad, b_gate, cos, sin)


def _head_norm_gate(o, g, z):
    r = o * lax.rsqrt(jnp.mean(o * o, axis=-1, keepdims=True) + EPS)
    return r * g * _silu(z.astype(F32))


def _split2(v):
    h1 = v.astype(BF16)
    h2 = (v - h1.astype(F32)).astype(BF16)
    return h1, h2


def _gla_out_kernel(qe_ref, ke_ref, kend_ref, dec_ref, va_ref, za_ref, g_ref, yb_ref, x_ref, mod_ref, wo_ref,
                    gpost_ref, o_ref, st_ref, ya_ref, wo_bf, *, chunks, layer):
    @pl.when((pl.program_id(0) == 0) & (pl.program_id(1) == 0))
    def _():
        wo_bf[...] = wo_ref[...].astype(BF16)

    @pl.when(pl.program_id(1) == 0)
    def _():
        st_ref[...] = jnp.zeros_like(st_ref)

    c_len = GLA_CHUNK
    n_c = GLA_STAGE_CHUNKS
    grp = n_c * c_len
    causal = (lax.broadcasted_iota(jnp.int32, (c_len, c_len), 0)
              >= lax.broadcasted_iota(jnp.int32, (c_len, c_len), 1))
    heads = range(GLA_HEADS)
    kcol = [slice(hd * GLA_DK, (hd + 1) * GLA_DK) for hd in heads]
    vcol = [slice(hd * GLA_DV, (hd + 1) * GLA_DV) for hd in heads]

    def prep(r0):
        rows = pl.ds(r0, grp)
        c0 = r0 // c_len
        decay = [jnp.broadcast_to(dec_ref[c0 + c:c0 + c + 1, :], (GLA_DV, GLA_QK)).T for c in range(n_c)]
        v = [[va_ref[pl.ds(r0 + c * c_len, c_len), vcol[hd]] for hd in heads] for c in range(n_c)]
        return dict(r0=r0, q_e=qe_ref[rows, :], k_e=ke_ref[rows, :], k_end=kend_ref[rows, :], decay=decay, v=v)

    crow = [slice(c * c_len, (c + 1) * c_len) for c in range(n_c)]

    def intra(p):
        q_e, k_e, k_end, v = p["q_e"], p["k_e"], p["k_end"], p["v"]
        a = [[jnp.where(causal, _dot_nt(q_e[crow[c], kcol[hd]], k_e[crow[c], kcol[hd]]), 0.0).astype(BF16)
              for hd in heads] for c in range(n_c)]
        p["inc"] = [[_dot_tn(k_end[crow[c], kcol[hd]], v[c][hd]) for hd in heads] for c in range(n_c)]
        p["o"] = [[_dot(a[c][hd], v[c][hd]) for hd in heads] for c in range(n_c)]

    def inter(p, st):
        for hd in heads:
            for c in range(n_c):
                p["o"][c][hd] = p["o"][c][hd] + _dot(p["q_e"][crow[c], kcol[hd]], st[hd].astype(BF16))
                st[hd] = st[hd] * p["decay"][c][kcol[hd], :] + p["inc"][c][hd]

    def epilogue(p):
        for c in range(n_c):
            for hd in heads:
                out_rows = pl.ds(p["r0"] + c * c_len, c_len)
                ya_ref[out_rows, vcol[hd]] = _head_norm_gate(
                    p["o"][c][hd], g_ref[layer:layer + 1, vcol[hd]], za_ref[out_rows, vcol[hd]]).astype(ya_ref.dtype)

    gated_gain = gpost_ref[layer:layer + 1, :] * mod_ref[:, 2 * D_MODEL:3 * D_MODEL]

    def project(p):
        rows = pl.ds(p["r0"], grp)
        y = _dot(ya_ref[rows, :], wo_bf[0:GLA_WIDTH, :]) + _dot(yb_ref[rows, :], wo_bf[GLA_WIDTH:, :])
        o_ref[rows, :] = x_ref[rows, :] + y * lax.rsqrt(jnp.mean(y * y, axis=-1, keepdims=True) + EPS) * gated_gain

    st = [st_ref[hd] for hd in heads]
    groups = [prep(g * grp) for g in range(chunks // n_c)]
    intra(groups[0])
    for g, p in enumerate(groups):
        inter(p, st)
        if g + 1 < len(groups):
            intra(groups[g + 1])
        epilogue(p)
        project(p)
    for hd in heads:
        st_ref[hd] = st[hd]


def _gla_out_proj(qe, ke, kend, dec, va, za, g_gla, yb, x, mod, w_out, g_post, layer):
    bsz, seq, d = x.shape
    ts = ROW_TILE
    row = lambda b, i: (b, i, 0)
    const = lambda b, i: (0, 0)
    return pl.pallas_call(
        functools.partial(_gla_out_kernel, chunks=ts // GLA_CHUNK, layer=layer),
        grid=(bsz, seq // ts),
        in_specs=[
            pl.BlockSpec((None, ts, GLA_QK), row),
            pl.BlockSpec((None, ts, GLA_QK), row),
            pl.BlockSpec((None, ts, GLA_QK), row),
            pl.BlockSpec((None, ts // GLA_CHUNK, GLA_QK), row),
            pl.BlockSpec((None, ts, GLA_WIDTH), row),
            pl.BlockSpec((None, ts, GLA_WIDTH), row),
            pl.BlockSpec(g_gla.shape, const),
            pl.BlockSpec((None, ts, DIL_WIDTH), row),
            pl.BlockSpec((None, ts, d), row),
            pl.BlockSpec((None, None, 1, 3 * d), lambda b, i: (layer, b, 0, 0)),
            pl.BlockSpec((None,) + w_out.shape[1:], lambda b, i: (layer, 0, 0), pipeline_mode=pl.Buffered(1)),
            pl.BlockSpec(g_post.shape, const),
        ],
        out_specs=pl.BlockSpec((None, ts, d), row),
        out_shape=jax.ShapeDtypeStruct((bsz, seq, d), F32),
        scratch_shapes=[pltpu.VMEM((GLA_HEADS, GLA_DK, GLA_DV), F32), pltpu.VMEM((ts, GLA_WIDTH), BF16),
                        pltpu.VMEM(w_out.shape[1:], BF16)],
        compiler_params=pltpu.CompilerParams(
            dimension_semantics=("arbitrary", "arbitrary"), vmem_limit_bytes=VMEM_LIMIT),
        name="gla_out_proj",
    )(qe, ke, kend, dec, va, za, g_gla, yb, x, mod, w_out, g_post)


def _dil_block_chunks(pattern, idx, seq):
    window, dil = DIL_PATTERNS[pattern]
    nb = seq // window
    lb = DIL_LB
    n_r = PERM_TILE // PERM_D
    if dil == PERM_D:
        return [((idx % nb) * PERM_TILE + (idx // nb) * n_r, 1)]
    if dil == 1:
        tile, part = idx // (PERM_TILE // lb), idx % (PERM_TILE // lb)
        rows = lb // PERM_D
        return [(tile * PERM_TILE + r * n_r + part * rows, 1) for r in range(PERM_D)]
    sub = dil // PERM_D
    res, n = idx // nb, idx % nb
    r4, c = res % PERM_D, res // PERM_D
    tiles = window // PERM_TILE
    return [(n * window + t * PERM_TILE + r4 * n_r + c, sub) for t in range(tiles)]


def _dil_kernel(q_ref, k_ref, v_ref, z_ref, g_ref, y_ref, qd, kd, vd, ynat,
                o1, o2, o3, m1, m2, m3, d1, d2, d3, *, seq, layer):
    lb = DIL_LB
    n_blocks = seq // lb
    qi = lax.broadcasted_iota(jnp.int32, (lb, 2 * lb), 0)
    ki = lax.broadcasted_iota(jnp.int32, (lb, 2 * lb), 1)

    def biases(pos_in_block):
        dist = pos_in_block(qi) + lb - (pos_in_block(ki & (lb - 1)) + (ki & lb))
        band = (dist >= 0) & (dist <= lb)
        b_any = jnp.where(band, 0.0, NEG_INF)
        b_first = jnp.where(band & (ki >= lb), 0.0, NEG_INF)
        return b_any, b_first, b_first[:, lb:]

    step_order = biases(lambda a: a)
    rows_p1 = lb // PERM_D
    p1_order = biases(lambda a: PERM_D * (a % rows_p1) + a // rows_p1)

    kd[0] = jnp.zeros((DIL_HD, lb), BF16)
    vd[0:lb, :] = jnp.zeros((lb, DIL_HD), BF16)

    n_r = PERM_TILE // PERM_D

    def combine(t0):
        rows = pl.ds(t0, PERM_TILE)
        a1, a2, a3 = m1[rows, :], m2[rows, :], m3[rows, :]
        m = jnp.maximum(jnp.maximum(a1, a2), a3)
        e1, e2, e3 = jnp.exp2(a1 - m), jnp.exp2(a2 - m), jnp.exp2(a3 - m)
        den = e1 * d1[rows, :] + e2 * d2[rows, :] + e3 * d3[rows, :]
        o = (e1 * o1[rows, :] + e2 * o2[rows, :] + e3 * o3[rows, :]) / den
        y = _head_norm_gate(o, g_ref[layer:layer + 1, :], z_ref[rows, :])
        for r in range(PERM_D):
            ynat[pl.ds(t0 + r, n_r, stride=PERM_D), :] = y[r * n_r:(r + 1) * n_r, :]
        y_ref[rows, :] = ynat[rows, :].astype(y_ref.dtype)

    order = sorted(range(len(DIL_PATTERNS)), key=lambda i: -DIL_PATTERNS[i][1])
    assert DIL_PATTERNS[order[-1]][1] == 1
    for pat in order:
        window, dil = DIL_PATTERNS[pat]
        o_scr, m_scr, d_scr = (o1, o2, o3)[pat], (m1, m2, m3)[pat], (d1, d2, d3)[pat]
        nb = seq // window
        bias_any, bias_first, bias_cur = p1_order if dil == 1 else step_order

        def load_block(ref, idx, pat=pat):
            chunks = _dil_block_chunks(pat, idx, seq)
            rows = lb // len(chunks)
            parts = [ref[pl.ds(s0, rows) if st == 1 else pl.ds(s0, rows, stride=st), :] for s0, st in chunks]
            return parts[0] if len(parts) == 1 else jnp.concatenate(parts, axis=0)

        def store_block(ref, idx, val, pat=pat):
            chunks = _dil_block_chunks(pat, idx, seq)
            rows = lb // len(chunks)
            for i, (s0, st) in enumerate(chunks):
                dst = pl.ds(s0, rows) if st == 1 else pl.ds(s0, rows, stride=st)
                ref[dst, :] = val[i * rows:(i + 1) * rows, :]

        for idx in range(n_blocks):
            dst = pl.ds(lb + idx * lb, lb)
            qd[dst, :] = load_block(q_ref, idx).astype(BF16)
            kd[idx + 1] = load_block(k_ref, idx).T.astype(BF16)
            vd[dst, :] = load_block(v_ref, idx).astype(BF16)

        def scores(idx, nb=nb, bias_any=bias_any, bias_cur=bias_cur):
            q = qd[pl.ds(lb + idx * lb, lb), :]
            if idx % nb == 0:
                return _dot(q, kd[idx + 1]) + bias_cur, pl.ds(lb + idx * lb, lb)
            s = _dot(q, jnp.concatenate([kd[idx], kd[idx + 1]], axis=1)) + bias_any
            return s, pl.ds(idx * lb, 2 * lb)

        def softmax(s):
            m = jnp.max(s, axis=-1, keepdims=True)
            return jnp.exp2((s - m).astype(BF16)), m

        def values(idx, p, m, keys, store_block=store_block, o_scr=o_scr, m_scr=m_scr, d_scr=d_scr):
            v_ones = jnp.concatenate([vd[keys, :], jnp.ones((p.shape[1], DIL_HD), BF16)], axis=1)
            acc = _dot(p, v_ones)
            store_block(o_scr, idx, acc[:, :DIL_HD])
            store_block(d_scr, idx, acc[:, DIL_HD:])
            store_block(m_scr, idx, jnp.broadcast_to(m, (lb, DIL_HD)))

        groups = [range(g * DIL_UNROLL, (g + 1) * DIL_UNROLL) for g in range(n_blocks // DIL_UNROLL)]
        pending = [scores(idx) for idx in groups[0]]
        for g, group in enumerate(groups):
            upcoming = [scores(idx) for idx in groups[g + 1]] if g + 1 < len(groups) else []
            probs = [softmax(s) for s, _ in pending]
            for idx, (p, m), (_, keys) in zip(group, probs, pending):
                values(idx, p, m, keys)
            pending = upcoming
            if dil == 1:
                for t0 in range(group[0] * lb, (group[-1] + 1) * lb, PERM_TILE):
                    combine(t0)


def _dilated(qb, kb, vb, zb, g_dil, layer):
    bsz, _, seq, _ = qb.shape
    blk = pl.BlockSpec((None, None, seq, DIL_HD), lambda b, h: (b, h, 0, 0))
    return pl.pallas_call(
        functools.partial(_dil_kernel, seq=seq, layer=layer),
        grid=(bsz, DIL_HEADS),
        in_specs=[blk, blk, blk, blk, pl.BlockSpec((g_dil.shape[0], DIL_HD), lambda b, h: (0, h))],
        out_specs=pl.BlockSpec((None, seq, DIL_HD), lambda b, h: (b, 0, h)),
        out_shape=jax.ShapeDtypeStruct((bsz, seq, DIL_WIDTH), BF16),
        scratch_shapes=[pltpu.VMEM((seq + DIL_LB, DIL_HD), BF16),
                        pltpu.VMEM((seq // DIL_LB + 1, DIL_HD, DIL_LB), BF16),
                        pltpu.VMEM((seq + DIL_LB, DIL_HD), BF16)]
        + [pltpu.VMEM((seq, DIL_HD), F32) for _ in range(10)],
        compiler_params=pltpu.CompilerParams(
            dimension_semantics=("arbitrary", "arbitrary"), vmem_limit_bytes=VMEM_LIMIT),
        name="dilated",
    )(qb, kb, vb, zb, g_dil)


def _rope_tables(seq):
    inv_freq = ROPE_THETA ** (-np.arange(0, DIL_HD, 2, dtype=np.float64) / DIL_HD)
    ang = np.arange(seq, dtype=np.float64)[:, None] * inv_freq[None, :]
    cos, sin = np.cos(ang), np.sin(ang)
    return (jnp.asarray(np.concatenate([cos, cos], axis=-1), F32),
            jnp.asarray(np.concatenate([-sin, sin], axis=-1), F32))


def kernel(x, c, w_ada, b_ada, g_pre, w_in, w_gate_up, b_gate_up, g_gla, g_dil, w_out, g_post):
    bsz, seq, d = x.shape
    depth = w_ada.shape[0]
    mod = _modulation(c, w_ada, b_ada).reshape(depth, bsz, 1, 3 * d)
    cos, sin = _rope_tables(seq)
    w_in_t = jnp.swapaxes(w_in, 1, 2).astype(BF16)
    w_gate_pad = jnp.pad(w_gate_up.astype(BF16), ((0, 0), (0, LR_PAD - GLA_LOWRANK), (0, 0)))
    for l in range(depth):
        qe, ke, kend, dec, va, za, qb, kb, vb, zb = _in_proj(
            x, mod, g_pre, w_in_t, w_gate_pad, b_gate_up, l, cos, sin)
        yb = _dilated(qb, kb, vb, zb, g_dil, l)
        x = _gla_out_proj(qe, ke, kend, dec, va, za, g_gla, yb, x, mod, w_out, g_post, l)
    return x
```

```python
import functools
import math

import jax
import jax.numpy as jnp
import numpy as np
from jax import lax
from jax.experimental import pallas as pl
from jax.experimental.pallas import tpu as pltpu

F32 = jnp.float32
BF16 = jnp.bfloat16

D_MODEL = 1024
GLA_HEADS = 4
GLA_DK = 64
GLA_DV = 128
GLA_QK = GLA_HEADS * GLA_DK
GLA_WIDTH = GLA_HEADS * GLA_DV
GLA_LOWRANK = 16
GLA_TAU = 16.0
GLA_CHUNK = 64
GLA_GROUP = 4
GLA_STAGE_CHUNKS = 8
DIL_HEADS = 4
DIL_HD = 128
DIL_WIDTH = DIL_HEADS * DIL_HD
DIL_PATTERNS = ((128, 1), (512, 4), (2048, 16))
DIL_LB = 128
PERM_TILE, PERM_D = DIL_PATTERNS[1]
ROPE_THETA = 10000.0
EPS = 1e-6
LANES = 128
LR_PAD = LANES
_GLA_SIZES = (("qa", GLA_QK), ("ka", GLA_QK), ("va", GLA_WIDTH), ("za", GLA_WIDTH), ("lr", LR_PAD))
W_GLA_COLS = {}
for _name, _size in _GLA_SIZES:
    _lo = sum(n for _, n in _GLA_SIZES[:len(W_GLA_COLS)])
    W_GLA_COLS[_name] = (_lo, _lo + _size)
W_DIL_START = W_GLA_COLS["lr"][0] + GLA_LOWRANK
W_DIL_NAMES = ("qb", "kb", "vb", "zb")

VMEM_LIMIT = 56 * 1024 * 1024
DIL_UNROLL = 16
ROW_TILE = 1024
NEG_INF = float("-inf")
LOG2E = math.log2(math.e)


def _silu(v):
    return v * jax.nn.sigmoid(v)


def _dot(a, b):
    return jnp.dot(a, b, preferred_element_type=F32)


def _dot_nt(a, b):
    return lax.dot_general(a, b, (((1,), (1,)), ((), ())), preferred_element_type=F32)


def _dot_tn(a, b):
    return lax.dot_general(a, b, (((0,), (0,)), ((), ())), preferred_element_type=F32)


def _mod_kernel(c_ref, w_ref, b_ref, o_ref):
    sc = _silu(c_ref[...]).astype(BF16)
    o_ref[...] = _dot(sc, w_ref[...].astype(BF16)) + b_ref[pl.ds(pl.program_id(0), 1), :]


def _modulation(c, w_ada, b_ada):
    depth, d, e = w_ada.shape
    bsz = c.shape[0]
    return pl.pallas_call(
        _mod_kernel,
        grid=(depth,),
        in_specs=[
            pl.BlockSpec((bsz, d), lambda l: (0, 0)),
            pl.BlockSpec((None, d, e), lambda l: (l, 0, 0)),
            pl.BlockSpec((depth, e), lambda l: (0, 0)),
        ],
        out_specs=pl.BlockSpec((None, bsz, e), lambda l: (l, 0, 0)),
        out_shape=jax.ShapeDtypeStruct((depth, bsz, e), F32),
        compiler_params=pltpu.CompilerParams(
            dimension_semantics=("arbitrary",), vmem_limit_bytes=VMEM_LIMIT),
        name="adaln_mod",
    )(c, w_ada, b_ada)


def _rope(v, cos, sin_signed):
    return v * cos + pltpu.roll(v, DIL_HD // 2, axis=1) * sin_signed


def _gla_decayed(q, k, la, tril_bd):
    c_len = GLA_CHUNK
    h1, h2 = _split2(la)
    b = _dot(tril_bd, h1) + _dot(tril_bd, h2)
    decay = [jnp.exp2(b[(c + 1) * c_len - 1:(c + 1) * c_len, :]) for c in range(GLA_GROUP)]
    decay_rows = jnp.concatenate([jnp.broadcast_to(d, (c_len, GLA_QK)) for d in decay], axis=0)
    k_e32 = k * jnp.exp2(-b)
    q_e = (q * jnp.exp2(b)).astype(BF16)
    return q_e, k_e32.astype(BF16), (k_e32 * decay_rows).astype(BF16), jnp.concatenate(decay, axis=0)


def _chunk_tril():
    grp = GLA_GROUP * GLA_CHUNK
    ri = lax.broadcasted_iota(jnp.int32, (grp, grp), 0)
    ci = lax.broadcasted_iota(jnp.int32, (grp, grp), 1)
    return ((ri >= ci) & ((ri & -GLA_CHUNK) == (ci & -GLA_CHUNK))).astype(BF16)


def _in_proj_kernel(x_ref, mod_ref, gpre_ref, w_ref, wg_ref, bg_ref, cos_ref, sin_ref,
                    qe_ref, ke_ref, kend_ref, dec_ref, va_ref, za_ref, qb_ref, kb_ref, vb_ref, zb_ref,
                    perm_ref, wz_ref, *, layer):
    @pl.when((pl.program_id(0) == 0) & (pl.program_id(1) == 0))
    def _():
        lo, hi = W_GLA_COLS["lr"]
        wz_ref[...] = _dot_tn(w_ref[lo:hi, :], wg_ref[...]).astype(BF16)

    shift = mod_ref[:, 0:D_MODEL]
    gain = gpre_ref[layer:layer + 1, :] * (1.0 + mod_ref[:, D_MODEL:2 * D_MODEL])
    tril_bd = _chunk_tril()
    n_res = PERM_TILE // PERM_D

    def store_residue_order(out_ref, hd, slab, t0, val):
        perm_ref[slab] = val
        for r in range(PERM_D):
            dst = slice(t0 + r * n_res, t0 + (r + 1) * n_res)
            out_ref[hd, dst, :] = perm_ref[slab, pl.ds(r, n_res, stride=PERM_D), :].astype(out_ref.dtype)

    for t0 in range(0, x_ref.shape[0], PERM_TILE):
        rows = slice(t0, t0 + PERM_TILE)
        x = x_ref[rows, :]
        h = (x * lax.rsqrt(jnp.mean(x * x, axis=-1, keepdims=True) + EPS) * gain + shift).astype(BF16)

        def proj(name):
            if name in W_DIL_NAMES:
                lo = W_DIL_START + W_DIL_NAMES.index(name) * DIL_WIDTH
                return _dot_nt(h, w_ref[lo:lo + DIL_WIDTH, :])
            lo, hi = W_GLA_COLS[name]
            return _dot_nt(h, w_ref[lo:hi, :])

        z = _dot(h, wz_ref[...]) + bg_ref[layer:layer + 1, :]
        la = (jnp.minimum(z, 0.0) - jnp.log1p(jnp.exp(-jnp.abs(z)))) * (LOG2E / GLA_TAU)
        va_ref[rows, :] = proj("va").astype(va_ref.dtype)
        za_ref[rows, :] = proj("za").astype(za_ref.dtype)
        q, k, v, zg = proj("qb"), proj("kb"), proj("vb"), proj("zb")
        q_a = proj("qa") * (GLA_DK ** -0.5)
        k_a = proj("ka")

        grp = GLA_GROUP * GLA_CHUNK
        for g0 in range(0, PERM_TILE, grp):
            gr = slice(g0, g0 + grp)
            q_e, k_e, k_end, decay = _gla_decayed(q_a[gr, :], k_a[gr, :], la[gr, :], tril_bd)
            out = slice(t0 + g0, t0 + g0 + grp)
            qe_ref[out, :] = q_e
            ke_ref[out, :] = k_e
            kend_ref[out, :] = k_end
            c0 = (t0 + g0) // GLA_CHUNK
            dec_ref[c0:c0 + GLA_GROUP, :] = decay

        cos = cos_ref[rows, :]
        sin = sin_ref[rows, :]
        for hd in range(DIL_HEADS):
            cols = slice(hd * DIL_HD, (hd + 1) * DIL_HD)
            store_residue_order(qb_ref, hd, 4 * hd, t0, _rope(q[:, cols], cos, sin) * (DIL_HD ** -0.5 * LOG2E))
            store_residue_order(kb_ref, hd, 4 * hd + 1, t0, _rope(k[:, cols], cos, sin))
            store_residue_order(vb_ref, hd, 4 * hd + 2, t0, v[:, cols])
            store_residue_order(zb_ref, hd, 4 * hd + 3, t0, zg[:, cols])


def _in_proj(x, mod, g_pre, w_in_t, w_gate_pad, b_gate, layer, cos, sin):
    bsz, seq, d = x.shape
    tm = ROW_TILE
    row = lambda b, i: (b, i, 0)
    const = lambda b, i: (0, 0)

    def of_layer(a):
        return pl.BlockSpec((None,) + a.shape[1:], lambda b, i: (layer,) + (0,) * (a.ndim - 1),
                            pipeline_mode=pl.Buffered(1))

    gla_cols = ((GLA_QK, BF16), (GLA_QK, BF16), (GLA_QK, BF16))
    gate_cols = ((GLA_WIDTH, BF16), (GLA_WIDTH, BF16))
    dil_types = (F32, F32, F32, BF16)
    return pl.pallas_call(
        functools.partial(_in_proj_kernel, layer=layer),
        grid=(bsz, seq // tm),
        in_specs=[
            pl.BlockSpec((None, tm, d), row),
            pl.BlockSpec((None, None, 1, 3 * d), lambda b, i: (layer, b, 0, 0)),
            pl.BlockSpec(g_pre.shape, const), of_layer(w_in_t), of_layer(w_gate_pad),
            pl.BlockSpec(b_gate.shape, const),
            pl.BlockSpec((tm, DIL_HD), lambda b, i: (i, 0)),
            pl.BlockSpec((tm, DIL_HD), lambda b, i: (i, 0)),
        ],
        out_specs=[pl.BlockSpec((None, tm, n), row) for n, _ in gla_cols]
        + [pl.BlockSpec((None, tm // GLA_CHUNK, GLA_QK), row)]
        + [pl.BlockSpec((None, tm, n), row) for n, _ in gate_cols]
        + [pl.BlockSpec((None, DIL_HEADS, tm, DIL_HD), lambda b, i: (b, 0, i, 0)) for _ in dil_types],
        out_shape=[jax.ShapeDtypeStruct((bsz, seq, n), dt) for n, dt in gla_cols]
        + [jax.ShapeDtypeStruct((bsz, seq // GLA_CHUNK, GLA_QK), F32)]
        + [jax.ShapeDtypeStruct((bsz, seq, n), dt) for n, dt in gate_cols]
        + [jax.ShapeDtypeStruct((bsz, DIL_HEADS, seq, DIL_HD), dt) for dt in dil_types],
        scratch_shapes=[pltpu.VMEM((4 * DIL_HEADS, PERM_TILE, DIL_HD), F32),
                        pltpu.VMEM((d, GLA_QK), BF16)],
        compiler_params=pltpu.CompilerParams(
            dimension_semantics=("arbitrary", "arbitrary"), vmem_limit_bytes=VMEM_LIMIT),
        name="in_proj",
    )(x, mod, g_pre, w_in_t, w_gate_pad, b_gate, cos, sin)


def _head_norm_gate(o, g, z):
    r = o * lax.rsqrt(jnp.mean(o * o, axis=-1, keepdims=True) + EPS)
    return r * g * _silu(z.astype(F32))


def _split2(v):
    h1 = v.astype(BF16)
    h2 = (v - h1.astype(F32)).astype(BF16)
    return h1, h2


def _gla_out_kernel(qe_ref, ke_ref, kend_ref, dec_ref, va_ref, za_ref, g_ref, yb_ref, x_ref, mod_ref, wo_ref,
                    gpost_ref, o_ref, st_ref, ya_ref, wo_bf, *, chunks, layer):
    @pl.when((pl.program_id(0) == 0) & (pl.program_id(1) == 0))
    def _():
        wo_bf[...] = wo_ref[...].astype(BF16)

    @pl.when(pl.program_id(1) == 0)
    def _():
        st_ref[...] = jnp.zeros_like(st_ref)

    c_len = GLA_CHUNK
    n_c = GLA_STAGE_CHUNKS
    grp = n_c * c_len
    causal = (lax.broadcasted_iota(jnp.int32, (c_len, c_len), 0)
              >= lax.broadcasted_iota(jnp.int32, (c_len, c_len), 1))
    heads = range(GLA_HEADS)
    kcol = [slice(hd * GLA_DK, (hd + 1) * GLA_DK) for hd in heads]
    vcol = [slice(hd * GLA_DV, (hd + 1) * GLA_DV) for hd in heads]

    def prep(r0):
        rows = pl.ds(r0, grp)
        c0 = r0 // c_len
        decay = [jnp.broadcast_to(dec_ref[c0 + c:c0 + c + 1, :], (GLA_DV, GLA_QK)).T for c in range(n_c)]
        v = [[va_ref[pl.ds(r0 + c * c_len, c_len), vcol[hd]] for hd in heads] for c in range(n_c)]
        return dict(r0=r0, q_e=qe_ref[rows, :], k_e=ke_ref[rows, :], k_end=kend_ref[rows, :], decay=decay, v=v)

    crow = [slice(c * c_len, (c + 1) * c_len) for c in range(n_c)]

    def intra(p):
        q_e, k_e, k_end, v = p["q_e"], p["k_e"], p["k_end"], p["v"]
        a = [[jnp.where(causal, _dot_nt(q_e[crow[c], kcol[hd]], k_e[crow[c], kcol[hd]]), 0.0).astype(BF16)
              for hd in heads] for c in range(n_c)]
        p["inc"] = [[_dot_tn(k_end[crow[c], kcol[hd]], v[c][hd]) for hd in heads] for c in range(n_c)]
        p["o"] = [[_dot(a[c][hd], v[c][hd]) for hd in heads] for c in range(n_c)]

    def inter(p, st):
        for hd in heads:
            for c in range(n_c):
                p["o"][c][hd] = p["o"][c][hd] + _dot(p["q_e"][crow[c], kcol[hd]], st[hd].astype(BF16))
                st[hd] = st[hd] * p["decay"][c][kcol[hd], :] + p["inc"][c][hd]

    def epilogue(p):
        for c in range(n_c):
            for hd in heads:
                out_rows = pl.ds(p["r0"] + c * c_len, c_len)
                ya_ref[out_rows, vcol[hd]] = _head_norm_gate(
                    p["o"][c][hd], g_ref[layer:layer + 1, vcol[hd]], za_ref[out_rows, vcol[hd]]).astype(ya_ref.dtype)

    gated_gain = gpost_ref[layer:layer + 1, :] * mod_ref[:, 2 * D_MODEL:3 * D_MODEL]

    def project(p):
        rows = pl.ds(p["r0"], grp)
        y = _dot(ya_ref[rows, :], wo_bf[0:GLA_WIDTH, :]) + _dot(yb_ref[rows, :], wo_bf[GLA_WIDTH:, :])
        o_ref[rows, :] = x_ref[rows, :] + y * lax.rsqrt(jnp.mean(y * y, axis=-1, keepdims=True) + EPS) * gated_gain

    st = [st_ref[hd] for hd in heads]
    groups = [prep(g * grp) for g in range(chunks // n_c)]
    intra(groups[0])
    for g, p in enumerate(groups):
        inter(p, st)
        if g + 1 < len(groups):
            intra(groups[g + 1])
        epilogue(p)
    for p in groups:
        project(p)
    for hd in heads:
        st_ref[hd] = st[hd]


def _gla_out_proj(qe, ke, kend, dec, va, za, g_gla, yb, x, mod, w_out, g_post, layer):
    bsz, seq, d = x.shape
    ts = ROW_TILE
    row = lambda b, i: (b, i, 0)
    const = lambda b, i: (0, 0)
    return pl.pallas_call(
        functools.partial(_gla_out_kernel, chunks=ts // GLA_CHUNK, layer=layer),
        grid=(bsz, seq // ts),
        in_specs=[
            pl.BlockSpec((None, ts, GLA_QK), row),
            pl.BlockSpec((None, ts, GLA_QK), row),
            pl.BlockSpec((None, ts, GLA_QK), row),
            pl.BlockSpec((None, ts // GLA_CHUNK, GLA_QK), row),
            pl.BlockSpec((None, ts, GLA_WIDTH), row),
            pl.BlockSpec((None, ts, GLA_WIDTH), row),
            pl.BlockSpec(g_gla.shape, const),
            pl.BlockSpec((None, ts, DIL_WIDTH), row),
            pl.BlockSpec((None, ts, d), row),
            pl.BlockSpec((None, None, 1, 3 * d), lambda b, i: (layer, b, 0, 0)),
            pl.BlockSpec((None,) + w_out.shape[1:], lambda b, i: (layer, 0, 0), pipeline_mode=pl.Buffered(1)),
            pl.BlockSpec(g_post.shape, const),
        ],
        out_specs=pl.BlockSpec((None, ts, d), row),
        out_shape=jax.ShapeDtypeStruct((bsz, seq, d), F32),
        scratch_shapes=[pltpu.VMEM((GLA_HEADS, GLA_DK, GLA_DV), F32), pltpu.VMEM((ts, GLA_WIDTH), BF16),
                        pltpu.VMEM(w_out.shape[1:], BF16)],
        compiler_params=pltpu.CompilerParams(
            dimension_semantics=("arbitrary", "arbitrary"), vmem_limit_bytes=VMEM_LIMIT),
        name="gla_out_proj",
    )(qe, ke, kend, dec, va, za, g_gla, yb, x, mod, w_out, g_post)


def _dil_block_chunks(pattern, idx, seq):
    window, dil = DIL_PATTERNS[pattern]
    nb = seq // window
    lb = DIL_LB
    n_r = PERM_TILE // PERM_D
    if dil == PERM_D:
        return [((idx % nb) * PERM_TILE + (idx // nb) * n_r, 1)]
    if dil == 1:
        tile, part = idx // (PERM_TILE // lb), idx % (PERM_TILE // lb)
        rows = lb // PERM_D
        return [(tile * PERM_TILE + r * n_r + part * rows, 1) for r in range(PERM_D)]
    sub = dil // PERM_D
    res, n = idx // nb, idx % nb
    r4, c = res % PERM_D, res // PERM_D
    tiles = window // PERM_TILE
    return [(n * window + t * PERM_TILE + r4 * n_r + c, sub) for t in range(tiles)]


def _dil_kernel(q_ref, k_ref, v_ref, z_ref, g_ref, y_ref, qd, kd, vd, ynat,
                o1, o2, o3, m1, m2, m3, d1, d2, d3, *, seq, layer):
    lb = DIL_LB
    n_blocks = seq // lb
    qi = lax.broadcasted_iota(jnp.int32, (lb, 2 * lb), 0)
    ki = lax.broadcasted_iota(jnp.int32, (lb, 2 * lb), 1)

    def biases(pos_in_block):
        dist = pos_in_block(qi) + lb - (pos_in_block(ki & (lb - 1)) + (ki & lb))
        band = (dist >= 0) & (dist <= lb)
        b_any = jnp.where(band, 0.0, NEG_INF)
        b_first = jnp.where(band & (ki >= lb), 0.0, NEG_INF)
        return b_any, b_first, b_first[:, lb:]

    step_order = biases(lambda a: a)
    rows_p1 = lb // PERM_D
    p1_order = biases(lambda a: PERM_D * (a % rows_p1) + a // rows_p1)

    kd[0] = jnp.zeros((DIL_HD, lb), BF16)
    vd[0:lb, :] = jnp.zeros((lb, DIL_HD), BF16)

    n_r = PERM_TILE // PERM_D

    def combine(t0):
        rows = pl.ds(t0, PERM_TILE)
        a1, a2, a3 = m1[rows, :], m2[rows, :], m3[rows, :]
        m = jnp.maximum(jnp.maximum(a1, a2), a3)
        e1, e2, e3 = jnp.exp2(a1 - m), jnp.exp2(a2 - m), jnp.exp2(a3 - m)
        den = e1 * d1[rows, :] + e2 * d2[rows, :] + e3 * d3[rows, :]
        o = (e1 * o1[rows, :] + e2 * o2[rows, :] + e3 * o3[rows, :]) / den
        y = _head_norm_gate(o, g_ref[layer:layer + 1, :], z_ref[rows, :])
        for r in range(PERM_D):
            ynat[pl.ds(t0 + r, n_r, stride=PERM_D), :] = y[r * n_r:(r + 1) * n_r, :]
        y_ref[rows, :] = ynat[rows, :].astype(y_ref.dtype)

    order = sorted(range(len(DIL_PATTERNS)), key=lambda i: -DIL_PATTERNS[i][1])
    assert DIL_PATTERNS[order[-1]][1] == 1
    for pat in order:
        window, dil = DIL_PATTERNS[pat]
        o_scr, m_scr, d_scr = (o1, o2, o3)[pat], (m1, m2, m3)[pat], (d1, d2, d3)[pat]
        nb = seq // window
        bias_any, bias_first, bias_cur = p1_order if dil == 1 else step_order

        def load_block(ref, idx, pat=pat):
            chunks = _dil_block_chunks(pat, idx, seq)
            rows = lb // len(chunks)
            parts = [ref[pl.ds(s0, rows) if st == 1 else pl.ds(s0, rows, stride=st), :] for s0, st in chunks]
            return parts[0] if len(parts) == 1 else jnp.concatenate(parts, axis=0)

        def store_block(ref, idx, val, pat=pat):
            chunks = _dil_block_chunks(pat, idx, seq)
            rows = lb // len(chunks)
            for i, (s0, st) in enumerate(chunks):
                dst = pl.ds(s0, rows) if st == 1 else pl.ds(s0, rows, stride=st)
                ref[dst, :] = val[i * rows:(i + 1) * rows, :]

        for idx in range(n_blocks):
            dst = pl.ds(lb + idx * lb, lb)
            qd[dst, :] = load_block(q_ref, idx).astype(BF16)
            kd[idx + 1] = load_block(k_ref, idx).T.astype(BF16)
            vd[dst, :] = load_block(v_ref, idx).astype(BF16)

        def scores(idx, nb=nb, bias_any=bias_any, bias_cur=bias_cur):
            q = qd[pl.ds(lb + idx * lb, lb), :]
            if idx % nb == 0:
                return _dot(q, kd[idx + 1]) + bias_cur, pl.ds(lb + idx * lb, lb)
            s = _dot(q, jnp.concatenate([kd[idx], kd[idx + 1]], axis=1)) + bias_any
            return s, pl.ds(idx * lb, 2 * lb)

        def softmax(s):
            m = jnp.max(s, axis=-1, keepdims=True)
            return jnp.exp2((s - m).astype(BF16)), m

        def values(idx, p, m, keys, store_block=store_block, o_scr=o_scr, m_scr=m_scr, d_scr=d_scr):
            v_ones = jnp.concatenate([vd[keys, :], jnp.ones((p.shape[1], DIL_HD), BF16)], axis=1)
            acc = _dot(p, v_ones)
            store_block(o_scr, idx, acc[:, :DIL_HD])
            store_block(d_scr, idx, acc[:, DIL_HD:])
            store_block(m_scr, idx, jnp.broadcast_to(m, (lb, DIL_HD)))

        groups = [range(g * DIL_UNROLL, (g + 1) * DIL_UNROLL) for g in range(n_blocks // DIL_UNROLL)]
        pending = [scores(idx) for idx in groups[0]]
        for g, group in enumerate(groups):
            upcoming = [scores(idx) for idx in groups[g + 1]] if g + 1 < len(groups) else []
            probs = [softmax(s) for s, _ in pending]
            for idx, (p, m), (_, keys) in zip(group, probs, pending):
                values(idx, p, m, keys)
            pending = upcoming
            if dil == 1:
                for t0 in range(group[0] * lb, (group[-1] + 1) * lb, PERM_TILE):
                    combine(t0)


def _dilated(qb, kb, vb, zb, g_dil, layer):
    bsz, _, seq, _ = qb.shape
    blk = pl.BlockSpec((None, None, seq, DIL_HD), lambda b, h: (b, h, 0, 0))
    return pl.pallas_call(
        functools.partial(_dil_kernel, seq=seq, layer=layer),
        grid=(bsz, DIL_HEADS),
        in_specs=[blk, blk, blk, blk, pl.BlockSpec((g_dil.shape[0], DIL_HD), lambda b, h: (0, h))],
        out_specs=pl.BlockSpec((None, seq, DIL_HD), lambda b, h: (b, 0, h)),
        out_shape=jax.ShapeDtypeStruct((bsz, seq, DIL_WIDTH), BF16),
        scratch_shapes=[pltpu.VMEM((seq + DIL_LB, DIL_HD), BF16),
                        pltpu.VMEM((seq // DIL_LB + 1, DIL_HD, DIL_LB), BF16),
                        pltpu.VMEM((seq + DIL_LB, DIL_HD), BF16)]
        + [pltpu.VMEM((seq, DIL_HD), F32) for _ in range(10)],
        compiler_params=pltpu.CompilerParams(
            dimension_semantics=("arbitrary", "arbitrary"), vmem_limit_bytes=VMEM_LIMIT),
        name="dilated",
    )(qb, kb, vb, zb, g_dil)


def _rope_tables(seq):
    inv_freq = ROPE_THETA ** (-np.arange(0, DIL_HD, 2, dtype=np.float64) / DIL_HD)
    ang = np.arange(seq, dtype=np.float64)[:, None] * inv_freq[None, :]
    cos, sin = np.cos(ang), np.sin(ang)
    return (jnp.asarray(np.concatenate([cos, cos], axis=-1), F32),
            jnp.asarray(np.concatenate([-sin, sin], axis=-1), F32))


def kernel(x, c, w_ada, b_ada, g_pre, w_in, w_gate_up, b_gate_up, g_gla, g_dil, w_out, g_post):
    bsz, seq, d = x.shape
    depth = w_ada.shape[0]
    mod = _modulation(c, w_ada, b_ada).reshape(depth, bsz, 1, 3 * d)
    cos, sin = _rope_tables(seq)
    w_in_t = jnp.swapaxes(w_in, 1, 2).astype(BF16)
    w_gate_pad = jnp.pad(w_gate_up.astype(BF16), ((0, 0), (0, LR_PAD - GLA_LOWRANK), (0, 0)))
    for l in range(depth):
        qe, ke, kend, dec, va, za, qb, kb, vb, zb = _in_proj(
            x, mod, g_pre, w_in_t, w_gate_pad, b_gate_up, l, cos, sin)
        yb = _dilated(qb, kb, vb, zb, g_dil, l)
        x = _gla_out_proj(qe, ke, kend, dec, va, za, g_gla, yb, x, mod, w_out, g_post, l)
    return x
```

```python
import functools
import math

import jax
import jax.numpy as jnp
import numpy as np
from jax import lax
from jax.experimental import pallas as pl
from jax.experimental.pallas import tpu as pltpu

F32 = jnp.float32
BF16 = jnp.bfloat16

D_MODEL = 1024
GLA_HEADS = 4
GLA_DK = 64
GLA_DV = 128
GLA_QK = GLA_HEADS * GLA_DK
GLA_WIDTH = GLA_HEADS * GLA_DV
GLA_LOWRANK = 16
GLA_TAU = 16.0
GLA_CHUNK = 64
GLA_GROUP = 4
GLA_STAGE_CHUNKS = 8
DIL_HEADS = 4
DIL_HD = 128
DIL_WIDTH = DIL_HEADS * DIL_HD
DIL_PATTERNS = ((128, 1), (512, 4), (2048, 16))
DIL_LB = 128
PERM_TILE, PERM_D = DIL_PATTERNS[1]
ROPE_THETA = 10000.0
EPS = 1e-6
LANES = 128
LR_PAD = LANES
_GLA_SIZES = (("qa", GLA_QK), ("ka", GLA_QK), ("va", GLA_WIDTH), ("za", GLA_WIDTH), ("lr", LR_PAD))
W_GLA_COLS = {}
for _name, _size in _GLA_SIZES:
    _lo = sum(n for _, n in _GLA_SIZES[:len(W_GLA_COLS)])
    W_GLA_COLS[_name] = (_lo, _lo + _size)
W_DIL_START = W_GLA_COLS["lr"][0] + GLA_LOWRANK
W_DIL_NAMES = ("qb", "kb", "vb", "zb")

VMEM_LIMIT = 56 * 1024 * 1024
DIL_UNROLL = 16
ROW_TILE = 1024
NEG_INF = float("-inf")
LOG2E = math.log2(math.e)


def _silu(v):
    return v * jax.nn.sigmoid(v)


def _dot(a, b):
    return jnp.dot(a, b, preferred_element_type=F32)


def _dot_nt(a, b):
    return lax.dot_general(a, b, (((1,), (1,)), ((), ())), preferred_element_type=F32)


def _dot_tn(a, b):
    return lax.dot_general(a, b, (((0,), (0,)), ((), ())), preferred_element_type=F32)


def _mod_kernel(c_ref, w_ref, b_ref, o_ref):
    sc = _silu(c_ref[...]).astype(BF16)
    o_ref[...] = _dot(sc, w_ref[...].astype(BF16)) + b_ref[pl.ds(pl.program_id(0), 1), :]


def _modulation(c, w_ada, b_ada):
    depth, d, e = w_ada.shape
    bsz = c.shape[0]
    return pl.pallas_call(
        _mod_kernel,
        grid=(depth,),
        in_specs=[
            pl.BlockSpec((bsz, d), lambda l: (0, 0)),
            pl.BlockSpec((None, d, e), lambda l: (l, 0, 0)),
            pl.BlockSpec((depth, e), lambda l: (0, 0)),
        ],
        out_specs=pl.BlockSpec((None, bsz, e), lambda l: (l, 0, 0)),
        out_shape=jax.ShapeDtypeStruct((depth, bsz, e), F32),
        compiler_params=pltpu.CompilerParams(
            dimension_semantics=("arbitrary",), vmem_limit_bytes=VMEM_LIMIT),
        name="adaln_mod",
    )(c, w_ada, b_ada)


def _rope(v, cos, sin_signed):
    return v * cos + pltpu.roll(v, DIL_HD // 2, axis=1) * sin_signed


def _gla_decayed(q, k, la, tril_bd):
    c_len = GLA_CHUNK
    h1, h2 = _split2(la)
    b = _dot(tril_bd, h1) + _dot(tril_bd, h2)
    decay = [jnp.exp2(b[(c + 1) * c_len - 1:(c + 1) * c_len, :]) for c in range(GLA_GROUP)]
    decay_rows = jnp.concatenate([jnp.broadcast_to(d, (c_len, GLA_QK)) for d in decay], axis=0)
    k_e32 = k * jnp.exp2(-b)
    q_e = (q * jnp.exp2(b)).astype(BF16)
    return q_e, k_e32.astype(BF16), (k_e32 * decay_rows).astype(BF16), jnp.concatenate(decay, axis=0)


def _chunk_tril():
    grp = GLA_GROUP * GLA_CHUNK
    ri = lax.broadcasted_iota(jnp.int32, (grp, grp), 0)
    ci = lax.broadcasted_iota(jnp.int32, (grp, grp), 1)
    return ((ri >= ci) & ((ri & -GLA_CHUNK) == (ci & -GLA_CHUNK))).astype(BF16)


def _in_proj_kernel(x_ref, mod_ref, gpre_ref, w_ref, wg_ref, bg_ref, cos_ref, sin_ref,
                    qe_ref, ke_ref, kend_ref, dec_ref, va_ref, za_ref, qb_ref, kb_ref, vb_ref, zb_ref,
                    perm_ref, wz_ref, *, layer):
    @pl.when((pl.program_id(0) == 0) & (pl.program_id(1) == 0))
    def _():
        lo, hi = W_GLA_COLS["lr"]
        wz_ref[...] = _dot_tn(w_ref[lo:hi, :], wg_ref[...]).astype(BF16)

    shift = mod_ref[:, 0:D_MODEL]
    gain = gpre_ref[layer:layer + 1, :] * (1.0 + mod_ref[:, D_MODEL:2 * D_MODEL])
    tril_bd = _chunk_tril()
    n_res = PERM_TILE // PERM_D

    def store_residue_order(out_ref, hd, slab, t0, val):
        perm_ref[slab] = val
        for r in range(PERM_D):
            dst = slice(t0 + r * n_res, t0 + (r + 1) * n_res)
            out_ref[hd, dst, :] = perm_ref[slab, pl.ds(r, n_res, stride=PERM_D), :].astype(out_ref.dtype)

    for t0 in range(0, x_ref.shape[0], PERM_TILE):
        rows = slice(t0, t0 + PERM_TILE)
        x = x_ref[rows, :]
        h = (x * lax.rsqrt(jnp.mean(x * x, axis=-1, keepdims=True) + EPS) * gain + shift).astype(BF16)

        def proj(name):
            if name in W_DIL_NAMES:
                lo = W_DIL_START + W_DIL_NAMES.index(name) * DIL_WIDTH
                return _dot_nt(h, w_ref[lo:lo + DIL_WIDTH, :])
            lo, hi = W_GLA_COLS[name]
            return _dot_nt(h, w_ref[lo:hi, :])

        z = _dot(h, wz_ref[...]) + bg_ref[layer:layer + 1, :]
        la = (jnp.minimum(z, 0.0) - jnp.log1p(jnp.exp(-jnp.abs(z)))) * (LOG2E / GLA_TAU)
        va_ref[rows, :] = proj("va").astype(va_ref.dtype)
        za_ref[rows, :] = proj("za").astype(za_ref.dtype)
        q, k, v, zg = proj("qb"), proj("kb"), proj("vb"), proj("zb")
        q_a = proj("qa") * (GLA_DK ** -0.5)
        k_a = proj("ka")

        grp = GLA_GROUP * GLA_CHUNK
        for g0 in range(0, PERM_TILE, grp):
            gr = slice(g0, g0 + grp)
            q_e, k_e, k_end, decay = _gla_decayed(q_a[gr, :], k_a[gr, :], la[gr, :], tril_bd)
            out = slice(t0 + g0, t0 + g0 + grp)
            qe_ref[out, :] = q_e
            ke_ref[out, :] = k_e
            kend_ref[out, :] = k_end
            c0 = (t0 + g0) // GLA_CHUNK
            dec_ref[c0:c0 + GLA_GROUP, :] = decay

        cos = cos_ref[rows, :]
        sin = sin_ref[rows, :]
        for hd in range(DIL_HEADS):
            cols = slice(hd * DIL_HD, (hd + 1) * DIL_HD)
            store_residue_order(qb_ref, hd, 4 * hd, t0, _rope(q[:, cols], cos, sin) * (DIL_HD ** -0.5 * LOG2E))
            store_residue_order(kb_ref, hd, 4 * hd + 1, t0, _rope(k[:, cols], cos, sin))
            store_residue_order(vb_ref, hd, 4 * hd + 2, t0, v[:, cols])
            store_residue_order(zb_ref, hd, 4 * hd + 3, t0, zg[:, cols])


def _in_proj(x, mod, g_pre, w_in_t, w_gate_pad, b_gate, layer, cos, sin):
    bsz, seq, d = x.shape
    tm = ROW_TILE
    row = lambda b, i: (b, i, 0)
    const = lambda b, i: (0, 0)

    def of_layer(a):
        return pl.BlockSpec((None,) + a.shape[1:], lambda b, i: (layer,) + (0,) * (a.ndim - 1),
                            pipeline_mode=pl.Buffered(1))

    gla_cols = ((GLA_QK, BF16), (GLA_QK, BF16), (GLA_QK, BF16))
    gate_cols = ((GLA_WIDTH, BF16), (GLA_WIDTH, BF16))
    dil_types = (F32, F32, F32, BF16)
    return pl.pallas_call(
        functools.partial(_in_proj_kernel, layer=layer),
        grid=(bsz, seq // tm),
        in_specs=[
            pl.BlockSpec((None, tm, d), row),
            pl.BlockSpec((None, None, 1, 3 * d), lambda b, i: (layer, b, 0, 0)),
            pl.BlockSpec(g_pre.shape, const), of_layer(w_in_t), of_layer(w_gate_pad),
            pl.BlockSpec(b_gate.shape, const),
            pl.BlockSpec((tm, DIL_HD), lambda b, i: (i, 0)),
            pl.BlockSpec((tm, DIL_HD), lambda b, i: (i, 0)),
        ],
        out_specs=[pl.BlockSpec((None, tm, n), row) for n, _ in gla_cols]
        + [pl.BlockSpec((None, tm // GLA_CHUNK, GLA_QK), row)]
        + [pl.BlockSpec((None, tm, n), row) for n, _ in gate_cols]
        + [pl.BlockSpec((None, DIL_HEADS, tm, DIL_HD), lambda b, i: (b, 0, i, 0)) for _ in dil_types],
        out_shape=[jax.ShapeDtypeStruct((bsz, seq, n), dt) for n, dt in gla_cols]
        + [jax.ShapeDtypeStruct((bsz, seq // GLA_CHUNK, GLA_QK), F32)]
        + [jax.ShapeDtypeStruct((bsz, seq, n), dt) for n, dt in gate_cols]
        + [jax.ShapeDtypeStruct((bsz, DIL_HEADS, seq, DIL_HD), dt) for dt in dil_types],
        scratch_shapes=[pltpu.VMEM((4 * DIL_HEADS, PERM_TILE, DIL_HD), F32),
                        pltpu.VMEM((d, GLA_QK), BF16)],
        compiler_params=pltpu.CompilerParams(
            dimension_semantics=("arbitrary", "arbitrary"), vmem_limit_bytes=VMEM_LIMIT),
        name="in_proj",
    )(x, mod, g_pre, w_in_t, w_gate_pad, b_gate, cos, sin)


def _head_norm_gate(o, g, z):
    r = o * lax.rsqrt(jnp.mean(o * o, axis=-1, keepdims=True) + EPS)
    return r * g * _silu(z.astype(F32))


def _split2(v):
    h1 = v.astype(BF16)
    h2 = (v - h1.astype(F32)).astype(BF16)
    return h1, h2


def _gla_out_kernel(qe_ref, ke_ref, kend_ref, dec_ref, va_ref, za_ref, g_ref, yb_ref, x_ref, mod_ref, wo_ref,
                    gpost_ref, o_ref, st_ref, ya_ref, wo_bf, *, chunks, layer):
    @pl.when((pl.program_id(0) == 0) & (pl.program_id(1) == 0))
    def _():
        wo_bf[...] = wo_ref[...].astype(BF16)

    @pl.when(pl.program_id(1) == 0)
    def _():
        st_ref[...] = jnp.zeros_like(st_ref)

    c_len = GLA_CHUNK
    n_c = GLA_STAGE_CHUNKS
    grp = n_c * c_len
    causal = (lax.broadcasted_iota(jnp.int32, (c_len, c_len), 0)
              >= lax.broadcasted_iota(jnp.int32, (c_len, c_len), 1))
    heads = range(GLA_HEADS)
    kcol = [slice(hd * GLA_DK, (hd + 1) * GLA_DK) for hd in heads]
    vcol = [slice(hd * GLA_DV, (hd + 1) * GLA_DV) for hd in heads]

    def prep(r0):
        rows = pl.ds(r0, grp)
        c0 = r0 // c_len
        decay = [jnp.broadcast_to(dec_ref[c0 + c:c0 + c + 1, :], (GLA_DV, GLA_QK)).T for c in range(n_c)]
        v = [[va_ref[pl.ds(r0 + c * c_len, c_len), vcol[hd]] for hd in heads] for c in range(n_c)]
        return dict(r0=r0, q_e=qe_ref[rows, :], k_e=ke_ref[rows, :], k_end=kend_ref[rows, :], decay=decay, v=v)

    crow = [slice(c * c_len, (c + 1) * c_len) for c in range(n_c)]

    def intra(p):
        q_e, k_e, k_end, v = p["q_e"], p["k_e"], p["k_end"], p["v"]
        a = [[jnp.where(causal, _dot_nt(q_e[crow[c], kcol[hd]], k_e[crow[c], kcol[hd]]), 0.0).astype(BF16)
              for hd in heads] for c in range(n_c)]
        p["inc"] = [[_dot_tn(k_end[crow[c], kcol[hd]], v[c][hd]) for hd in heads] for c in range(n_c)]
        p["o"] = [[_dot(a[c][hd], v[c][hd]) for hd in heads] for c in range(n_c)]

    def inter(p, st):
        for hd in heads:
            for c in range(n_c):
                p["o"][c][hd] = p["o"][c][hd] + _dot(p["q_e"][crow[c], kcol[hd]], st[hd].astype(BF16))
                st[hd] = st[hd] * p["decay"][c][kcol[hd], :] + p["inc"][c][hd]

    def epilogue(p):
        for c in range(n_c):
            for hd in heads:
                out_rows = pl.ds(p["r0"] + c * c_len, c_len)
                ya_ref[out_rows, vcol[hd]] = _head_norm_gate(
                    p["o"][c][hd], g_ref[layer:layer + 1, vcol[hd]], za_ref[out_rows, vcol[hd]]).astype(ya_ref.dtype)

    gated_gain = gpost_ref[layer:layer + 1, :] * mod_ref[:, 2 * D_MODEL:3 * D_MODEL]

    def project(p):
        rows = pl.ds(p["r0"], grp)
        yb = jnp.concatenate([yb_ref[hd, rows, :] for hd in range(DIL_HEADS)], axis=1)
        y = _dot(ya_ref[rows, :], wo_bf[0:GLA_WIDTH, :]) + _dot(yb, wo_bf[GLA_WIDTH:, :])
        o_ref[rows, :] = x_ref[rows, :] + y * lax.rsqrt(jnp.mean(y * y, axis=-1, keepdims=True) + EPS) * gated_gain

    st = [st_ref[hd] for hd in heads]
    groups = [prep(g * grp) for g in range(chunks // n_c)]
    intra(groups[0])
    for g, p in enumerate(groups):
        inter(p, st)
        if g + 1 < len(groups):
            intra(groups[g + 1])
        epilogue(p)
        project(p)
    for hd in heads:
        st_ref[hd] = st[hd]


def _gla_out_proj(qe, ke, kend, dec, va, za, g_gla, yb, x, mod, w_out, g_post, layer):
    bsz, seq, d = x.shape
    ts = ROW_TILE
    row = lambda b, i: (b, i, 0)
    const = lambda b, i: (0, 0)
    return pl.pallas_call(
        functools.partial(_gla_out_kernel, chunks=ts // GLA_CHUNK, layer=layer),
        grid=(bsz, seq // ts),
        in_specs=[
            pl.BlockSpec((None, ts, GLA_QK), row),
            pl.BlockSpec((None, ts, GLA_QK), row),
            pl.BlockSpec((None, ts, GLA_QK), row),
            pl.BlockSpec((None, ts // GLA_CHUNK, GLA_QK), row),
            pl.BlockSpec((None, ts, GLA_WIDTH), row),
            pl.BlockSpec((None, ts, GLA_WIDTH), row),
            pl.BlockSpec(g_gla.shape, const),
            pl.BlockSpec((None, DIL_HEADS, ts, DIL_HD), lambda b, i: (b, 0, i, 0)),
            pl.BlockSpec((None, ts, d), row),
            pl.BlockSpec((None, None, 1, 3 * d), lambda b, i: (layer, b, 0, 0)),
            pl.BlockSpec((None,) + w_out.shape[1:], lambda b, i: (layer, 0, 0), pipeline_mode=pl.Buffered(1)),
            pl.BlockSpec(g_post.shape, const),
        ],
        out_specs=pl.BlockSpec((None, ts, d), row),
        out_shape=jax.ShapeDtypeStruct((bsz, seq, d), F32),
        scratch_shapes=[pltpu.VMEM((GLA_HEADS, GLA_DK, GLA_DV), F32), pltpu.VMEM((ts, GLA_WIDTH), BF16),
                        pltpu.VMEM(w_out.shape[1:], BF16)],
        compiler_params=pltpu.CompilerParams(
            dimension_semantics=("arbitrary", "arbitrary"), vmem_limit_bytes=VMEM_LIMIT),
        name="gla_out_proj",
    )(qe, ke, kend, dec, va, za, g_gla, yb, x, mod, w_out, g_post)


def _dil_block_chunks(pattern, idx, seq):
    window, dil = DIL_PATTERNS[pattern]
    nb = seq // window
    lb = DIL_LB
    n_r = PERM_TILE // PERM_D
    if dil == PERM_D:
        return [((idx % nb) * PERM_TILE + (idx // nb) * n_r, 1)]
    if dil == 1:
        tile, part = idx // (PERM_TILE // lb), idx % (PERM_TILE // lb)
        rows = lb // PERM_D
        return [(tile * PERM_TILE + r * n_r + part * rows, 1) for r in range(PERM_D)]
    sub = dil // PERM_D
    res, n = idx // nb, idx % nb
    r4, c = res % PERM_D, res // PERM_D
    tiles = window // PERM_TILE
    return [(n * window + t * PERM_TILE + r4 * n_r + c, sub) for t in range(tiles)]


def _dil_kernel(q_ref, k_ref, v_ref, z_ref, g_ref, y_ref, qd, kd, vd, ynat,
                o1, o2, o3, m1, m2, m3, d1, d2, d3, *, seq, layer):
    lb = DIL_LB
    n_blocks = seq // lb
    qi = lax.broadcasted_iota(jnp.int32, (lb, 2 * lb), 0)
    ki = lax.broadcasted_iota(jnp.int32, (lb, 2 * lb), 1)

    def biases(pos_in_block):
        dist = pos_in_block(qi) + lb - (pos_in_block(ki & (lb - 1)) + (ki & lb))
        band = (dist >= 0) & (dist <= lb)
        b_any = jnp.where(band, 0.0, NEG_INF)
        b_first = jnp.where(band & (ki >= lb), 0.0, NEG_INF)
        return b_any, b_first, b_first[:, lb:]

    step_order = biases(lambda a: a)
    rows_p1 = lb // PERM_D
    p1_order = biases(lambda a: PERM_D * (a % rows_p1) + a // rows_p1)

    kd[0] = jnp.zeros((DIL_HD, lb), BF16)
    vd[0:lb, :] = jnp.zeros((lb, DIL_HD), BF16)

    n_r = PERM_TILE // PERM_D

    def combine(t0):
        rows = pl.ds(t0, PERM_TILE)
        a1, a2, a3 = m1[rows, :], m2[rows, :], m3[rows, :]
        m = jnp.maximum(jnp.maximum(a1, a2), a3)
        e1, e2, e3 = jnp.exp2(a1 - m), jnp.exp2(a2 - m), jnp.exp2(a3 - m)
        den = e1 * d1[rows, :] + e2 * d2[rows, :] + e3 * d3[rows, :]
        o = (e1 * o1[rows, :] + e2 * o2[rows, :] + e3 * o3[rows, :]) / den
        y = _head_norm_gate(o, g_ref[layer:layer + 1, :], z_ref[rows, :])
        for r in range(PERM_D):
            ynat[pl.ds(t0 + r, n_r, stride=PERM_D), :] = y[r * n_r:(r + 1) * n_r, :]
        y_ref[rows, :] = ynat[rows, :].astype(y_ref.dtype)

    order = sorted(range(len(DIL_PATTERNS)), key=lambda i: -DIL_PATTERNS[i][1])
    assert DIL_PATTERNS[order[-1]][1] == 1
    for pat in order:
        window, dil = DIL_PATTERNS[pat]
        o_scr, m_scr, d_scr = (o1, o2, o3)[pat], (m1, m2, m3)[pat], (d1, d2, d3)[pat]
        nb = seq // window
        bias_any, bias_first, bias_cur = p1_order if dil == 1 else step_order

        def load_block(ref, idx, pat=pat):
            chunks = _dil_block_chunks(pat, idx, seq)
            rows = lb // len(chunks)
            parts = [ref[pl.ds(s0, rows) if st == 1 else pl.ds(s0, rows, stride=st), :] for s0, st in chunks]
            return parts[0] if len(parts) == 1 else jnp.concatenate(parts, axis=0)

        def store_block(ref, idx, val, pat=pat):
            chunks = _dil_block_chunks(pat, idx, seq)
            rows = lb // len(chunks)
            for i, (s0, st) in enumerate(chunks):
                dst = pl.ds(s0, rows) if st == 1 else pl.ds(s0, rows, stride=st)
                ref[dst, :] = val[i * rows:(i + 1) * rows, :]

        for idx in range(n_blocks):
            dst = pl.ds(lb + idx * lb, lb)
            qd[dst, :] = load_block(q_ref, idx).astype(BF16)
            kd[idx + 1] = load_block(k_ref, idx).T.astype(BF16)
            vd[dst, :] = load_block(v_ref, idx).astype(BF16)

        def scores(idx, nb=nb, bias_any=bias_any, bias_cur=bias_cur):
            q = qd[pl.ds(lb + idx * lb, lb), :]
            if idx % nb == 0:
                return _dot(q, kd[idx + 1]) + bias_cur, pl.ds(lb + idx * lb, lb)
            s = _dot(q, jnp.concatenate([kd[idx], kd[idx + 1]], axis=1)) + bias_any
            return s, pl.ds(idx * lb, 2 * lb)

        def softmax(s):
            m = jnp.max(s, axis=-1, keepdims=True)
            return jnp.exp2((s - m).astype(BF16)), m

        def values(idx, p, m, keys, store_block=store_block, o_scr=o_scr, m_scr=m_scr, d_scr=d_scr):
            v_ones = jnp.concatenate([vd[keys, :], jnp.ones((p.shape[1], DIL_HD), BF16)], axis=1)
            acc = _dot(p, v_ones)
            store_block(o_scr, idx, acc[:, :DIL_HD])
            store_block(d_scr, idx, acc[:, DIL_HD:])
            store_block(m_scr, idx, jnp.broadcast_to(m, (lb, DIL_HD)))

        groups = [range(g * DIL_UNROLL, (g + 1) * DIL_UNROLL) for g in range(n_blocks // DIL_UNROLL)]
        pending = [scores(idx) for idx in groups[0]]
        for g, group in enumerate(groups):
            upcoming = [scores(idx) for idx in groups[g + 1]] if g + 1 < len(groups) else []
            probs = [softmax(s) for s, _ in pending]
            for idx, (p, m), (_, keys) in zip(group, probs, pending):
                values(idx, p, m, keys)
            pending = upcoming
            if dil == 1:
                for t0 in range(group[0] * lb, (group[-1] + 1) * lb, PERM_TILE):
                    combine(t0)


def _dilated(qb, kb, vb, zb, g_dil, layer):
    bsz, _, seq, _ = qb.shape
    blk = pl.BlockSpec((None, None, seq, DIL_HD), lambda b, h: (b, h, 0, 0))
    return pl.pallas_call(
        functools.partial(_dil_kernel, seq=seq, layer=layer),
        grid=(bsz, DIL_HEADS),
        in_specs=[blk, blk, blk, blk, pl.BlockSpec((g_dil.shape[0], DIL_HD), lambda b, h: (0, h))],
        out_specs=blk,
        out_shape=jax.ShapeDtypeStruct((bsz, DIL_HEADS, seq, DIL_HD), BF16),
        scratch_shapes=[pltpu.VMEM((seq + DIL_LB, DIL_HD), BF16),
                        pltpu.VMEM((seq // DIL_LB + 1, DIL_HD, DIL_LB), BF16),
                        pltpu.VMEM((seq + DIL_LB, DIL_HD), BF16)]
        + [pltpu.VMEM((seq, DIL_HD), F32) for _ in range(10)],
        compiler_params=pltpu.CompilerParams(
            dimension_semantics=("arbitrary", "arbitrary"), vmem_limit_bytes=VMEM_LIMIT),
        name="dilated",
    )(qb, kb, vb, zb, g_dil)


def _rope_tables(seq):
    inv_freq = ROPE_THETA ** (-np.arange(0, DIL_HD, 2, dtype=np.float64) / DIL_HD)
    ang = np.arange(seq, dtype=np.float64)[:, None] * inv_freq[None, :]
    cos, sin = np.cos(ang), np.sin(ang)
    return (jnp.asarray(np.concatenate([cos, cos], axis=-1), F32),
            jnp.asarray(np.concatenate([-sin, sin], axis=-1), F32))


def kernel(x, c, w_ada, b_ada, g_pre, w_in, w_gate_up, b_gate_up, g_gla, g_dil, w_out, g_post):
    bsz, seq, d = x.shape
    depth = w_ada.shape[0]
    mod = _modulation(c, w_ada, b_ada).reshape(depth, bsz, 1, 3 * d)
    cos, sin = _rope_tables(seq)
    w_in_t = jnp.swapaxes(w_in, 1, 2).astype(BF16)
    w_gate_pad = jnp.pad(w_gate_up.astype(BF16), ((0, 0), (0, LR_PAD - GLA_LOWRANK), (0, 0)))
    for l in range(depth):
        qe, ke, kend, dec, va, za, qb, kb, vb, zb = _in_proj(
            x, mod, g_pre, w_in_t, w_gate_pad, b_gate_up, l, cos, sin)
        yb = _dilated(qb, kb, vb, zb, g_dil, l)
        x = _gla_out_proj(qe, ke, kend, dec, va, za, g_gla, yb, x, mod, w_out, g_post, l)
    return x
```
